```python
import math, functools
import jax, jax.numpy as jnp
from jax import lax
import numpy as np

D_MODEL = 1024
BATCH = 2
SEQ = 8192
DEPTH = 2
DEC_BATCH = 32
DEC_SEQ = 4
PAST_LEN = 8192
PAGE_SIZE = 128

MIX_WIDTH = D_MODEL
GROUP_WIDTH = MIX_WIDTH // 4
S5_WIDTH = GROUP_WIDTH
S5_CH = 16
S5_GROUPS = S5_WIDTH // S5_CH
S5_STATE = 64
S5_DT_MIN = 1e-3
S5_DT_MAX = 1e-1
LRU_WIDTH = GROUP_WIDTH
LRU_HEADS = 8
LRU_HD = LRU_WIDTH // LRU_HEADS
LRU_CONV = 4
LRU_C = 8.0
CC_WIDTH = GROUP_WIDTH
CC_CONV = 31
ATT_WIDTH = GROUP_WIDTH
HEAD_DIM = 64
ATT_HEADS = ATT_WIDTH // HEAD_DIM
IDX_HEADS = 4
IDX_DIM = 64
TOPK_MAX = 256
Q_BLOCK = 128
ROPE_THETA = 10000.0
ATT_SCALE = HEAD_DIM ** -0.5
IDX_SCALE = (IDX_DIM * IDX_HEADS) ** -0.5
D_FF = 2816
N_EXPERTS = 8
TOP_K = 2
D_FF_EXPERT = 3584
N_DENSE = (DEPTH + 1) // 2
N_MOE = DEPTH // 2
DN_ALPHA = (2.0 * DEPTH) ** 0.25
DN_BETA = (8.0 * DEPTH) ** -0.25
LN_EPS = 1e-5
IN_SIZES = (S5_WIDTH, LRU_WIDTH, LRU_WIDTH, 2 * CC_WIDTH, ATT_WIDTH, ATT_WIDTH, ATT_WIDTH,
            IDX_HEADS * IDX_DIM, IDX_DIM, IDX_HEADS)
D_IN = sum(IN_SIZES)
IN_SPLITS = [int(s) for s in np.cumsum(IN_SIZES)[:-1]]

kernel_name = 'hybrid_s5_rglru_conformer_dsa_decode_step'


def layer_norm(x, g, b):
    xf = x.astype(jnp.float32)
    mu = jnp.mean(xf, -1, keepdims=True)
    var = jnp.mean(jnp.square(xf - mu), -1, keepdims=True)
    return ((xf - mu) * lax.rsqrt(var + LN_EPS) * g.astype(jnp.float32) + b.astype(jnp.float32)).astype(x.dtype)


def rope(x, pos):
    half = x.shape[-1] // 2
    inv = ROPE_THETA ** (-(jnp.arange(half, dtype=jnp.float32) / half))
    ang = pos.astype(jnp.float32)[:, None] * inv[None, :]
    ang = ang.reshape((ang.shape[0],) + (1,) * (x.ndim - 3) + (half,))
    cos, sin = jnp.cos(ang), jnp.sin(ang)
    x1 = x[..., :half].astype(jnp.float32)
    x2 = x[..., half:].astype(jnp.float32)
    return jnp.concatenate([x1 * cos - x2 * sin, x2 * cos + x1 * sin], -1).astype(x.dtype)


def linear_scan(a, b, h0):
    b = b.at[:, 0].add(a[:, 0] * h0)
    def comb(l, r):
        return (l[0] * r[0], r[0] * l[1] + r[1])
    _, h = lax.associative_scan(comb, (a, b), axis=1)
    return h


def causal_dwconv(x, buf, w, b):
    width, ch = w.shape
    xp = jnp.concatenate([buf.astype(x.dtype), x], axis=1)
    y = lax.conv_general_dilated(xp, w[:, None, :].astype(x.dtype), window_strides=(1,), padding='VALID',
                                 dimension_numbers=('NWC', 'WIO', 'NWC'), feature_group_count=ch)
    return y + b.astype(x.dtype), xp[:, xp.shape[1] - (width - 1):]


def s5_mixer(u, h0_re, h0_im, lp):
    f32 = jnp.float32
    lam = lax.complex(lp['s5_a_re'].astype(f32), lp['s5_a_im'].astype(f32))
    dt = jnp.exp(lp['s5_log_dt'].astype(f32))[:, None]
    a_bar = jnp.exp(lam * dt)
    b_bar = ((a_bar - 1.0) / lam)[..., None] * lax.complex(lp['s5_b_re'].astype(f32), lp['s5_b_im'].astype(f32))
    c = lax.complex(lp['s5_c_re'].astype(f32), lp['s5_c_im'].astype(f32))
    bsz, t, _ = u.shape
    uf = u.astype(f32)
    bu = jnp.einsum('btgc,gpc->btgp', uf.reshape(bsz, t, S5_GROUPS, S5_CH).astype(jnp.complex64), b_bar)
    h = linear_scan(jnp.broadcast_to(a_bar, bu.shape), bu, lax.complex(h0_re.astype(f32), h0_im.astype(f32)))
    y = jnp.real(jnp.einsum('btgp,gcp->btgc', h, c)).reshape(bsz, t, S5_WIDTH) + lp['s5_d'].astype(f32) * uf
    g = jax.nn.gelu(y)
    z1, z2 = jnp.split(g @ lp['s5_w_glu'].astype(f32) + lp['s5_b_glu'].astype(f32), 2, axis=-1)
    out = z1 * jax.nn.sigmoid(z2)
    h_last = h[:, -1]
    return out.astype(u.dtype), jnp.real(h_last).astype(h0_re.dtype), jnp.imag(h_last).astype(h0_im.dtype)


def rglru_mixer(x_in, gate_in, h0, conv_buf, lp):
    f32 = jnp.float32
    xc, new_buf = causal_dwconv(x_in, conv_buf, lp['lru_conv_w'], lp['lru_conv_b'])
    bsz, t, _ = xc.shape
    xh = xc.reshape(bsz, t, LRU_HEADS, LRU_HD)
    def block_diag(w, b):
        return jnp.einsum('bthi,hij->bthj', xh, w).reshape(bsz, t, LRU_WIDTH) + b
    r = jax.nn.sigmoid(block_diag(lp['lru_w_a'], lp['lru_b_a']).astype(f32))
    i = jax.nn.sigmoid(block_diag(lp['lru_w_x'], lp['lru_b_x']).astype(f32))
    log_a = -LRU_C * r * jax.nn.softplus(-lp['lru_lambda'].astype(f32))
    a = jnp.exp(log_a)
    b = jnp.sqrt(-jnp.expm1(2.0 * log_a)) * (i * xc.astype(f32))
    h = linear_scan(a, b, h0.astype(f32))
    out = h * jax.nn.gelu(gate_in.astype(f32))
    return out.astype(x_in.dtype), h[:, -1].astype(h0.dtype), new_buf


def conformer_conv(z, buf, lp):
    a, g = jnp.split(z, 2, axis=-1)
    u = a * jax.nn.sigmoid(g)
    c, new_buf = causal_dwconv(u, buf, lp['cc_dw_w'], lp['cc_dw_b'])
    c = layer_norm(c, lp['cc_ln_g'], lp['cc_ln_b'])
    return jax.nn.silu(c), new_buf


def topk_count(n_keys):
    return min(TOPK_MAX, n_keys // 4)


def indexer_scores(qi, wi, kidx):
    s = jnp.einsum('bthd,bsd->bths', qi, kidx, preferred_element_type=jnp.float32)
    return jnp.einsum('bths,bth->bts', jax.nn.relu(s), wi.astype(jnp.float32)) * IDX_SCALE


def select_keys(scores, qpos, topk):
    kpos = jnp.arange(scores.shape[-1], dtype=jnp.int32)
    visible = kpos[None, :] <= qpos[:, None]
    scores = jnp.where(visible[None], scores, -jnp.inf)
    _, idx = lax.top_k(scores, topk)
    return idx, idx <= qpos[None, :, None]


def sparse_attend(q, k_sel, v_sel, valid):
    s = jnp.einsum('bthd,btkhd->bthk', q, k_sel, preferred_element_type=jnp.float32) * ATT_SCALE
    s = jnp.where(valid[:, :, None, :], s, -jnp.inf)
    p = jax.nn.softmax(s, axis=-1)
    return jnp.einsum('bthk,btkhd->bthd', p.astype(v_sel.dtype), v_sel)


_gather_rows = jax.vmap(lambda arr, ix: arr[ix])


def dsa_prompt(q, k, v, qi, wi, kidx):
    bsz, t = q.shape[0], q.shape[1]
    topk = topk_count(t)
    nb = t // Q_BLOCK
    def blockify(a):
        return jnp.moveaxis(a.reshape((bsz, nb, Q_BLOCK) + a.shape[2:]), 1, 0)
    def one(args):
        qb, qib, wib, start = args
        qpos = start + jnp.arange(Q_BLOCK, dtype=jnp.int32)
        idx, valid = select_keys(indexer_scores(qib, wib, kidx), qpos, topk)
        return sparse_attend(qb, _gather_rows(k, idx), _gather_rows(v, idx), valid)
    starts = jnp.arange(nb, dtype=jnp.int32) * Q_BLOCK
    out = lax.map(one, (blockify(q), blockify(qi), blockify(wi), starts))
    return jnp.moveaxis(out, 0, 1).reshape(bsz, t, ATT_WIDTH)


def dsa_sample(q, k, v, qi, wi, kidx, pool_k, pool_v, pool_kidx, page_table):
    db, t = q.shape[0], q.shape[1]
    past = page_table.shape[1] * PAGE_SIZE
    kidx_past = pool_kidx[page_table].reshape(db, past, IDX_DIM).astype(kidx.dtype)
    kidx_all = jnp.concatenate([kidx_past, kidx], axis=1)
    qpos = past + jnp.arange(t, dtype=jnp.int32)
    idx, valid = select_keys(indexer_scores(qi, wi, kidx_all), qpos, topk_count(past + t))
    in_past = idx < past
    pidx = jnp.minimum(idx, past - 1)
    phys = jnp.take_along_axis(page_table, (pidx // PAGE_SIZE).reshape(db, -1), axis=1).reshape(idx.shape)
    off = pidx % PAGE_SIZE
    nidx = jnp.clip(idx - past, 0, t - 1)
    def sel(pool, new):
        return jnp.where(in_past[..., None, None], pool[phys, off].astype(new.dtype), _gather_rows(new, nidx))
    return sparse_attend(q, sel(pool_k, k), sel(pool_v, v), valid).reshape(db, t, ATT_WIDTH)


def swiglu(x, w_gate, w_up, w_down):
    return (jax.nn.silu(x @ w_gate) * (x @ w_up)) @ w_down


def moe_swiglu(x, w_router, b_router, w_gate, w_up, w_down):
    logits = (x @ w_router).astype(jnp.float32) + b_router.astype(jnp.float32)
    top_val, top_idx = lax.top_k(logits, TOP_K)
    gate = jax.nn.softmax(top_val, axis=-1)
    comb = jnp.einsum('...k,...ke->...e', gate, jax.nn.one_hot(top_idx, N_EXPERTS, dtype=jnp.float32)).astype(x.dtype)
    out = jnp.zeros_like(x)
    for e in range(N_EXPERTS):
        out = out + comb[..., e:e + 1] * swiglu(x, w_gate[e], w_up[e], w_down[e])
    return out


def decoder_layer(x, pos, lp, channel_fn, s5_re0, s5_im0, lru_h0, lru_buf0, cc_buf0, attend):
    bsz, t, _ = x.shape
    z = x @ lp['w_in']
    u_s5, x_lru, g_lru, z_cc, q, k, v, qi, kidx, wi = jnp.split(z, IN_SPLITS, axis=-1)
    y_s5, s5_re, s5_im = s5_mixer(u_s5, s5_re0, s5_im0, lp)
    y_lru, lru_h, lru_buf = rglru_mixer(x_lru, g_lru, lru_h0, lru_buf0, lp)
    y_cc, cc_buf = conformer_conv(z_cc, cc_buf0, lp)
    q = rope(q.reshape(bsz, t, ATT_HEADS, HEAD_DIM), pos)
    k = rope(k.reshape(bsz, t, ATT_HEADS, HEAD_DIM), pos)
    v = v.reshape(bsz, t, ATT_HEADS, HEAD_DIM)
    qi = rope(qi.reshape(bsz, t, IDX_HEADS, IDX_DIM), pos)
    kidx = rope(kidx, pos)
    y_att = attend(q, k, v, qi, wi, kidx)
    y = jnp.concatenate([y_s5, y_lru, y_cc, y_att], axis=-1) @ lp['w_out']
    h = layer_norm(DN_ALPHA * x + y, lp['ln1_g'], lp['ln1_b'])
    out = layer_norm(DN_ALPHA * h + channel_fn(h), lp['ln2_g'], lp['ln2_b'])
    return out, (k, v, kidx, s5_re, s5_im, lru_h, lru_buf, cc_buf)


def setup_inputs(seed: int = 0) -> dict:
    f32 = jnp.float32
    key = jax.random.key(seed)
    keys = iter(jax.random.split(key, 64))
    def nrm(shape, scale=1.0):
        return scale * jax.random.normal(next(keys), shape, f32)
    def uni(shape, lo, hi):
        return jax.random.uniform(next(keys), shape, f32, lo, hi)
    n_pages = PAST_LEN // PAGE_SIZE
    n_used = DEC_BATCH * n_pages
    n_pool = n_used + max(1, n_used // 4)
    page_table = jax.random.permutation(next(keys), n_pool)[:n_used].reshape(DEC_BATCH, n_pages).astype(jnp.int32)
    lam_sig = uni((DEPTH, LRU_WIDTH), 0.9, 0.999) ** (1.0 / LRU_C)
    s5_shape = (DEPTH, S5_GROUPS, S5_STATE)
    return {
        'x_prompt': nrm((BATCH, SEQ, D_MODEL)),
        'x_sample': nrm((DEC_BATCH, DEC_SEQ, D_MODEL)),
        'cache_k': nrm((DEPTH, n_pool, PAGE_SIZE, ATT_HEADS, HEAD_DIM)),
        'cache_v': nrm((DEPTH, n_pool, PAGE_SIZE, ATT_HEADS, HEAD_DIM)),
        'cache_kidx': nrm((DEPTH, n_pool, PAGE_SIZE, IDX_DIM)),
        'state_s5_re': nrm((DEPTH, DEC_BATCH, S5_GROUPS, S5_STATE), 0.5),
        'state_s5_im': nrm((DEPTH, DEC_BATCH, S5_GROUPS, S5_STATE), 0.5),
        'state_lru_h': nrm((DEPTH, DEC_BATCH, LRU_WIDTH), 0.5),
        'state_lru_conv': nrm((DEPTH, DEC_BATCH, LRU_CONV - 1, LRU_WIDTH)),
        'state_cc_conv': nrm((DEPTH, DEC_BATCH, CC_CONV - 1, CC_WIDTH)),
        'page_table': page_table,
        'w_in': nrm((DEPTH, D_MODEL, D_IN), D_MODEL ** -0.5),
        's5_a_re': -0.5 + nrm(s5_shape, 0.01),
        's5_a_im': math.pi * jnp.broadcast_to(jnp.arange(S5_STATE, dtype=f32), s5_shape) + nrm(s5_shape, 0.01),
        's5_b_re': nrm((DEPTH, S5_GROUPS, S5_STATE, S5_CH), S5_CH ** -0.5),
        's5_b_im': nrm((DEPTH, S5_GROUPS, S5_STATE, S5_CH), S5_CH ** -0.5),
        's5_c_re': nrm((DEPTH, S5_GROUPS, S5_CH, S5_STATE), S5_STATE ** -0.5),
        's5_c_im': nrm((DEPTH, S5_GROUPS, S5_CH, S5_STATE), S5_STATE ** -0.5),
        's5_d': nrm((DEPTH, S5_WIDTH)),
        's5_log_dt': uni((DEPTH, S5_GROUPS), math.log(S5_DT_MIN), math.log(S5_DT_MAX)),
        's5_w_glu': nrm((DEPTH, S5_WIDTH, 2 * S5_WIDTH), S5_WIDTH ** -0.5),
        's5_b_glu': nrm((DEPTH, 2 * S5_WIDTH), 0.01),
        'lru_conv_w': nrm((DEPTH, LRU_CONV, LRU_WIDTH), LRU_CONV ** -0.5),
        'lru_conv_b': nrm((DEPTH, LRU_WIDTH), 0.01),
        'lru_w_a': nrm((DEPTH, LRU_HEADS, LRU_HD, LRU_HD), LRU_HD ** -0.5),
        'lru_b_a': nrm((DEPTH, LRU_WIDTH), 0.01),
        'lru_w_x': nrm((DEPTH, LRU_HEADS, LRU_HD, LRU_HD), LRU_HD ** -0.5),
        'lru_b_x': nrm((DEPTH, LRU_WIDTH), 0.01),
        'lru_lambda': jnp.log(lam_sig) - jnp.log1p(-lam_sig),
        'cc_dw_w': nrm((DEPTH, CC_CONV, CC_WIDTH), CC_CONV ** -0.5),
        'cc_dw_b': nrm((DEPTH, CC_WIDTH), 0.01),
        'cc_ln_g': 1.0 + nrm((DEPTH, CC_WIDTH), 0.01),
        'cc_ln_b': nrm((DEPTH, CC_WIDTH), 0.01),
        'w_out': nrm((DEPTH, MIX_WIDTH, D_MODEL), DN_BETA * MIX_WIDTH ** -0.5),
        'ln1_g': 1.0 + nrm((DEPTH, D_MODEL), 0.01),
        'ln1_b': nrm((DEPTH, D_MODEL), 0.01),
        'ln2_g': 1.0 + nrm((DEPTH, D_MODEL), 0.01),
        'ln2_b': nrm((DEPTH, D_MODEL), 0.01),
        'ffn_w_gate': nrm((N_DENSE, D_MODEL, D_FF), D_MODEL ** -0.5),
        'ffn_w_up': nrm((N_DENSE, D_MODEL, D_FF), D_MODEL ** -0.5),
        'ffn_w_down': nrm((N_DENSE, D_FF, D_MODEL), DN_BETA * D_FF ** -0.5),
        'moe_w_router': nrm((N_MOE, D_MODEL, N_EXPERTS), D_MODEL ** -0.5),
        'moe_b_router': nrm((N_MOE, N_EXPERTS), 0.01),
        'moe_w_gate': nrm((N_MOE, N_EXPERTS, D_MODEL, D_FF_EXPERT), D_MODEL ** -0.5),
        'moe_w_up': nrm((N_MOE, N_EXPERTS, D_MODEL, D_FF_EXPERT), D_MODEL ** -0.5),
        'moe_w_down': nrm((N_MOE, N_EXPERTS, D_FF_EXPERT, D_MODEL), DN_BETA * D_FF_EXPERT ** -0.5),
    }


def reference(x_prompt, x_sample, cache_k, cache_v, cache_kidx, state_s5_re, state_s5_im, state_lru_h,
              state_lru_conv, state_cc_conv, page_table, w_in, s5_a_re, s5_a_im, s5_b_re, s5_b_im, s5_c_re,
              s5_c_im, s5_d, s5_log_dt, s5_w_glu, s5_b_glu, lru_conv_w, lru_conv_b, lru_w_a, lru_b_a, lru_w_x,
              lru_b_x, lru_lambda, cc_dw_w, cc_dw_b, cc_ln_g, cc_ln_b, w_out, ln1_g, ln1_b, ln2_g, ln2_b,
              ffn_w_gate, ffn_w_up, ffn_w_down, moe_w_router, moe_b_router, moe_w_gate, moe_w_up, moe_w_down):
    bp, tp = x_prompt.shape[0], x_prompt.shape[1]
    ts = x_sample.shape[1]
    past = page_table.shape[1] * PAGE_SIZE
    pos_p = jnp.arange(tp, dtype=jnp.int32)
    pos_s = past + jnp.arange(ts, dtype=jnp.int32)
    dt_p = x_prompt.dtype
    xp, xs = x_prompt, x_sample
    states_p, states_s = [], []
    for l in range(DEPTH):
        lp = dict(w_in=w_in[l], s5_a_re=s5_a_re[l], s5_a_im=s5_a_im[l], s5_b_re=s5_b_re[l], s5_b_im=s5_b_im[l],
                  s5_c_re=s5_c_re[l], s5_c_im=s5_c_im[l], s5_d=s5_d[l], s5_log_dt=s5_log_dt[l],
                  s5_w_glu=s5_w_glu[l], s5_b_glu=s5_b_glu[l], lru_conv_w=lru_conv_w[l], lru_conv_b=lru_conv_b[l],
                  lru_w_a=lru_w_a[l], lru_b_a=lru_b_a[l], lru_w_x=lru_w_x[l], lru_b_x=lru_b_x[l],
                  lru_lambda=lru_lambda[l], cc_dw_w=cc_dw_w[l], cc_dw_b=cc_dw_b[l], cc_ln_g=cc_ln_g[l],
                  cc_ln_b=cc_ln_b[l], w_out=w_out[l], ln1_g=ln1_g[l], ln1_b=ln1_b[l], ln2_g=ln2_g[l], ln2_b=ln2_b[l])
        j = l // 2
        if l % 2 == 0:
            channel_fn = functools.partial(swiglu, w_gate=ffn_w_gate[j], w_up=ffn_w_up[j], w_down=ffn_w_down[j])
        else:
            channel_fn = functools.partial(moe_swiglu, w_router=moe_w_router[j], b_router=moe_b_router[j],
                                           w_gate=moe_w_gate[j], w_up=moe_w_up[j], w_down=moe_w_down[j])
        xp, st_p = decoder_layer(
            xp, pos_p, lp, channel_fn,
            jnp.zeros((bp, S5_GROUPS, S5_STATE), dt_p), jnp.zeros((bp, S5_GROUPS, S5_STATE), dt_p),
            jnp.zeros((bp, LRU_WIDTH), dt_p), jnp.zeros((bp, LRU_CONV - 1, LRU_WIDTH), dt_p),
            jnp.zeros((bp, CC_CONV - 1, CC_WIDTH), dt_p), dsa_prompt)
        attend_s = functools.partial(dsa_sample, pool_k=cache_k[l], pool_v=cache_v[l],
                                     pool_kidx=cache_kidx[l], page_table=page_table)
        xs, st_s = decoder_layer(xs, pos_s, lp, channel_fn, state_s5_re[l], state_s5_im[l], state_lru_h[l],
                                 state_lru_conv[l], state_cc_conv[l], attend_s)
        states_p.append(st_p)
        states_s.append(st_s)
    new_p = [jnp.stack(col) for col in zip(*states_p)]
    new_s = [jnp.stack(col) for col in zip(*states_s)]
    return (xp, xs, new_p[0], new_s[0], new_p[1], new_s[1], new_p[2], new_s[2], new_p[3], new_s[3],
            new_p[4], new_s[4], new_p[5], new_s[5], new_p[6], new_s[6], new_p[7], new_s[7])
```

```python
import functools
import math

import jax
import jax.numpy as jnp
from jax import lax
from jax.experimental import pallas as pl
from jax.experimental.pallas import tpu as pltpu

F32 = jnp.float32
BF16 = jnp.bfloat16

S5_CH = 16
LRU_C = 8.0
HEAD_DIM = 64
IDX_DIM = 64
IDX_HEADS = 4
TOPK_MAX = 256
Q_BLOCK = 128
ROPE_THETA = 10000.0
MOE_TOP_K = 2
LN_EPS = 1e-5
ATT_SCALE = HEAD_DIM ** -0.5
IDX_SCALE = (IDX_DIM * IDX_HEADS) ** -0.5

LANES = 128
NEG_BIG = -1e30
VMEM_LIMIT = 56 * 1024 * 1024


def _params(*sem):
    return pltpu.CompilerParams(dimension_semantics=sem, vmem_limit_bytes=VMEM_LIMIT)


def _dot(a, b):
    return jnp.dot(a, b, preferred_element_type=F32)


def _dot_t(a, b):
    return lax.dot_general(a, b, (((1,), (1,)), ((), ())), preferred_element_type=F32)


def _split3(x):
    hi = x.astype(BF16)
    lo = (x - hi.astype(F32)).astype(BF16)
    return hi, lo


def _layer_norm(x, g, b):
    mu = jnp.mean(x, axis=-1, keepdims=True)
    xc = x - mu
    var = jnp.mean(xc * xc, axis=-1, keepdims=True)
    return xc * lax.rsqrt(var + LN_EPS) * g + b


def _rope(z, cos, sin):
    w = z.shape[-1]
    half = HEAD_DIM // 2
    lane = lax.broadcasted_iota(jnp.int32, z.shape, 1)
    first = (lane & (HEAD_DIM - 1)) < half
    partner = jnp.where(first, pltpu.roll(z, w - half, axis=1), pltpu.roll(z, half, axis=1))
    return z * cos + partner * sin


def _in_proj_kernel(x_ref, w_ref, cos_ref, sin_ref, zmix_ref, q_ref, qi_ref, k_ref, v_ref, kw_ref,
                    *, mix_w, gw):
    xb = x_ref[...].astype(BF16)
    cos = cos_ref[...]
    sin = sin_ref[...]

    def proj(lo, width):
        return _dot(xb, w_ref[:, lo:lo + width])

    zmix_ref[...] = proj(0, mix_w)
    o = mix_w
    q_ref[...] = (_rope(proj(o, gw), cos, sin) * ATT_SCALE).astype(BF16)
    k_ref[...] = _rope(proj(o + gw, gw), cos, sin)
    v_ref[...] = proj(o + 2 * gw, gw)
    qi_ref[...] = _rope(proj(o + 3 * gw, gw), cos, sin)
    kw = proj(o + 4 * gw, LANES)
    is_key = lax.broadcasted_iota(jnp.int32, kw.shape, 1) < IDX_DIM
    kw_ref[...] = _rope(kw, jnp.where(is_key, cos[:, :LANES], 1.0), jnp.where(is_key, sin[:, :LANES], 0.0))


def _in_proj(x2, w_pad, cos_t, sin_t, tm, mix_w, gw):
    m, d = x2.shape
    tab_blocks = cos_t.shape[0] // tm
    row = lambda i: (i, 0)
    tab = lambda i: (i % tab_blocks, 0)
    outs = [
        jax.ShapeDtypeStruct((m, mix_w), F32),
        jax.ShapeDtypeStruct((m, gw), BF16),
        jax.ShapeDtypeStruct((m, gw), F32),
        jax.ShapeDtypeStruct((m, gw), F32),
        jax.ShapeDtypeStruct((m, gw), F32),
        jax.ShapeDtypeStruct((m, LANES), F32),
    ]
    return pl.pallas_call(
        functools.partial(_in_proj_kernel, mix_w=mix_w, gw=gw),
        grid=(m // tm,),
        in_specs=[pl.BlockSpec((tm, d), row),
                  pl.BlockSpec(w_pad.shape, lambda i: (0, 0)),
                  pl.BlockSpec((tm, gw), tab),
                  pl.BlockSpec((tm, gw), tab)],
        out_specs=[pl.BlockSpec((tm, s.shape[1]), row) for s in outs],
        out_shape=outs,
        compiler_params=_params("parallel"),
        name="in_proj",
    )(x2, w_pad, cos_t, sin_t)


LRU_HALO = 8
CC_HALO = 32


def _convmix_kernel(xl_ref, ca_ref, cg_ref, lbuf_ref, cbuf_ref, lw_ref, lb_ref, wg_ref, bg_ref, sp_ref,
                    cw_ref, cb_ref, lng_ref, lnb_ref,
                    a_ref, b_ref, ycc_ref, lbuf_o, cbuf_o, xl_s, xc_s, *, tt):
    i = pl.program_id(1)
    kl = lw_ref.shape[0]
    kc = cw_ref.shape[0]
    w = xl_ref.shape[-1]
    l0 = LRU_HALO - (kl - 1)
    c0 = CC_HALO - (kc - 1)

    @pl.when(i == 0)
    def _():
        xl_s[l0:LRU_HALO, :] = lbuf_ref[...]
        xc_s[c0:CC_HALO, :] = cbuf_ref[...]

    xl_s[LRU_HALO:LRU_HALO + tt, :] = xl_ref[...]
    xc_s[CC_HALO:CC_HALO + tt, :] = ca_ref[...] * jax.nn.sigmoid(cg_ref[...])

    conv = xl_s[l0:l0 + tt, :] * lw_ref[0:1, :]
    for j in range(1, kl):
        conv = conv + xl_s[l0 + j:l0 + j + tt, :] * lw_ref[j:j + 1, :]
    conv = conv + lb_ref[...]
    gates = _dot(conv.astype(BF16), wg_ref[...]) + bg_ref[...]
    r = jax.nn.sigmoid(gates[:, :w])
    ig = jax.nn.sigmoid(gates[:, w:])
    log_a = (-LRU_C) * r * sp_ref[...]
    a = jnp.exp(log_a)
    a_ref[...] = a
    b_ref[...] = jnp.sqrt(-jnp.tanh(log_a) * (a * a + 1.0)) * (ig * conv)

    c = xc_s[c0:c0 + tt, :] * cw_ref[0:1, :]
    for j in range(1, kc):
        c = c + xc_s[c0 + j:c0 + j + tt, :] * cw_ref[j:j + 1, :]
    c = _layer_norm(c + cb_ref[...], lng_ref[...], lnb_ref[...])
    ycc_ref[...] = c * jax.nn.sigmoid(c)

    new_l = xl_s[l0 + tt:LRU_HALO + tt, :]
    new_c = xc_s[c0 + tt:CC_HALO + tt, :]
    xl_s[l0:LRU_HALO, :] = new_l
    xc_s[c0:CC_HALO, :] = new_c
    lbuf_o[...] = new_l
    cbuf_o[...] = new_c


def _convmix(zmix3, lbuf, cbuf, lw, lb, wg, bg, sp, cw, cb, lng, lnb, tt):
    nb, t, _ = zmix3.shape
    w = lw.shape[1]
    col = lambda c: pl.BlockSpec((None, tt, w), lambda b, i, c=c: (b, i, c))
    full2 = lambda a: pl.BlockSpec(a.shape, lambda b, i: (0, 0))
    per_b = lambda a: pl.BlockSpec((None,) + a.shape[1:], lambda b, i: (b, 0, 0))
    seq_out = jax.ShapeDtypeStruct((nb, t, w), F32)
    return pl.pallas_call(
        functools.partial(_convmix_kernel, tt=tt),
        grid=(nb, t // tt),
        in_specs=[col(1), col(3), col(4), per_b(lbuf), per_b(cbuf), full2(lw), full2(lb), full2(wg),
                  full2(bg), full2(sp), full2(cw), full2(cb), full2(lng), full2(lnb)],
        out_specs=[pl.BlockSpec((None, tt, w), lambda b, i: (b, i, 0))] * 3 + [per_b(lbuf), per_b(cbuf)],
        out_shape=[seq_out, seq_out, seq_out,
                   jax.ShapeDtypeStruct(lbuf.shape, F32), jax.ShapeDtypeStruct(cbuf.shape, F32)],
        scratch_shapes=[pltpu.VMEM((LRU_HALO + tt, w), F32), pltpu.VMEM((CC_HALO + tt, w), F32)],
        compiler_params=_params("parallel", "arbitrary"),
        name="convmix",
    )(zmix3, zmix3, zmix3, lbuf, cbuf, lw, lb, wg, bg, sp, cw, cb, lng, lnb)


def _lru_scan_kernel(a_ref, b_ref, h0_ref, h_ref, hl_ref, e_s, p_s, hi_s, *, chained):
    nl, nr, _ = a_ref.shape

    def run(init):
        def step(t, h):
            h = a_ref[t] * h + b_ref[t]
            h_ref[t] = h
            return h
        return lax.fori_loop(0, nl, step, init)

    if not chained:
        hl_ref[...] = run(h0_ref[...])
        return

    def local(t, c):
        h, p = c
        at = a_ref[t]
        return at * h + b_ref[t], at * p
    zeros = jnp.zeros(e_s.shape, F32)
    e, p = lax.fori_loop(0, nl, local, (zeros, zeros + 1.0))
    e_s[...] = e
    p_s[...] = p
    hi_s[0:1, :] = h0_ref[...]

    def carry(c, _):
        prev = pl.ds(c - 1, 1)
        hi_s[pl.ds(c, 1), :] = p_s[prev, :] * hi_s[prev, :] + e_s[prev, :]
        return 0
    lax.fori_loop(1, nr, carry, 0)
    last = run(hi_s[...])
    hl_ref[...] = last[nr - 1:nr, :]


def _lru_scan(a4, b4, h0, chained):
    nb, nl, nr, w = a4.shape
    r0 = h0.shape[1]
    nw = w // LANES
    blk = pl.BlockSpec((None, nl, nr, LANES), lambda n, j: (n, 0, 0, j))
    st = pl.BlockSpec((None, r0, LANES), lambda n, j: (n, 0, j))
    return pl.pallas_call(
        functools.partial(_lru_scan_kernel, chained=chained),
        grid=(nb, nw),
        in_specs=[blk, blk, st],
        out_specs=[blk, st],
        out_shape=[jax.ShapeDtypeStruct(a4.shape, F32), jax.ShapeDtypeStruct(h0.shape, F32)],
        scratch_shapes=[pltpu.VMEM((nr, LANES), F32)] * 3,
        compiler_params=_params("parallel", "parallel"),
        name="lru_scan",
    )(a4, b4, h0)


def _s5_scan_kernel(u_ref, bre_ref, bim_ref, are_ref, aim_ref, cc_ref, h0r_ref, h0i_ref,
                    ys_ref, hlr_ref, hli_ref, br_s, bi_s, er_s, ei_s, hr_s, hi_s, *, chained, rows_per_dot):
    g = pl.program_id(1)
    nl, nr, _ = u_ref.shape
    n = nl * nr
    ar = are_ref[...]
    ai = aim_ref[...]

    def drive(c, _):
        r0 = pl.multiple_of(c * rows_per_dot, rows_per_dot)
        ub = u_ref[pl.ds(c * (rows_per_dot // nr), rows_per_dot // nr)].reshape(rows_per_dot, u_ref.shape[-1])
        br_s[pl.ds(r0, rows_per_dot), :] = _dot(ub, bre_ref[...])
        bi_s[pl.ds(r0, rows_per_dot), :] = _dot(ub, bim_ref[...])
        return 0
    lax.fori_loop(0, n // rows_per_dot, drive, 0)

    def step_fn(store):
        def step(t, c):
            hr, hi = c
            rows = pl.ds(pl.multiple_of(t * nr, nr), nr)
            nhr = ar * hr - ai * hi + br_s[rows, :]
            nhi = ar * hi + ai * hr + bi_s[rows, :]
            if store:
                br_s[rows, :] = nhr
                bi_s[rows, :] = nhi
            return nhr, nhi
        return step

    if chained:
        zeros = jnp.zeros((nr, LANES), F32)
        er, ei = lax.fori_loop(0, nl, step_fn(False), (zeros, zeros))
        er_s[...] = er
        ei_s[...] = ei

        def power(_, c):
            pr, pi = c
            return ar * pr - ai * pi, ar * pi + ai * pr
        alr, ali = lax.fori_loop(0, nl - 1, power, (ar, ai))
        hr_s[0:1, :] = h0r_ref[...]
        hi_s[0:1, :] = h0i_ref[...]

        def carry(c, _):
            prev = pl.ds(c - 1, 1)
            pr = hr_s[prev, :]
            pi = hi_s[prev, :]
            hr_s[pl.ds(c, 1), :] = alr * pr - ali * pi + er_s[prev, :]
            hi_s[pl.ds(c, 1), :] = alr * pi + ali * pr + ei_s[prev, :]
            return 0
        lax.fori_loop(1, nr, carry, 0)
        init = (hr_s[...], hi_s[...])
    else:
        init = (h0r_ref[...], h0i_ref[...])

    lr, li = lax.fori_loop(0, nl, step_fn(True), init)
    if chained:
        hlr_ref[...] = lr[nr - 1:nr, :]
        hli_ref[...] = li[nr - 1:nr, :]
    else:
        hlr_ref[...] = lr
        hli_ref[...] = li

    def project(c, _):
        r0 = pl.multiple_of(c * rows_per_dot, rows_per_dot)
        rows = pl.ds(r0, rows_per_dot)
        hcat = jnp.concatenate([br_s[rows, :], bi_s[rows, :]], axis=1).astype(BF16)
        y = _dot(hcat, cc_ref[...]).reshape(rows_per_dot // nr, nr, ys_ref.shape[-1])
        tsl = pl.ds(c * (rows_per_dot // nr), rows_per_dot // nr)

        @pl.when(g == 0)
        def _():
            ys_ref[tsl] = y

        @pl.when(g > 0)
        def _():
            ys_ref[tsl] = ys_ref[tsl] + y
        return 0
    lax.fori_loop(0, n // rows_per_dot, project, 0)


def _s5_scan(u4, bre, bim, are, aim, ccat, h0r, h0i, chained):
    nb, nl, nr, w = u4.shape
    ng = bre.shape[0]
    r0 = h0r.shape[1]
    n = nl * nr
    rows_per_dot = math.gcd(n, 1024)
    rows_per_dot = max(rows_per_dot, nr)
    seq = pl.BlockSpec((None, nl, nr, w), lambda b, g: (b, 0, 0, 0))
    per_g = lambda a: pl.BlockSpec((None,) + a.shape[1:], lambda b, g: (g, 0, 0))
    st = pl.BlockSpec((None, r0, LANES), lambda b, g: (b, 0, g))
    small = pltpu.VMEM((nr, LANES), F32)
    return pl.pallas_call(
        functools.partial(_s5_scan_kernel, chained=chained, rows_per_dot=rows_per_dot),
        grid=(nb, ng),
        in_specs=[seq, per_g(bre), per_g(bim), per_g(are), per_g(aim), per_g(ccat), st, st],
        out_specs=[seq, st, st],
        out_shape=[jax.ShapeDtypeStruct(u4.shape, F32), jax.ShapeDtypeStruct(h0r.shape, F32),
                   jax.ShapeDtypeStruct(h0i.shape, F32)],
        scratch_shapes=[pltpu.VMEM((n, LANES), F32), pltpu.VMEM((n, LANES), F32), small, small, small, small],
        compiler_params=_params("parallel", "arbitrary"),
        name="s5_scan",
    )(u4, bre, bim, are, aim, ccat, h0r, h0i)


SELECT_MAX_ITERS = 200


def _bisect_state(lo0, kq, all_sel):
    inf = jnp.full(lo0.shape, jnp.inf, F32)
    zero = jnp.zeros(lo0.shape, F32)
    return lo0, inf, lo0, jnp.where(all_sel, 1.0, zero), zero


def _bisect_mid(lo, hi, mx):
    return jnp.where(hi == jnp.inf, mx, 0.5 * lo + 0.5 * hi)


def _bisect_update(state, mid, cnt, kq):
    lo, hi, theta, done, tie = state
    live = done == 0.0
    collapsed = jnp.logical_and(live, jnp.logical_or(mid == lo, mid == hi))
    hit = jnp.logical_and(live, cnt == kq)
    go_up = jnp.logical_and(live, jnp.logical_and(cnt > kq, jnp.logical_not(collapsed)))
    go_dn = jnp.logical_and(live, jnp.logical_and(cnt < kq, jnp.logical_not(collapsed)))
    theta = jnp.where(hit, mid, jnp.where(collapsed, hi, theta))
    tie = jnp.where(collapsed, 1.0, tie)
    done = jnp.where(jnp.logical_or(hit, collapsed), 1.0, done)
    lo = jnp.where(go_up, mid, lo)
    hi = jnp.where(go_dn, mid, hi)
    return lo, hi, theta, done, tie


def _dsa_prompt_kernel(qt_ref, qit_ref, wit_ref, kidx_ref, k_ref, vt_ref, o_ref, s_scr, *, topk, kc):
    i = pl.program_id(1)
    qb = Q_BLOCK
    n_heads = qt_ref.shape[0] // HEAD_DIM
    nch = ((i + 1) * qb + kc - 1) // kc
    qpos = i * qb + lax.broadcasted_iota(jnp.int32, (1, qb), 1)
    kiota = lax.broadcasted_iota(jnp.int32, (kc, qb), 0)
    ipad = qit_ref.shape[0] // IDX_HEADS

    def score_chunk(c, carry):
        mn, mx = carry
        r0 = pl.multiple_of(c * kc, kc)
        kk = kidx_ref[pl.ds(r0, kc), :]
        s = jnp.zeros((kc, qb), F32)
        for h in range(IDX_HEADS):
            sh = _dot(kk, qit_ref[h * ipad:(h + 1) * ipad, :])
            s = s + jnp.maximum(sh, 0.0) * wit_ref[h:h + 1, :]
        s = s * IDX_SCALE
        vis = (r0 + kiota) <= qpos
        s_scr[pl.ds(r0, kc), :] = jnp.where(vis, s, -jnp.inf)
        mn = jnp.minimum(mn, jnp.min(jnp.where(vis, s, jnp.inf), axis=0, keepdims=True))
        mx = jnp.maximum(mx, jnp.max(jnp.where(vis, s, -jnp.inf), axis=0, keepdims=True))
        return mn, mx
    mn, mx = lax.fori_loop(0, nch, score_chunk,
                           (jnp.full((1, qb), jnp.inf, F32), jnp.full((1, qb), -jnp.inf, F32)))

    def count(pred_fn):
        def body(c, acc):
            r0 = pl.multiple_of(c * kc, kc)
            hit = jnp.where(pred_fn(s_scr[pl.ds(r0, kc), :], r0 + kiota), 1.0, 0.0)
            return acc + jnp.sum(hit.reshape(kc // 8, 8, qb), axis=0)
        acc = lax.fori_loop(0, nch, body, jnp.zeros((8, qb), F32))
        return jnp.sum(acc, axis=0, keepdims=True)

    n_vis = (qpos + 1).astype(F32)
    kq = jnp.minimum(n_vis, float(topk))
    state = _bisect_state(mn, kq, n_vis <= float(topk))

    def cond(c):
        it, st = c
        return jnp.logical_and(it < SELECT_MAX_ITERS, jnp.min(st[3]) == 0.0)

    def body(c):
        it, st = c
        mid = _bisect_mid(st[0], st[1], mx)
        cnt = count(lambda s, kpos: s >= mid)
        return it + 1, _bisect_update(st, mid, cnt, kq)
    _, (lo, hi, theta, done, tie) = lax.while_loop(cond, body, (jnp.int32(0), state))

    @pl.when(jnp.max(tie) > 0.0)
    def _():
        need = kq - count(lambda s, kpos: s >= theta)
        tied = lambda s: jnp.logical_and(tie > 0.0, s == lo)
        nbits = max(1, (s_scr.shape[0] - 1).bit_length())

        def jstep(b, j):
            cand = j + (jnp.int32(1) << (nbits - 1 - b))
            c_lt = count(lambda s, kpos: jnp.logical_and(tied(s), kpos < cand))
            return jnp.where(c_lt < need, cand, j)
        jlast = lax.fori_loop(0, nbits, jstep, jnp.zeros((1, qb), jnp.int32))

        def promote(c, _):
            r0 = pl.multiple_of(c * kc, kc)
            s = s_scr[pl.ds(r0, kc), :]
            keep = jnp.logical_and(tied(s), (r0 + kiota) <= jlast)
            s_scr[pl.ds(r0, kc), :] = jnp.where(keep, theta, s)
            return 0
        lax.fori_loop(0, nch, promote, 0)

    rows = lax.broadcasted_iota(jnp.int32, qt_ref.shape, 0)
    for h in range(n_heads):
        in_head = jnp.logical_and(rows >= h * HEAD_DIM, rows < (h + 1) * HEAD_DIM)
        qh = jnp.where(in_head, qt_ref[...], jnp.zeros_like(qt_ref[...]))

        def att_chunk(c, carry):
            m, l, acc = carry
            r0 = pl.multiple_of(c * kc, kc)
            sel = s_scr[pl.ds(r0, kc), :] >= theta
            logit = jnp.where(sel, _dot(k_ref[pl.ds(r0, kc), :], qh), NEG_BIG)
            m_new = jnp.maximum(m, jnp.max(logit, axis=0, keepdims=True))
            alpha = jnp.exp(m - m_new)
            p = jnp.exp(logit - m_new)
            l = alpha * l + jnp.sum(p, axis=0, keepdims=True)
            pv = _dot(vt_ref[h * HEAD_DIM:(h + 1) * HEAD_DIM, pl.ds(r0, kc)], p.astype(BF16))
            return m_new, l, alpha * acc + pv
        m0 = jnp.full((1, qb), NEG_BIG, F32)
        m, l, acc = lax.fori_loop(0, nch, att_chunk,
                                  (m0, jnp.zeros((1, qb), F32), jnp.zeros((HEAD_DIM, qb), F32)))
        o_ref[h * HEAD_DIM:(h + 1) * HEAD_DIM, :] = acc / l


def _dsa_prompt(qt, qit, wit, kidx3, kb, vt, topk, kc):
    nb, w, t = qt.shape
    per_b = lambda a: pl.BlockSpec((None,) + a.shape[1:], lambda b, i: (b, 0, 0))
    qcol = lambda a: pl.BlockSpec((None, a.shape[1], Q_BLOCK), lambda b, i: (b, 0, i))
    return pl.pallas_call(
        functools.partial(_dsa_prompt_kernel, topk=topk, kc=kc),
        grid=(nb, t // Q_BLOCK),
        in_specs=[qcol(qt), qcol(qit), qcol(wit), per_b(kidx3), per_b(kb), per_b(vt)],
        out_specs=pl.BlockSpec((None, w, Q_BLOCK), lambda b, i: (b, 0, i)),
        out_shape=jax.ShapeDtypeStruct((nb, w, t), F32),
        scratch_shapes=[pltpu.VMEM((t, Q_BLOCK), F32)],
        compiler_params=_params("parallel", "arbitrary"),
        name="dsa_prompt",
    )(qt, qit, wit, kidx3, kb, vt)


SAMPLE_PAGES_PER_STEP = 8
QROWS = 8


def _dsa_sample_select_kernel(pt_ref, qhi_ref, qlo_ref, wi_ref, knew_ref, *rest, topk, n_new, gp):
    pages = rest[:gp]
    bias_ref, s_scr = rest[gp], rest[gp + 1]
    j = pl.program_id(1)
    nj = pl.num_programs(1)
    psz = pages[0].shape[0]
    past = nj * gp * psz
    qhi = qhi_ref[...]
    qlo = qlo_ref[...]
    wi = wi_ref[...]

    def scores(kpage):
        khi, klo = _split3(kpage)
        s = _dot_t(qhi, khi) + _dot_t(qhi, klo) + _dot_t(qlo, khi)
        s = jnp.maximum(s, 0.0) * wi
        tot = s[0:QROWS, :]
        for h in range(1, IDX_HEADS):
            tot = tot + s[h * QROWS:(h + 1) * QROWS, :]
        return tot * IDX_SCALE

    for g in range(gp):
        off = pl.multiple_of((j * gp + g) * psz, psz)
        s_scr[:, pl.ds(off, psz)] = scores(pages[g][...])

    @pl.when(j == nj - 1)
    def _():
        qrow = lax.broadcasted_iota(jnp.int32, (QROWS, psz), 0)
        kcol = lax.broadcasted_iota(jnp.int32, (QROWS, psz), 1)
        vis_new = jnp.logical_and(kcol <= qrow, kcol < n_new)
        s_scr[:, past:past + psz] = jnp.where(vis_new, scores(knew_ref[...]), -jnp.inf)
        s = s_scr[...]
        nk = s.shape[1]
        kpos = lax.broadcasted_iota(jnp.int32, s.shape, 1)
        qrow1 = lax.broadcasted_iota(jnp.int32, (QROWS, 1), 0)
        n_vis = (past + jnp.minimum(qrow1 + 1, n_new)).astype(F32)
        kq = jnp.minimum(n_vis, float(topk))
        fin = s > -jnp.inf
        mn = jnp.min(jnp.where(fin, s, jnp.inf), axis=1, keepdims=True)
        mx = jnp.max(s, axis=1, keepdims=True)
        count = lambda pred: jnp.sum(jnp.where(pred, 1.0, 0.0), axis=1, keepdims=True)
        state = _bisect_state(mn, kq, jnp.logical_or(n_vis <= float(topk), qrow1 >= n_new))

        def cond(c):
            it, st = c
            return jnp.logical_and(it < SELECT_MAX_ITERS, jnp.min(st[3]) == 0.0)

        def body(c):
            it, st = c
            mid = _bisect_mid(st[0], st[1], mx)
            return it + 1, _bisect_update(st, mid, count(s >= mid), kq)
        _, (lo, hi, theta, done, tie) = lax.while_loop(cond, body, (jnp.int32(0), state))

        sel = s >= theta
        bias_ref[...] = jnp.where(sel, 0.0, NEG_BIG)

        @pl.when(jnp.max(tie) > 0.0)
        def _():
            need = kq - count(sel)
            tied = jnp.logical_and(tie > 0.0, s == lo)
            nbits = max(1, (nk - 1).bit_length())

            def jstep(b, jl):
                cand = jl + (jnp.int32(1) << (nbits - 1 - b))
                c_lt = count(jnp.logical_and(tied, kpos < cand))
                return jnp.where(c_lt < need, cand, jl)
            jlast = lax.fori_loop(0, nbits, jstep, jnp.zeros((QROWS, 1), jnp.int32))
            keep = jnp.logical_or(sel, jnp.logical_and(tied, kpos <= jlast))
            bias_ref[...] = jnp.where(keep, 0.0, NEG_BIG)


def _dsa_sample_select(layer, page_table, qhi, qlo, wi_b, knew, pool_kidx, topk, n_new):
    nb, n_pages = page_table.shape
    gp = math.gcd(n_pages, SAMPLE_PAGES_PER_STEP)
    psz, di = pool_kidx.shape[2], pool_kidx.shape[3]
    nk = (n_pages + 1) * psz
    per_b = lambda a: pl.BlockSpec((None,) + a.shape[1:], lambda b, j, pt: (b, 0, 0))
    page = lambda g: pl.BlockSpec((None, None, psz, di), lambda b, j, pt, g=g: (layer, pt[b, j * gp + g], 0, 0))
    grid_spec = pltpu.PrefetchScalarGridSpec(
        num_scalar_prefetch=1,
        grid=(nb, n_pages // gp),
        in_specs=[per_b(qhi), per_b(qlo), per_b(wi_b), per_b(knew)] + [page(g) for g in range(gp)],
        out_specs=pl.BlockSpec((None, QROWS, nk), lambda b, j, pt: (b, 0, 0)),
        scratch_shapes=[pltpu.VMEM((QROWS, nk), F32)],
    )
    return pl.pallas_call(
        functools.partial(_dsa_sample_select_kernel, topk=topk, n_new=n_new, gp=gp),
        grid_spec=grid_spec,
        out_shape=jax.ShapeDtypeStruct((nb, QROWS, nk), F32),
        compiler_params=_params("parallel", "arbitrary"),
        name="dsa_sample_select",
    )(page_table, qhi, qlo, wi_b, knew, *([pool_kidx] * gp))


def _dsa_sample_attend_kernel(pt_ref, q_ref, knew_ref, vnew_ref, *rest, gp):
    bias_refs = rest[:gp]
    kpages = rest[gp:2 * gp]
    vpages = rest[2 * gp:3 * gp]
    bnew_ref = rest[3 * gp]
    o_ref, m_s, l_s, acc_s = rest[3 * gp + 1:]
    j = pl.program_id(1)
    nj = pl.num_programs(1)
    q = q_ref[...]
    n_heads = q.shape[0] // QROWS

    @pl.when(j == 0)
    def _():
        m_s[...] = jnp.full(m_s.shape, NEG_BIG, F32)
        l_s[...] = jnp.zeros(l_s.shape, F32)
        acc_s[...] = jnp.zeros(acc_s.shape, F32)

    def absorb(kpage, vpage, bias8):
        bias = jnp.concatenate([bias8] * n_heads, axis=0)
        logit = _dot_t(q, kpage.astype(BF16)) + bias
        m = m_s[...]
        m_new = jnp.maximum(m, jnp.max(logit, axis=1, keepdims=True))
        alpha = jnp.exp(m - m_new)
        p = jnp.exp(logit - m_new)
        l_s[...] = alpha * l_s[...] + jnp.sum(p, axis=1, keepdims=True)
        acc_s[...] = alpha * acc_s[...] + _dot(p.astype(BF16), vpage.astype(BF16))
        m_s[...] = m_new

    for g in range(gp):
        absorb(kpages[g][...], vpages[g][...], bias_refs[g][...])

    @pl.when(j == nj - 1)
    def _():
        absorb(knew_ref[...], vnew_ref[...], bnew_ref[...])
        out = acc_s[...] / l_s[...]
        lane = lax.broadcasted_iota(jnp.int32, (QROWS, out.shape[1]), 1)
        y = jnp.zeros((QROWS, out.shape[1]), F32)
        for h in range(n_heads):
            in_head = jnp.logical_and(lane >= h * HEAD_DIM, lane < (h + 1) * HEAD_DIM)
            y = y + jnp.where(in_head, out[h * QROWS:(h + 1) * QROWS, :], 0.0)
        o_ref[...] = y


def _dsa_sample_attend(layer, page_table, q32, knew, vnew, bias, pool_k, pool_v):
    nb, n_pages = page_table.shape
    gp = math.gcd(n_pages, SAMPLE_PAGES_PER_STEP)
    psz, w = pool_k.shape[2], pool_k.shape[3]
    per_b = lambda a: pl.BlockSpec((None,) + a.shape[1:], lambda b, j, pt: (b, 0, 0))
    page = lambda g: pl.BlockSpec((None, None, psz, w), lambda b, j, pt, g=g: (layer, pt[b, j * gp + g], 0, 0))
    bias_pg = lambda g: pl.BlockSpec((None, QROWS, psz), lambda b, j, pt, g=g: (b, 0, j * gp + g))
    bias_new = pl.BlockSpec((None, QROWS, psz), lambda b, j, pt: (b, 0, n_pages))
    grid_spec = pltpu.PrefetchScalarGridSpec(
        num_scalar_prefetch=1,
        grid=(nb, n_pages // gp),
        in_specs=([per_b(q32), per_b(knew), per_b(vnew)] + [bias_pg(g) for g in range(gp)]
                  + [page(g) for g in range(gp)] + [page(g) for g in range(gp)] + [bias_new]),
        out_specs=pl.BlockSpec((None, QROWS, w), lambda b, j, pt: (b, 0, 0)),
        scratch_shapes=[pltpu.VMEM((q32.shape[1], 1), F32), pltpu.VMEM((q32.shape[1], 1), F32),
                        pltpu.VMEM((q32.shape[1], w), F32)],
    )
    return pl.pallas_call(
        functools.partial(_dsa_sample_attend_kernel, gp=gp),
        grid_spec=grid_spec,
        out_shape=jax.ShapeDtypeStruct((nb, QROWS, w), F32),
        compiler_params=_params("parallel", "arbitrary"),
        name="dsa_sample_attend",
    )(page_table, q32, knew, vnew, *([bias] * gp), *([pool_k] * gp), *([pool_v] * gp), bias)


def _mixout_kernel(x_ref, u_ref, g_ref, ys_ref, h_ref, ycc_ref, yatt_ref, d_ref, wglu_ref, bglu_ref,
                   wout_ref, g1_ref, b1_ref, o_ref, *, alpha):
    w = u_ref.shape[-1]
    y = ys_ref[...] + d_ref[...] * u_ref[...]
    z = _dot(jax.nn.gelu(y).astype(BF16), wglu_ref[...]) + bglu_ref[...]
    y_s5 = z[:, :w] * jax.nn.sigmoid(z[:, w:])
    y_lru = h_ref[...] * jax.nn.gelu(g_ref[...])
    acc = _dot(y_s5.astype(BF16), wout_ref[0:w, :])
    acc = acc + _dot(y_lru.astype(BF16), wout_ref[w:2 * w, :])
    acc = acc + _dot(ycc_ref[...].astype(BF16), wout_ref[2 * w:3 * w, :])
    acc = acc + _dot(yatt_ref[...].astype(BF16), wout_ref[3 * w:4 * w, :])
    o_ref[...] = _layer_norm(alpha * x_ref[...] + acc, g1_ref[...], b1_ref[...])


def _mixout(x2, zmix, ys, h, ycc, yatt, d, wglu, bglu, wout, g1, b1, tm, alpha):
    m, dm = x2.shape
    w = ys.shape[1]
    row = lambda a: pl.BlockSpec((tm, a.shape[1]), lambda i: (i, 0))
    full = lambda a: pl.BlockSpec(a.shape, lambda i: (0, 0))
    zcol = lambda c: pl.BlockSpec((tm, w), lambda i, c=c: (i, c))
    return pl.pallas_call(
        functools.partial(_mixout_kernel, alpha=alpha),
        grid=(m // tm,),
        in_specs=[row(x2), zcol(0), zcol(2), row(ys), row(h), row(ycc), row(yatt), full(d), full(wglu),
                  full(bglu), full(wout), full(g1), full(b1)],
        out_specs=pl.BlockSpec((tm, dm), lambda i: (i, 0)),
        out_shape=jax.ShapeDtypeStruct((m, dm), F32),
        compiler_params=_params("parallel"),
        name="mixout",
    )(x2, zmix, zmix, ys, h, ycc, yatt, d, wglu, bglu, wout, g1, b1)


def _ffn_kernel(h_ref, wg_ref, wu_ref, wd_ref, g2_ref, b2_ref, o_ref, acc_s, hb_s, *, alpha):
    f = pl.program_id(1)

    @pl.when(f == 0)
    def _():
        acc_s[...] = jnp.zeros(acc_s.shape, F32)
        hb_s[...] = h_ref[...].astype(BF16)

    hb = hb_s[...]
    a = _dot(hb, wg_ref[...])
    act = (a * jax.nn.sigmoid(a)) * _dot(hb, wu_ref[...])
    acc_s[...] += _dot(act.astype(BF16), wd_ref[...])

    @pl.when(f == pl.num_programs(1) - 1)
    def _():
        o_ref[...] = _layer_norm(alpha * h_ref[...] + acc_s[...], g2_ref[...], b2_ref[...])


def _ffn(h, wg, wu, wd, g2, b2, tm, tf, alpha):
    m, dm = h.shape
    dff = wg.shape[1]
    return pl.pallas_call(
        functools.partial(_ffn_kernel, alpha=alpha),
        grid=(m // tm, dff // tf),
        in_specs=[pl.BlockSpec((tm, dm), lambda i, f: (i, 0)),
                  pl.BlockSpec((dm, tf), lambda i, f: (0, f)),
                  pl.BlockSpec((dm, tf), lambda i, f: (0, f)),
                  pl.BlockSpec((tf, dm), lambda i, f: (f, 0)),
                  pl.BlockSpec(g2.shape, lambda i, f: (0, 0)),
                  pl.BlockSpec(b2.shape, lambda i, f: (0, 0))],
        out_specs=pl.BlockSpec((tm, dm), lambda i, f: (i, 0)),
        out_shape=jax.ShapeDtypeStruct((m, dm), F32),
        scratch_shapes=[pltpu.VMEM((tm, dm), F32), pltpu.VMEM((tm, dm), BF16)],
        compiler_params=_params("parallel", "arbitrary"),
        name="ffn",
    )(h, wg, wu, wd, g2, b2)


def _moe_kernel(h_ref, wrh_ref, wrl_ref, br_ref, wg_ref, wu_ref, wd_ref, g2_ref, b2_ref, o_ref,
                acc_s, hb_s, comb_s, *, alpha, n_exp):
    e = pl.program_id(1)
    f = pl.program_id(2)
    first = jnp.logical_and(e == 0, f == 0)
    last = jnp.logical_and(e == pl.num_programs(1) - 1, f == pl.num_programs(2) - 1)

    @pl.when(first)
    def _():
        h = h_ref[...]
        acc_s[...] = jnp.zeros(acc_s.shape, F32)
        hhi, hlo = _split3(h)
        hb_s[...] = hhi
        logits = _dot(hhi, wrh_ref[...]) + _dot(hhi, wrl_ref[...]) + _dot(hlo, wrh_ref[...]) + br_ref[...]
        lane = lax.broadcasted_iota(jnp.int32, logits.shape, 1)
        m1 = jnp.max(logits, axis=1, keepdims=True)
        i1 = jnp.min(jnp.where(logits == m1, lane, LANES), axis=1, keepdims=True)
        rest = jnp.where(lane == i1, -jnp.inf, logits)
        m2 = jnp.max(rest, axis=1, keepdims=True)
        i2 = jnp.min(jnp.where(rest == m2, lane, LANES), axis=1, keepdims=True)
        e2 = jnp.exp(m2 - m1)
        den = 1.0 + e2
        comb = jnp.where(lane == i1, 1.0 / den, 0.0) + jnp.where(lane == i2, e2 / den, 0.0)
        for ex in range(n_exp):
            col = jnp.sum(jnp.where(lane == ex, comb, 0.0), axis=1, keepdims=True)
            comb_s[ex] = jnp.broadcast_to(col, comb_s.shape[1:])

    hb = hb_s[...]
    a = _dot(hb, wg_ref[...])
    act = (a * jax.nn.sigmoid(a)) * _dot(hb, wu_ref[...])
    cw = comb_s[e]
    act = act * jnp.concatenate([cw] * (act.shape[1] // LANES), axis=1)
    acc_s[...] += _dot(act.astype(BF16), wd_ref[...])

    @pl.when(last)
    def _():
        o_ref[...] = _layer_norm(alpha * h_ref[...] + acc_s[...], g2_ref[...], b2_ref[...])


def _moe(h, wrh, wrl, br, wg, wu, wd, g2, b2, tm, tf, alpha):
    m, dm = h.shape
    n_exp, _, dff = wg.shape
    c2 = lambda a: pl.BlockSpec(a.shape, lambda i, e, f: (0, 0))
    return pl.pallas_call(
        functools.partial(_moe_kernel, alpha=alpha, n_exp=n_exp),
        grid=(m // tm, n_exp, dff // tf),
        in_specs=[pl.BlockSpec((tm, dm), lambda i, e, f: (i, 0)),
                  c2(wrh), c2(wrl), c2(br),
                  pl.BlockSpec((None, dm, tf), lambda i, e, f: (e, 0, f)),
                  pl.BlockSpec((None, dm, tf), lambda i, e, f: (e, 0, f)),
                  pl.BlockSpec((None, tf, dm), lambda i, e, f: (e, f, 0)),
                  c2(g2), c2(b2)],
        out_specs=pl.BlockSpec((tm, dm), lambda i, e, f: (i, 0)),
        out_shape=jax.ShapeDtypeStruct((m, dm), F32),
        scratch_shapes=[pltpu.VMEM((tm, dm), F32), pltpu.VMEM((tm, dm), BF16),
                        pltpu.VMEM((n_exp, tm, LANES), F32)],
        compiler_params=_params("parallel", "arbitrary", "arbitrary"),
        name="moe",
    )(h, wrh, wrl, br, wg, wu, wd, g2, b2)


def _row_tile(m, target):
    t = math.gcd(m, target)
    return t if t % 8 == 0 else m


def _chunk_len(t):
    return math.gcd(t, 128)


def _rope_tables(pos, reps):
    half = HEAD_DIM // 2
    inv = ROPE_THETA ** (-(jnp.arange(half, dtype=F32) / half))
    ang = pos.astype(F32)[:, None] * inv[None, :]
    cos, sin = jnp.cos(ang), jnp.sin(ang)
    cos_h = jnp.concatenate([cos, cos], axis=-1)
    sin_h = jnp.concatenate([-sin, sin], axis=-1)
    return jnp.tile(cos_h, (1, reps)), jnp.tile(sin_h, (1, reps))


def _s5_params(a_re, a_im, b_re, b_im, c_re, c_im, log_dt):
    ng, ns = a_re.shape
    lam = lax.complex(a_re.astype(F32), a_im.astype(F32))
    dt = jnp.exp(log_dt.astype(F32))[:, None]
    a_bar = jnp.exp(lam * dt)
    b_bar = ((a_bar - 1.0) / lam)[..., None] * lax.complex(b_re.astype(F32), b_im.astype(F32))
    eye = jnp.eye(ng, dtype=F32)
    gpl = LANES // ns
    nslab = ng // gpl

    def b_mat(x):
        full = jnp.einsum('gpc,gh->gchp', x, eye).reshape(ng * S5_CH, ng * ns)
        return jnp.transpose(full.reshape(ng * S5_CH, nslab, LANES), (1, 0, 2)).astype(BF16)

    def c_mat(x):
        return jnp.einsum('gcp,gh->gphc', x, eye).reshape(nslab, LANES, ng * S5_CH)

    ccat = jnp.concatenate([c_mat(c_re.astype(F32)), -c_mat(c_im.astype(F32))], axis=1).astype(BF16)
    are = jnp.real(a_bar).reshape(nslab, 1, LANES)
    aim = jnp.imag(a_bar).reshape(nslab, 1, LANES)
    return b_mat(jnp.real(b_bar)), b_mat(jnp.imag(b_bar)), are, aim, ccat


def _to_chunks(x, nl):
    nb, t, w = x.shape
    return jnp.swapaxes(x.reshape(nb, t // nl, nl, w), 1, 2)


def _from_chunks(x):
    nb, nl, nr, w = x.shape
    return jnp.swapaxes(x, 1, 2).reshape(nb, nr * nl, w)


def _mixers(lp, zmix, nb, t, s5_re0, s5_im0, lru_h0, lru_buf0, cc_buf0, chained):
    w = lp['gw']
    zmix3 = zmix.reshape(nb, t, zmix.shape[-1])
    tt = math.gcd(t, 512)
    a, b, ycc, lru_buf, cc_buf = _convmix(zmix3, lru_buf0, cc_buf0, lp['lru_conv_w'], lp['lru_conv_b'],
                                          lp['lru_wg'], lp['lru_bg'], lp['lru_sp'], lp['cc_dw_w'], lp['cc_dw_b'],
                                          lp['cc_ln_g'], lp['cc_ln_b'], tt)
    u = zmix3[:, :, :w].astype(BF16)
    ns = s5_re0.shape[-2] * s5_re0.shape[-1]
    if chained:
        nl = _chunk_len(t)
        a4, b4, u4 = _to_chunks(a, nl), _to_chunks(b, nl), _to_chunks(u, nl)
        h0 = lru_h0.reshape(nb, 1, w)
        s0r, s0i = s5_re0.reshape(nb, 1, ns), s5_im0.reshape(nb, 1, ns)
    else:
        to_rows = lambda x: jnp.swapaxes(x, 0, 1)[None]
        a4, b4, u4 = to_rows(a), to_rows(b), to_rows(u)
        h0 = lru_h0.reshape(1, nb, w)
        s0r, s0i = s5_re0.reshape(1, nb, ns), s5_im0.reshape(1, nb, ns)
    h4, lru_h = _lru_scan(a4, b4, h0, chained)
    ys4, s5_re, s5_im = _s5_scan(u4, lp['s5_bre'], lp['s5_bim'], lp['s5_are'], lp['s5_aim'], lp['s5_ccat'],
                                 s0r, s0i, chained)
    if chained:
        h, ys = _from_chunks(h4), _from_chunks(ys4)
    else:
        h, ys = jnp.swapaxes(h4[0], 0, 1), jnp.swapaxes(ys4[0], 0, 1)
    states = (s5_re.reshape(s5_re0.shape), s5_im.reshape(s5_im0.shape), lru_h.reshape(lru_h0.shape),
              lru_buf, cc_buf)
    return ys.reshape(nb * t, w), h.reshape(nb * t, w), ycc.reshape(nb * t, w), states


def _idx3(x, order):
    hi, lo = _split3(x)
    parts = [hi if o == 'h' else lo for o in order]
    pad = jnp.zeros(x.shape[:-1] + (2 * LANES - len(order) * x.shape[-1],), BF16)
    return jnp.concatenate(parts + [pad], axis=-1)


def _attend_prompt(q, qi, k, v, kw, nb, t):
    w = q.shape[-1]
    topk = min(TOPK_MAX, t // 4)
    qt = jnp.swapaxes(q.reshape(nb, t, w), 1, 2)
    qi3 = _idx3(qi.reshape(nb, t, IDX_HEADS, IDX_DIM), 'hhl')
    qit = jnp.transpose(qi3, (0, 2, 3, 1)).reshape(nb, IDX_HEADS * 2 * LANES, t)
    wi = kw[:, IDX_DIM:IDX_DIM + IDX_HEADS].reshape(nb, t, IDX_HEADS)
    wit = jnp.pad(jnp.swapaxes(wi, 1, 2), ((0, 0), (0, 8 - IDX_HEADS), (0, 0)))
    kidx3 = _idx3(kw[:, :IDX_DIM].reshape(nb, t, IDX_DIM), 'hlh')
    kb = k.reshape(nb, t, w).astype(BF16)
    vt = jnp.swapaxes(v.reshape(nb, t, w), 1, 2).astype(BF16)
    kc = math.gcd(t, 512)
    yt = _dsa_prompt(qt, qit, wit, kidx3, kb, vt, topk, kc)
    return jnp.swapaxes(yt, 1, 2).reshape(nb * t, w)


def _attend_sample(layer, q, qi, k, v, kw, nb, t, pool_k, pool_v, pool_kidx, page_table):
    w = q.shape[-1]
    n_heads = w // HEAD_DIM
    psz = pool_kidx.shape[2]
    past = page_table.shape[1] * psz
    topk = min(TOPK_MAX, (past + t) // 4)
    pad_q = lambda x: jnp.pad(x, ((0, 0), (0, 0), (0, QROWS - t), (0, 0)))

    qi_h = jnp.swapaxes(qi.reshape(nb, t, IDX_HEADS, IDX_DIM), 1, 2)
    qhi, qlo = _split3(pad_q(qi_h).reshape(nb, IDX_HEADS * QROWS, IDX_DIM))
    wi = kw[:, IDX_DIM:IDX_DIM + IDX_HEADS].reshape(nb, t, IDX_HEADS)
    wi_b = jnp.broadcast_to(pad_q(jnp.swapaxes(wi, 1, 2)[..., None]).reshape(nb, IDX_HEADS * QROWS, 1),
                            (nb, IDX_HEADS * QROWS, psz))
    kidx_new = jnp.pad(kw[:, :IDX_DIM].reshape(nb, t, IDX_DIM), ((0, 0), (0, psz - t), (0, 0)))
    bias = _dsa_sample_select(layer, page_table, qhi, qlo, wi_b, kidx_new, pool_kidx, topk, t)

    head_of_lane = jnp.arange(w) // HEAD_DIM
    q_rows = jnp.broadcast_to(pad_q(q.reshape(nb, 1, t, w)), (nb, n_heads, QROWS, w))
    q32 = jnp.where(head_of_lane[None, None, None, :] == jnp.arange(n_heads)[None, :, None, None],
                    q_rows, jnp.zeros_like(q_rows)).reshape(nb, n_heads * QROWS, w)
    k_new = jnp.pad(k.reshape(nb, t, w), ((0, 0), (0, psz - t), (0, 0)))
    v_new = jnp.pad(v.reshape(nb, t, w), ((0, 0), (0, psz - t), (0, 0)))
    pool_k2 = pool_k.reshape(pool_k.shape[:3] + (w,))
    pool_v2 = pool_v.reshape(pool_v.shape[:3] + (w,))
    y = _dsa_sample_attend(layer, page_table, q32, k_new, v_new, bias, pool_k2, pool_v2)
    return y[:, :t, :].reshape(nb * t, w)


def _prep_layer(l, w_in, s5_a_re, s5_a_im, s5_b_re, s5_b_im, s5_c_re, s5_c_im, s5_d, s5_log_dt, s5_w_glu,
                s5_b_glu, lru_conv_w, lru_conv_b, lru_w_a, lru_b_a, lru_w_x, lru_b_x, lru_lambda, cc_dw_w,
                cc_dw_b, cc_ln_g, cc_ln_b, w_out, ln1_g, ln1_b, ln2_g, ln2_b):
    gw = s5_d.shape[1]
    d_in = w_in.shape[2]
    mix_w = 5 * gw
    pad_to = mix_w + 4 * gw + LANES
    assert d_in == mix_w + 4 * gw + IDX_DIM + IDX_HEADS and gw == IDX_HEADS * IDX_DIM
    row = lambda x: x[l].reshape(1, -1).astype(F32)
    nh, hd, _ = lru_w_a.shape[1:]
    eye = jnp.eye(nh, dtype=F32)
    bd = lambda wt: jnp.einsum('hij,hk->hikj', wt[l].astype(F32), eye).reshape(nh * hd, nh * hd)
    s5_bre, s5_bim, s5_are, s5_aim, s5_ccat = _s5_params(s5_a_re[l], s5_a_im[l], s5_b_re[l], s5_b_im[l],
                                                         s5_c_re[l], s5_c_im[l], s5_log_dt[l])
    return dict(
        gw=gw, mix_w=mix_w,
        w_in=jnp.pad(w_in[l], ((0, 0), (0, pad_to - d_in))).astype(BF16),
        s5_bre=s5_bre, s5_bim=s5_bim, s5_are=s5_are, s5_aim=s5_aim, s5_ccat=s5_ccat,
        s5_d=row(s5_d), s5_w_glu=s5_w_glu[l].astype(BF16), s5_b_glu=row(s5_b_glu),
        lru_conv_w=lru_conv_w[l].astype(F32), lru_conv_b=row(lru_conv_b),
        lru_wg=jnp.concatenate([bd(lru_w_a), bd(lru_w_x)], axis=1).astype(BF16),
        lru_bg=jnp.concatenate([row(lru_b_a), row(lru_b_x)], axis=1),
        lru_sp=jax.nn.softplus(-row(lru_lambda)),
        cc_dw_w=cc_dw_w[l].astype(F32), cc_dw_b=row(cc_dw_b), cc_ln_g=row(cc_ln_g), cc_ln_b=row(cc_ln_b),
        w_out=w_out[l].astype(BF16), ln1_g=row(ln1_g), ln1_b=row(ln1_b), ln2_g=row(ln2_g), ln2_b=row(ln2_b),
    )


def kernel(x_prompt, x_sample, cache_k, cache_v, cache_kidx, state_s5_re, state_s5_im, state_lru_h, state_lru_conv, state_cc_conv, page_table, w_in, s5_a_re, s5_a_im, s5_b_re, s5_b_im, s5_c_re, s5_c_im, s5_d, s5_log_dt, s5_w_glu, s5_b_glu, lru_conv_w, lru_conv_b, lru_w_a, lru_b_a, lru_w_x, lru_b_x, lru_lambda, cc_dw_w, cc_dw_b, cc_ln_g, cc_ln_b, w_out, ln1_g, ln1_b, ln2_g, ln2_b, ffn_w_gate, ffn_w_up, ffn_w_down, moe_w_router, moe_b_router, moe_w_gate, moe_w_up, moe_w_down):
    bp, tp, dm = x_prompt.shape
    bs, ts, _ = x_sample.shape
    depth = w_in.shape[0]
    past = page_table.shape[1] * cache_k.shape[2]
    alpha = (2.0 * depth) ** 0.25
    n_heads = cache_k.shape[3]
    n_exp = moe_w_router.shape[-1]
    gw = s5_d.shape[1]

    cos_p, sin_p = _rope_tables(jnp.arange(tp, dtype=jnp.int32), gw // HEAD_DIM)
    cos_s, sin_s = _rope_tables(past + jnp.arange(ts, dtype=jnp.int32), gw // HEAD_DIM)
    cos_s, sin_s = jnp.tile(cos_s, (bs, 1)), jnp.tile(sin_s, (bs, 1))

    tm_p = _row_tile(tp, 512)
    xp = x_prompt.reshape(bp * tp, dm)
    xs = x_sample.reshape(bs * ts, dm)
    zeros_p = lambda *shape: jnp.zeros((bp,) + shape, x_prompt.dtype)
    states_p, states_s = [], []

    for l in range(depth):
        lp = _prep_layer(l, w_in, s5_a_re, s5_a_im, s5_b_re, s5_b_im, s5_c_re, s5_c_im, s5_d, s5_log_dt,
                         s5_w_glu, s5_b_glu, lru_conv_w, lru_conv_b, lru_w_a, lru_b_a, lru_w_x, lru_b_x,
                         lru_lambda, cc_dw_w, cc_dw_b, cc_ln_g, cc_ln_b, w_out, ln1_g, ln1_b, ln2_g, ln2_b)
        mix_w = lp['mix_w']
        j = l // 2
        if l % 2 == 0:
            cw = (ffn_w_gate[j].astype(BF16), ffn_w_up[j].astype(BF16), ffn_w_down[j].astype(BF16))
        else:
            wr = jnp.pad(moe_w_router[j].astype(F32), ((0, 0), (0, LANES - n_exp)))
            wrh, wrl = _split3(wr)
            br = jnp.pad(moe_b_router[j].astype(F32), (0, LANES - n_exp), constant_values=NEG_BIG).reshape(1, LANES)
            cw = (wrh, wrl, br, moe_w_gate[j].astype(BF16), moe_w_up[j].astype(BF16), moe_w_down[j].astype(BF16))

        def channel(h, tm):
            if l % 2 == 0:
                return _ffn(h, *cw, lp['ln2_g'], lp['ln2_b'], tm, math.gcd(cw[0].shape[1], 256), alpha)
            return _moe(h, *cw, lp['ln2_g'], lp['ln2_b'], tm, math.gcd(cw[3].shape[2], 512), alpha)

        zmix, q, qi, k, v, kw = _in_proj(xp, lp['w_in'], cos_p, sin_p, tm_p, mix_w, gw)
        ys, h, ycc, st = _mixers(lp, zmix, bp, tp,
                                 zeros_p(*state_s5_re.shape[2:]), zeros_p(*state_s5_im.shape[2:]),
                                 zeros_p(*state_lru_h.shape[2:]), zeros_p(*state_lru_conv.shape[2:]),
                                 zeros_p(*state_cc_conv.shape[2:]), chained=True)
        yatt = _attend_prompt(q, qi, k, v, kw, bp, tp)
        h1 = _mixout(xp, zmix, ys, h, ycc, yatt, lp['s5_d'], lp['s5_w_glu'], lp['s5_b_glu'], lp['w_out'],
                     lp['ln1_g'], lp['ln1_b'], tm_p, alpha)
        xp = channel(h1, _row_tile(bp * tp, 1024))
        states_p.append((k.reshape(bp, tp, n_heads, HEAD_DIM), v.reshape(bp, tp, n_heads, HEAD_DIM),
                         kw[:, :IDX_DIM].reshape(bp, tp, IDX_DIM)) + st)

        ms = bs * ts
        zmix, q, qi, k, v, kw = _in_proj(xs, lp['w_in'], cos_s, sin_s, ms, mix_w, gw)
        ys, h, ycc, st = _mixers(lp, zmix, bs, ts, state_s5_re[l], state_s5_im[l], state_lru_h[l],
                                 state_lru_conv[l], state_cc_conv[l], chained=False)
        yatt = _attend_sample(l, q, qi, k, v, kw, bs, ts, cache_k, cache_v, cache_kidx, page_table)
        h1 = _mixout(xs, zmix, ys, h, ycc, yatt, lp['s5_d'], lp['s5_w_glu'], lp['s5_b_glu'], lp['w_out'],
                     lp['ln1_g'], lp['ln1_b'], ms, alpha)
        xs = channel(h1, ms)
        states_s.append((k.reshape(bs, ts, n_heads, HEAD_DIM), v.reshape(bs, ts, n_heads, HEAD_DIM),
                         kw[:, :IDX_DIM].reshape(bs, ts, IDX_DIM)) + st)

    new_p = [jnp.stack(col) for col in zip(*states_p)]
    new_s = [jnp.stack(col) for col in zip(*states_s)]
    out = [xp.reshape(bp, tp, dm), xs.reshape(bs, ts, dm)]
    for a, b in zip(new_p, new_s):
        out += [a, b]
    return tuple(out)
```

```python
import functools
import math

import jax
import jax.numpy as jnp
from jax import lax
from jax.experimental import pallas as pl
from jax.experimental.pallas import tpu as pltpu

F32 = jnp.float32
BF16 = jnp.bfloat16

S5_CH = 16
LRU_C = 8.0
HEAD_DIM = 64
IDX_DIM = 64
IDX_HEADS = 4
TOPK_MAX = 256
Q_BLOCK = 128
ROPE_THETA = 10000.0
MOE_TOP_K = 2
LN_EPS = 1e-5
ATT_SCALE = HEAD_DIM ** -0.5
IDX_SCALE = (IDX_DIM * IDX_HEADS) ** -0.5

LANES = 128
NEG_BIG = -1e30
VMEM_LIMIT = 56 * 1024 * 1024


def _params(*sem):
    return pltpu.CompilerParams(dimension_semantics=sem, vmem_limit_bytes=VMEM_LIMIT)


def _dot(a, b):
    return jnp.dot(a, b, preferred_element_type=F32)


def _dot_t(a, b):
    return lax.dot_general(a, b, (((1,), (1,)), ((), ())), preferred_element_type=F32)


def _split3(x):
    hi = x.astype(BF16)
    lo = (x - hi.astype(F32)).astype(BF16)
    return hi, lo


def _layer_norm(x, g, b):
    mu = jnp.mean(x, axis=-1, keepdims=True)
    xc = x - mu
    var = jnp.mean(xc * xc, axis=-1, keepdims=True)
    return xc * lax.rsqrt(var + LN_EPS) * g + b


def _rope(z, cos, sin):
    w = z.shape[-1]
    half = HEAD_DIM // 2
    lane = lax.broadcasted_iota(jnp.int32, z.shape, 1)
    first = (lane & (HEAD_DIM - 1)) < half
    partner = jnp.where(first, pltpu.roll(z, w - half, axis=1), pltpu.roll(z, half, axis=1))
    return z * cos + partner * sin


def _in_proj_kernel(x_ref, w_ref, cos_ref, sin_ref, zmix_ref, q_ref, qi_ref, k_ref, v_ref, kw_ref,
                    *, mix_w, gw):
    xb = x_ref[...].astype(BF16)
    cos = cos_ref[...]
    sin = sin_ref[...]

    def proj(lo, width):
        return _dot(xb, w_ref[:, lo:lo + width])

    zmix_ref[...] = proj(0, mix_w)
    o = mix_w
    q_ref[...] = (_rope(proj(o, gw), cos, sin) * ATT_SCALE).astype(BF16)
    k_ref[...] = _rope(proj(o + gw, gw), cos, sin)
    v_ref[...] = proj(o + 2 * gw, gw)
    qi_ref[...] = _rope(proj(o + 3 * gw, gw), cos, sin)
    kw = proj(o + 4 * gw, LANES)
    is_key = lax.broadcasted_iota(jnp.int32, kw.shape, 1) < IDX_DIM
    kw_ref[...] = _rope(kw, jnp.where(is_key, cos[:, :LANES], 1.0), jnp.where(is_key, sin[:, :LANES], 0.0))


def _in_proj(x2, w_pad, cos_t, sin_t, tm, mix_w, gw):
    m, d = x2.shape
    tab_blocks = cos_t.shape[0] // tm
    row = lambda i: (i, 0)
    tab = lambda i: (i % tab_blocks, 0)
    outs = [
        jax.ShapeDtypeStruct((m, mix_w), F32),
        jax.ShapeDtypeStruct((m, gw), BF16),
        jax.ShapeDtypeStruct((m, gw), F32),
        jax.ShapeDtypeStruct((m, gw), F32),
        jax.ShapeDtypeStruct((m, gw), F32),
        jax.ShapeDtypeStruct((m, LANES), F32),
    ]
    return pl.pallas_call(
        functools.partial(_in_proj_kernel, mix_w=mix_w, gw=gw),
        grid=(m // tm,),
        in_specs=[pl.BlockSpec((tm, d), row),
                  pl.BlockSpec(w_pad.shape, lambda i: (0, 0)),
                  pl.BlockSpec((tm, gw), tab),
                  pl.BlockSpec((tm, gw), tab)],
        out_specs=[pl.BlockSpec((tm, s.shape[1]), row) for s in outs],
        out_shape=outs,
        compiler_params=_params("parallel"),
        name="in_proj",
    )(x2, w_pad, cos_t, sin_t)


LRU_HALO = 8
CC_HALO = 32


def _convmix_kernel(xl_ref, ca_ref, cg_ref, lbuf_ref, cbuf_ref, lw_ref, lb_ref, wg_ref, bg_ref, sp_ref,
                    cw_ref, cb_ref, lng_ref, lnb_ref,
                    a_ref, b_ref, ycc_ref, lbuf_o, cbuf_o, xl_s, xc_s, *, tt):
    i = pl.program_id(1)
    kl = lw_ref.shape[0]
    kc = cw_ref.shape[0]
    w = xl_ref.shape[-1]
    l0 = LRU_HALO - (kl - 1)
    c0 = CC_HALO - (kc - 1)

    @pl.when(i == 0)
    def _():
        xl_s[l0:LRU_HALO, :] = lbuf_ref[...]
        xc_s[c0:CC_HALO, :] = cbuf_ref[...]

    xl_s[LRU_HALO:LRU_HALO + tt, :] = xl_ref[...]
    xc_s[CC_HALO:CC_HALO + tt, :] = ca_ref[...] * jax.nn.sigmoid(cg_ref[...])

    conv = xl_s[l0:l0 + tt, :] * lw_ref[0:1, :]
    for j in range(1, kl):
        conv = conv + xl_s[l0 + j:l0 + j + tt, :] * lw_ref[j:j + 1, :]
    conv = conv + lb_ref[...]
    gates = _dot(conv.astype(BF16), wg_ref[...]) + bg_ref[...]
    r = jax.nn.sigmoid(gates[:, :w])
    ig = jax.nn.sigmoid(gates[:, w:])
    log_a = (-LRU_C) * r * sp_ref[...]
    a = jnp.exp(log_a)
    a_ref[...] = a
    b_ref[...] = jnp.sqrt(-jnp.tanh(log_a) * (a * a + 1.0)) * (ig * conv)

    c = xc_s[c0:c0 + tt, :] * cw_ref[0:1, :]
    for j in range(1, kc):
        c = c + xc_s[c0 + j:c0 + j + tt, :] * cw_ref[j:j + 1, :]
    c = _layer_norm(c + cb_ref[...], lng_ref[...], lnb_ref[...])
    ycc_ref[...] = c * jax.nn.sigmoid(c)

    new_l = xl_s[l0 + tt:LRU_HALO + tt, :]
    new_c = xc_s[c0 + tt:CC_HALO + tt, :]
    xl_s[l0:LRU_HALO, :] = new_l
    xc_s[c0:CC_HALO, :] = new_c
    lbuf_o[...] = new_l
    cbuf_o[...] = new_c


def _convmix(zmix3, lbuf, cbuf, lw, lb, wg, bg, sp, cw, cb, lng, lnb, tt):
    nb, t, _ = zmix3.shape
    w = lw.shape[1]
    col = lambda c: pl.BlockSpec((None, tt, w), lambda b, i, c=c: (b, i, c))
    full2 = lambda a: pl.BlockSpec(a.shape, lambda b, i: (0, 0))
    per_b = lambda a: pl.BlockSpec((None,) + a.shape[1:], lambda b, i: (b, 0, 0))
    seq_out = jax.ShapeDtypeStruct((nb, t, w), F32)
    return pl.pallas_call(
        functools.partial(_convmix_kernel, tt=tt),
        grid=(nb, t // tt),
        in_specs=[col(1), col(3), col(4), per_b(lbuf), per_b(cbuf), full2(lw), full2(lb), full2(wg),
                  full2(bg), full2(sp), full2(cw), full2(cb), full2(lng), full2(lnb)],
        out_specs=[pl.BlockSpec((None, tt, w), lambda b, i: (b, i, 0))] * 3 + [per_b(lbuf), per_b(cbuf)],
        out_shape=[seq_out, seq_out, seq_out,
                   jax.ShapeDtypeStruct(lbuf.shape, F32), jax.ShapeDtypeStruct(cbuf.shape, F32)],
        scratch_shapes=[pltpu.VMEM((LRU_HALO + tt, w), F32), pltpu.VMEM((CC_HALO + tt, w), F32)],
        compiler_params=_params("parallel", "arbitrary"),
        name="convmix",
    )(zmix3, zmix3, zmix3, lbuf, cbuf, lw, lb, wg, bg, sp, cw, cb, lng, lnb)


def _lru_scan_kernel(a_ref, b_ref, h0_ref, h_ref, hl_ref, e_s, p_s, hi_s, *, chained):
    nl, nr, _ = a_ref.shape

    def run(init):
        def step(t, h):
            h = a_ref[t] * h + b_ref[t]
            h_ref[t] = h
            return h
        return lax.fori_loop(0, nl, step, init)

    if not chained:
        hl_ref[...] = run(h0_ref[...])
        return

    def local(t, c):
        h, p = c
        at = a_ref[t]
        return at * h + b_ref[t], at * p
    zeros = jnp.zeros(e_s.shape, F32)
    e, p = lax.fori_loop(0, nl, local, (zeros, zeros + 1.0))
    e_s[...] = e
    p_s[...] = p
    hi_s[0:1, :] = h0_ref[...]

    def carry(c, _):
        prev = pl.ds(c - 1, 1)
        hi_s[pl.ds(c, 1), :] = p_s[prev, :] * hi_s[prev, :] + e_s[prev, :]
        return 0
    lax.fori_loop(1, nr, carry, 0)
    last = run(hi_s[...])
    hl_ref[...] = last[nr - 1:nr, :]


def _lru_scan(a4, b4, h0, chained):
    nb, nl, nr, w = a4.shape
    r0 = h0.shape[1]
    nw = w // LANES
    blk = pl.BlockSpec((None, nl, nr, LANES), lambda n, j: (n, 0, 0, j))
    st = pl.BlockSpec((None, r0, LANES), lambda n, j: (n, 0, j))
    return pl.pallas_call(
        functools.partial(_lru_scan_kernel, chained=chained),
        grid=(nb, nw),
        in_specs=[blk, blk, st],
        out_specs=[blk, st],
        out_shape=[jax.ShapeDtypeStruct(a4.shape, F32), jax.ShapeDtypeStruct(h0.shape, F32)],
        scratch_shapes=[pltpu.VMEM((nr, LANES), F32)] * 3,
        compiler_params=_params("parallel", "parallel"),
        name="lru_scan",
    )(a4, b4, h0)


def _s5_scan_kernel(u_ref, bre_ref, bim_ref, are_ref, aim_ref, cc_ref, h0r_ref, h0i_ref,
                    ys_ref, hlr_ref, hli_ref, br_s, bi_s, er_s, ei_s, hr_s, hi_s, *, chained, rows_per_dot):
    g = pl.program_id(1)
    nl, nr, _ = u_ref.shape
    n = nl * nr
    ar = are_ref[...]
    ai = aim_ref[...]

    def drive(c, _):
        r0 = pl.multiple_of(c * rows_per_dot, rows_per_dot)
        ub = u_ref[pl.ds(c * (rows_per_dot // nr), rows_per_dot // nr)].reshape(rows_per_dot, u_ref.shape[-1])
        br_s[pl.ds(r0, rows_per_dot), :] = _dot(ub, bre_ref[...])
        bi_s[pl.ds(r0, rows_per_dot), :] = _dot(ub, bim_ref[...])
        return 0
    lax.fori_loop(0, n // rows_per_dot, drive, 0)

    def step_fn(store):
        def step(t, c):
            hr, hi = c
            rows = pl.ds(pl.multiple_of(t * nr, nr), nr)
            nhr = ar * hr - ai * hi + br_s[rows, :]
            nhi = ar * hi + ai * hr + bi_s[rows, :]
            if store:
                br_s[rows, :] = nhr
                bi_s[rows, :] = nhi
            return nhr, nhi
        return step

    if chained:
        zeros = jnp.zeros((nr, LANES), F32)
        er, ei = lax.fori_loop(0, nl, step_fn(False), (zeros, zeros))
        er_s[...] = er
        ei_s[...] = ei

        def power(_, c):
            pr, pi = c
            return ar * pr - ai * pi, ar * pi + ai * pr
        alr, ali = lax.fori_loop(0, nl - 1, power, (ar, ai))
        hr_s[0:1, :] = h0r_ref[...]
        hi_s[0:1, :] = h0i_ref[...]

        def carry(c, _):
            prev = pl.ds(c - 1, 1)
            pr = hr_s[prev, :]
            pi = hi_s[prev, :]
            hr_s[pl.ds(c, 1), :] = alr * pr - ali * pi + er_s[prev, :]
            hi_s[pl.ds(c, 1), :] = alr * pi + ali * pr + ei_s[prev, :]
            return 0
        lax.fori_loop(1, nr, carry, 0)
        init = (hr_s[...], hi_s[...])
    else:
        init = (h0r_ref[...], h0i_ref[...])

    lr, li = lax.fori_loop(0, nl, step_fn(True), init)
    if chained:
        hlr_ref[...] = lr[nr - 1:nr, :]
        hli_ref[...] = li[nr - 1:nr, :]
    else:
        hlr_ref[...] = lr
        hli_ref[...] = li

    def project(c, _):
        r0 = pl.multiple_of(c * rows_per_dot, rows_per_dot)
        rows = pl.ds(r0, rows_per_dot)
        hcat = jnp.concatenate([br_s[rows, :], bi_s[rows, :]], axis=1).astype(BF16)
        y = _dot(hcat, cc_ref[...]).reshape(rows_per_dot // nr, nr, ys_ref.shape[-1])
        tsl = pl.ds(c * (rows_per_dot // nr), rows_per_dot // nr)

        @pl.when(g == 0)
        def _():
            ys_ref[tsl] = y

        @pl.when(g > 0)
        def _():
            ys_ref[tsl] = ys_ref[tsl] + y
        return 0
    lax.fori_loop(0, n // rows_per_dot, project, 0)


def _s5_scan(u4, bre, bim, are, aim, ccat, h0r, h0i, chained):
    nb, nl, nr, w = u4.shape
    ng = bre.shape[0]
    r0 = h0r.shape[1]
    n = nl * nr
    rows_per_dot = math.gcd(n, 1024)
    rows_per_dot = max(rows_per_dot, nr)
    seq = pl.BlockSpec((None, nl, nr, w), lambda b, g: (b, 0, 0, 0))
    per_g = lambda a: pl.BlockSpec((None,) + a.shape[1:], lambda b, g: (g, 0, 0))
    st = pl.BlockSpec((None, r0, LANES), lambda b, g: (b, 0, g))
    small = pltpu.VMEM((nr, LANES), F32)
    return pl.pallas_call(
        functools.partial(_s5_scan_kernel, chained=chained, rows_per_dot=rows_per_dot),
        grid=(nb, ng),
        in_specs=[seq, per_g(bre), per_g(bim), per_g(are), per_g(aim), per_g(ccat), st, st],
        out_specs=[seq, st, st],
        out_shape=[jax.ShapeDtypeStruct(u4.shape, F32), jax.ShapeDtypeStruct(h0r.shape, F32),
                   jax.ShapeDtypeStruct(h0i.shape, F32)],
        scratch_shapes=[pltpu.VMEM((n, LANES), F32), pltpu.VMEM((n, LANES), F32), small, small, small, small],
        compiler_params=_params("parallel", "arbitrary"),
        name="s5_scan",
    )(u4, bre, bim, are, aim, ccat, h0r, h0i)


BISECT_ITERS = 24


def _select_threshold(count_ge, max_below, mn, mx, kq, all_sel):
    inf = jnp.full(mn.shape, jnp.inf, F32)
    zero = jnp.zeros(mn.shape, F32)
    state = (mn, inf, mn, jnp.where(all_sel, 1.0, zero), zero)

    def unfinished(st):
        return jnp.min(st[3]) == 0.0

    def bisect(c):
        it, (lo, hi, theta, done, tie) = c
        mid = jnp.where(hi == jnp.inf, mx, 0.5 * lo + 0.5 * hi)
        cnt = count_ge(mid)
        live = done == 0.0
        collapsed = jnp.logical_and(live, jnp.logical_or(mid == lo, mid == hi))
        hit = jnp.logical_and(live, cnt == kq)
        move = jnp.logical_and(live, jnp.logical_not(collapsed))
        theta = jnp.where(hit, mid, jnp.where(collapsed, hi, theta))
        tie = jnp.where(collapsed, 1.0, tie)
        done = jnp.where(jnp.logical_or(hit, collapsed), 1.0, done)
        lo = jnp.where(jnp.logical_and(move, cnt > kq), mid, lo)
        hi = jnp.where(jnp.logical_and(move, cnt < kq), mid, hi)
        return it + 1, (lo, hi, theta, done, tie)

    _, state = lax.while_loop(lambda c: jnp.logical_and(c[0] < BISECT_ITERS, unfinished(c[1])),
                              bisect, (jnp.int32(0), state))

    def exact_step(st):
        lo, hi, theta, done, tie = st
        live = done == 0.0
        tau = max_below(hi)
        cnt = count_ge(tau)
        found = jnp.logical_and(live, cnt >= kq)
        clean = jnp.logical_and(found, cnt == kq)
        theta = jnp.where(clean, tau, jnp.where(found, hi, theta))
        tie = jnp.where(jnp.logical_and(found, jnp.logical_not(clean)), 1.0, tie)
        lo = jnp.where(found, tau, lo)
        hi = jnp.where(jnp.logical_and(live, jnp.logical_not(found)), tau, hi)
        done = jnp.where(found, 1.0, done)
        return lo, hi, theta, done, tie

    lo, hi, theta, done, tie = lax.while_loop(unfinished, exact_step, state)
    return theta, lo, tie


COUNT_ROWS = 64


def _dsa_prompt_kernel(qt_ref, qit_ref, wit_ref, kidx_ref, k_ref, vt_ref, o_ref, s_scr, *, topk, kc):
    i = pl.program_id(1)
    qb = Q_BLOCK
    width = k_ref.shape[1]
    n_pairs = qt_ref.shape[0] // width
    nch = ((i + 1) * qb + kc - 1) // kc
    qpos = i * qb + lax.broadcasted_iota(jnp.int32, (1, qb), 1)
    kiota = lax.broadcasted_iota(jnp.int32, (kc, qb), 0)

    def chunk_rows(c):
        return pl.ds(pl.multiple_of(c * kc, kc), kc)

    def score_chunk(c, carry):
        mn, mx = carry
        kk = kidx_ref[chunk_rows(c), :]
        s = jnp.zeros((kc, qb), F32)
        for p in range(n_pairs):
            sp = _dot(kk, qit_ref[p * width:(p + 1) * width, :])
            for hh in range(2):
                h = 2 * p + hh
                s = s + jnp.maximum(sp[:, hh * qb:(hh + 1) * qb], 0.0) * wit_ref[h:h + 1, :]
        s = s * IDX_SCALE
        vis = (c * kc + kiota) <= qpos
        s_scr[chunk_rows(c), :] = jnp.where(vis, s, -jnp.inf)
        mn = jnp.minimum(mn, jnp.min(jnp.where(vis, s, jnp.inf), axis=0, keepdims=True))
        mx = jnp.maximum(mx, jnp.max(jnp.where(vis, s, -jnp.inf), axis=0, keepdims=True))
        return mn, mx
    mn, mx = lax.fori_loop(0, nch, score_chunk,
                           (jnp.full((1, qb), jnp.inf, F32), jnp.full((1, qb), -jnp.inf, F32)))

    def fold(x, op):
        return op(x.reshape(kc // COUNT_ROWS, COUNT_ROWS, qb), axis=0)

    def count(pred_fn):
        def body(c, acc):
            hit = jnp.where(pred_fn(s_scr[chunk_rows(c), :], c * kc + kiota), 1.0, 0.0)
            return acc + fold(hit, jnp.sum)
        acc = lax.fori_loop(0, nch, body, jnp.zeros((COUNT_ROWS, qb), F32))
        return jnp.sum(acc, axis=0, keepdims=True)

    def max_below(x):
        def body(c, acc):
            s = s_scr[chunk_rows(c), :]
            return jnp.maximum(acc, fold(jnp.where(s < x, s, -jnp.inf), jnp.max))
        acc = lax.fori_loop(0, nch, body, jnp.full((COUNT_ROWS, qb), -jnp.inf, F32))
        return jnp.max(acc, axis=0, keepdims=True)

    n_vis = (qpos + 1).astype(F32)
    kq = jnp.minimum(n_vis, float(topk))
    theta, lo, tie = _select_threshold(lambda x: count(lambda s, kpos: s >= x), max_below,
                                       mn, mx, kq, n_vis <= float(topk))

    @pl.when(jnp.max(tie) > 0.0)
    def _():
        need = kq - count(lambda s, kpos: s >= theta)
        tied = lambda s: jnp.logical_and(tie > 0.0, s == lo)
        nbits = max(1, (s_scr.shape[0] - 1).bit_length())

        def jstep(b, j):
            cand = j + (jnp.int32(1) << (nbits - 1 - b))
            c_lt = count(lambda s, kpos: jnp.logical_and(tied(s), kpos < cand))
            return jnp.where(c_lt < need, cand, j)
        jlast = lax.fori_loop(0, nbits, jstep, jnp.zeros((1, qb), jnp.int32))

        def promote(c, _):
            s = s_scr[chunk_rows(c), :]
            keep = jnp.logical_and(tied(s), (c * kc + kiota) <= jlast)
            s_scr[chunk_rows(c), :] = jnp.where(keep, theta, s)
            return 0
        lax.fori_loop(0, nch, promote, 0)

    n_heads = 2 * n_pairs

    def att_chunk(c, carry):
        ms, ls, accs = carry
        sel = s_scr[chunk_rows(c), :] >= theta
        kch = k_ref[chunk_rows(c), :]
        new_m, new_l, new_acc = [], [], []
        for p in range(n_pairs):
            lg = _dot(kch, qt_ref[p * width:(p + 1) * width, :])
            for hh in range(2):
                h = 2 * p + hh
                logit = jnp.where(sel, lg[:, hh * qb:(hh + 1) * qb], NEG_BIG)
                m_new = jnp.maximum(ms[h], jnp.max(logit, axis=0, keepdims=True))
                alpha = jnp.exp(ms[h] - m_new)
                pr = jnp.exp(logit - m_new)
                new_m.append(m_new)
                new_l.append(alpha * ls[h] + jnp.sum(pr, axis=0, keepdims=True))
                pv = _dot(vt_ref[h * HEAD_DIM:(h + 1) * HEAD_DIM, chunk_rows(c)], pr.astype(BF16))
                new_acc.append(alpha * accs[h] + pv)
        return tuple(new_m), tuple(new_l), tuple(new_acc)

    init = (tuple(jnp.full((1, qb), NEG_BIG, F32) for _ in range(n_heads)),
            tuple(jnp.zeros((1, qb), F32) for _ in range(n_heads)),
            tuple(jnp.zeros((HEAD_DIM, qb), F32) for _ in range(n_heads)))
    _, ls, accs = lax.fori_loop(0, nch, att_chunk, init)
    for h in range(n_heads):
        o_ref[h * HEAD_DIM:(h + 1) * HEAD_DIM, :] = accs[h] / ls[h]


def _dsa_prompt(qt2, qit2, wit, kidx3, kb, vt, topk, kc):
    nb, w, t = vt.shape
    per_b = lambda a: pl.BlockSpec((None,) + a.shape[1:], lambda b, i: (b, 0, 0))
    qblk = lambda a: pl.BlockSpec((None, None) + a.shape[2:], lambda b, i: (b, i, 0, 0))
    return pl.pallas_call(
        functools.partial(_dsa_prompt_kernel, topk=topk, kc=kc),
        grid=(nb, t // Q_BLOCK),
        in_specs=[qblk(qt2), qblk(qit2), pl.BlockSpec((None, wit.shape[1], Q_BLOCK), lambda b, i: (b, 0, i)),
                  per_b(kidx3), per_b(kb), per_b(vt)],
        out_specs=pl.BlockSpec((None, w, Q_BLOCK), lambda b, i: (b, 0, i)),
        out_shape=jax.ShapeDtypeStruct((nb, w, t), F32),
        scratch_shapes=[pltpu.VMEM((t, Q_BLOCK), F32)],
        compiler_params=_params("parallel", "arbitrary"),
        name="dsa_prompt",
    )(qt2, qit2, wit, kidx3, kb, vt)


SAMPLE_PAGES_PER_STEP = 8
QROWS = 8


def _dsa_sample_select_kernel(pt_ref, qhi_ref, qlo_ref, wi_ref, knew_ref, *rest, topk, n_new, gp):
    pages = rest[:gp]
    bias_ref, s_scr = rest[gp], rest[gp + 1]
    j = pl.program_id(1)
    nj = pl.num_programs(1)
    psz = pages[0].shape[0]
    past = nj * gp * psz
    qhi = qhi_ref[...]
    qlo = qlo_ref[...]
    wi = wi_ref[...]

    def scores(kpage):
        khi, klo = _split3(kpage)
        s = _dot_t(qhi, khi) + _dot_t(qhi, klo) + _dot_t(qlo, khi)
        s = jnp.maximum(s, 0.0) * wi
        tot = s[0:QROWS, :]
        for h in range(1, IDX_HEADS):
            tot = tot + s[h * QROWS:(h + 1) * QROWS, :]
        return tot * IDX_SCALE

    for g in range(gp):
        off = pl.multiple_of((j * gp + g) * psz, psz)
        s_scr[:, pl.ds(off, psz)] = scores(pages[g][...])

    @pl.when(j == nj - 1)
    def _():
        qrow = lax.broadcasted_iota(jnp.int32, (QROWS, psz), 0)
        kcol = lax.broadcasted_iota(jnp.int32, (QROWS, psz), 1)
        vis_new = jnp.logical_and(kcol <= qrow, kcol < n_new)
        s_scr[:, past:past + psz] = jnp.where(vis_new, scores(knew_ref[...]), -jnp.inf)
        s = s_scr[...]
        nk = s.shape[1]
        kpos = lax.broadcasted_iota(jnp.int32, s.shape, 1)
        qrow1 = lax.broadcasted_iota(jnp.int32, (QROWS, 1), 0)
        n_vis = (past + jnp.minimum(qrow1 + 1, n_new)).astype(F32)
        kq = jnp.minimum(n_vis, float(topk))
        fin = s > -jnp.inf
        mn = jnp.min(jnp.where(fin, s, jnp.inf), axis=1, keepdims=True)
        mx = jnp.max(s, axis=1, keepdims=True)
        count = lambda pred: jnp.sum(jnp.where(pred, 1.0, 0.0), axis=1, keepdims=True)
        theta, lo, tie = _select_threshold(
            lambda x: count(s >= x),
            lambda x: jnp.max(jnp.where(s < x, s, -jnp.inf), axis=1, keepdims=True),
            mn, mx, kq, jnp.logical_or(n_vis <= float(topk), qrow1 >= n_new))

        sel = s >= theta
        bias_ref[...] = jnp.where(sel, 0.0, NEG_BIG)

        @pl.when(jnp.max(tie) > 0.0)
        def _():
            need = kq - count(sel)
            tied = jnp.logical_and(tie > 0.0, s == lo)
            nbits = max(1, (nk - 1).bit_length())

            def jstep(b, jl):
                cand = jl + (jnp.int32(1) << (nbits - 1 - b))
                c_lt = count(jnp.logical_and(tied, kpos < cand))
                return jnp.where(c_lt < need, cand, jl)
            jlast = lax.fori_loop(0, nbits, jstep, jnp.zeros((QROWS, 1), jnp.int32))
            keep = jnp.logical_or(sel, jnp.logical_and(tied, kpos <= jlast))
            bias_ref[...] = jnp.where(keep, 0.0, NEG_BIG)


def _dsa_sample_select(layer, page_table, qhi, qlo, wi_b, knew, pool_kidx, topk, n_new):
    nb, n_pages = page_table.shape
    gp = math.gcd(n_pages, SAMPLE_PAGES_PER_STEP)
    psz, di = pool_kidx.shape[2], pool_kidx.shape[3]
    nk = (n_pages + 1) * psz
    per_b = lambda a: pl.BlockSpec((None,) + a.shape[1:], lambda b, j, pt: (b, 0, 0))
    page = lambda g: pl.BlockSpec((None, None, psz, di), lambda b, j, pt, g=g: (layer, pt[b, j * gp + g], 0, 0))
    grid_spec = pltpu.PrefetchScalarGridSpec(
        num_scalar_prefetch=1,
        grid=(nb, n_pages // gp),
        in_specs=[per_b(qhi), per_b(qlo), per_b(wi_b), per_b(knew)] + [page(g) for g in range(gp)],
        out_specs=pl.BlockSpec((None, QROWS, nk), lambda b, j, pt: (b, 0, 0)),
        scratch_shapes=[pltpu.VMEM((QROWS, nk), F32)],
    )
    return pl.pallas_call(
        functools.partial(_dsa_sample_select_kernel, topk=topk, n_new=n_new, gp=gp),
        grid_spec=grid_spec,
        out_shape=jax.ShapeDtypeStruct((nb, QROWS, nk), F32),
        compiler_params=_params("parallel", "arbitrary"),
        name="dsa_sample_select",
    )(page_table, qhi, qlo, wi_b, knew, *([pool_kidx] * gp))


def _dsa_sample_attend_kernel(pt_ref, q_ref, knew_ref, vnew_ref, *rest, gp):
    bias_refs = rest[:gp]
    kpages = rest[gp:2 * gp]
    vpages = rest[2 * gp:3 * gp]
    bnew_ref = rest[3 * gp]
    o_ref, m_s, l_s, acc_s = rest[3 * gp + 1:]
    j = pl.program_id(1)
    nj = pl.num_programs(1)
    q = q_ref[...]
    n_heads = q.shape[0] // QROWS

    @pl.when(j == 0)
    def _():
        m_s[...] = jnp.full(m_s.shape, NEG_BIG, F32)
        l_s[...] = jnp.zeros(l_s.shape, F32)
        acc_s[...] = jnp.zeros(acc_s.shape, F32)

    def absorb(kpage, vpage, bias8):
        bias = jnp.concatenate([bias8] * n_heads, axis=0)
        logit = _dot_t(q, kpage.astype(BF16)) + bias
        m = m_s[...]
        m_new = jnp.maximum(m, jnp.max(logit, axis=1, keepdims=True))
        alpha = jnp.exp(m - m_new)
        p = jnp.exp(logit - m_new)
        l_s[...] = alpha * l_s[...] + jnp.sum(p, axis=1, keepdims=True)
        acc_s[...] = alpha * acc_s[...] + _dot(p.astype(BF16), vpage.astype(BF16))
        m_s[...] = m_new

    for g in range(gp):
        absorb(kpages[g][...], vpages[g][...], bias_refs[g][...])

    @pl.when(j == nj - 1)
    def _():
        absorb(knew_ref[...], vnew_ref[...], bnew_ref[...])
        out = acc_s[...] / l_s[...]
        lane = lax.broadcasted_iota(jnp.int32, (QROWS, out.shape[1]), 1)
        y = jnp.zeros((QROWS, out.shape[1]), F32)
        for h in range(n_heads):
            in_head = jnp.logical_and(lane >= h * HEAD_DIM, lane < (h + 1) * HEAD_DIM)
            y = y + jnp.where(in_head, out[h * QROWS:(h + 1) * QROWS, :], 0.0)
        o_ref[...] = y


def _dsa_sample_attend(layer, page_table, q32, knew, vnew, bias, pool_k, pool_v):
    nb, n_pages = page_table.shape
    gp = math.gcd(n_pages, SAMPLE_PAGES_PER_STEP)
    psz, w = pool_k.shape[2], pool_k.shape[3]
    per_b = lambda a: pl.BlockSpec((None,) + a.shape[1:], lambda b, j, pt: (b, 0, 0))
    page = lambda g: pl.BlockSpec((None, None, psz, w), lambda b, j, pt, g=g: (layer, pt[b, j * gp + g], 0, 0))
    bias_pg = lambda g: pl.BlockSpec((None, QROWS, psz), lambda b, j, pt, g=g: (b, 0, j * gp + g))
    bias_new = pl.BlockSpec((None, QROWS, psz), lambda b, j, pt: (b, 0, n_pages))
    grid_spec = pltpu.PrefetchScalarGridSpec(
        num_scalar_prefetch=1,
        grid=(nb, n_pages // gp),
        in_specs=([per_b(q32), per_b(knew), per_b(vnew)] + [bias_pg(g) for g in range(gp)]
                  + [page(g) for g in range(gp)] + [page(g) for g in range(gp)] + [bias_new]),
        out_specs=pl.BlockSpec((None, QROWS, w), lambda b, j, pt: (b, 0, 0)),
        scratch_shapes=[pltpu.VMEM((q32.shape[1], 1), F32), pltpu.VMEM((q32.shape[1], 1), F32),
                        pltpu.VMEM((q32.shape[1], w), F32)],
    )
    return pl.pallas_call(
        functools.partial(_dsa_sample_attend_kernel, gp=gp),
        grid_spec=grid_spec,
        out_shape=jax.ShapeDtypeStruct((nb, QROWS, w), F32),
        compiler_params=_params("parallel", "arbitrary"),
        name="dsa_sample_attend",
    )(page_table, q32, knew, vnew, *([bias] * gp), *([pool_k] * gp), *([pool_v] * gp), bias)


def _mixout_kernel(x_ref, u_ref, g_ref, ys_ref, h_ref, ycc_ref, yatt_ref, d_ref, wglu_ref, bglu_ref,
                   wout_ref, g1_ref, b1_ref, o_ref, *, alpha):
    w = u_ref.shape[-1]
    y = ys_ref[...] + d_ref[...] * u_ref[...]
    z = _dot(jax.nn.gelu(y).astype(BF16), wglu_ref[...]) + bglu_ref[...]
    y_s5 = z[:, :w] * jax.nn.sigmoid(z[:, w:])
    y_lru = h_ref[...] * jax.nn.gelu(g_ref[...])
    acc = _dot(y_s5.astype(BF16), wout_ref[0:w, :])
    acc = acc + _dot(y_lru.astype(BF16), wout_ref[w:2 * w, :])
    acc = acc + _dot(ycc_ref[...].astype(BF16), wout_ref[2 * w:3 * w, :])
    acc = acc + _dot(yatt_ref[...].astype(BF16), wout_ref[3 * w:4 * w, :])
    o_ref[...] = _layer_norm(alpha * x_ref[...] + acc, g1_ref[...], b1_ref[...])


def _mixout(x2, zmix, ys, h, ycc, yatt, d, wglu, bglu, wout, g1, b1, tm, alpha):
    m, dm = x2.shape
    w = ys.shape[1]
    row = lambda a: pl.BlockSpec((tm, a.shape[1]), lambda i: (i, 0))
    full = lambda a: pl.BlockSpec(a.shape, lambda i: (0, 0))
    zcol = lambda c: pl.BlockSpec((tm, w), lambda i, c=c: (i, c))
    return pl.pallas_call(
        functools.partial(_mixout_kernel, alpha=alpha),
        grid=(m // tm,),
        in_specs=[row(x2), zcol(0), zcol(2), row(ys), row(h), row(ycc), row(yatt), full(d), full(wglu),
                  full(bglu), full(wout), full(g1), full(b1)],
        out_specs=pl.BlockSpec((tm, dm), lambda i: (i, 0)),
        out_shape=jax.ShapeDtypeStruct((m, dm), F32),
        compiler_params=_params("parallel"),
        name="mixout",
    )(x2, zmix, zmix, ys, h, ycc, yatt, d, wglu, bglu, wout, g1, b1)


def _ffn_kernel(h_ref, wg_ref, wu_ref, wd_ref, g2_ref, b2_ref, o_ref, acc_s, hb_s, *, alpha):
    f = pl.program_id(1)

    @pl.when(f == 0)
    def _():
        acc_s[...] = jnp.zeros(acc_s.shape, F32)
        hb_s[...] = h_ref[...].astype(BF16)

    hb = hb_s[...]
    a = _dot(hb, wg_ref[...])
    act = (a * jax.nn.sigmoid(a)) * _dot(hb, wu_ref[...])
    acc_s[...] += _dot(act.astype(BF16), wd_ref[...])

    @pl.when(f == pl.num_programs(1) - 1)
    def _():
        o_ref[...] = _layer_norm(alpha * h_ref[...] + acc_s[...], g2_ref[...], b2_ref[...])


def _ffn(h, wg, wu, wd, g2, b2, tm, tf, alpha):
    m, dm = h.shape
    dff = wg.shape[1]
    return pl.pallas_call(
        functools.partial(_ffn_kernel, alpha=alpha),
        grid=(m // tm, dff // tf),
        in_specs=[pl.BlockSpec((tm, dm), lambda i, f: (i, 0)),
                  pl.BlockSpec((dm, tf), lambda i, f: (0, f)),
                  pl.BlockSpec((dm, tf), lambda i, f: (0, f)),
                  pl.BlockSpec((tf, dm), lambda i, f: (f, 0)),
                  pl.BlockSpec(g2.shape, lambda i, f: (0, 0)),
                  pl.BlockSpec(b2.shape, lambda i, f: (0, 0))],
        out_specs=pl.BlockSpec((tm, dm), lambda i, f: (i, 0)),
        out_shape=jax.ShapeDtypeStruct((m, dm), F32),
        scratch_shapes=[pltpu.VMEM((tm, dm), F32), pltpu.VMEM((tm, dm), BF16)],
        compiler_params=_params("parallel", "arbitrary"),
        name="ffn",
    )(h, wg, wu, wd, g2, b2)


def _moe_kernel(h_ref, wrh_ref, wrl_ref, br_ref, wg_ref, wu_ref, wd_ref, g2_ref, b2_ref, o_ref,
                acc_s, hb_s, comb_s, *, alpha, n_exp):
    e = pl.program_id(1)
    f = pl.program_id(2)
    first = jnp.logical_and(e == 0, f == 0)
    last = jnp.logical_and(e == pl.num_programs(1) - 1, f == pl.num_programs(2) - 1)

    @pl.when(first)
    def _():
        h = h_ref[...]
        acc_s[...] = jnp.zeros(acc_s.shape, F32)
        hhi, hlo = _split3(h)
        hb_s[...] = hhi
        logits = _dot(hhi, wrh_ref[...]) + _dot(hhi, wrl_ref[...]) + _dot(hlo, wrh_ref[...]) + br_ref[...]
        lane = lax.broadcasted_iota(jnp.int32, logits.shape, 1)
        m1 = jnp.max(logits, axis=1, keepdims=True)
        i1 = jnp.min(jnp.where(logits == m1, lane, LANES), axis=1, keepdims=True)
        rest = jnp.where(lane == i1, -jnp.inf, logits)
        m2 = jnp.max(rest, axis=1, keepdims=True)
        i2 = jnp.min(jnp.where(rest == m2, lane, LANES), axis=1, keepdims=True)
        e2 = jnp.exp(m2 - m1)
        den = 1.0 + e2
        comb = jnp.where(lane == i1, 1.0 / den, 0.0) + jnp.where(lane == i2, e2 / den, 0.0)
        for ex in range(n_exp):
            col = jnp.sum(jnp.where(lane == ex, comb, 0.0), axis=1, keepdims=True)
            comb_s[ex] = jnp.broadcast_to(col, comb_s.shape[1:])

    hb = hb_s[...]
    a = _dot(hb, wg_ref[...])
    act = (a * jax.nn.sigmoid(a)) * _dot(hb, wu_ref[...])
    cw = comb_s[e]
    act = act * jnp.concatenate([cw] * (act.shape[1] // LANES), axis=1)
    acc_s[...] += _dot(act.astype(BF16), wd_ref[...])

    @pl.when(last)
    def _():
        o_ref[...] = _layer_norm(alpha * h_ref[...] + acc_s[...], g2_ref[...], b2_ref[...])


def _moe(h, wrh, wrl, br, wg, wu, wd, g2, b2, tm, tf, alpha):
    m, dm = h.shape
    n_exp, _, dff = wg.shape
    c2 = lambda a: pl.BlockSpec(a.shape, lambda i, e, f: (0, 0))
    return pl.pallas_call(
        functools.partial(_moe_kernel, alpha=alpha, n_exp=n_exp),
        grid=(m // tm, n_exp, dff // tf),
        in_specs=[pl.BlockSpec((tm, dm), lambda i, e, f: (i, 0)),
                  c2(wrh), c2(wrl), c2(br),
                  pl.BlockSpec((None, dm, tf), lambda i, e, f: (e, 0, f)),
                  pl.BlockSpec((None, dm, tf), lambda i, e, f: (e, 0, f)),
                  pl.BlockSpec((None, tf, dm), lambda i, e, f: (e, f, 0)),
                  c2(g2), c2(b2)],
        out_specs=pl.BlockSpec((tm, dm), lambda i, e, f: (i, 0)),
        out_shape=jax.ShapeDtypeStruct((m, dm), F32),
        scratch_shapes=[pltpu.VMEM((tm, dm), F32), pltpu.VMEM((tm, dm), BF16),
                        pltpu.VMEM((n_exp, tm, LANES), F32)],
        compiler_params=_params("parallel", "arbitrary", "arbitrary"),
        name="moe",
    )(h, wrh, wrl, br, wg, wu, wd, g2, b2)


def _row_tile(m, target):
    t = math.gcd(m, target)
    return t if t % 8 == 0 else m


def _chunk_len(t):
    return math.gcd(t, 128)


def _rope_tables(pos, reps):
    half = HEAD_DIM // 2
    inv = ROPE_THETA ** (-(jnp.arange(half, dtype=F32) / half))
    ang = pos.astype(F32)[:, None] * inv[None, :]
    cos, sin = jnp.cos(ang), jnp.sin(ang)
    cos_h = jnp.concatenate([cos, cos], axis=-1)
    sin_h = jnp.concatenate([-sin, sin], axis=-1)
    return jnp.tile(cos_h, (1, reps)), jnp.tile(sin_h, (1, reps))


def _s5_params(a_re, a_im, b_re, b_im, c_re, c_im, log_dt):
    ng, ns = a_re.shape
    lr, li = a_re.astype(F32), a_im.astype(F32)
    dt = jnp.exp(log_dt.astype(F32))[:, None]
    mag = jnp.exp(lr * dt)
    ar, ai = mag * jnp.cos(li * dt), mag * jnp.sin(li * dt)
    den = lr * lr + li * li
    qr = ((ar - 1.0) * lr + ai * li) / den
    qi = (ai * lr - (ar - 1.0) * li) / den
    br, bi = b_re.astype(F32), b_im.astype(F32)
    bbr = qr[..., None] * br - qi[..., None] * bi
    bbi = qr[..., None] * bi + qi[..., None] * br
    eye = jnp.eye(ng, dtype=F32)
    gpl = LANES // ns
    nslab = ng // gpl

    def b_mat(x):
        full = jnp.einsum('gpc,gh->gchp', x, eye).reshape(ng * S5_CH, ng * ns)
        return jnp.transpose(full.reshape(ng * S5_CH, nslab, LANES), (1, 0, 2)).astype(BF16)

    def c_mat(x):
        return jnp.einsum('gcp,gh->gphc', x, eye).reshape(nslab, LANES, ng * S5_CH)

    ccat = jnp.concatenate([c_mat(c_re.astype(F32)), -c_mat(c_im.astype(F32))], axis=1).astype(BF16)
    return b_mat(bbr), b_mat(bbi), ar.reshape(nslab, 1, LANES), ai.reshape(nslab, 1, LANES), ccat


def _to_chunks(x, nl):
    nb, t, w = x.shape
    return jnp.swapaxes(x.reshape(nb, t // nl, nl, w), 1, 2)


def _from_chunks(x):
    nb, nl, nr, w = x.shape
    return jnp.swapaxes(x, 1, 2).reshape(nb, nr * nl, w)


def _mixers(lp, zmix, nb, t, s5_re0, s5_im0, lru_h0, lru_buf0, cc_buf0, chained):
    w = lp['gw']
    zmix3 = zmix.reshape(nb, t, zmix.shape[-1])
    tt = math.gcd(t, 512)
    a, b, ycc, lru_buf, cc_buf = _convmix(zmix3, lru_buf0, cc_buf0, lp['lru_conv_w'], lp['lru_conv_b'],
                                          lp['lru_wg'], lp['lru_bg'], lp['lru_sp'], lp['cc_dw_w'], lp['cc_dw_b'],
                                          lp['cc_ln_g'], lp['cc_ln_b'], tt)
    u = zmix3[:, :, :w].astype(BF16)
    ns = s5_re0.shape[-2] * s5_re0.shape[-1]
    if chained:
        nl = _chunk_len(t)
        a4, b4, u4 = _to_chunks(a, nl), _to_chunks(b, nl), _to_chunks(u, nl)
        h0 = lru_h0.reshape(nb, 1, w)
        s0r, s0i = s5_re0.reshape(nb, 1, ns), s5_im0.reshape(nb, 1, ns)
    else:
        to_rows = lambda x: jnp.swapaxes(x, 0, 1)[None]
        a4, b4, u4 = to_rows(a), to_rows(b), to_rows(u)
        h0 = lru_h0.reshape(1, nb, w)
        s0r, s0i = s5_re0.reshape(1, nb, ns), s5_im0.reshape(1, nb, ns)
    h4, lru_h = _lru_scan(a4, b4, h0, chained)
    ys4, s5_re, s5_im = _s5_scan(u4, lp['s5_bre'], lp['s5_bim'], lp['s5_are'], lp['s5_aim'], lp['s5_ccat'],
                                 s0r, s0i, chained)
    if chained:
        h, ys = _from_chunks(h4), _from_chunks(ys4)
    else:
        h, ys = jnp.swapaxes(h4[0], 0, 1), jnp.swapaxes(ys4[0], 0, 1)
    states = (s5_re.reshape(s5_re0.shape), s5_im.reshape(s5_im0.shape), lru_h.reshape(lru_h0.shape),
              lru_buf, cc_buf)
    return ys.reshape(nb * t, w), h.reshape(nb * t, w), ycc.reshape(nb * t, w), states


def _idx3(x, order):
    hi, lo = _split3(x)
    parts = [hi if o == 'h' else lo for o in order]
    pad = jnp.zeros(x.shape[:-1] + (2 * LANES - len(order) * x.shape[-1],), BF16)
    return jnp.concatenate(parts + [pad], axis=-1)


def _attend_prompt(q, qi, k, v, kw, nb, t):
    w = q.shape[-1]
    topk = min(TOPK_MAX, t // 4)
    n_heads = w // HEAD_DIM
    nqb = t // Q_BLOCK

    def pair_blocks(x):
        f = x.shape[-1]
        x = x.reshape(nb, nqb, Q_BLOCK, x.shape[2] // 2, 2, f)
        return jnp.transpose(x, (0, 1, 3, 5, 4, 2)).reshape(nb, nqb, -1, 2 * Q_BLOCK)
    head_of_lane = jnp.arange(w) // HEAD_DIM
    q_masked = jnp.where(head_of_lane[None, None, None, :] == jnp.arange(n_heads)[None, None, :, None],
                         q.reshape(nb, t, 1, w), jnp.zeros((), q.dtype))
    qt = pair_blocks(q_masked)
    qi3 = _idx3(qi.reshape(nb, t, IDX_HEADS, IDX_DIM), 'hhl')
    qit = pair_blocks(qi3)
    wi = kw[:, IDX_DIM:IDX_DIM + IDX_HEADS].reshape(nb, t, IDX_HEADS)
    wit = jnp.pad(jnp.swapaxes(wi, 1, 2), ((0, 0), (0, 8 - IDX_HEADS), (0, 0)))
    kidx3 = _idx3(kw[:, :IDX_DIM].reshape(nb, t, IDX_DIM), 'hlh')
    kb = k.reshape(nb, t, w).astype(BF16)
    vt = jnp.swapaxes(v.reshape(nb, t, w), 1, 2).astype(BF16)
    kc = math.gcd(t, 512)
    yt = _dsa_prompt(qt, qit, wit, kidx3, kb, vt, topk, kc)
    return jnp.swapaxes(yt, 1, 2).reshape(nb * t, w)


def _attend_sample(layer, q, qi, k, v, kw, nb, t, pool_k, pool_v, pool_kidx, page_table):
    w = q.shape[-1]
    n_heads = w // HEAD_DIM
    psz = pool_kidx.shape[2]
    past = page_table.shape[1] * psz
    topk = min(TOPK_MAX, (past + t) // 4)
    pad_q = lambda x: jnp.pad(x, ((0, 0), (0, 0), (0, QROWS - t), (0, 0)))

    qi_h = jnp.swapaxes(qi.reshape(nb, t, IDX_HEADS, IDX_DIM), 1, 2)
    qhi, qlo = _split3(pad_q(qi_h).reshape(nb, IDX_HEADS * QROWS, IDX_DIM))
    wi = kw[:, IDX_DIM:IDX_DIM + IDX_HEADS].reshape(nb, t, IDX_HEADS)
    wi_b = jnp.broadcast_to(pad_q(jnp.swapaxes(wi, 1, 2)[..., None]).reshape(nb, IDX_HEADS * QROWS, 1),
                            (nb, IDX_HEADS * QROWS, psz))
    kidx_new = jnp.pad(kw[:, :IDX_DIM].reshape(nb, t, IDX_DIM), ((0, 0), (0, psz - t), (0, 0)))
    bias = _dsa_sample_select(layer, page_table, qhi, qlo, wi_b, kidx_new, pool_kidx, topk, t)

    head_of_lane = jnp.arange(w) // HEAD_DIM
    q_rows = jnp.broadcast_to(pad_q(q.reshape(nb, 1, t, w)), (nb, n_heads, QROWS, w))
    q32 = jnp.where(head_of_lane[None, None, None, :] == jnp.arange(n_heads)[None, :, None, None],
                    q_rows, jnp.zeros_like(q_rows)).reshape(nb, n_heads * QROWS, w)
    k_new = jnp.pad(k.reshape(nb, t, w), ((0, 0), (0, psz - t), (0, 0)))
    v_new = jnp.pad(v.reshape(nb, t, w), ((0, 0), (0, psz - t), (0, 0)))
    pool_k2 = pool_k.reshape(pool_k.shape[:3] + (w,))
    pool_v2 = pool_v.reshape(pool_v.shape[:3] + (w,))
    y = _dsa_sample_attend(layer, page_table, q32, k_new, v_new, bias, pool_k2, pool_v2)
    return y[:, :t, :].reshape(nb * t, w)


def _prep_layer(l, w_in, s5_a_re, s5_a_im, s5_b_re, s5_b_im, s5_c_re, s5_c_im, s5_d, s5_log_dt, s5_w_glu,
                s5_b_glu, lru_conv_w, lru_conv_b, lru_w_a, lru_b_a, lru_w_x, lru_b_x, lru_lambda, cc_dw_w,
                cc_dw_b, cc_ln_g, cc_ln_b, w_out, ln1_g, ln1_b, ln2_g, ln2_b):
    gw = s5_d.shape[1]
    d_in = w_in.shape[2]
    mix_w = 5 * gw
    pad_to = mix_w + 4 * gw + LANES
    assert d_in == mix_w + 4 * gw + IDX_DIM + IDX_HEADS and gw == IDX_HEADS * IDX_DIM
    row = lambda x: x[l].reshape(1, -1).astype(F32)
    nh, hd, _ = lru_w_a.shape[1:]
    eye = jnp.eye(nh, dtype=F32)
    bd = lambda wt: jnp.einsum('hij,hk->hikj', wt[l].astype(F32), eye).reshape(nh * hd, nh * hd)
    s5_bre, s5_bim, s5_are, s5_aim, s5_ccat = _s5_params(s5_a_re[l], s5_a_im[l], s5_b_re[l], s5_b_im[l],
                                                         s5_c_re[l], s5_c_im[l], s5_log_dt[l])
    return dict(
        gw=gw, mix_w=mix_w,
        w_in=jnp.pad(w_in[l], ((0, 0), (0, pad_to - d_in))).astype(BF16),
        s5_bre=s5_bre, s5_bim=s5_bim, s5_are=s5_are, s5_aim=s5_aim, s5_ccat=s5_ccat,
        s5_d=row(s5_d), s5_w_glu=s5_w_glu[l].astype(BF16), s5_b_glu=row(s5_b_glu),
        lru_conv_w=lru_conv_w[l].astype(F32), lru_conv_b=row(lru_conv_b),
        lru_wg=jnp.concatenate([bd(lru_w_a), bd(lru_w_x)], axis=1).astype(BF16),
        lru_bg=jnp.concatenate([row(lru_b_a), row(lru_b_x)], axis=1),
        lru_sp=jax.nn.softplus(-row(lru_lambda)),
        cc_dw_w=cc_dw_w[l].astype(F32), cc_dw_b=row(cc_dw_b), cc_ln_g=row(cc_ln_g), cc_ln_b=row(cc_ln_b),
        w_out=w_out[l].astype(BF16), ln1_g=row(ln1_g), ln1_b=row(ln1_b), ln2_g=row(ln2_g), ln2_b=row(ln2_b),
    )


def kernel(x_prompt, x_sample, cache_k, cache_v, cache_kidx, state_s5_re, state_s5_im, state_lru_h, state_lru_conv, state_cc_conv, page_table, w_in, s5_a_re, s5_a_im, s5_b_re, s5_b_im, s5_c_re, s5_c_im, s5_d, s5_log_dt, s5_w_glu, s5_b_glu, lru_conv_w, lru_conv_b, lru_w_a, lru_b_a, lru_w_x, lru_b_x, lru_lambda, cc_dw_w, cc_dw_b, cc_ln_g, cc_ln_b, w_out, ln1_g, ln1_b, ln2_g, ln2_b, ffn_w_gate, ffn_w_up, ffn_w_down, moe_w_router, moe_b_router, moe_w_gate, moe_w_up, moe_w_down):
    bp, tp, dm = x_prompt.shape
    bs, ts, _ = x_sample.shape
    depth = w_in.shape[0]
    past = page_table.shape[1] * cache_k.shape[2]
    alpha = (2.0 * depth) ** 0.25
    n_heads = cache_k.shape[3]
    n_exp = moe_w_router.shape[-1]
    gw = s5_d.shape[1]

    cos_p, sin_p = _rope_tables(jnp.arange(tp, dtype=jnp.int32), gw // HEAD_DIM)
    cos_s, sin_s = _rope_tables(past + jnp.arange(ts, dtype=jnp.int32), gw // HEAD_DIM)
    cos_s, sin_s = jnp.tile(cos_s, (bs, 1)), jnp.tile(sin_s, (bs, 1))

    tm_p = _row_tile(tp, 512)
    xp = x_prompt.reshape(bp * tp, dm)
    xs = x_sample.reshape(bs * ts, dm)
    zeros_p = lambda *shape: jnp.zeros((bp,) + shape, x_prompt.dtype)
    states_p, states_s = [], []

    for l in range(depth):
        lp = _prep_layer(l, w_in, s5_a_re, s5_a_im, s5_b_re, s5_b_im, s5_c_re, s5_c_im, s5_d, s5_log_dt,
                         s5_w_glu, s5_b_glu, lru_conv_w, lru_conv_b, lru_w_a, lru_b_a, lru_w_x, lru_b_x,
                         lru_lambda, cc_dw_w, cc_dw_b, cc_ln_g, cc_ln_b, w_out, ln1_g, ln1_b, ln2_g, ln2_b)
        mix_w = lp['mix_w']
        j = l // 2
        if l % 2 == 0:
            cw = (ffn_w_gate[j].astype(BF16), ffn_w_up[j].astype(BF16), ffn_w_down[j].astype(BF16))
        else:
            wr = jnp.pad(moe_w_router[j].astype(F32), ((0, 0), (0, LANES - n_exp)))
            wrh, wrl = _split3(wr)
            br = jnp.pad(moe_b_router[j].astype(F32), (0, LANES - n_exp), constant_values=NEG_BIG).reshape(1, LANES)
            cw = (wrh, wrl, br, moe_w_gate[j].astype(BF16), moe_w_up[j].astype(BF16), moe_w_down[j].astype(BF16))

        def channel(h, tm):
            if l % 2 == 0:
                return _ffn(h, *cw, lp['ln2_g'], lp['ln2_b'], tm, math.gcd(cw[0].shape[1], 256), alpha)
            return _moe(h, *cw, lp['ln2_g'], lp['ln2_b'], tm, math.gcd(cw[3].shape[2], 512), alpha)

        zmix, q, qi, k, v, kw = _in_proj(xp, lp['w_in'], cos_p, sin_p, tm_p, mix_w, gw)
        ys, h, ycc, st = _mixers(lp, zmix, bp, tp,
                                 zeros_p(*state_s5_re.shape[2:]), zeros_p(*state_s5_im.shape[2:]),
                                 zeros_p(*state_lru_h.shape[2:]), zeros_p(*state_lru_conv.shape[2:]),
                                 zeros_p(*state_cc_conv.shape[2:]), chained=True)
        yatt = _attend_prompt(q, qi, k, v, kw, bp, tp)
        h1 = _mixout(xp, zmix, ys, h, ycc, yatt, lp['s5_d'], lp['s5_w_glu'], lp['s5_b_glu'], lp['w_out'],
                     lp['ln1_g'], lp['ln1_b'], tm_p, alpha)
        xp = channel(h1, _row_tile(bp * tp, 1024))
        states_p.append((k.reshape(bp, tp, n_heads, HEAD_DIM), v.reshape(bp, tp, n_heads, HEAD_DIM),
                         kw[:, :IDX_DIM].reshape(bp, tp, IDX_DIM)) + st)

        ms = bs * ts
        zmix, q, qi, k, v, kw = _in_proj(xs, lp['w_in'], cos_s, sin_s, ms, mix_w, gw)
        ys, h, ycc, st = _mixers(lp, zmix, bs, ts, state_s5_re[l], state_s5_im[l], state_lru_h[l],
                                 state_lru_conv[l], state_cc_conv[l], chained=False)
        yatt = _attend_sample(l, q, qi, k, v, kw, bs, ts, cache_k, cache_v, cache_kidx, page_table)
        h1 = _mixout(xs, zmix, ys, h, ycc, yatt, lp['s5_d'], lp['s5_w_glu'], lp['s5_b_glu'], lp['w_out'],
                     lp['ln1_g'], lp['ln1_b'], ms, alpha)
        xs = channel(h1, ms)
        states_s.append((k.reshape(bs, ts, n_heads, HEAD_DIM), v.reshape(bs, ts, n_heads, HEAD_DIM),
                         kw[:, :IDX_DIM].reshape(bs, ts, IDX_DIM)) + st)

    new_p = [jnp.stack(col) for col in zip(*states_p)]
    new_s = [jnp.stack(col) for col in zip(*states_s)]
    out = [xp.reshape(bp, tp, dm), xs.reshape(bs, ts, dm)]
    for a, b in zip(new_p, new_s):
        out += [a, b]
    return tuple(out)
```

```python
import functools
import math

import jax
import jax.numpy as jnp
from jax import lax
from jax.experimental import pallas as pl
from jax.experimental.pallas import tpu as pltpu

F32 = jnp.float32
BF16 = jnp.bfloat16

S5_CH = 16
LRU_C = 8.0
HEAD_DIM = 64
IDX_DIM = 64
IDX_HEADS = 4
TOPK_MAX = 256
Q_BLOCK = 128
ROPE_THETA = 10000.0
MOE_TOP_K = 2
LN_EPS = 1e-5
ATT_SCALE = HEAD_DIM ** -0.5
IDX_SCALE = (IDX_DIM * IDX_HEADS) ** -0.5

LANES = 128
NEG_BIG = -1e30
VMEM_LIMIT = 56 * 1024 * 1024


def _params(*sem):
    return pltpu.CompilerParams(dimension_semantics=sem, vmem_limit_bytes=VMEM_LIMIT)


def _dot(a, b):
    return jnp.dot(a, b, preferred_element_type=F32)


def _dot_t(a, b):
    return lax.dot_general(a, b, (((1,), (1,)), ((), ())), preferred_element_type=F32)


def _split3(x):
    hi = x.astype(BF16)
    lo = (x - hi.astype(F32)).astype(BF16)
    return hi, lo


def _layer_norm(x, g, b):
    mu = jnp.mean(x, axis=-1, keepdims=True)
    xc = x - mu
    var = jnp.mean(xc * xc, axis=-1, keepdims=True)
    return xc * lax.rsqrt(var + LN_EPS) * g + b


def _rope(z, cos, sin):
    w = z.shape[-1]
    half = HEAD_DIM // 2
    lane = lax.broadcasted_iota(jnp.int32, z.shape, 1)
    first = (lane & (HEAD_DIM - 1)) < half
    partner = jnp.where(first, pltpu.roll(z, w - half, axis=1), pltpu.roll(z, half, axis=1))
    return z * cos + partner * sin


def _in_proj_kernel(x_ref, w_ref, cos_ref, sin_ref, zmix_ref, q_ref, qi_ref, k_ref, v_ref, kw_ref,
                    *, mix_w, gw):
    xb = x_ref[...].astype(BF16)
    cos = cos_ref[...]
    sin = sin_ref[...]

    def proj(lo, width):
        return _dot(xb, w_ref[:, lo:lo + width])

    zmix_ref[...] = proj(0, mix_w)
    o = mix_w
    q_ref[...] = (_rope(proj(o, gw), cos, sin) * ATT_SCALE).astype(BF16)
    k_ref[...] = _rope(proj(o + gw, gw), cos, sin)
    v_ref[...] = proj(o + 2 * gw, gw)
    qi_ref[...] = _rope(proj(o + 3 * gw, gw), cos, sin)
    kw = proj(o + 4 * gw, LANES)
    is_key = lax.broadcasted_iota(jnp.int32, kw.shape, 1) < IDX_DIM
    kw_ref[...] = _rope(kw, jnp.where(is_key, cos[:, :LANES], 1.0), jnp.where(is_key, sin[:, :LANES], 0.0))


def _in_proj(x2, w_pad, cos_t, sin_t, tm, mix_w, gw):
    m, d = x2.shape
    tab_blocks = cos_t.shape[0] // tm
    row = lambda i: (i, 0)
    tab = lambda i: (i % tab_blocks, 0)
    outs = [
        jax.ShapeDtypeStruct((m, mix_w), F32),
        jax.ShapeDtypeStruct((m, gw), BF16),
        jax.ShapeDtypeStruct((m, gw), F32),
        jax.ShapeDtypeStruct((m, gw), F32),
        jax.ShapeDtypeStruct((m, gw), F32),
        jax.ShapeDtypeStruct((m, LANES), F32),
    ]
    return pl.pallas_call(
        functools.partial(_in_proj_kernel, mix_w=mix_w, gw=gw),
        grid=(m // tm,),
        in_specs=[pl.BlockSpec((tm, d), row),
                  pl.BlockSpec(w_pad.shape, lambda i: (0, 0)),
                  pl.BlockSpec((tm, gw), tab),
                  pl.BlockSpec((tm, gw), tab)],
        out_specs=[pl.BlockSpec((tm, s.shape[1]), row) for s in outs],
        out_shape=outs,
        compiler_params=_params("parallel"),
        name="in_proj",
    )(x2, w_pad, cos_t, sin_t)


LRU_HALO = 8
CC_HALO = 32


def _convmix_kernel(xl_ref, ca_ref, cg_ref, lbuf_ref, cbuf_ref, lw_ref, lb_ref, wg_ref, bg_ref, sp_ref,
                    cw_ref, cb_ref, lng_ref, lnb_ref,
                    a_ref, b_ref, ycc_ref, lbuf_o, cbuf_o, xl_s, xc_s, *, tt):
    i = pl.program_id(1)
    kl = lw_ref.shape[0]
    kc = cw_ref.shape[0]
    w = xl_ref.shape[-1]
    l0 = LRU_HALO - (kl - 1)
    c0 = CC_HALO - (kc - 1)

    @pl.when(i == 0)
    def _():
        xl_s[l0:LRU_HALO, :] = lbuf_ref[...]
        xc_s[c0:CC_HALO, :] = cbuf_ref[...]

    xl_s[LRU_HALO:LRU_HALO + tt, :] = xl_ref[...]
    xc_s[CC_HALO:CC_HALO + tt, :] = ca_ref[...] * jax.nn.sigmoid(cg_ref[...])

    conv = xl_s[l0:l0 + tt, :] * lw_ref[0:1, :]
    for j in range(1, kl):
        conv = conv + xl_s[l0 + j:l0 + j + tt, :] * lw_ref[j:j + 1, :]
    conv = conv + lb_ref[...]
    gates = _dot(conv.astype(BF16), wg_ref[...]) + bg_ref[...]
    r = jax.nn.sigmoid(gates[:, :w])
    ig = jax.nn.sigmoid(gates[:, w:])
    log_a = (-LRU_C) * r * sp_ref[...]
    a = jnp.exp(log_a)
    a_ref[...] = a
    b_ref[...] = jnp.sqrt(-jnp.tanh(log_a) * (a * a + 1.0)) * (ig * conv)

    c = xc_s[c0:c0 + tt, :] * cw_ref[0:1, :]
    for j in range(1, kc):
        c = c + xc_s[c0 + j:c0 + j + tt, :] * cw_ref[j:j + 1, :]
    c = _layer_norm(c + cb_ref[...], lng_ref[...], lnb_ref[...])
    ycc_ref[...] = c * jax.nn.sigmoid(c)

    new_l = xl_s[l0 + tt:LRU_HALO + tt, :]
    new_c = xc_s[c0 + tt:CC_HALO + tt, :]
    xl_s[l0:LRU_HALO, :] = new_l
    xc_s[c0:CC_HALO, :] = new_c
    lbuf_o[...] = new_l
    cbuf_o[...] = new_c


def _convmix(zmix3, lbuf, cbuf, lw, lb, wg, bg, sp, cw, cb, lng, lnb, tt):
    nb, t, _ = zmix3.shape
    w = lw.shape[1]
    col = lambda c: pl.BlockSpec((None, tt, w), lambda b, i, c=c: (b, i, c))
    full2 = lambda a: pl.BlockSpec(a.shape, lambda b, i: (0, 0))
    per_b = lambda a: pl.BlockSpec((None,) + a.shape[1:], lambda b, i: (b, 0, 0))
    seq_out = jax.ShapeDtypeStruct((nb, t, w), F32)
    return pl.pallas_call(
        functools.partial(_convmix_kernel, tt=tt),
        grid=(nb, t // tt),
        in_specs=[col(1), col(3), col(4), per_b(lbuf), per_b(cbuf), full2(lw), full2(lb), full2(wg),
                  full2(bg), full2(sp), full2(cw), full2(cb), full2(lng), full2(lnb)],
        out_specs=[pl.BlockSpec((None, tt, w), lambda b, i: (b, i, 0))] * 3 + [per_b(lbuf), per_b(cbuf)],
        out_shape=[seq_out, seq_out, seq_out,
                   jax.ShapeDtypeStruct(lbuf.shape, F32), jax.ShapeDtypeStruct(cbuf.shape, F32)],
        scratch_shapes=[pltpu.VMEM((LRU_HALO + tt, w), F32), pltpu.VMEM((CC_HALO + tt, w), F32)],
        compiler_params=_params("parallel", "arbitrary"),
        name="convmix",
    )(zmix3, zmix3, zmix3, lbuf, cbuf, lw, lb, wg, bg, sp, cw, cb, lng, lnb)


def _lru_scan_kernel(a_ref, b_ref, h0_ref, h_ref, hl_ref, e_s, p_s, hi_s, *, chained):
    nl, nr, _ = a_ref.shape

    def run(init):
        def step(t, h):
            h = a_ref[t] * h + b_ref[t]
            h_ref[t] = h
            return h
        return lax.fori_loop(0, nl, step, init)

    if not chained:
        hl_ref[...] = run(h0_ref[...])
        return

    def local(t, c):
        h, p = c
        at = a_ref[t]
        return at * h + b_ref[t], at * p
    zeros = jnp.zeros(e_s.shape, F32)
    e, p = lax.fori_loop(0, nl, local, (zeros, zeros + 1.0))
    e_s[...] = e
    p_s[...] = p
    hi_s[0:1, :] = h0_ref[...]

    def carry(c, _):
        prev = pl.ds(c - 1, 1)
        hi_s[pl.ds(c, 1), :] = p_s[prev, :] * hi_s[prev, :] + e_s[prev, :]
        return 0
    lax.fori_loop(1, nr, carry, 0)
    last = run(hi_s[...])
    hl_ref[...] = last[nr - 1:nr, :]


def _lru_scan(a4, b4, h0, chained):
    nb, nl, nr, w = a4.shape
    r0 = h0.shape[1]
    nw = w // LANES
    blk = pl.BlockSpec((None, nl, nr, LANES), lambda n, j: (n, 0, 0, j))
    st = pl.BlockSpec((None, r0, LANES), lambda n, j: (n, 0, j))
    return pl.pallas_call(
        functools.partial(_lru_scan_kernel, chained=chained),
        grid=(nb, nw),
        in_specs=[blk, blk, st],
        out_specs=[blk, st],
        out_shape=[jax.ShapeDtypeStruct(a4.shape, F32), jax.ShapeDtypeStruct(h0.shape, F32)],
        scratch_shapes=[pltpu.VMEM((nr, LANES), F32)] * 3,
        compiler_params=_params("parallel", "parallel"),
        name="lru_scan",
    )(a4, b4, h0)


def _s5_scan_kernel(u_ref, bre_ref, bim_ref, are_ref, aim_ref, cc_ref, h0r_ref, h0i_ref,
                    ys_ref, hlr_ref, hli_ref, br_s, bi_s, er_s, ei_s, hr_s, hi_s, *, chained, rows_per_dot):
    g = pl.program_id(1)
    nl, nr, _ = u_ref.shape
    n = nl * nr
    ar = are_ref[...]
    ai = aim_ref[...]

    def drive(c, _):
        r0 = pl.multiple_of(c * rows_per_dot, rows_per_dot)
        ub = u_ref[pl.ds(c * (rows_per_dot // nr), rows_per_dot // nr)].reshape(rows_per_dot, u_ref.shape[-1])
        br_s[pl.ds(r0, rows_per_dot), :] = _dot(ub, bre_ref[...])
        bi_s[pl.ds(r0, rows_per_dot), :] = _dot(ub, bim_ref[...])
        return 0
    lax.fori_loop(0, n // rows_per_dot, drive, 0)

    def step_fn(store):
        def step(t, c):
            hr, hi = c
            rows = pl.ds(pl.multiple_of(t * nr, nr), nr)
            nhr = ar * hr - ai * hi + br_s[rows, :]
            nhi = ar * hi + ai * hr + bi_s[rows, :]
            if store:
                br_s[rows, :] = nhr
                bi_s[rows, :] = nhi
            return nhr, nhi
        return step

    if chained:
        zeros = jnp.zeros((nr, LANES), F32)
        er, ei = lax.fori_loop(0, nl, step_fn(False), (zeros, zeros))
        er_s[...] = er
        ei_s[...] = ei

        def power(_, c):
            pr, pi = c
            return ar * pr - ai * pi, ar * pi + ai * pr
        alr, ali = lax.fori_loop(0, nl - 1, power, (ar, ai))
        hr_s[0:1, :] = h0r_ref[...]
        hi_s[0:1, :] = h0i_ref[...]

        def carry(c, _):
            prev = pl.ds(c - 1, 1)
            pr = hr_s[prev, :]
            pi = hi_s[prev, :]
            hr_s[pl.ds(c, 1), :] = alr * pr - ali * pi + er_s[prev, :]
            hi_s[pl.ds(c, 1), :] = alr * pi + ali * pr + ei_s[prev, :]
            return 0
        lax.fori_loop(1, nr, carry, 0)
        init = (hr_s[...], hi_s[...])
    else:
        init = (h0r_ref[...], h0i_ref[...])

    lr, li = lax.fori_loop(0, nl, step_fn(True), init)
    if chained:
        hlr_ref[...] = lr[nr - 1:nr, :]
        hli_ref[...] = li[nr - 1:nr, :]
    else:
        hlr_ref[...] = lr
        hli_ref[...] = li

    def project(c, _):
        r0 = pl.multiple_of(c * rows_per_dot, rows_per_dot)
        rows = pl.ds(r0, rows_per_dot)
        hcat = jnp.concatenate([br_s[rows, :], bi_s[rows, :]], axis=1).astype(BF16)
        y = _dot(hcat, cc_ref[...]).reshape(rows_per_dot // nr, nr, ys_ref.shape[-1])
        tsl = pl.ds(c * (rows_per_dot // nr), rows_per_dot // nr)

        @pl.when(g == 0)
        def _():
            ys_ref[tsl] = y

        @pl.when(g > 0)
        def _():
            ys_ref[tsl] = ys_ref[tsl] + y
        return 0
    lax.fori_loop(0, n // rows_per_dot, project, 0)


def _s5_scan(u4, bre, bim, are, aim, ccat, h0r, h0i, chained):
    nb, nl, nr, w = u4.shape
    ng = bre.shape[0]
    r0 = h0r.shape[1]
    n = nl * nr
    rows_per_dot = math.gcd(n, 1024)
    rows_per_dot = max(rows_per_dot, nr)
    seq = pl.BlockSpec((None, nl, nr, w), lambda b, g: (b, 0, 0, 0))
    per_g = lambda a: pl.BlockSpec((None,) + a.shape[1:], lambda b, g: (g, 0, 0))
    st = pl.BlockSpec((None, r0, LANES), lambda b, g: (b, 0, g))
    small = pltpu.VMEM((nr, LANES), F32)
    return pl.pallas_call(
        functools.partial(_s5_scan_kernel, chained=chained, rows_per_dot=rows_per_dot),
        grid=(nb, ng),
        in_specs=[seq, per_g(bre), per_g(bim), per_g(are), per_g(aim), per_g(ccat), st, st],
        out_specs=[seq, st, st],
        out_shape=[jax.ShapeDtypeStruct(u4.shape, F32), jax.ShapeDtypeStruct(h0r.shape, F32),
                   jax.ShapeDtypeStruct(h0i.shape, F32)],
        scratch_shapes=[pltpu.VMEM((n, LANES), F32), pltpu.VMEM((n, LANES), F32), small, small, small, small],
        compiler_params=_params("parallel", "arbitrary"),
        name="s5_scan",
    )(u4, bre, bim, are, aim, ccat, h0r, h0i)


BISECT_ITERS = 24


def _select_threshold(count_ge, max_below, mn, mx, kq, all_sel):
    inf = jnp.full(mn.shape, jnp.inf, F32)
    zero = jnp.zeros(mn.shape, F32)
    state = (mn, inf, mn, jnp.where(all_sel, 1.0, zero), zero)

    def unfinished(st):
        return jnp.min(st[3]) == 0.0

    def bisect(c):
        it, (lo, hi, theta, done, tie) = c
        mid = jnp.where(hi == jnp.inf, mx, 0.5 * lo + 0.5 * hi)
        cnt = count_ge(mid)
        live = done == 0.0
        collapsed = jnp.logical_and(live, jnp.logical_or(mid == lo, mid == hi))
        hit = jnp.logical_and(live, cnt == kq)
        move = jnp.logical_and(live, jnp.logical_not(collapsed))
        theta = jnp.where(hit, mid, jnp.where(collapsed, hi, theta))
        tie = jnp.where(collapsed, 1.0, tie)
        done = jnp.where(jnp.logical_or(hit, collapsed), 1.0, done)
        lo = jnp.where(jnp.logical_and(move, cnt > kq), mid, lo)
        hi = jnp.where(jnp.logical_and(move, cnt < kq), mid, hi)
        return it + 1, (lo, hi, theta, done, tie)

    _, state = lax.while_loop(lambda c: jnp.logical_and(c[0] < BISECT_ITERS, unfinished(c[1])),
                              bisect, (jnp.int32(0), state))

    def exact_step(st):
        lo, hi, theta, done, tie = st
        live = done == 0.0
        tau = max_below(hi)
        cnt = count_ge(tau)
        found = jnp.logical_and(live, cnt >= kq)
        clean = jnp.logical_and(found, cnt == kq)
        theta = jnp.where(clean, tau, jnp.where(found, hi, theta))
        tie = jnp.where(jnp.logical_and(found, jnp.logical_not(clean)), 1.0, tie)
        lo = jnp.where(found, tau, lo)
        hi = jnp.where(jnp.logical_and(live, jnp.logical_not(found)), tau, hi)
        done = jnp.where(found, 1.0, done)
        return lo, hi, theta, done, tie

    lo, hi, theta, done, tie = lax.while_loop(unfinished, exact_step, state)
    return theta, lo, tie


COUNT_ROWS = 64


def _dsa_prompt_kernel(qt_ref, qit_ref, wit_ref, kidx_ref, k_ref, vt_ref, o_ref, s_scr, *, topk, kc):
    i = pl.program_id(1)
    qb = Q_BLOCK
    width = k_ref.shape[1]
    n_pairs = qt_ref.shape[0] // width
    nch = ((i + 1) * qb + kc - 1) // kc
    qpos = i * qb + lax.broadcasted_iota(jnp.int32, (1, qb), 1)
    kiota = lax.broadcasted_iota(jnp.int32, (kc, qb), 0)

    def chunk_rows(c):
        return pl.ds(pl.multiple_of(c * kc, kc), kc)

    def score_chunk(c, carry):
        mn, mx = carry
        kk = kidx_ref[chunk_rows(c), :]
        s = jnp.zeros((kc, qb), F32)
        for p in range(n_pairs):
            sp = _dot(kk, qit_ref[p * width:(p + 1) * width, :])
            for hh in range(2):
                h = 2 * p + hh
                s = s + jnp.maximum(sp[:, hh * qb:(hh + 1) * qb], 0.0) * wit_ref[h:h + 1, :]
        s = s * IDX_SCALE
        vis = (c * kc + kiota) <= qpos
        s_scr[chunk_rows(c), :] = jnp.where(vis, s, -jnp.inf)
        mn = jnp.minimum(mn, jnp.min(jnp.where(vis, s, jnp.inf), axis=0, keepdims=True))
        mx = jnp.maximum(mx, jnp.max(jnp.where(vis, s, -jnp.inf), axis=0, keepdims=True))
        return mn, mx
    nch2 = (nch + 1) // 2
    mn, mx = lax.fori_loop(0, 2 * nch2, score_chunk,
                           (jnp.full((1, qb), jnp.inf, F32), jnp.full((1, qb), -jnp.inf, F32)))

    def fold(x, op):
        return op(x.reshape(kc // COUNT_ROWS, COUNT_ROWS, qb), axis=0)

    def count(pred_fn):
        def body(c, acc):
            hit = jnp.where(pred_fn(s_scr[chunk_rows(c), :], c * kc + kiota), 1.0, 0.0)
            return acc + fold(hit, jnp.sum)
        acc = lax.fori_loop(0, nch, body, jnp.zeros((COUNT_ROWS, qb), F32))
        return jnp.sum(acc, axis=0, keepdims=True)

    def max_below(x):
        def body(c, acc):
            s = s_scr[chunk_rows(c), :]
            return jnp.maximum(acc, fold(jnp.where(s < x, s, -jnp.inf), jnp.max))
        acc = lax.fori_loop(0, nch, body, jnp.full((COUNT_ROWS, qb), -jnp.inf, F32))
        return jnp.max(acc, axis=0, keepdims=True)

    n_vis = (qpos + 1).astype(F32)
    kq = jnp.minimum(n_vis, float(topk))
    theta, lo, tie = _select_threshold(lambda x: count(lambda s, kpos: s >= x), max_below,
                                       mn, mx, kq, n_vis <= float(topk))

    @pl.when(jnp.max(tie) > 0.0)
    def _():
        need = kq - count(lambda s, kpos: s >= theta)
        tied = lambda s: jnp.logical_and(tie > 0.0, s == lo)
        tri = jnp.where(lax.broadcasted_iota(jnp.int32, (kc, kc), 0) >= lax.broadcasted_iota(jnp.int32, (kc, kc), 1),
                        1.0, 0.0).astype(BF16)

        def promote(c, seen):
            s = s_scr[chunk_rows(c), :]
            is_tied = tied(s)
            rank = seen + _dot(tri, jnp.where(is_tied, 1.0, 0.0).astype(BF16))
            keep = jnp.logical_and(is_tied, rank <= need)
            s_scr[chunk_rows(c), :] = jnp.where(keep, theta, s)
            return rank[kc - 1:kc, :]
        lax.fori_loop(0, nch, promote, jnp.zeros((1, qb), F32))

    n_heads = 2 * n_pairs
    kc2 = 2 * kc

    def att_chunk(c, carry):
        ms, ls, accs = carry
        rows2 = pl.ds(pl.multiple_of(c * kc2, kc2), kc2)
        sel = s_scr[rows2, :] >= theta
        kch = k_ref[rows2, :]
        new_m, new_l, new_acc = [], [], []
        for p in range(n_pairs):
            lg = _dot(kch, qt_ref[p * width:(p + 1) * width, :])
            for hh in range(2):
                h = 2 * p + hh
                logit = jnp.where(sel, lg[:, hh * qb:(hh + 1) * qb], NEG_BIG)
                m_new = jnp.maximum(ms[h], jnp.max(logit, axis=0, keepdims=True))
                alpha = jnp.exp(ms[h] - m_new)
                pr = jnp.exp(logit - m_new)
                new_m.append(m_new)
                new_l.append(alpha * ls[h] + jnp.sum(pr, axis=0, keepdims=True))
                pv = _dot(vt_ref[h * HEAD_DIM:(h + 1) * HEAD_DIM, rows2], pr.astype(BF16))
                new_acc.append(alpha * accs[h] + pv)
        return tuple(new_m), tuple(new_l), tuple(new_acc)

    init = (tuple(jnp.full((1, qb), NEG_BIG, F32) for _ in range(n_heads)),
            tuple(jnp.zeros((1, qb), F32) for _ in range(n_heads)),
            tuple(jnp.zeros((HEAD_DIM, qb), F32) for _ in range(n_heads)))
    _, ls, accs = lax.fori_loop(0, nch2, att_chunk, init)
    for h in range(n_heads):
        o_ref[h * HEAD_DIM:(h + 1) * HEAD_DIM, :] = accs[h] / ls[h]


def _dsa_prompt(qt2, qit2, wit, kidx3, kb, vt, topk, kc):
    nb, w, t = vt.shape
    per_b = lambda a: pl.BlockSpec((None,) + a.shape[1:], lambda b, i: (b, 0, 0))
    qblk = lambda a: pl.BlockSpec((None, None) + a.shape[2:], lambda b, i: (b, i, 0, 0))
    return pl.pallas_call(
        functools.partial(_dsa_prompt_kernel, topk=topk, kc=kc),
        grid=(nb, t // Q_BLOCK),
        in_specs=[qblk(qt2), qblk(qit2), pl.BlockSpec((None, wit.shape[1], Q_BLOCK), lambda b, i: (b, 0, i)),
                  per_b(kidx3), per_b(kb), per_b(vt)],
        out_specs=pl.BlockSpec((None, w, Q_BLOCK), lambda b, i: (b, 0, i)),
        out_shape=jax.ShapeDtypeStruct((nb, w, t), F32),
        scratch_shapes=[pltpu.VMEM((t, Q_BLOCK), F32)],
        compiler_params=_params("parallel", "arbitrary"),
        name="dsa_prompt",
    )(qt2, qit2, wit, kidx3, kb, vt)


SAMPLE_PAGES_PER_STEP = 8
QROWS = 8


def _dsa_sample_select_kernel(pt_ref, qhi_ref, qlo_ref, wi_ref, knew_ref, *rest, topk, n_new, gp):
    pages = rest[:gp]
    bias_ref, s_scr = rest[gp], rest[gp + 1]
    j = pl.program_id(1)
    nj = pl.num_programs(1)
    psz = pages[0].shape[1]
    past = nj * gp * psz
    qhi = qhi_ref[...]
    qlo = qlo_ref[...]
    wi = wi_ref[...]

    def scores(kpage_t):
        khi, klo = _split3(kpage_t)
        s = _dot(qhi, khi) + _dot(qhi, klo) + _dot(qlo, khi)
        s = jnp.maximum(s, 0.0) * wi
        tot = s[0:QROWS, :]
        for h in range(1, IDX_HEADS):
            tot = tot + s[h * QROWS:(h + 1) * QROWS, :]
        return tot * IDX_SCALE

    for g in range(gp):
        off = pl.multiple_of((j * gp + g) * psz, psz)
        s_scr[:, pl.ds(off, psz)] = scores(pages[g][...])

    @pl.when(j == nj - 1)
    def _():
        qrow = lax.broadcasted_iota(jnp.int32, (QROWS, psz), 0)
        kcol = lax.broadcasted_iota(jnp.int32, (QROWS, psz), 1)
        vis_new = jnp.logical_and(kcol <= qrow, kcol < n_new)
        s_scr[:, past:past + psz] = jnp.where(vis_new, scores(knew_ref[...]), -jnp.inf)
        s = s_scr[...]
        nk = s.shape[1]
        kpos = lax.broadcasted_iota(jnp.int32, s.shape, 1)
        qrow1 = lax.broadcasted_iota(jnp.int32, (QROWS, 1), 0)
        n_vis = (past + jnp.minimum(qrow1 + 1, n_new)).astype(F32)
        kq = jnp.minimum(n_vis, float(topk))
        fin = s > -jnp.inf
        mn = jnp.min(jnp.where(fin, s, jnp.inf), axis=1, keepdims=True)
        mx = jnp.max(s, axis=1, keepdims=True)
        count = lambda pred: jnp.sum(jnp.where(pred, 1.0, 0.0), axis=1, keepdims=True)
        theta, lo, tie = _select_threshold(
            lambda x: count(s >= x),
            lambda x: jnp.max(jnp.where(s < x, s, -jnp.inf), axis=1, keepdims=True),
            mn, mx, kq, jnp.logical_or(n_vis <= float(topk), qrow1 >= n_new))

        sel = s >= theta
        bias_ref[...] = jnp.where(sel, 0.0, NEG_BIG)

        @pl.when(jnp.max(tie) > 0.0)
        def _():
            need = kq - count(sel)
            tied = jnp.logical_and(tie > 0.0, s == lo)
            nbits = max(1, (nk - 1).bit_length())

            def jstep(b, jl):
                cand = jl + (jnp.int32(1) << (nbits - 1 - b))
                c_lt = count(jnp.logical_and(tied, kpos < cand))
                return jnp.where(c_lt < need, cand, jl)
            jlast = lax.fori_loop(0, nbits, jstep, jnp.zeros((QROWS, 1), jnp.int32))
            keep = jnp.logical_or(sel, jnp.logical_and(tied, kpos <= jlast))
            bias_ref[...] = jnp.where(keep, 0.0, NEG_BIG)


def _dsa_sample_select(layer, page_table, qhi, qlo, wi_b, knew, pool_kidx, topk, n_new):
    nb, n_pages = page_table.shape
    gp = math.gcd(n_pages, SAMPLE_PAGES_PER_STEP)
    di, psz = pool_kidx.shape[2], pool_kidx.shape[3]
    nk = (n_pages + 1) * psz
    per_b = lambda a: pl.BlockSpec((None,) + a.shape[1:], lambda b, j, pt: (b, 0, 0))
    page = lambda g: pl.BlockSpec((None, None, di, psz), lambda b, j, pt, g=g: (layer, pt[b, j * gp + g], 0, 0))
    grid_spec = pltpu.PrefetchScalarGridSpec(
        num_scalar_prefetch=1,
        grid=(nb, n_pages // gp),
        in_specs=[per_b(qhi), per_b(qlo), per_b(wi_b), per_b(knew)] + [page(g) for g in range(gp)],
        out_specs=pl.BlockSpec((None, QROWS, nk), lambda b, j, pt: (b, 0, 0)),
        scratch_shapes=[pltpu.VMEM((QROWS, nk), F32)],
    )
    return pl.pallas_call(
        functools.partial(_dsa_sample_select_kernel, topk=topk, n_new=n_new, gp=gp),
        grid_spec=grid_spec,
        out_shape=jax.ShapeDtypeStruct((nb, QROWS, nk), F32),
        compiler_params=_params("parallel", "arbitrary"),
        name="dsa_sample_select",
    )(page_table, qhi, qlo, wi_b, knew, *([pool_kidx] * gp))


def _dsa_sample_attend_kernel(pt_ref, q_ref, knew_ref, vnew_ref, *rest, gp):
    bias_refs = rest[:gp]
    kpages = rest[gp:2 * gp]
    vpages = rest[2 * gp:3 * gp]
    bnew_ref = rest[3 * gp]
    o_ref, m_s, l_s, acc_s = rest[3 * gp + 1:]
    j = pl.program_id(1)
    nj = pl.num_programs(1)
    q = q_ref[...]
    n_heads = q.shape[0] // QROWS

    @pl.when(j == 0)
    def _():
        m_s[...] = jnp.full(m_s.shape, NEG_BIG, F32)
        l_s[...] = jnp.zeros(l_s.shape, F32)
        acc_s[...] = jnp.zeros(acc_s.shape, F32)

    def absorb(kts, vts, biases):
        psz = kts[0].shape[1]
        logit = jnp.concatenate(
            [_dot(q, kt[...].astype(BF16)) + jnp.concatenate([b[...]] * n_heads, axis=0)
             for kt, b in zip(kts, biases)], axis=1)
        m = m_s[...]
        m_new = jnp.maximum(m, jnp.max(logit, axis=1, keepdims=True))
        alpha = jnp.exp(m - m_new)
        p = jnp.exp(logit - m_new)
        l_s[...] = alpha * l_s[...] + jnp.sum(p, axis=1, keepdims=True)
        pv = _dot_t(p[:, 0:psz].astype(BF16), vts[0][...].astype(BF16))
        for g in range(1, len(vts)):
            pv = pv + _dot_t(p[:, g * psz:(g + 1) * psz].astype(BF16), vts[g][...].astype(BF16))
        acc_s[...] = alpha * acc_s[...] + pv
        m_s[...] = m_new

    absorb(kpages, vpages, bias_refs)

    @pl.when(j == nj - 1)
    def _():
        absorb([knew_ref], [vnew_ref], [bnew_ref])
        out = acc_s[...] / l_s[...]
        lane = lax.broadcasted_iota(jnp.int32, (QROWS, out.shape[1]), 1)
        y = jnp.zeros((QROWS, out.shape[1]), F32)
        for h in range(n_heads):
            in_head = jnp.logical_and(lane >= h * HEAD_DIM, lane < (h + 1) * HEAD_DIM)
            y = y + jnp.where(in_head, out[h * QROWS:(h + 1) * QROWS, :], 0.0)
        o_ref[...] = y


def _dsa_sample_attend(layer, page_table, q32, knew, vnew, bias, pool_k, pool_v):
    nb, n_pages = page_table.shape
    gp = math.gcd(n_pages, SAMPLE_PAGES_PER_STEP)
    w, psz = pool_k.shape[2], pool_k.shape[3]
    per_b = lambda a: pl.BlockSpec((None,) + a.shape[1:], lambda b, j, pt: (b, 0, 0))
    page = lambda g: pl.BlockSpec((None, None, w, psz), lambda b, j, pt, g=g: (layer, pt[b, j * gp + g], 0, 0))
    bias_pg = lambda g: pl.BlockSpec((None, QROWS, psz), lambda b, j, pt, g=g: (b, 0, j * gp + g))
    bias_new = pl.BlockSpec((None, QROWS, psz), lambda b, j, pt: (b, 0, n_pages))
    grid_spec = pltpu.PrefetchScalarGridSpec(
        num_scalar_prefetch=1,
        grid=(nb, n_pages // gp),
        in_specs=([per_b(q32), per_b(knew), per_b(vnew)] + [bias_pg(g) for g in range(gp)]
                  + [page(g) for g in range(gp)] + [page(g) for g in range(gp)] + [bias_new]),
        out_specs=pl.BlockSpec((None, QROWS, w), lambda b, j, pt: (b, 0, 0)),
        scratch_shapes=[pltpu.VMEM((q32.shape[1], 1), F32), pltpu.VMEM((q32.shape[1], 1), F32),
                        pltpu.VMEM((q32.shape[1], w), F32)],
    )
    return pl.pallas_call(
        functools.partial(_dsa_sample_attend_kernel, gp=gp),
        grid_spec=grid_spec,
        out_shape=jax.ShapeDtypeStruct((nb, QROWS, w), F32),
        compiler_params=_params("parallel", "arbitrary"),
        name="dsa_sample_attend",
    )(page_table, q32, knew, vnew, *([bias] * gp), *([pool_k] * gp), *([pool_v] * gp), bias)


def _mixout_kernel(x_ref, u_ref, g_ref, ys_ref, h_ref, ycc_ref, yatt_ref, d_ref, wglu_ref, bglu_ref,
                   wout_ref, g1_ref, b1_ref, o_ref, *, alpha):
    w = u_ref.shape[-1]
    y = ys_ref[...] + d_ref[...] * u_ref[...]
    z = _dot(jax.nn.gelu(y).astype(BF16), wglu_ref[...]) + bglu_ref[...]
    y_s5 = z[:, :w] * jax.nn.sigmoid(z[:, w:])
    y_lru = h_ref[...] * jax.nn.gelu(g_ref[...])
    acc = _dot(y_s5.astype(BF16), wout_ref[0:w, :])
    acc = acc + _dot(y_lru.astype(BF16), wout_ref[w:2 * w, :])
    acc = acc + _dot(ycc_ref[...].astype(BF16), wout_ref[2 * w:3 * w, :])
    acc = acc + _dot(yatt_ref[...].astype(BF16), wout_ref[3 * w:4 * w, :])
    o_ref[...] = _layer_norm(alpha * x_ref[...] + acc, g1_ref[...], b1_ref[...])


def _mixout(x2, zmix, ys, h, ycc, yatt, d, wglu, bglu, wout, g1, b1, tm, alpha):
    m, dm = x2.shape
    w = ys.shape[1]
    row = lambda a: pl.BlockSpec((tm, a.shape[1]), lambda i: (i, 0))
    full = lambda a: pl.BlockSpec(a.shape, lambda i: (0, 0))
    zcol = lambda c: pl.BlockSpec((tm, w), lambda i, c=c: (i, c))
    return pl.pallas_call(
        functools.partial(_mixout_kernel, alpha=alpha),
        grid=(m // tm,),
        in_specs=[row(x2), zcol(0), zcol(2), row(ys), row(h), row(ycc), row(yatt), full(d), full(wglu),
                  full(bglu), full(wout), full(g1), full(b1)],
        out_specs=pl.BlockSpec((tm, dm), lambda i: (i, 0)),
        out_shape=jax.ShapeDtypeStruct((m, dm), F32),
        compiler_params=_params("parallel"),
        name="mixout",
    )(x2, zmix, zmix, ys, h, ycc, yatt, d, wglu, bglu, wout, g1, b1)


def _ffn_kernel(h_ref, wg_ref, wu_ref, wd_ref, g2_ref, b2_ref, o_ref, acc_s, hb_s, *, alpha):
    f = pl.program_id(1)

    @pl.when(f == 0)
    def _():
        acc_s[...] = jnp.zeros(acc_s.shape, F32)
        hb_s[...] = h_ref[...].astype(BF16)

    hb = hb_s[...]
    a = _dot(hb, wg_ref[...])
    act = (a * jax.nn.sigmoid(a)) * _dot(hb, wu_ref[...])
    acc_s[...] += _dot(act.astype(BF16), wd_ref[...])

    @pl.when(f == pl.num_programs(1) - 1)
    def _():
        o_ref[...] = _layer_norm(alpha * h_ref[...] + acc_s[...], g2_ref[...], b2_ref[...])


def _ffn(h, wg, wu, wd, g2, b2, tm, tf, alpha):
    m, dm = h.shape
    dff = wg.shape[1]
    return pl.pallas_call(
        functools.partial(_ffn_kernel, alpha=alpha),
        grid=(m // tm, dff // tf),
        in_specs=[pl.BlockSpec((tm, dm), lambda i, f: (i, 0)),
                  pl.BlockSpec((dm, tf), lambda i, f: (0, f)),
                  pl.BlockSpec((dm, tf), lambda i, f: (0, f)),
                  pl.BlockSpec((tf, dm), lambda i, f: (f, 0)),
                  pl.BlockSpec(g2.shape, lambda i, f: (0, 0)),
                  pl.BlockSpec(b2.shape, lambda i, f: (0, 0))],
        out_specs=pl.BlockSpec((tm, dm), lambda i, f: (i, 0)),
        out_shape=jax.ShapeDtypeStruct((m, dm), F32),
        scratch_shapes=[pltpu.VMEM((tm, dm), F32), pltpu.VMEM((tm, dm), BF16)],
        compiler_params=_params("parallel", "arbitrary"),
        name="ffn",
    )(h, wg, wu, wd, g2, b2)


MOE_CHUNK_ROWS = 288


def _moe_kernel(h_ref, wrh_ref, wrl_ref, br_ref, wg_ref, wu_ref, wd_ref, g2_ref, b2_ref, o_ref,
                hb_s, xg_s, y_s, gate_s, slot_s, slot_t_s, *, alpha, n_exp, cr):
    e = pl.program_id(1)
    f = pl.program_id(2)
    nf = pl.num_programs(2)
    tm, dm = h_ref.shape
    first = jnp.logical_and(e == 0, f == 0)
    last = jnp.logical_and(e == pl.num_programs(1) - 1, f == nf - 1)

    @pl.when(first)
    def _():
        hhi, hlo = _split3(h_ref[...])
        hb_s[...] = hhi
        logits = _dot(hhi, wrh_ref[...]) + _dot(hhi, wrl_ref[...]) + _dot(hlo, wrh_ref[...]) + br_ref[...]
        lane = lax.broadcasted_iota(jnp.int32, logits.shape, 1)
        m1 = jnp.max(logits, axis=1, keepdims=True)
        i1 = jnp.min(jnp.where(logits == m1, lane, LANES), axis=1, keepdims=True)
        rest = jnp.where(lane == i1, -jnp.inf, logits)
        m2 = jnp.max(rest, axis=1, keepdims=True)
        i2 = jnp.min(jnp.where(rest == m2, lane, LANES), axis=1, keepdims=True)
        e2 = jnp.exp(m2 - m1)
        den = 1.0 + e2
        gate = jnp.where(lane == i1, 1.0 / den, 0.0) + jnp.where(lane == i2, e2 / den, 0.0)
        routed = jnp.logical_or(lane == i1, lane == i2)
        ind = jnp.where(routed, 1.0, 0.0).astype(BF16)
        tri = jnp.where(lax.broadcasted_iota(jnp.int32, (tm, tm), 0) >= lax.broadcasted_iota(jnp.int32, (tm, tm), 1),
                        1.0, 0.0).astype(BF16)
        slot = jnp.where(routed, _dot(tri, ind) - 1.0, -1.0)
        eye = jnp.where(lax.broadcasted_iota(jnp.int32, (LANES, LANES), 0)
                        == lax.broadcasted_iota(jnp.int32, (LANES, LANES), 1), 1.0, 0.0).astype(BF16)
        ind_t = _dot_t(eye, ind)
        slot_t = jnp.where(ind_t > 0.0, _dot_t(ind_t.astype(BF16), tri) - 1.0, -1.0)
        slot_t_s[...] = slot_t[0:slot_t_s.shape[0], :]
        for ex in range(n_exp):
            pick = lambda x: jnp.sum(jnp.where(lane == ex, x, 0.0), axis=1, keepdims=True)
            gate_s[ex] = jnp.broadcast_to(pick(gate), gate_s.shape[1:])
            slot_s[ex] = jnp.broadcast_to(pick(slot), slot_s.shape[1:])
        o_ref[...] = jnp.zeros(o_ref.shape, F32)

    slot_b = slot_s[e]
    n_pass = ((jnp.max(slot_b) + 1.0).astype(jnp.int32) + cr - 1) // cr

    def pass_rows(r):
        return pl.ds(pl.multiple_of(r * cr, 16), cr)

    @pl.when(f == 0)
    def _():
        srow = slot_t_s[pl.ds(e, 1), :]

        def pack(r, _):
            want = (r * cr + lax.broadcasted_iota(jnp.int32, (cr, tm), 0)).astype(F32)
            onehot = jnp.where(srow == want, 1.0, 0.0).astype(BF16)
            xg_s[pass_rows(r), :] = _dot(onehot, hb_s[...]).astype(BF16)
            return 0
        lax.fori_loop(0, n_pass, pack, 0)

    def expert_pass(r, _):
        x = xg_s[pass_rows(r), :]
        a = _dot(x, wg_ref[...])
        act = (a * jax.nn.sigmoid(a)) * _dot(x, wu_ref[...])
        y = _dot(act.astype(BF16), wd_ref[...])

        @pl.when(f == 0)
        def _():
            y_s[pass_rows(r), :] = y

        @pl.when(f > 0)
        def _():
            y_s[pass_rows(r), :] = y_s[pass_rows(r), :] + y
        return 0
    lax.fori_loop(0, n_pass, expert_pass, 0)

    @pl.when(f == nf - 1)
    def _():
        reps = (cr + LANES - 1) // LANES
        slot_w = jnp.concatenate([slot_b] * reps, axis=1)[:, :cr]
        gate_w = jnp.concatenate([gate_s[e]] * (dm // LANES), axis=1)

        def unpack(r, _):
            want = (r * cr + lax.broadcasted_iota(jnp.int32, (tm, cr), 1)).astype(F32)
            onehot = jnp.where(slot_w == want, 1.0, 0.0).astype(BF16)
            o_ref[...] += gate_w * _dot(onehot, y_s[pass_rows(r), :].astype(BF16))
            return 0
        lax.fori_loop(0, n_pass, unpack, 0)

    @pl.when(last)
    def _():
        o_ref[...] = _layer_norm(alpha * h_ref[...] + o_ref[...], g2_ref[...], b2_ref[...])


def _moe(h, wrh, wrl, br, wg, wu, wd, g2, b2, tm, tf, alpha):
    m, dm = h.shape
    n_exp, _, dff = wg.shape
    cr = min(MOE_CHUNK_ROWS, tm)
    cap = -(-tm // cr) * cr
    c2 = lambda a: pl.BlockSpec(a.shape, lambda i, e, f: (0, 0))
    return pl.pallas_call(
        functools.partial(_moe_kernel, alpha=alpha, n_exp=n_exp, cr=cr),
        grid=(m // tm, n_exp, dff // tf),
        in_specs=[pl.BlockSpec((tm, dm), lambda i, e, f: (i, 0)),
                  c2(wrh), c2(wrl), c2(br),
                  pl.BlockSpec((None, dm, tf), lambda i, e, f: (e, 0, f)),
                  pl.BlockSpec((None, dm, tf), lambda i, e, f: (e, 0, f)),
                  pl.BlockSpec((None, tf, dm), lambda i, e, f: (e, f, 0)),
                  c2(g2), c2(b2)],
        out_specs=pl.BlockSpec((tm, dm), lambda i, e, f: (i, 0)),
        out_shape=jax.ShapeDtypeStruct((m, dm), F32),
        scratch_shapes=[pltpu.VMEM((tm, dm), BF16),
                        pltpu.VMEM((cap, dm), BF16),
                        pltpu.VMEM((cap, dm), F32),
                        pltpu.VMEM((n_exp, tm, LANES), F32),
                        pltpu.VMEM((n_exp, tm, LANES), F32),
                        pltpu.VMEM((max(8, n_exp), tm), F32)],
        compiler_params=_params("parallel", "arbitrary", "arbitrary"),
        name="moe",
    )(h, wrh, wrl, br, wg, wu, wd, g2, b2)


def _row_tile(m, target):
    t = math.gcd(m, target)
    return t if t % 8 == 0 else m


def _chunk_len(t):
    return math.gcd(t, 128)


def _rope_tables(pos, reps):
    half = HEAD_DIM // 2
    inv = ROPE_THETA ** (-(jnp.arange(half, dtype=F32) / half))
    ang = pos.astype(F32)[:, None] * inv[None, :]
    cos, sin = jnp.cos(ang), jnp.sin(ang)
    cos_h = jnp.concatenate([cos, cos], axis=-1)
    sin_h = jnp.concatenate([-sin, sin], axis=-1)
    return jnp.tile(cos_h, (1, reps)), jnp.tile(sin_h, (1, reps))


def _s5_params(a_re, a_im, b_re, b_im, c_re, c_im, log_dt):
    ng, ns = a_re.shape
    lr, li = a_re.astype(F32), a_im.astype(F32)
    dt = jnp.exp(log_dt.astype(F32))[:, None]
    mag = jnp.exp(lr * dt)
    ar, ai = mag * jnp.cos(li * dt), mag * jnp.sin(li * dt)
    den = lr * lr + li * li
    qr = ((ar - 1.0) * lr + ai * li) / den
    qi = (ai * lr - (ar - 1.0) * li) / den
    br, bi = b_re.astype(F32), b_im.astype(F32)
    bbr = qr[..., None] * br - qi[..., None] * bi
    bbi = qr[..., None] * bi + qi[..., None] * br
    eye = jnp.eye(ng, dtype=F32)
    gpl = LANES // ns
    nslab = ng // gpl

    def b_mat(x):
        full = jnp.einsum('gpc,gh->gchp', x, eye).reshape(ng * S5_CH, ng * ns)
        return jnp.transpose(full.reshape(ng * S5_CH, nslab, LANES), (1, 0, 2)).astype(BF16)

    def c_mat(x):
        return jnp.einsum('gcp,gh->gphc', x, eye).reshape(nslab, LANES, ng * S5_CH)

    ccat = jnp.concatenate([c_mat(c_re.astype(F32)), -c_mat(c_im.astype(F32))], axis=1).astype(BF16)
    return b_mat(bbr), b_mat(bbi), ar.reshape(nslab, 1, LANES), ai.reshape(nslab, 1, LANES), ccat


def _to_chunks(x, nl):
    nb, t, w = x.shape
    return jnp.swapaxes(x.reshape(nb, t // nl, nl, w), 1, 2)


def _from_chunks(x):
    nb, nl, nr, w = x.shape
    return jnp.swapaxes(x, 1, 2).reshape(nb, nr * nl, w)


def _mixers(lp, zmix, nb, t, s5_re0, s5_im0, lru_h0, lru_buf0, cc_buf0, chained):
    w = lp['gw']
    zmix3 = zmix.reshape(nb, t, zmix.shape[-1])
    tt = math.gcd(t, 512)
    a, b, ycc, lru_buf, cc_buf = _convmix(zmix3, lru_buf0, cc_buf0, lp['lru_conv_w'], lp['lru_conv_b'],
                                          lp['lru_wg'], lp['lru_bg'], lp['lru_sp'], lp['cc_dw_w'], lp['cc_dw_b'],
                                          lp['cc_ln_g'], lp['cc_ln_b'], tt)
    u = zmix3[:, :, :w].astype(BF16)
    ns = s5_re0.shape[-2] * s5_re0.shape[-1]
    if chained:
        nl = _chunk_len(t)
        a4, b4, u4 = _to_chunks(a, nl), _to_chunks(b, nl), _to_chunks(u, nl)
        h0 = lru_h0.reshape(nb, 1, w)
        s0r, s0i = s5_re0.reshape(nb, 1, ns), s5_im0.reshape(nb, 1, ns)
    else:
        to_rows = lambda x: jnp.swapaxes(x, 0, 1)[None]
        a4, b4, u4 = to_rows(a), to_rows(b), to_rows(u)
        h0 = lru_h0.reshape(1, nb, w)
        s0r, s0i = s5_re0.reshape(1, nb, ns), s5_im0.reshape(1, nb, ns)
    h4, lru_h = _lru_scan(a4, b4, h0, chained)
    ys4, s5_re, s5_im = _s5_scan(u4, lp['s5_bre'], lp['s5_bim'], lp['s5_are'], lp['s5_aim'], lp['s5_ccat'],
                                 s0r, s0i, chained)
    if chained:
        h, ys = _from_chunks(h4), _from_chunks(ys4)
    else:
        h, ys = jnp.swapaxes(h4[0], 0, 1), jnp.swapaxes(ys4[0], 0, 1)
    states = (s5_re.reshape(s5_re0.shape), s5_im.reshape(s5_im0.shape), lru_h.reshape(lru_h0.shape),
              lru_buf, cc_buf)
    return ys.reshape(nb * t, w), h.reshape(nb * t, w), ycc.reshape(nb * t, w), states


def _idx3(x, order):
    hi, lo = _split3(x)
    parts = [hi if o == 'h' else lo for o in order]
    pad = jnp.zeros(x.shape[:-1] + (2 * LANES - len(order) * x.shape[-1],), BF16)
    return jnp.concatenate(parts + [pad], axis=-1)


def _attend_prompt(q, qi, k, v, kw, nb, t):
    w = q.shape[-1]
    topk = min(TOPK_MAX, t // 4)
    n_heads = w // HEAD_DIM
    nqb = t // Q_BLOCK

    def pair_blocks(x):
        f = x.shape[-1]
        x = x.reshape(nb, nqb, Q_BLOCK, x.shape[2] // 2, 2, f)
        return jnp.transpose(x, (0, 1, 3, 5, 4, 2)).reshape(nb, nqb, -1, 2 * Q_BLOCK)
    head_of_lane = jnp.arange(w) // HEAD_DIM
    q_masked = jnp.where(head_of_lane[None, None, None, :] == jnp.arange(n_heads)[None, None, :, None],
                         q.reshape(nb, t, 1, w), jnp.zeros((), q.dtype))
    qt = pair_blocks(q_masked)
    qi3 = _idx3(qi.reshape(nb, t, IDX_HEADS, IDX_DIM), 'hhl')
    qit = pair_blocks(qi3)
    wi = kw[:, IDX_DIM:IDX_DIM + IDX_HEADS].reshape(nb, t, IDX_HEADS)
    wit = jnp.pad(jnp.swapaxes(wi, 1, 2), ((0, 0), (0, 8 - IDX_HEADS), (0, 0)))
    kidx3 = _idx3(kw[:, :IDX_DIM].reshape(nb, t, IDX_DIM), 'hlh')
    kb = k.reshape(nb, t, w).astype(BF16)
    vt = jnp.swapaxes(v.reshape(nb, t, w), 1, 2).astype(BF16)
    kc = math.gcd(t // 2, 512)
    yt = _dsa_prompt(qt, qit, wit, kidx3, kb, vt, topk, kc)
    return jnp.swapaxes(yt, 1, 2).reshape(nb * t, w)


def _attend_sample(layer, q, qi, k, v, kw, nb, t, pool_k, pool_v, pool_kidx, page_table):
    w = q.shape[-1]
    n_heads = w // HEAD_DIM
    psz = pool_kidx.shape[2]
    past = page_table.shape[1] * psz
    topk = min(TOPK_MAX, (past + t) // 4)
    pad_q = lambda x: jnp.pad(x, ((0, 0), (0, 0), (0, QROWS - t), (0, 0)))

    qi_h = jnp.swapaxes(qi.reshape(nb, t, IDX_HEADS, IDX_DIM), 1, 2)
    qhi, qlo = _split3(pad_q(qi_h).reshape(nb, IDX_HEADS * QROWS, IDX_DIM))
    wi = kw[:, IDX_DIM:IDX_DIM + IDX_HEADS].reshape(nb, t, IDX_HEADS)
    wi_b = jnp.broadcast_to(pad_q(jnp.swapaxes(wi, 1, 2)[..., None]).reshape(nb, IDX_HEADS * QROWS, 1),
                            (nb, IDX_HEADS * QROWS, psz))
    new_page_t = lambda x: jnp.swapaxes(jnp.pad(x, ((0, 0), (0, psz - t), (0, 0))), 1, 2)
    kidx_new = new_page_t(kw[:, :IDX_DIM].reshape(nb, t, IDX_DIM))
    bias = _dsa_sample_select(layer, page_table, qhi, qlo, wi_b, kidx_new, jnp.swapaxes(pool_kidx, 2, 3),
                              topk, t)

    head_of_lane = jnp.arange(w) // HEAD_DIM
    q_rows = jnp.broadcast_to(pad_q(q.reshape(nb, 1, t, w)), (nb, n_heads, QROWS, w))
    q32 = jnp.where(head_of_lane[None, None, None, :] == jnp.arange(n_heads)[None, :, None, None],
                    q_rows, jnp.zeros_like(q_rows)).reshape(nb, n_heads * QROWS, w)
    k_new = new_page_t(k.reshape(nb, t, w))
    v_new = new_page_t(v.reshape(nb, t, w))
    pool_t = lambda x: jnp.transpose(x, (0, 1, 3, 4, 2)).reshape(x.shape[:2] + (w, psz))
    y = _dsa_sample_attend(layer, page_table, q32, k_new, v_new, bias, pool_t(pool_k), pool_t(pool_v))
    return y[:, :t, :].reshape(nb * t, w)


def _prep_layer(l, w_in, s5_a_re, s5_a_im, s5_b_re, s5_b_im, s5_c_re, s5_c_im, s5_d, s5_log_dt, s5_w_glu,
                s5_b_glu, lru_conv_w, lru_conv_b, lru_w_a, lru_b_a, lru_w_x, lru_b_x, lru_lambda, cc_dw_w,
                cc_dw_b, cc_ln_g, cc_ln_b, w_out, ln1_g, ln1_b, ln2_g, ln2_b):
    gw = s5_d.shape[1]
    d_in = w_in.shape[2]
    mix_w = 5 * gw
    pad_to = mix_w + 4 * gw + LANES
    assert d_in == mix_w + 4 * gw + IDX_DIM + IDX_HEADS and gw == IDX_HEADS * IDX_DIM
    row = lambda x: x[l].reshape(1, -1).astype(F32)
    nh, hd, _ = lru_w_a.shape[1:]
    eye = jnp.eye(nh, dtype=F32)
    bd = lambda wt: jnp.einsum('hij,hk->hikj', wt[l].astype(F32), eye).reshape(nh * hd, nh * hd)
    s5_bre, s5_bim, s5_are, s5_aim, s5_ccat = _s5_params(s5_a_re[l], s5_a_im[l], s5_b_re[l], s5_b_im[l],
                                                         s5_c_re[l], s5_c_im[l], s5_log_dt[l])
    return dict(
        gw=gw, mix_w=mix_w,
        w_in=jnp.pad(w_in[l], ((0, 0), (0, pad_to - d_in))).astype(BF16),
        s5_bre=s5_bre, s5_bim=s5_bim, s5_are=s5_are, s5_aim=s5_aim, s5_ccat=s5_ccat,
        s5_d=row(s5_d), s5_w_glu=s5_w_glu[l].astype(BF16), s5_b_glu=row(s5_b_glu),
        lru_conv_w=lru_conv_w[l].astype(F32), lru_conv_b=row(lru_conv_b),
        lru_wg=jnp.concatenate([bd(lru_w_a), bd(lru_w_x)], axis=1).astype(BF16),
        lru_bg=jnp.concatenate([row(lru_b_a), row(lru_b_x)], axis=1),
        lru_sp=jax.nn.softplus(-row(lru_lambda)),
        cc_dw_w=cc_dw_w[l].astype(F32), cc_dw_b=row(cc_dw_b), cc_ln_g=row(cc_ln_g), cc_ln_b=row(cc_ln_b),
        w_out=w_out[l].astype(BF16), ln1_g=row(ln1_g), ln1_b=row(ln1_b), ln2_g=row(ln2_g), ln2_b=row(ln2_b),
    )


def kernel(x_prompt, x_sample, cache_k, cache_v, cache_kidx, state_s5_re, state_s5_im, state_lru_h, state_lru_conv, state_cc_conv, page_table, w_in, s5_a_re, s5_a_im, s5_b_re, s5_b_im, s5_c_re, s5_c_im, s5_d, s5_log_dt, s5_w_glu, s5_b_glu, lru_conv_w, lru_conv_b, lru_w_a, lru_b_a, lru_w_x, lru_b_x, lru_lambda, cc_dw_w, cc_dw_b, cc_ln_g, cc_ln_b, w_out, ln1_g, ln1_b, ln2_g, ln2_b, ffn_w_gate, ffn_w_up, ffn_w_down, moe_w_router, moe_b_router, moe_w_gate, moe_w_up, moe_w_down):
    bp, tp, dm = x_prompt.shape
    bs, ts, _ = x_sample.shape
    depth = w_in.shape[0]
    past = page_table.shape[1] * cache_k.shape[2]
    alpha = (2.0 * depth) ** 0.25
    n_heads = cache_k.shape[3]
    n_exp = moe_w_router.shape[-1]
    gw = s5_d.shape[1]

    cos_p, sin_p = _rope_tables(jnp.arange(tp, dtype=jnp.int32), gw // HEAD_DIM)
    cos_s, sin_s = _rope_tables(past + jnp.arange(ts, dtype=jnp.int32), gw // HEAD_DIM)
    cos_s, sin_s = jnp.tile(cos_s, (bs, 1)), jnp.tile(sin_s, (bs, 1))

    tm_p = _row_tile(tp, 512)
    xp = x_prompt.reshape(bp * tp, dm)
    xs = x_sample.reshape(bs * ts, dm)
    zeros_p = lambda *shape: jnp.zeros((bp,) + shape, x_prompt.dtype)
    states_p, states_s = [], []

    for l in range(depth):
        lp = _prep_layer(l, w_in, s5_a_re, s5_a_im, s5_b_re, s5_b_im, s5_c_re, s5_c_im, s5_d, s5_log_dt,
                         s5_w_glu, s5_b_glu, lru_conv_w, lru_conv_b, lru_w_a, lru_b_a, lru_w_x, lru_b_x,
                         lru_lambda, cc_dw_w, cc_dw_b, cc_ln_g, cc_ln_b, w_out, ln1_g, ln1_b, ln2_g, ln2_b)
        mix_w = lp['mix_w']
        j = l // 2
        if l % 2 == 0:
            cw = (ffn_w_gate[j].astype(BF16), ffn_w_up[j].astype(BF16), ffn_w_down[j].astype(BF16))
        else:
            wr = jnp.pad(moe_w_router[j].astype(F32), ((0, 0), (0, LANES - n_exp)))
            wrh, wrl = _split3(wr)
            br = jnp.pad(moe_b_router[j].astype(F32), (0, LANES - n_exp), constant_values=NEG_BIG).reshape(1, LANES)
            cw = (wrh, wrl, br, moe_w_gate[j].astype(BF16), moe_w_up[j].astype(BF16), moe_w_down[j].astype(BF16))

        def channel(h, tm):
            if l % 2 == 0:
                return _ffn(h, *cw, lp['ln2_g'], lp['ln2_b'], tm, math.gcd(cw[0].shape[1], 256), alpha)
            return _moe(h, *cw, lp['ln2_g'], lp['ln2_b'], tm, math.gcd(cw[3].shape[2], 512), alpha)

        zmix, q, qi, k, v, kw = _in_proj(xp, lp['w_in'], cos_p, sin_p, tm_p, mix_w, gw)
        ys, h, ycc, st = _mixers(lp, zmix, bp, tp,
                                 zeros_p(*state_s5_re.shape[2:]), zeros_p(*state_s5_im.shape[2:]),
                                 zeros_p(*state_lru_h.shape[2:]), zeros_p(*state_lru_conv.shape[2:]),
                                 zeros_p(*state_cc_conv.shape[2:]), chained=True)
        yatt = _attend_prompt(q, qi, k, v, kw, bp, tp)
        h1 = _mixout(xp, zmix, ys, h, ycc, yatt, lp['s5_d'], lp['s5_w_glu'], lp['s5_b_glu'], lp['w_out'],
                     lp['ln1_g'], lp['ln1_b'], tm_p, alpha)
        xp = channel(h1, _row_tile(bp * tp, 1024))
        states_p.append((k.reshape(bp, tp, n_heads, HEAD_DIM), v.reshape(bp, tp, n_heads, HEAD_DIM),
                         kw[:, :IDX_DIM].reshape(bp, tp, IDX_DIM)) + st)

        ms = bs * ts
        zmix, q, qi, k, v, kw = _in_proj(xs, lp['w_in'], cos_s, sin_s, ms, mix_w, gw)
        ys, h, ycc, st = _mixers(lp, zmix, bs, ts, state_s5_re[l], state_s5_im[l], state_lru_h[l],
                                 state_lru_conv[l], state_cc_conv[l], chained=False)
        yatt = _attend_sample(l, q, qi, k, v, kw, bs, ts, cache_k, cache_v, cache_kidx, page_table)
        h1 = _mixout(xs, zmix, ys, h, ycc, yatt, lp['s5_d'], lp['s5_w_glu'], lp['s5_b_glu'], lp['w_out'],
                     lp['ln1_g'], lp['ln1_b'], ms, alpha)
        xs = channel(h1, ms)
        states_s.append((k.reshape(bs, ts, n_heads, HEAD_DIM), v.reshape(bs, ts, n_heads, HEAD_DIM),
                         kw[:, :IDX_DIM].reshape(bs, ts, IDX_DIM)) + st)

    new_p = [jnp.stack(col) for col in zip(*states_p)]
    new_s = [jnp.stack(col) for col in zip(*states_s)]
    out = [xp.reshape(bp, tp, dm), xs.reshape(bs, ts, dm)]
    for a, b in zip(new_p, new_s):
        out += [a, b]
    return tuple(out)
```

```python
import functools
import math

import jax
import jax.numpy as jnp
from jax import lax
from jax.experimental import pallas as pl
from jax.experimental.pallas import tpu as pltpu

F32 = jnp.float32
BF16 = jnp.bfloat16

S5_CH = 16
LRU_C = 8.0
HEAD_DIM = 64
IDX_DIM = 64
IDX_HEADS = 4
TOPK_MAX = 256
Q_BLOCK = 128
ROPE_THETA = 10000.0
MOE_TOP_K = 2
LN_EPS = 1e-5
ATT_SCALE = HEAD_DIM ** -0.5
IDX_SCALE = (IDX_DIM * IDX_HEADS) ** -0.5

LANES = 128
NEG_BIG = -1e30
VMEM_LIMIT = 56 * 1024 * 1024


def _params(*sem):
    return pltpu.CompilerParams(dimension_semantics=sem, vmem_limit_bytes=VMEM_LIMIT)


def _dot(a, b):
    return jnp.dot(a, b, preferred_element_type=F32)


def _dot_t(a, b):
    return lax.dot_general(a, b, (((1,), (1,)), ((), ())), preferred_element_type=F32)


def _split3(x):
    hi = x.astype(BF16)
    lo = (x - hi.astype(F32)).astype(BF16)
    return hi, lo


def _layer_norm(x, g, b):
    mu = jnp.mean(x, axis=-1, keepdims=True)
    xc = x - mu
    var = jnp.mean(xc * xc, axis=-1, keepdims=True)
    return xc * lax.rsqrt(var + LN_EPS) * g + b


def _rope(z, cos, sin):
    w = z.shape[-1]
    half = HEAD_DIM // 2
    lane = lax.broadcasted_iota(jnp.int32, z.shape, 1)
    first = (lane & (HEAD_DIM - 1)) < half
    partner = jnp.where(first, pltpu.roll(z, w - half, axis=1), pltpu.roll(z, half, axis=1))
    return z * cos + partner * sin


def _in_proj_kernel(x_ref, w_ref, cos_ref, sin_ref, zmix_ref, q_ref, qi_ref, k_ref, v_ref, kw_ref,
                    *, mix_w, gw):
    xb = x_ref[...].astype(BF16)
    cos = cos_ref[...]
    sin = sin_ref[...]

    def proj(lo, width):
        return _dot(xb, w_ref[:, lo:lo + width])

    zmix_ref[...] = proj(0, mix_w)
    o = mix_w
    q_ref[...] = (_rope(proj(o, gw), cos, sin) * ATT_SCALE).astype(BF16)
    k_ref[...] = _rope(proj(o + gw, gw), cos, sin)
    v_ref[...] = proj(o + 2 * gw, gw)
    qi_ref[...] = _rope(proj(o + 3 * gw, gw), cos, sin)
    kw = proj(o + 4 * gw, LANES)
    is_key = lax.broadcasted_iota(jnp.int32, kw.shape, 1) < IDX_DIM
    kw_ref[...] = _rope(kw, jnp.where(is_key, cos[:, :LANES], 1.0), jnp.where(is_key, sin[:, :LANES], 0.0))


def _in_proj(x2, w_pad, cos_t, sin_t, tm, mix_w, gw):
    m, d = x2.shape
    tab_blocks = cos_t.shape[0] // tm
    row = lambda i: (i, 0)
    tab = lambda i: (i % tab_blocks, 0)
    outs = [
        jax.ShapeDtypeStruct((m, mix_w), F32),
        jax.ShapeDtypeStruct((m, gw), BF16),
        jax.ShapeDtypeStruct((m, gw), F32),
        jax.ShapeDtypeStruct((m, gw), F32),
        jax.ShapeDtypeStruct((m, gw), F32),
        jax.ShapeDtypeStruct((m, LANES), F32),
    ]
    return pl.pallas_call(
        functools.partial(_in_proj_kernel, mix_w=mix_w, gw=gw),
        grid=(m // tm,),
        in_specs=[pl.BlockSpec((tm, d), row),
                  pl.BlockSpec(w_pad.shape, lambda i: (0, 0)),
                  pl.BlockSpec((tm, gw), tab),
                  pl.BlockSpec((tm, gw), tab)],
        out_specs=[pl.BlockSpec((tm, s.shape[1]), row) for s in outs],
        out_shape=outs,
        compiler_params=_params("parallel"),
        name="in_proj",
    )(x2, w_pad, cos_t, sin_t)


LRU_HALO = 8
CC_HALO = 32


def _convmix_kernel(xl_ref, ca_ref, cg_ref, lbuf_ref, cbuf_ref, lw_ref, lb_ref, wg_ref, bg_ref, sp_ref,
                    cw_ref, cb_ref, lng_ref, lnb_ref,
                    a_ref, b_ref, ycc_ref, lbuf_o, cbuf_o, xl_s, xc_s, *, tt):
    i = pl.program_id(1)
    kl = lw_ref.shape[0]
    kc = cw_ref.shape[0]
    w = xl_ref.shape[-1]
    l0 = LRU_HALO - (kl - 1)
    c0 = CC_HALO - (kc - 1)

    @pl.when(i == 0)
    def _():
        xl_s[l0:LRU_HALO, :] = lbuf_ref[...]
        xc_s[c0:CC_HALO, :] = cbuf_ref[...]

    xl_s[LRU_HALO:LRU_HALO + tt, :] = xl_ref[...]
    xc_s[CC_HALO:CC_HALO + tt, :] = ca_ref[...] * jax.nn.sigmoid(cg_ref[...])

    conv = xl_s[l0:l0 + tt, :] * lw_ref[0:1, :]
    for j in range(1, kl):
        conv = conv + xl_s[l0 + j:l0 + j + tt, :] * lw_ref[j:j + 1, :]
    conv = conv + lb_ref[...]
    gates = _dot(conv.astype(BF16), wg_ref[...]) + bg_ref[...]
    r = jax.nn.sigmoid(gates[:, :w])
    ig = jax.nn.sigmoid(gates[:, w:])
    log_a = (-LRU_C) * r * sp_ref[...]
    a = jnp.exp(log_a)
    a_ref[...] = a
    b_ref[...] = jnp.sqrt(-jnp.tanh(log_a) * (a * a + 1.0)) * (ig * conv)

    c = xc_s[c0:c0 + tt, :] * cw_ref[0:1, :]
    for j in range(1, kc):
        c = c + xc_s[c0 + j:c0 + j + tt, :] * cw_ref[j:j + 1, :]
    c = _layer_norm(c + cb_ref[...], lng_ref[...], lnb_ref[...])
    ycc_ref[...] = c * jax.nn.sigmoid(c)

    new_l = xl_s[l0 + tt:LRU_HALO + tt, :]
    new_c = xc_s[c0 + tt:CC_HALO + tt, :]
    xl_s[l0:LRU_HALO, :] = new_l
    xc_s[c0:CC_HALO, :] = new_c
    lbuf_o[...] = new_l
    cbuf_o[...] = new_c


def _convmix(zmix3, lbuf, cbuf, lw, lb, wg, bg, sp, cw, cb, lng, lnb, tt):
    nb, t, _ = zmix3.shape
    w = lw.shape[1]
    col = lambda c: pl.BlockSpec((None, tt, w), lambda b, i, c=c: (b, i, c))
    full2 = lambda a: pl.BlockSpec(a.shape, lambda b, i: (0, 0))
    per_b = lambda a: pl.BlockSpec((None,) + a.shape[1:], lambda b, i: (b, 0, 0))
    seq_out = jax.ShapeDtypeStruct((nb, t, w), F32)
    return pl.pallas_call(
        functools.partial(_convmix_kernel, tt=tt),
        grid=(nb, t // tt),
        in_specs=[col(1), col(3), col(4), per_b(lbuf), per_b(cbuf), full2(lw), full2(lb), full2(wg),
                  full2(bg), full2(sp), full2(cw), full2(cb), full2(lng), full2(lnb)],
        out_specs=[pl.BlockSpec((None, tt, w), lambda b, i: (b, i, 0))] * 3 + [per_b(lbuf), per_b(cbuf)],
        out_shape=[seq_out, seq_out, seq_out,
                   jax.ShapeDtypeStruct(lbuf.shape, F32), jax.ShapeDtypeStruct(cbuf.shape, F32)],
        scratch_shapes=[pltpu.VMEM((LRU_HALO + tt, w), F32), pltpu.VMEM((CC_HALO + tt, w), F32)],
        compiler_params=_params("parallel", "arbitrary"),
        name="convmix",
    )(zmix3, zmix3, zmix3, lbuf, cbuf, lw, lb, wg, bg, sp, cw, cb, lng, lnb)


def _lru_scan_kernel(a_ref, b_ref, h0_ref, h_ref, hl_ref, e_s, p_s, hi_s, *, chained):
    nl, nr, _ = a_ref.shape

    def run(init):
        def step(t, h):
            h = a_ref[t] * h + b_ref[t]
            h_ref[t] = h
            return h
        return lax.fori_loop(0, nl, step, init)

    if not chained:
        hl_ref[...] = run(h0_ref[...])
        return

    def local(t, c):
        h, p = c
        at = a_ref[t]
        return at * h + b_ref[t], at * p
    zeros = jnp.zeros(e_s.shape, F32)
    e, p = lax.fori_loop(0, nl, local, (zeros, zeros + 1.0))
    e_s[...] = e
    p_s[...] = p
    hi_s[0:1, :] = h0_ref[...]

    def carry(c, _):
        prev = pl.ds(c - 1, 1)
        hi_s[pl.ds(c, 1), :] = p_s[prev, :] * hi_s[prev, :] + e_s[prev, :]
        return 0
    lax.fori_loop(1, nr, carry, 0)
    last = run(hi_s[...])
    hl_ref[...] = last[nr - 1:nr, :]


def _lru_scan(a4, b4, h0, chained):
    nb, nl, nr, w = a4.shape
    r0 = h0.shape[1]
    nw = w // LANES
    blk = pl.BlockSpec((None, nl, nr, LANES), lambda n, j: (n, 0, 0, j))
    st = pl.BlockSpec((None, r0, LANES), lambda n, j: (n, 0, j))
    return pl.pallas_call(
        functools.partial(_lru_scan_kernel, chained=chained),
        grid=(nb, nw),
        in_specs=[blk, blk, st],
        out_specs=[blk, st],
        out_shape=[jax.ShapeDtypeStruct(a4.shape, F32), jax.ShapeDtypeStruct(h0.shape, F32)],
        scratch_shapes=[pltpu.VMEM((nr, LANES), F32)] * 3,
        compiler_params=_params("parallel", "parallel"),
        name="lru_scan",
    )(a4, b4, h0)


def _s5_scan_kernel(u_ref, bre_ref, bim_ref, are_ref, aim_ref, cc_ref, h0r_ref, h0i_ref,
                    ys_ref, hlr_ref, hli_ref, br_s, bi_s, er_s, ei_s, hr_s, hi_s, *, chained, rows_per_dot):
    g = pl.program_id(1)
    nl, nr, _ = u_ref.shape
    n = nl * nr
    ar = are_ref[...]
    ai = aim_ref[...]

    def drive(c, _):
        r0 = pl.multiple_of(c * rows_per_dot, rows_per_dot)
        ub = u_ref[pl.ds(c * (rows_per_dot // nr), rows_per_dot // nr)].reshape(rows_per_dot, u_ref.shape[-1])
        br_s[pl.ds(r0, rows_per_dot), :] = _dot(ub, bre_ref[...])
        bi_s[pl.ds(r0, rows_per_dot), :] = _dot(ub, bim_ref[...])
        return 0
    lax.fori_loop(0, n // rows_per_dot, drive, 0)

    def step_fn(store):
        def step(t, c):
            hr, hi = c
            rows = pl.ds(pl.multiple_of(t * nr, nr), nr)
            nhr = ar * hr - ai * hi + br_s[rows, :]
            nhi = ar * hi + ai * hr + bi_s[rows, :]
            if store:
                br_s[rows, :] = nhr
                bi_s[rows, :] = nhi
            return nhr, nhi
        return step

    if chained:
        zeros = jnp.zeros((nr, LANES), F32)
        er, ei = lax.fori_loop(0, nl, step_fn(False), (zeros, zeros))
        er_s[...] = er
        ei_s[...] = ei

        def power(_, c):
            pr, pi = c
            return ar * pr - ai * pi, ar * pi + ai * pr
        alr, ali = lax.fori_loop(0, nl - 1, power, (ar, ai))
        hr_s[0:1, :] = h0r_ref[...]
        hi_s[0:1, :] = h0i_ref[...]

        def carry(c, _):
            prev = pl.ds(c - 1, 1)
            pr = hr_s[prev, :]
            pi = hi_s[prev, :]
            hr_s[pl.ds(c, 1), :] = alr * pr - ali * pi + er_s[prev, :]
            hi_s[pl.ds(c, 1), :] = alr * pi + ali * pr + ei_s[prev, :]
            return 0
        lax.fori_loop(1, nr, carry, 0)
        init = (hr_s[...], hi_s[...])
    else:
        init = (h0r_ref[...], h0i_ref[...])

    lr, li = lax.fori_loop(0, nl, step_fn(True), init)
    if chained:
        hlr_ref[...] = lr[nr - 1:nr, :]
        hli_ref[...] = li[nr - 1:nr, :]
    else:
        hlr_ref[...] = lr
        hli_ref[...] = li

    def project(c, _):
        r0 = pl.multiple_of(c * rows_per_dot, rows_per_dot)
        rows = pl.ds(r0, rows_per_dot)
        hcat = jnp.concatenate([br_s[rows, :], bi_s[rows, :]], axis=1).astype(BF16)
        y = _dot(hcat, cc_ref[...]).reshape(rows_per_dot // nr, nr, ys_ref.shape[-1])
        tsl = pl.ds(c * (rows_per_dot // nr), rows_per_dot // nr)

        @pl.when(g == 0)
        def _():
            ys_ref[tsl] = y

        @pl.when(g > 0)
        def _():
            ys_ref[tsl] = ys_ref[tsl] + y
        return 0
    lax.fori_loop(0, n // rows_per_dot, project, 0)


def _s5_scan(u4, bre, bim, are, aim, ccat, h0r, h0i, chained):
    nb, nl, nr, w = u4.shape
    ng = bre.shape[0]
    r0 = h0r.shape[1]
    n = nl * nr
    rows_per_dot = math.gcd(n, 1024)
    rows_per_dot = max(rows_per_dot, nr)
    seq = pl.BlockSpec((None, nl, nr, w), lambda b, g: (b, 0, 0, 0))
    per_g = lambda a: pl.BlockSpec((None,) + a.shape[1:], lambda b, g: (g, 0, 0))
    st = pl.BlockSpec((None, r0, LANES), lambda b, g: (b, 0, g))
    small = pltpu.VMEM((nr, LANES), F32)
    return pl.pallas_call(
        functools.partial(_s5_scan_kernel, chained=chained, rows_per_dot=rows_per_dot),
        grid=(nb, ng),
        in_specs=[seq, per_g(bre), per_g(bim), per_g(are), per_g(aim), per_g(ccat), st, st],
        out_specs=[seq, st, st],
        out_shape=[jax.ShapeDtypeStruct(u4.shape, F32), jax.ShapeDtypeStruct(h0r.shape, F32),
                   jax.ShapeDtypeStruct(h0i.shape, F32)],
        scratch_shapes=[pltpu.VMEM((n, LANES), F32), pltpu.VMEM((n, LANES), F32), small, small, small, small],
        compiler_params=_params("parallel", "arbitrary"),
        name="s5_scan",
    )(u4, bre, bim, are, aim, ccat, h0r, h0i)


BISECT_ITERS = 24


def _select_threshold(count_ge, max_below, mn, mx, kq, all_sel):
    inf = jnp.full(mn.shape, jnp.inf, F32)
    zero = jnp.zeros(mn.shape, F32)
    state = (mn, inf, mn, jnp.where(all_sel, 1.0, zero), zero)

    def unfinished(st):
        return jnp.min(st[3]) == 0.0

    def bisect(c):
        it, (lo, hi, theta, done, tie) = c
        mid = jnp.where(hi == jnp.inf, mx, 0.5 * lo + 0.5 * hi)
        cnt = count_ge(mid)
        live = done == 0.0
        collapsed = jnp.logical_and(live, jnp.logical_or(mid == lo, mid == hi))
        hit = jnp.logical_and(live, cnt == kq)
        move = jnp.logical_and(live, jnp.logical_not(collapsed))
        theta = jnp.where(hit, mid, jnp.where(collapsed, hi, theta))
        tie = jnp.where(collapsed, 1.0, tie)
        done = jnp.where(jnp.logical_or(hit, collapsed), 1.0, done)
        lo = jnp.where(jnp.logical_and(move, cnt > kq), mid, lo)
        hi = jnp.where(jnp.logical_and(move, cnt < kq), mid, hi)
        return it + 1, (lo, hi, theta, done, tie)

    _, state = lax.while_loop(lambda c: jnp.logical_and(c[0] < BISECT_ITERS, unfinished(c[1])),
                              bisect, (jnp.int32(0), state))

    def exact_step(st):
        lo, hi, theta, done, tie = st
        live = done == 0.0
        tau = max_below(hi)
        cnt = count_ge(tau)
        found = jnp.logical_and(live, cnt >= kq)
        clean = jnp.logical_and(found, cnt == kq)
        theta = jnp.where(clean, tau, jnp.where(found, hi, theta))
        tie = jnp.where(jnp.logical_and(found, jnp.logical_not(clean)), 1.0, tie)
        lo = jnp.where(found, tau, lo)
        hi = jnp.where(jnp.logical_and(live, jnp.logical_not(found)), tau, hi)
        done = jnp.where(found, 1.0, done)
        return lo, hi, theta, done, tie

    lo, hi, theta, done, tie = lax.while_loop(unfinished, exact_step, state)
    return theta, lo, tie


COUNT_ROWS = 64


def _dsa_prompt_kernel(qt_ref, qit_ref, wit_ref, kidx_ref, k_ref, vt_ref, o_ref, s_scr, lg_a, lg_b, *, topk, kc):
    i = pl.program_id(1)
    qb = Q_BLOCK
    width = k_ref.shape[1]
    n_pairs = qt_ref.shape[0] // width
    nch = ((i + 1) * qb + kc - 1) // kc
    qpos = i * qb + lax.broadcasted_iota(jnp.int32, (1, qb), 1)
    kiota = lax.broadcasted_iota(jnp.int32, (kc, qb), 0)

    def chunk_rows(c):
        return pl.ds(pl.multiple_of(c * kc, kc), kc)

    def score_chunk(c, carry):
        mn, mx = carry
        kk = kidx_ref[chunk_rows(c), :]
        s = jnp.zeros((kc, qb), F32)
        for p in range(n_pairs):
            sp = _dot(kk, qit_ref[p * width:(p + 1) * width, :])
            for hh in range(2):
                h = 2 * p + hh
                s = s + jnp.maximum(sp[:, hh * qb:(hh + 1) * qb], 0.0) * wit_ref[h:h + 1, :]
        s = s * IDX_SCALE
        vis = (c * kc + kiota) <= qpos
        s_scr[chunk_rows(c), :] = jnp.where(vis, s, -jnp.inf)
        mn = jnp.minimum(mn, jnp.min(jnp.where(vis, s, jnp.inf), axis=0, keepdims=True))
        mx = jnp.maximum(mx, jnp.max(jnp.where(vis, s, -jnp.inf), axis=0, keepdims=True))
        return mn, mx
    nch2 = (nch + 1) // 2
    mn, mx = lax.fori_loop(0, 2 * nch2, score_chunk,
                           (jnp.full((1, qb), jnp.inf, F32), jnp.full((1, qb), -jnp.inf, F32)))

    def fold(x, op):
        return op(x.reshape(kc // COUNT_ROWS, COUNT_ROWS, qb), axis=0)

    def count(pred_fn):
        def body(c, acc):
            hit = jnp.where(pred_fn(s_scr[chunk_rows(c), :], c * kc + kiota), 1.0, 0.0)
            return acc + fold(hit, jnp.sum)
        acc = lax.fori_loop(0, nch, body, jnp.zeros((COUNT_ROWS, qb), F32))
        return jnp.sum(acc, axis=0, keepdims=True)

    def max_below(x):
        def body(c, acc):
            s = s_scr[chunk_rows(c), :]
            return jnp.maximum(acc, fold(jnp.where(s < x, s, -jnp.inf), jnp.max))
        acc = lax.fori_loop(0, nch, body, jnp.full((COUNT_ROWS, qb), -jnp.inf, F32))
        return jnp.max(acc, axis=0, keepdims=True)

    n_vis = (qpos + 1).astype(F32)
    kq = jnp.minimum(n_vis, float(topk))
    theta, lo, tie = _select_threshold(lambda x: count(lambda s, kpos: s >= x), max_below,
                                       mn, mx, kq, n_vis <= float(topk))

    @pl.when(jnp.max(tie) > 0.0)
    def _():
        need = kq - count(lambda s, kpos: s >= theta)
        tied = lambda s: jnp.logical_and(tie > 0.0, s == lo)
        tri = jnp.where(lax.broadcasted_iota(jnp.int32, (kc, kc), 0) >= lax.broadcasted_iota(jnp.int32, (kc, kc), 1),
                        1.0, 0.0).astype(BF16)

        def promote(c, seen):
            s = s_scr[chunk_rows(c), :]
            is_tied = tied(s)
            rank = seen + _dot(tri, jnp.where(is_tied, 1.0, 0.0).astype(BF16))
            keep = jnp.logical_and(is_tied, rank <= need)
            s_scr[chunk_rows(c), :] = jnp.where(keep, theta, s)
            return rank[kc - 1:kc, :]
        lax.fori_loop(0, nch, promote, jnp.zeros((1, qb), F32))

    n_heads = 2 * n_pairs
    kc2 = 2 * kc

    def chunk2_rows(c):
        return pl.ds(pl.multiple_of(jnp.minimum(c, nch2 - 1) * kc2, kc2), kc2)

    def put_logits(dst, c):
        kch = k_ref[chunk2_rows(c), :]
        for p in range(n_pairs):
            dst[p] = _dot(kch, qt_ref[p * width:(p + 1) * width, :])

    def absorb(src, c, carry):
        ms, ls, accs = carry
        rows2 = chunk2_rows(c)
        sel = jnp.logical_and(s_scr[rows2, :] >= theta, c < nch2)
        new_m, new_l, new_acc = [], [], []
        for p in range(n_pairs):
            for hh in range(2):
                h = 2 * p + hh
                logit = jnp.where(sel, src[p, :, hh * qb:(hh + 1) * qb], NEG_BIG)
                m_new = jnp.maximum(ms[h], jnp.max(logit, axis=0, keepdims=True))
                alpha = jnp.exp(ms[h] - m_new)
                pr = jnp.exp(logit - m_new)
                new_m.append(m_new)
                new_l.append(alpha * ls[h] + jnp.sum(pr, axis=0, keepdims=True))
                pv = _dot(vt_ref[h * HEAD_DIM:(h + 1) * HEAD_DIM, rows2], pr.astype(BF16))
                new_acc.append(alpha * accs[h] + pv)
        return tuple(new_m), tuple(new_l), tuple(new_acc)

    def att_pair(j, carry):
        c = 2 * j
        put_logits(lg_b, c + 1)
        carry = absorb(lg_a, c, carry)
        put_logits(lg_a, c + 2)
        return absorb(lg_b, c + 1, carry)

    init = (tuple(jnp.full((1, qb), NEG_BIG, F32) for _ in range(n_heads)),
            tuple(jnp.zeros((1, qb), F32) for _ in range(n_heads)),
            tuple(jnp.zeros((HEAD_DIM, qb), F32) for _ in range(n_heads)))
    put_logits(lg_a, 0)
    _, ls, accs = lax.fori_loop(0, (nch2 + 1) // 2, att_pair, init)
    for h in range(n_heads):
        o_ref[h * HEAD_DIM:(h + 1) * HEAD_DIM, :] = accs[h] / ls[h]


def _dsa_prompt(qt2, qit2, wit, kidx3, kb, vt, topk, kc):
    nb, w, t = vt.shape
    per_b = lambda a: pl.BlockSpec((None,) + a.shape[1:], lambda b, i: (b, 0, 0))
    qblk = lambda a: pl.BlockSpec((None, None) + a.shape[2:], lambda b, i: (b, i, 0, 0))
    return pl.pallas_call(
        functools.partial(_dsa_prompt_kernel, topk=topk, kc=kc),
        grid=(nb, t // Q_BLOCK),
        in_specs=[qblk(qt2), qblk(qit2), pl.BlockSpec((None, wit.shape[1], Q_BLOCK), lambda b, i: (b, 0, i)),
                  per_b(kidx3), per_b(kb), per_b(vt)],
        out_specs=pl.BlockSpec((None, w, Q_BLOCK), lambda b, i: (b, 0, i)),
        out_shape=jax.ShapeDtypeStruct((nb, w, t), F32),
        scratch_shapes=[pltpu.VMEM((t, Q_BLOCK), F32)]
        + [pltpu.VMEM((qt2.shape[2] // kb.shape[2], 2 * kc, 2 * Q_BLOCK), F32)] * 2,
        compiler_params=_params("parallel", "arbitrary"),
        name="dsa_prompt",
    )(qt2, qit2, wit, kidx3, kb, vt)


SAMPLE_PAGES_PER_STEP = 16
QROWS = 8


def _dsa_sample_select_kernel(pt_ref, qhi_ref, qlo_ref, wi_ref, knew_ref, *rest, topk, n_new, gp):
    pages = rest[:gp]
    bias_ref, s_scr = rest[gp], rest[gp + 1]
    j = pl.program_id(1)
    nj = pl.num_programs(1)
    psz = pages[0].shape[1]
    past = nj * gp * psz
    qhi = qhi_ref[...]
    qlo = qlo_ref[...]
    wi = wi_ref[...]

    def scores(kpage_t):
        khi, klo = _split3(kpage_t)
        s = _dot(qhi, khi) + _dot(qhi, klo) + _dot(qlo, khi)
        s = jnp.maximum(s, 0.0) * wi
        tot = s[0:QROWS, :]
        for h in range(1, IDX_HEADS):
            tot = tot + s[h * QROWS:(h + 1) * QROWS, :]
        return tot * IDX_SCALE

    for g in range(gp):
        off = pl.multiple_of((j * gp + g) * psz, psz)
        s_scr[:, pl.ds(off, psz)] = scores(pages[g][...])

    @pl.when(j == nj - 1)
    def _():
        qrow = lax.broadcasted_iota(jnp.int32, (QROWS, psz), 0)
        kcol = lax.broadcasted_iota(jnp.int32, (QROWS, psz), 1)
        vis_new = jnp.logical_and(kcol <= qrow, kcol < n_new)
        s_scr[:, past:past + psz] = jnp.where(vis_new, scores(knew_ref[...]), -jnp.inf)
        s = s_scr[...]
        nk = s.shape[1]
        kpos = lax.broadcasted_iota(jnp.int32, s.shape, 1)
        qrow1 = lax.broadcasted_iota(jnp.int32, (QROWS, 1), 0)
        n_vis = (past + jnp.minimum(qrow1 + 1, n_new)).astype(F32)
        kq = jnp.minimum(n_vis, float(topk))
        fin = s > -jnp.inf
        mn = jnp.min(jnp.where(fin, s, jnp.inf), axis=1, keepdims=True)
        mx = jnp.max(s, axis=1, keepdims=True)
        count = lambda pred: jnp.sum(jnp.where(pred, 1.0, 0.0), axis=1, keepdims=True)
        theta, lo, tie = _select_threshold(
            lambda x: count(s >= x),
            lambda x: jnp.max(jnp.where(s < x, s, -jnp.inf), axis=1, keepdims=True),
            mn, mx, kq, jnp.logical_or(n_vis <= float(topk), qrow1 >= n_new))

        sel = s >= theta
        bias_ref[...] = jnp.where(sel, 0.0, NEG_BIG)

        @pl.when(jnp.max(tie) > 0.0)
        def _():
            need = kq - count(sel)
            tied = jnp.logical_and(tie > 0.0, s == lo)
            nbits = max(1, (nk - 1).bit_length())

            def jstep(b, jl):
                cand = jl + (jnp.int32(1) << (nbits - 1 - b))
                c_lt = count(jnp.logical_and(tied, kpos < cand))
                return jnp.where(c_lt < need, cand, jl)
            jlast = lax.fori_loop(0, nbits, jstep, jnp.zeros((QROWS, 1), jnp.int32))
            keep = jnp.logical_or(sel, jnp.logical_and(tied, kpos <= jlast))
            bias_ref[...] = jnp.where(keep, 0.0, NEG_BIG)


def _dsa_sample_select(layer, page_table, qhi, qlo, wi_b, knew, pool_kidx, topk, n_new):
    nb, n_pages = page_table.shape
    gp = math.gcd(n_pages, SAMPLE_PAGES_PER_STEP)
    di, psz = pool_kidx.shape[2], pool_kidx.shape[3]
    nk = (n_pages + 1) * psz
    per_b = lambda a: pl.BlockSpec((None,) + a.shape[1:], lambda b, j, pt: (b, 0, 0))
    page = lambda g: pl.BlockSpec((None, None, di, psz), lambda b, j, pt, g=g: (layer, pt[b, j * gp + g], 0, 0))
    grid_spec = pltpu.PrefetchScalarGridSpec(
        num_scalar_prefetch=1,
        grid=(nb, n_pages // gp),
        in_specs=[per_b(qhi), per_b(qlo), per_b(wi_b), per_b(knew)] + [page(g) for g in range(gp)],
        out_specs=pl.BlockSpec((None, QROWS, nk), lambda b, j, pt: (b, 0, 0)),
        scratch_shapes=[pltpu.VMEM((QROWS, nk), F32)],
    )
    return pl.pallas_call(
        functools.partial(_dsa_sample_select_kernel, topk=topk, n_new=n_new, gp=gp),
        grid_spec=grid_spec,
        out_shape=jax.ShapeDtypeStruct((nb, QROWS, nk), F32),
        compiler_params=_params("parallel", "arbitrary"),
        name="dsa_sample_select",
    )(page_table, qhi, qlo, wi_b, knew, *([pool_kidx] * gp))


def _dsa_sample_attend_kernel(pt_ref, q_ref, knew_ref, vnew_ref, *rest, gp):
    bias_refs = rest[:gp]
    kpages = rest[gp:2 * gp]
    vpages = rest[2 * gp:3 * gp]
    bnew_ref = rest[3 * gp]
    o_ref, m_s, l_s, acc_s = rest[3 * gp + 1:]
    j = pl.program_id(1)
    nj = pl.num_programs(1)
    q = q_ref[...]
    n_heads = q.shape[0] // QROWS

    @pl.when(j == 0)
    def _():
        m_s[...] = jnp.full(m_s.shape, NEG_BIG, F32)
        l_s[...] = jnp.zeros(l_s.shape, F32)
        acc_s[...] = jnp.zeros(acc_s.shape, F32)

    def absorb(kts, vts, biases):
        psz = kts[0].shape[1]
        logit = jnp.concatenate(
            [_dot(q, kt[...].astype(BF16)) + jnp.concatenate([b[...]] * n_heads, axis=0)
             for kt, b in zip(kts, biases)], axis=1)
        m = m_s[...]
        m_new = jnp.maximum(m, jnp.max(logit, axis=1, keepdims=True))
        alpha = jnp.exp(m - m_new)
        p = jnp.exp(logit - m_new)
        l_s[...] = alpha * l_s[...] + jnp.sum(p, axis=1, keepdims=True)
        pv = _dot_t(p[:, 0:psz].astype(BF16), vts[0][...].astype(BF16))
        for g in range(1, len(vts)):
            pv = pv + _dot_t(p[:, g * psz:(g + 1) * psz].astype(BF16), vts[g][...].astype(BF16))
        acc_s[...] = alpha * acc_s[...] + pv
        m_s[...] = m_new

    absorb(kpages, vpages, bias_refs)

    @pl.when(j == nj - 1)
    def _():
        absorb([knew_ref], [vnew_ref], [bnew_ref])
        out = acc_s[...] / l_s[...]
        lane = lax.broadcasted_iota(jnp.int32, (QROWS, out.shape[1]), 1)
        y = jnp.zeros((QROWS, out.shape[1]), F32)
        for h in range(n_heads):
            in_head = jnp.logical_and(lane >= h * HEAD_DIM, lane < (h + 1) * HEAD_DIM)
            y = y + jnp.where(in_head, out[h * QROWS:(h + 1) * QROWS, :], 0.0)
        o_ref[...] = y


def _dsa_sample_attend(layer, page_table, q32, knew, vnew, bias, pool_k, pool_v):
    nb, n_pages = page_table.shape
    gp = math.gcd(n_pages, SAMPLE_PAGES_PER_STEP)
    w, psz = pool_k.shape[2], pool_k.shape[3]
    per_b = lambda a: pl.BlockSpec((None,) + a.shape[1:], lambda b, j, pt: (b, 0, 0))
    page = lambda g: pl.BlockSpec((None, None, w, psz), lambda b, j, pt, g=g: (layer, pt[b, j * gp + g], 0, 0))
    bias_pg = lambda g: pl.BlockSpec((None, QROWS, psz), lambda b, j, pt, g=g: (b, 0, j * gp + g))
    bias_new = pl.BlockSpec((None, QROWS, psz), lambda b, j, pt: (b, 0, n_pages))
    grid_spec = pltpu.PrefetchScalarGridSpec(
        num_scalar_prefetch=1,
        grid=(nb, n_pages // gp),
        in_specs=([per_b(q32), per_b(knew), per_b(vnew)] + [bias_pg(g) for g in range(gp)]
                  + [page(g) for g in range(gp)] + [page(g) for g in range(gp)] + [bias_new]),
        out_specs=pl.BlockSpec((None, QROWS, w), lambda b, j, pt: (b, 0, 0)),
        scratch_shapes=[pltpu.VMEM((q32.shape[1], 1), F32), pltpu.VMEM((q32.shape[1], 1), F32),
                        pltpu.VMEM((q32.shape[1], w), F32)],
    )
    return pl.pallas_call(
        functools.partial(_dsa_sample_attend_kernel, gp=gp),
        grid_spec=grid_spec,
        out_shape=jax.ShapeDtypeStruct((nb, QROWS, w), F32),
        compiler_params=_params("parallel", "arbitrary"),
        name="dsa_sample_attend",
    )(page_table, q32, knew, vnew, *([bias] * gp), *([pool_k] * gp), *([pool_v] * gp), bias)


def _mixout_kernel(x_ref, u_ref, g_ref, ys_ref, h_ref, ycc_ref, yatt_ref, d_ref, wglu_ref, bglu_ref,
                   wout_ref, g1_ref, b1_ref, o_ref, *, alpha):
    w = u_ref.shape[-1]
    y = ys_ref[...] + d_ref[...] * u_ref[...]
    z = _dot(jax.nn.gelu(y).astype(BF16), wglu_ref[...]) + bglu_ref[...]
    y_s5 = z[:, :w] * jax.nn.sigmoid(z[:, w:])
    y_lru = h_ref[...] * jax.nn.gelu(g_ref[...])
    acc = _dot(y_s5.astype(BF16), wout_ref[0:w, :])
    acc = acc + _dot(y_lru.astype(BF16), wout_ref[w:2 * w, :])
    acc = acc + _dot(ycc_ref[...].astype(BF16), wout_ref[2 * w:3 * w, :])
    acc = acc + _dot(yatt_ref[...].astype(BF16), wout_ref[3 * w:4 * w, :])
    o_ref[...] = _layer_norm(alpha * x_ref[...] + acc, g1_ref[...], b1_ref[...])


def _mixout(x2, zmix, ys, h, ycc, yatt, d, wglu, bglu, wout, g1, b1, tm, alpha):
    m, dm = x2.shape
    w = ys.shape[1]
    row = lambda a: pl.BlockSpec((tm, a.shape[1]), lambda i: (i, 0))
    full = lambda a: pl.BlockSpec(a.shape, lambda i: (0, 0))
    zcol = lambda c: pl.BlockSpec((tm, w), lambda i, c=c: (i, c))
    return pl.pallas_call(
        functools.partial(_mixout_kernel, alpha=alpha),
        grid=(m // tm,),
        in_specs=[row(x2), zcol(0), zcol(2), row(ys), row(h), row(ycc), row(yatt), full(d), full(wglu),
                  full(bglu), full(wout), full(g1), full(b1)],
        out_specs=pl.BlockSpec((tm, dm), lambda i: (i, 0)),
        out_shape=jax.ShapeDtypeStruct((m, dm), F32),
        compiler_params=_params("parallel"),
        name="mixout",
    )(x2, zmix, zmix, ys, h, ycc, yatt, d, wglu, bglu, wout, g1, b1)


def _ffn_kernel(h_ref, wg_ref, wu_ref, wd_ref, g2_ref, b2_ref, o_ref, acc_s, hb_s, *, alpha):
    f = pl.program_id(1)

    @pl.when(f == 0)
    def _():
        acc_s[...] = jnp.zeros(acc_s.shape, F32)
        hb_s[...] = h_ref[...].astype(BF16)

    hb = hb_s[...]
    a = _dot(hb, wg_ref[...])
    act = (a * jax.nn.sigmoid(a)) * _dot(hb, wu_ref[...])
    acc_s[...] += _dot(act.astype(BF16), wd_ref[...])

    @pl.when(f == pl.num_programs(1) - 1)
    def _():
        o_ref[...] = _layer_norm(alpha * h_ref[...] + acc_s[...], g2_ref[...], b2_ref[...])


def _col_tiles(w, tf):
    dm, dff = w.shape[-2:]
    return jnp.swapaxes(w.reshape(w.shape[:-1] + (dff // tf, tf)), -3, -2)


def _ffn(h, wg, wu, wd, g2, b2, tm, alpha):
    m, dm = h.shape
    nf, _, tf = wg.shape
    return pl.pallas_call(
        functools.partial(_ffn_kernel, alpha=alpha),
        grid=(m // tm, nf),
        in_specs=[pl.BlockSpec((tm, dm), lambda i, f: (i, 0)),
                  pl.BlockSpec((None, dm, tf), lambda i, f: (f, 0, 0)),
                  pl.BlockSpec((None, dm, tf), lambda i, f: (f, 0, 0)),
                  pl.BlockSpec((tf, dm), lambda i, f: (f, 0)),
                  pl.BlockSpec(g2.shape, lambda i, f: (0, 0)),
                  pl.BlockSpec(b2.shape, lambda i, f: (0, 0))],
        out_specs=pl.BlockSpec((tm, dm), lambda i, f: (i, 0)),
        out_shape=jax.ShapeDtypeStruct((m, dm), F32),
        scratch_shapes=[pltpu.VMEM((tm, dm), F32), pltpu.VMEM((tm, dm), BF16)],
        compiler_params=_params("parallel", "arbitrary"),
        name="ffn",
    )(h, wg, wu, wd, g2, b2)


MOE_CHUNK_ROWS = 288


def _moe_kernel(h_ref, wrh_ref, wrl_ref, br_ref, wg_ref, wu_ref, wd_ref, g2_ref, b2_ref, o_ref,
                hb_s, xg_s, y_s, gate_s, slot_s, slot_t_s, *, alpha, n_exp, cr):
    e = pl.program_id(1)
    f = pl.program_id(2)
    nf = pl.num_programs(2)
    tm, dm = h_ref.shape
    first = jnp.logical_and(e == 0, f == 0)
    last = jnp.logical_and(e == pl.num_programs(1) - 1, f == nf - 1)

    @pl.when(first)
    def _():
        hhi, hlo = _split3(h_ref[...])
        hb_s[...] = hhi
        logits = _dot(hhi, wrh_ref[...]) + _dot(hhi, wrl_ref[...]) + _dot(hlo, wrh_ref[...]) + br_ref[...]
        lane = lax.broadcasted_iota(jnp.int32, logits.shape, 1)
        m1 = jnp.max(logits, axis=1, keepdims=True)
        i1 = jnp.min(jnp.where(logits == m1, lane, LANES), axis=1, keepdims=True)
        rest = jnp.where(lane == i1, -jnp.inf, logits)
        m2 = jnp.max(rest, axis=1, keepdims=True)
        i2 = jnp.min(jnp.where(rest == m2, lane, LANES), axis=1, keepdims=True)
        e2 = jnp.exp(m2 - m1)
        den = 1.0 + e2
        gate = jnp.where(lane == i1, 1.0 / den, 0.0) + jnp.where(lane == i2, e2 / den, 0.0)
        routed = jnp.logical_or(lane == i1, lane == i2)
        ind = jnp.where(routed, 1.0, 0.0).astype(BF16)
        tri = jnp.where(lax.broadcasted_iota(jnp.int32, (tm, tm), 0) >= lax.broadcasted_iota(jnp.int32, (tm, tm), 1),
                        1.0, 0.0).astype(BF16)
        slot = jnp.where(routed, _dot(tri, ind) - 1.0, -1.0)
        eye = jnp.where(lax.broadcasted_iota(jnp.int32, (LANES, LANES), 0)
                        == lax.broadcasted_iota(jnp.int32, (LANES, LANES), 1), 1.0, 0.0).astype(BF16)
        ind_t = _dot_t(eye, ind)
        slot_t = jnp.where(ind_t > 0.0, _dot_t(ind_t.astype(BF16), tri) - 1.0, -1.0)
        slot_t_s[...] = slot_t[0:slot_t_s.shape[0], :]
        for ex in range(n_exp):
            pick = lambda x: jnp.sum(jnp.where(lane == ex, x, 0.0), axis=1, keepdims=True)
            gate_s[ex] = jnp.broadcast_to(pick(gate), gate_s.shape[1:])
            slot_s[ex] = jnp.broadcast_to(pick(slot), slot_s.shape[1:])
        o_ref[...] = jnp.zeros(o_ref.shape, F32)

    slot_b = slot_s[e]
    n_pass = ((jnp.max(slot_b) + 1.0).astype(jnp.int32) + cr - 1) // cr

    def pass_rows(r):
        return pl.ds(pl.multiple_of(r * cr, 16), cr)

    @pl.when(f == 0)
    def _():
        srow = slot_t_s[pl.ds(e, 1), :]

        def pack(r, _):
            want = (r * cr + lax.broadcasted_iota(jnp.int32, (cr, tm), 0)).astype(F32)
            onehot = jnp.where(srow == want, 1.0, 0.0).astype(BF16)
            xg_s[pass_rows(r), :] = _dot(onehot, hb_s[...]).astype(BF16)
            return 0
        lax.fori_loop(0, n_pass, pack, 0)

    def expert_pass(r, _):
        x = xg_s[pass_rows(r), :]
        a = _dot(x, wg_ref[...])
        act = (a * jax.nn.sigmoid(a)) * _dot(x, wu_ref[...])
        y = _dot(act.astype(BF16), wd_ref[...])

        @pl.when(f == 0)
        def _():
            y_s[pass_rows(r), :] = y

        @pl.when(f > 0)
        def _():
            y_s[pass_rows(r), :] = y_s[pass_rows(r), :] + y
        return 0
    lax.fori_loop(0, n_pass, expert_pass, 0)

    @pl.when(f == nf - 1)
    def _():
        reps = (cr + LANES - 1) // LANES
        slot_w = jnp.concatenate([slot_b] * reps, axis=1)[:, :cr]
        gate_w = jnp.concatenate([gate_s[e]] * (dm // LANES), axis=1)

        def unpack(r, _):
            want = (r * cr + lax.broadcasted_iota(jnp.int32, (tm, cr), 1)).astype(F32)
            onehot = jnp.where(slot_w == want, 1.0, 0.0).astype(BF16)
            o_ref[...] += gate_w * _dot(onehot, y_s[pass_rows(r), :].astype(BF16))
            return 0
        lax.fori_loop(0, n_pass, unpack, 0)

    @pl.when(last)
    def _():
        o_ref[...] = _layer_norm(alpha * h_ref[...] + o_ref[...], g2_ref[...], b2_ref[...])


def _moe(h, wrh, wrl, br, wg, wu, wd, g2, b2, tm, alpha):
    m, dm = h.shape
    n_exp, nf, _, tf = wg.shape
    cr = min(MOE_CHUNK_ROWS, tm)
    cap = -(-tm // cr) * cr
    c2 = lambda a: pl.BlockSpec(a.shape, lambda i, e, f: (0, 0))
    return pl.pallas_call(
        functools.partial(_moe_kernel, alpha=alpha, n_exp=n_exp, cr=cr),
        grid=(m // tm, n_exp, nf),
        in_specs=[pl.BlockSpec((tm, dm), lambda i, e, f: (i, 0)),
                  c2(wrh), c2(wrl), c2(br),
                  pl.BlockSpec((None, None, dm, tf), lambda i, e, f: (e, f, 0, 0)),
                  pl.BlockSpec((None, None, dm, tf), lambda i, e, f: (e, f, 0, 0)),
                  pl.BlockSpec((None, tf, dm), lambda i, e, f: (e, f, 0)),
                  c2(g2), c2(b2)],
        out_specs=pl.BlockSpec((tm, dm), lambda i, e, f: (i, 0)),
        out_shape=jax.ShapeDtypeStruct((m, dm), F32),
        scratch_shapes=[pltpu.VMEM((tm, dm), BF16),
                        pltpu.VMEM((cap, dm), BF16),
                        pltpu.VMEM((cap, dm), F32),
                        pltpu.VMEM((n_exp, tm, LANES), F32),
                        pltpu.VMEM((n_exp, tm, LANES), F32),
                        pltpu.VMEM((max(8, n_exp), tm), F32)],
        compiler_params=_params("parallel", "arbitrary", "arbitrary"),
        name="moe",
    )(h, wrh, wrl, br, wg, wu, wd, g2, b2)


def _row_tile(m, target):
    t = math.gcd(m, target)
    return t if t % 8 == 0 else m


def _chunk_len(t):
    return math.gcd(t, 128)


def _rope_tables(pos, reps):
    half = HEAD_DIM // 2
    inv = ROPE_THETA ** (-(jnp.arange(half, dtype=F32) / half))
    ang = pos.astype(F32)[:, None] * inv[None, :]
    cos, sin = jnp.cos(ang), jnp.sin(ang)
    cos_h = jnp.concatenate([cos, cos], axis=-1)
    sin_h = jnp.concatenate([-sin, sin], axis=-1)
    return jnp.tile(cos_h, (1, reps)), jnp.tile(sin_h, (1, reps))


def _s5_params(a_re, a_im, b_re, b_im, c_re, c_im, log_dt):
    ng, ns = a_re.shape
    lr, li = a_re.astype(F32), a_im.astype(F32)
    dt = jnp.exp(log_dt.astype(F32))[:, None]
    mag = jnp.exp(lr * dt)
    ar, ai = mag * jnp.cos(li * dt), mag * jnp.sin(li * dt)
    den = lr * lr + li * li
    qr = ((ar - 1.0) * lr + ai * li) / den
    qi = (ai * lr - (ar - 1.0) * li) / den
    br, bi = b_re.astype(F32), b_im.astype(F32)
    bbr = qr[..., None] * br - qi[..., None] * bi
    bbi = qr[..., None] * bi + qi[..., None] * br
    eye = jnp.eye(ng, dtype=F32)
    gpl = LANES // ns
    nslab = ng // gpl

    def b_mat(x):
        full = jnp.einsum('gpc,gh->gchp', x, eye).reshape(ng * S5_CH, ng * ns)
        return jnp.transpose(full.reshape(ng * S5_CH, nslab, LANES), (1, 0, 2)).astype(BF16)

    def c_mat(x):
        return jnp.einsum('gcp,gh->gphc', x, eye).reshape(nslab, LANES, ng * S5_CH)

    ccat = jnp.concatenate([c_mat(c_re.astype(F32)), -c_mat(c_im.astype(F32))], axis=1).astype(BF16)
    return b_mat(bbr), b_mat(bbi), ar.reshape(nslab, 1, LANES), ai.reshape(nslab, 1, LANES), ccat


def _to_chunks(x, nl):
    nb, t, w = x.shape
    return jnp.swapaxes(x.reshape(nb, t // nl, nl, w), 1, 2)


def _from_chunks(x):
    nb, nl, nr, w = x.shape
    return jnp.swapaxes(x, 1, 2).reshape(nb, nr * nl, w)


def _mixers(lp, zmix, nb, t, s5_re0, s5_im0, lru_h0, lru_buf0, cc_buf0, chained):
    w = lp['gw']
    zmix3 = zmix.reshape(nb, t, zmix.shape[-1])
    tt = math.gcd(t, 512)
    a, b, ycc, lru_buf, cc_buf = _convmix(zmix3, lru_buf0, cc_buf0, lp['lru_conv_w'], lp['lru_conv_b'],
                                          lp['lru_wg'], lp['lru_bg'], lp['lru_sp'], lp['cc_dw_w'], lp['cc_dw_b'],
                                          lp['cc_ln_g'], lp['cc_ln_b'], tt)
    u = zmix3[:, :, :w].astype(BF16)
    ns = s5_re0.shape[-2] * s5_re0.shape[-1]
    if chained:
        nl = _chunk_len(t)
        a4, b4, u4 = _to_chunks(a, nl), _to_chunks(b, nl), _to_chunks(u, nl)
        h0 = lru_h0.reshape(nb, 1, w)
        s0r, s0i = s5_re0.reshape(nb, 1, ns), s5_im0.reshape(nb, 1, ns)
    else:
        to_rows = lambda x: jnp.swapaxes(x, 0, 1)[None]
        a4, b4, u4 = to_rows(a), to_rows(b), to_rows(u)
        h0 = lru_h0.reshape(1, nb, w)
        s0r, s0i = s5_re0.reshape(1, nb, ns), s5_im0.reshape(1, nb, ns)
    h4, lru_h = _lru_scan(a4, b4, h0, chained)
    ys4, s5_re, s5_im = _s5_scan(u4, lp['s5_bre'], lp['s5_bim'], lp['s5_are'], lp['s5_aim'], lp['s5_ccat'],
                                 s0r, s0i, chained)
    if chained:
        h, ys = _from_chunks(h4), _from_chunks(ys4)
    else:
        h, ys = jnp.swapaxes(h4[0], 0, 1), jnp.swapaxes(ys4[0], 0, 1)
    states = (s5_re.reshape(s5_re0.shape), s5_im.reshape(s5_im0.shape), lru_h.reshape(lru_h0.shape),
              lru_buf, cc_buf)
    return ys.reshape(nb * t, w), h.reshape(nb * t, w), ycc.reshape(nb * t, w), states


def _idx3(x, order):
    hi, lo = _split3(x)
    parts = [hi if o == 'h' else lo for o in order]
    pad = jnp.zeros(x.shape[:-1] + (2 * LANES - len(order) * x.shape[-1],), BF16)
    return jnp.concatenate(parts + [pad], axis=-1)


def _attend_prompt(q, qi, k, v, kw, nb, t):
    w = q.shape[-1]
    topk = min(TOPK_MAX, t // 4)
    n_heads = w // HEAD_DIM
    nqb = t // Q_BLOCK

    def pair_blocks(x):
        f = x.shape[-1]
        x = x.reshape(nb, nqb, Q_BLOCK, x.shape[2] // 2, 2, f)
        return jnp.transpose(x, (0, 1, 3, 5, 4, 2)).reshape(nb, nqb, -1, 2 * Q_BLOCK)
    head_of_lane = jnp.arange(w) // HEAD_DIM
    q_masked = jnp.where(head_of_lane[None, None, None, :] == jnp.arange(n_heads)[None, None, :, None],
                         q.reshape(nb, t, 1, w), jnp.zeros((), q.dtype))
    qt = pair_blocks(q_masked)
    qi3 = _idx3(qi.reshape(nb, t, IDX_HEADS, IDX_DIM), 'hhl')
    qit = pair_blocks(qi3)
    wi = kw[:, IDX_DIM:IDX_DIM + IDX_HEADS].reshape(nb, t, IDX_HEADS)
    wit = jnp.pad(jnp.swapaxes(wi, 1, 2), ((0, 0), (0, 8 - IDX_HEADS), (0, 0)))
    kidx3 = _idx3(kw[:, :IDX_DIM].reshape(nb, t, IDX_DIM), 'hlh')
    kb = k.reshape(nb, t, w).astype(BF16)
    vt = jnp.swapaxes(v.reshape(nb, t, w), 1, 2).astype(BF16)
    kc = math.gcd(t // 2, 512)
    yt = _dsa_prompt(qt, qit, wit, kidx3, kb, vt, topk, kc)
    return jnp.swapaxes(yt, 1, 2).reshape(nb * t, w)


def _attend_sample(layer, q, qi, k, v, kw, nb, t, pool_k, pool_v, pool_kidx, page_table):
    w = q.shape[-1]
    n_heads = w // HEAD_DIM
    psz = pool_kidx.shape[2]
    past = page_table.shape[1] * psz
    topk = min(TOPK_MAX, (past + t) // 4)
    pad_q = lambda x: jnp.pad(x, ((0, 0), (0, 0), (0, QROWS - t), (0, 0)))

    qi_h = jnp.swapaxes(qi.reshape(nb, t, IDX_HEADS, IDX_DIM), 1, 2)
    qhi, qlo = _split3(pad_q(qi_h).reshape(nb, IDX_HEADS * QROWS, IDX_DIM))
    wi = kw[:, IDX_DIM:IDX_DIM + IDX_HEADS].reshape(nb, t, IDX_HEADS)
    wi_b = jnp.broadcast_to(pad_q(jnp.swapaxes(wi, 1, 2)[..., None]).reshape(nb, IDX_HEADS * QROWS, 1),
                            (nb, IDX_HEADS * QROWS, psz))
    new_page_t = lambda x: jnp.swapaxes(jnp.pad(x, ((0, 0), (0, psz - t), (0, 0))), 1, 2)
    kidx_new = new_page_t(kw[:, :IDX_DIM].reshape(nb, t, IDX_DIM))
    bias = _dsa_sample_select(layer, page_table, qhi, qlo, wi_b, kidx_new, jnp.swapaxes(pool_kidx, 2, 3),
                              topk, t)

    head_of_lane = jnp.arange(w) // HEAD_DIM
    q_rows = jnp.broadcast_to(pad_q(q.reshape(nb, 1, t, w)), (nb, n_heads, QROWS, w))
    q32 = jnp.where(head_of_lane[None, None, None, :] == jnp.arange(n_heads)[None, :, None, None],
                    q_rows, jnp.zeros_like(q_rows)).reshape(nb, n_heads * QROWS, w)
    k_new = new_page_t(k.reshape(nb, t, w))
    v_new = new_page_t(v.reshape(nb, t, w))
    pool_t = lambda x: jnp.transpose(x, (0, 1, 3, 4, 2)).reshape(x.shape[:2] + (w, psz))
    y = _dsa_sample_attend(layer, page_table, q32, k_new, v_new, bias, pool_t(pool_k), pool_t(pool_v))
    return y[:, :t, :].reshape(nb * t, w)


def _prep_layer(l, w_in, s5_a_re, s5_a_im, s5_b_re, s5_b_im, s5_c_re, s5_c_im, s5_d, s5_log_dt, s5_w_glu,
                s5_b_glu, lru_conv_w, lru_conv_b, lru_w_a, lru_b_a, lru_w_x, lru_b_x, lru_lambda, cc_dw_w,
                cc_dw_b, cc_ln_g, cc_ln_b, w_out, ln1_g, ln1_b, ln2_g, ln2_b):
    gw = s5_d.shape[1]
    d_in = w_in.shape[2]
    mix_w = 5 * gw
    pad_to = mix_w + 4 * gw + LANES
    assert d_in == mix_w + 4 * gw + IDX_DIM + IDX_HEADS and gw == IDX_HEADS * IDX_DIM
    row = lambda x: x[l].reshape(1, -1).astype(F32)
    nh, hd, _ = lru_w_a.shape[1:]
    eye = jnp.eye(nh, dtype=F32)
    bd = lambda wt: jnp.einsum('hij,hk->hikj', wt[l].astype(F32), eye).reshape(nh * hd, nh * hd)
    s5_bre, s5_bim, s5_are, s5_aim, s5_ccat = _s5_params(s5_a_re[l], s5_a_im[l], s5_b_re[l], s5_b_im[l],
                                                         s5_c_re[l], s5_c_im[l], s5_log_dt[l])
    return dict(
        gw=gw, mix_w=mix_w,
        w_in=jnp.pad(w_in[l], ((0, 0), (0, pad_to - d_in))).astype(BF16),
        s5_bre=s5_bre, s5_bim=s5_bim, s5_are=s5_are, s5_aim=s5_aim, s5_ccat=s5_ccat,
        s5_d=row(s5_d), s5_w_glu=s5_w_glu[l].astype(BF16), s5_b_glu=row(s5_b_glu),
        lru_conv_w=lru_conv_w[l].astype(F32), lru_conv_b=row(lru_conv_b),
        lru_wg=jnp.concatenate([bd(lru_w_a), bd(lru_w_x)], axis=1).astype(BF16),
        lru_bg=jnp.concatenate([row(lru_b_a), row(lru_b_x)], axis=1),
        lru_sp=jax.nn.softplus(-row(lru_lambda)),
        cc_dw_w=cc_dw_w[l].astype(F32), cc_dw_b=row(cc_dw_b), cc_ln_g=row(cc_ln_g), cc_ln_b=row(cc_ln_b),
        w_out=w_out[l].astype(BF16), ln1_g=row(ln1_g), ln1_b=row(ln1_b), ln2_g=row(ln2_g), ln2_b=row(ln2_b),
    )


def kernel(x_prompt, x_sample, cache_k, cache_v, cache_kidx, state_s5_re, state_s5_im, state_lru_h, state_lru_conv, state_cc_conv, page_table, w_in, s5_a_re, s5_a_im, s5_b_re, s5_b_im, s5_c_re, s5_c_im, s5_d, s5_log_dt, s5_w_glu, s5_b_glu, lru_conv_w, lru_conv_b, lru_w_a, lru_b_a, lru_w_x, lru_b_x, lru_lambda, cc_dw_w, cc_dw_b, cc_ln_g, cc_ln_b, w_out, ln1_g, ln1_b, ln2_g, ln2_b, ffn_w_gate, ffn_w_up, ffn_w_down, moe_w_router, moe_b_router, moe_w_gate, moe_w_up, moe_w_down):
    bp, tp, dm = x_prompt.shape
    bs, ts, _ = x_sample.shape
    depth = w_in.shape[0]
    past = page_table.shape[1] * cache_k.shape[2]
    alpha = (2.0 * depth) ** 0.25
    n_heads = cache_k.shape[3]
    n_exp = moe_w_router.shape[-1]
    gw = s5_d.shape[1]

    cos_p, sin_p = _rope_tables(jnp.arange(tp, dtype=jnp.int32), gw // HEAD_DIM)
    cos_s, sin_s = _rope_tables(past + jnp.arange(ts, dtype=jnp.int32), gw // HEAD_DIM)
    cos_s, sin_s = jnp.tile(cos_s, (bs, 1)), jnp.tile(sin_s, (bs, 1))

    tm_p = _row_tile(tp, 512)
    xp = x_prompt.reshape(bp * tp, dm)
    xs = x_sample.reshape(bs * ts, dm)
    zeros_p = lambda *shape: jnp.zeros((bp,) + shape, x_prompt.dtype)
    states_p, states_s = [], []

    for l in range(depth):
        lp = _prep_layer(l, w_in, s5_a_re, s5_a_im, s5_b_re, s5_b_im, s5_c_re, s5_c_im, s5_d, s5_log_dt,
                         s5_w_glu, s5_b_glu, lru_conv_w, lru_conv_b, lru_w_a, lru_b_a, lru_w_x, lru_b_x,
                         lru_lambda, cc_dw_w, cc_dw_b, cc_ln_g, cc_ln_b, w_out, ln1_g, ln1_b, ln2_g, ln2_b)
        mix_w = lp['mix_w']
        j = l // 2
        if l % 2 == 0:
            tf = math.gcd(ffn_w_gate.shape[2], 256)
            cw = (_col_tiles(ffn_w_gate[j].astype(BF16), tf), _col_tiles(ffn_w_up[j].astype(BF16), tf),
                  ffn_w_down[j].astype(BF16))
        else:
            wr = jnp.pad(moe_w_router[j].astype(F32), ((0, 0), (0, LANES - n_exp)))
            wrh, wrl = _split3(wr)
            br = jnp.pad(moe_b_router[j].astype(F32), (0, LANES - n_exp), constant_values=NEG_BIG).reshape(1, LANES)
            tf = math.gcd(moe_w_gate.shape[3], 512)
            cw = (wrh, wrl, br, _col_tiles(moe_w_gate[j].astype(BF16), tf), _col_tiles(moe_w_up[j].astype(BF16), tf),
                  moe_w_down[j].astype(BF16))

        def channel(h, tm):
            if l % 2 == 0:
                return _ffn(h, *cw, lp['ln2_g'], lp['ln2_b'], tm, alpha)
            return _moe(h, *cw, lp['ln2_g'], lp['ln2_b'], tm, alpha)

        zmix, q, qi, k, v, kw = _in_proj(xp, lp['w_in'], cos_p, sin_p, tm_p, mix_w, gw)
        ys, h, ycc, st = _mixers(lp, zmix, bp, tp,
                                 zeros_p(*state_s5_re.shape[2:]), zeros_p(*state_s5_im.shape[2:]),
                                 zeros_p(*state_lru_h.shape[2:]), zeros_p(*state_lru_conv.shape[2:]),
                                 zeros_p(*state_cc_conv.shape[2:]), chained=True)
        yatt = _attend_prompt(q, qi, k, v, kw, bp, tp)
        h1 = _mixout(xp, zmix, ys, h, ycc, yatt, lp['s5_d'], lp['s5_w_glu'], lp['s5_b_glu'], lp['w_out'],
                     lp['ln1_g'], lp['ln1_b'], tm_p, alpha)
        xp = channel(h1, _row_tile(bp * tp, 1024))
        states_p.append((k.reshape(bp, tp, n_heads, HEAD_DIM), v.reshape(bp, tp, n_heads, HEAD_DIM),
                         kw[:, :IDX_DIM].reshape(bp, tp, IDX_DIM)) + st)

        ms = bs * ts
        zmix, q, qi, k, v, kw = _in_proj(xs, lp['w_in'], cos_s, sin_s, ms, mix_w, gw)
        ys, h, ycc, st = _mixers(lp, zmix, bs, ts, state_s5_re[l], state_s5_im[l], state_lru_h[l],
                                 state_lru_conv[l], state_cc_conv[l], chained=False)
        yatt = _attend_sample(l, q, qi, k, v, kw, bs, ts, cache_k, cache_v, cache_kidx, page_table)
        h1 = _mixout(xs, zmix, ys, h, ycc, yatt, lp['s5_d'], lp['s5_w_glu'], lp['s5_b_glu'], lp['w_out'],
                     lp['ln1_g'], lp['ln1_b'], ms, alpha)
        xs = channel(h1, ms)
        states_s.append((k.reshape(bs, ts, n_heads, HEAD_DIM), v.reshape(bs, ts, n_heads, HEAD_DIM),
                         kw[:, :IDX_DIM].reshape(bs, ts, IDX_DIM)) + st)

    new_p = [jnp.stack(col) for col in zip(*states_p)]
    new_s = [jnp.stack(col) for col in zip(*states_s)]
    out = [xp.reshape(bp, tp, dm), xs.reshape(bs, ts, dm)]
    for a, b in zip(new_p, new_s):
        out += [a, b]
    return tuple(out)
```

```python
import functools
import math

import jax
import jax.numpy as jnp
from jax import lax
from jax.experimental import pallas as pl
from jax.experimental.pallas import tpu as pltpu

F32 = jnp.float32
BF16 = jnp.bfloat16

S5_CH = 16
LRU_C = 8.0
HEAD_DIM = 64
IDX_DIM = 64
IDX_HEADS = 4
TOPK_MAX = 256
Q_BLOCK = 128
ROPE_THETA = 10000.0
MOE_TOP_K = 2
LN_EPS = 1e-5
ATT_SCALE = HEAD_DIM ** -0.5
IDX_SCALE = (IDX_DIM * IDX_HEADS) ** -0.5

LANES = 128
NEG_BIG = -1e30
VMEM_LIMIT = 56 * 1024 * 1024


def _params(*sem):
    return pltpu.CompilerParams(dimension_semantics=sem, vmem_limit_bytes=VMEM_LIMIT)


def _dot(a, b):
    return jnp.dot(a, b, preferred_element_type=F32)


def _dot_t(a, b):
    return lax.dot_general(a, b, (((1,), (1,)), ((), ())), preferred_element_type=F32)


def _split3(x):
    hi = x.astype(BF16)
    lo = (x - hi.astype(F32)).astype(BF16)
    return hi, lo


def _layer_norm(x, g, b):
    mu = jnp.mean(x, axis=-1, keepdims=True)
    xc = x - mu
    var = jnp.mean(xc * xc, axis=-1, keepdims=True)
    return xc * lax.rsqrt(var + LN_EPS) * g + b


def _rope(z, cos, sin):
    w = z.shape[-1]
    half = HEAD_DIM // 2
    lane = lax.broadcasted_iota(jnp.int32, z.shape, 1)
    first = (lane & (HEAD_DIM - 1)) < half
    partner = jnp.where(first, pltpu.roll(z, w - half, axis=1), pltpu.roll(z, half, axis=1))
    return z * cos + partner * sin


def _in_proj_kernel(x_ref, w_ref, cos_ref, sin_ref, zmix_ref, q_ref, qi_ref, k_ref, v_ref, kw_ref,
                    *, mix_w, gw):
    xb = x_ref[...].astype(BF16)
    cos = cos_ref[...]
    sin = sin_ref[...]

    def proj(lo, width):
        return _dot(xb, w_ref[:, lo:lo + width])

    zmix_ref[...] = proj(0, mix_w)
    o = mix_w
    q_ref[...] = (_rope(proj(o, gw), cos, sin) * ATT_SCALE).astype(BF16)
    k_ref[...] = _rope(proj(o + gw, gw), cos, sin)
    v_ref[...] = proj(o + 2 * gw, gw)
    qi_ref[...] = _rope(proj(o + 3 * gw, gw), cos, sin)
    kw = proj(o + 4 * gw, LANES)
    is_key = lax.broadcasted_iota(jnp.int32, kw.shape, 1) < IDX_DIM
    kw_ref[...] = _rope(kw, jnp.where(is_key, cos[:, :LANES], 1.0), jnp.where(is_key, sin[:, :LANES], 0.0))


def _in_proj(x2, w_pad, cos_t, sin_t, tm, mix_w, gw):
    m, d = x2.shape
    tab_blocks = cos_t.shape[0] // tm
    row = lambda i: (i, 0)
    tab = lambda i: (i % tab_blocks, 0)
    outs = [
        jax.ShapeDtypeStruct((m, mix_w), F32),
        jax.ShapeDtypeStruct((m, gw), BF16),
        jax.ShapeDtypeStruct((m, gw), F32),
        jax.ShapeDtypeStruct((m, gw), F32),
        jax.ShapeDtypeStruct((m, gw), F32),
        jax.ShapeDtypeStruct((m, LANES), F32),
    ]
    return pl.pallas_call(
        functools.partial(_in_proj_kernel, mix_w=mix_w, gw=gw),
        grid=(m // tm,),
        in_specs=[pl.BlockSpec((tm, d), row),
                  pl.BlockSpec(w_pad.shape, lambda i: (0, 0)),
                  pl.BlockSpec((tm, gw), tab),
                  pl.BlockSpec((tm, gw), tab)],
        out_specs=[pl.BlockSpec((tm, s.shape[1]), row) for s in outs],
        out_shape=outs,
        compiler_params=_params("parallel"),
        name="in_proj",
    )(x2, w_pad, cos_t, sin_t)


LRU_HALO = 8
CC_HALO = 32


def _convmix_kernel(xl_ref, ca_ref, cg_ref, lbuf_ref, cbuf_ref, lw_ref, lb_ref, wg_ref, bg_ref, sp_ref,
                    cw_ref, cb_ref, lng_ref, lnb_ref,
                    a_ref, b_ref, ycc_ref, lbuf_o, cbuf_o, xl_s, xc_s, *, tt):
    i = pl.program_id(1)
    kl = lw_ref.shape[0]
    kc = cw_ref.shape[0]
    w = xl_ref.shape[-1]
    l0 = LRU_HALO - (kl - 1)
    c0 = CC_HALO - (kc - 1)

    @pl.when(i == 0)
    def _():
        xl_s[l0:LRU_HALO, :] = lbuf_ref[...]
        xc_s[c0:CC_HALO, :] = cbuf_ref[...]

    xl_s[LRU_HALO:LRU_HALO + tt, :] = xl_ref[...]
    xc_s[CC_HALO:CC_HALO + tt, :] = ca_ref[...] * jax.nn.sigmoid(cg_ref[...])

    conv = xl_s[l0:l0 + tt, :] * lw_ref[0:1, :]
    for j in range(1, kl):
        conv = conv + xl_s[l0 + j:l0 + j + tt, :] * lw_ref[j:j + 1, :]
    conv = conv + lb_ref[...]
    gates = _dot(conv.astype(BF16), wg_ref[...]) + bg_ref[...]
    r = jax.nn.sigmoid(gates[:, :w])
    ig = jax.nn.sigmoid(gates[:, w:])
    log_a = (-LRU_C) * r * sp_ref[...]
    a = jnp.exp(log_a)
    a_ref[...] = a
    b_ref[...] = jnp.sqrt(-jnp.tanh(log_a) * (a * a + 1.0)) * (ig * conv)

    c = xc_s[c0:c0 + tt, :] * cw_ref[0:1, :]
    for j in range(1, kc):
        c = c + xc_s[c0 + j:c0 + j + tt, :] * cw_ref[j:j + 1, :]
    c = _layer_norm(c + cb_ref[...], lng_ref[...], lnb_ref[...])
    ycc_ref[...] = c * jax.nn.sigmoid(c)

    new_l = xl_s[l0 + tt:LRU_HALO + tt, :]
    new_c = xc_s[c0 + tt:CC_HALO + tt, :]
    xl_s[l0:LRU_HALO, :] = new_l
    xc_s[c0:CC_HALO, :] = new_c
    lbuf_o[...] = new_l
    cbuf_o[...] = new_c


def _convmix(zmix3, lbuf, cbuf, lw, lb, wg, bg, sp, cw, cb, lng, lnb, tt):
    nb, t, _ = zmix3.shape
    w = lw.shape[1]
    col = lambda c: pl.BlockSpec((None, tt, w), lambda b, i, c=c: (b, i, c))
    full2 = lambda a: pl.BlockSpec(a.shape, lambda b, i: (0, 0))
    per_b = lambda a: pl.BlockSpec((None,) + a.shape[1:], lambda b, i: (b, 0, 0))
    seq_out = jax.ShapeDtypeStruct((nb, t, w), F32)
    return pl.pallas_call(
        functools.partial(_convmix_kernel, tt=tt),
        grid=(nb, t // tt),
        in_specs=[col(1), col(3), col(4), per_b(lbuf), per_b(cbuf), full2(lw), full2(lb), full2(wg),
                  full2(bg), full2(sp), full2(cw), full2(cb), full2(lng), full2(lnb)],
        out_specs=[pl.BlockSpec((None, tt, w), lambda b, i: (b, i, 0))] * 3 + [per_b(lbuf), per_b(cbuf)],
        out_shape=[seq_out, seq_out, seq_out,
                   jax.ShapeDtypeStruct(lbuf.shape, F32), jax.ShapeDtypeStruct(cbuf.shape, F32)],
        scratch_shapes=[pltpu.VMEM((LRU_HALO + tt, w), F32), pltpu.VMEM((CC_HALO + tt, w), F32)],
        compiler_params=_params("parallel", "arbitrary"),
        name="convmix",
    )(zmix3, zmix3, zmix3, lbuf, cbuf, lw, lb, wg, bg, sp, cw, cb, lng, lnb)


def _lru_scan_kernel(a_ref, b_ref, h0_ref, h_ref, hl_ref, e_s, p_s, hi_s, *, chained):
    nl, nr, _ = a_ref.shape

    def run(init):
        def step(t, h):
            h = a_ref[t] * h + b_ref[t]
            h_ref[t] = h
            return h
        return lax.fori_loop(0, nl, step, init)

    if not chained:
        hl_ref[...] = run(h0_ref[...])
        return

    def local(t, c):
        h, p = c
        at = a_ref[t]
        return at * h + b_ref[t], at * p
    zeros = jnp.zeros(e_s.shape, F32)
    e, p = lax.fori_loop(0, nl, local, (zeros, zeros + 1.0))
    e_s[...] = e
    p_s[...] = p
    hi_s[0:1, :] = h0_ref[...]

    def carry(c, _):
        prev = pl.ds(c - 1, 1)
        hi_s[pl.ds(c, 1), :] = p_s[prev, :] * hi_s[prev, :] + e_s[prev, :]
        return 0
    lax.fori_loop(1, nr, carry, 0)
    last = run(hi_s[...])
    hl_ref[...] = last[nr - 1:nr, :]


def _lru_scan(a4, b4, h0, chained):
    nb, nl, nr, w = a4.shape
    r0 = h0.shape[1]
    nw = w // LANES
    blk = pl.BlockSpec((None, nl, nr, LANES), lambda n, j: (n, 0, 0, j))
    st = pl.BlockSpec((None, r0, LANES), lambda n, j: (n, 0, j))
    return pl.pallas_call(
        functools.partial(_lru_scan_kernel, chained=chained),
        grid=(nb, nw),
        in_specs=[blk, blk, st],
        out_specs=[blk, st],
        out_shape=[jax.ShapeDtypeStruct(a4.shape, F32), jax.ShapeDtypeStruct(h0.shape, F32)],
        scratch_shapes=[pltpu.VMEM((nr, LANES), F32)] * 3,
        compiler_params=_params("parallel", "parallel"),
        name="lru_scan",
    )(a4, b4, h0)


def _s5_scan_kernel(u_ref, bre_ref, bim_ref, are_ref, aim_ref, cc_ref, h0r_ref, h0i_ref,
                    ys_ref, hlr_ref, hli_ref, br_s, bi_s, er_s, ei_s, hr_s, hi_s, *, chained, rows_per_dot):
    g = pl.program_id(1)
    nl, nr, _ = u_ref.shape
    n = nl * nr
    ar = are_ref[...]
    ai = aim_ref[...]

    def drive(c, _):
        r0 = pl.multiple_of(c * rows_per_dot, rows_per_dot)
        ub = u_ref[pl.ds(c * (rows_per_dot // nr), rows_per_dot // nr)].reshape(rows_per_dot, u_ref.shape[-1])
        br_s[pl.ds(r0, rows_per_dot), :] = _dot(ub, bre_ref[...])
        bi_s[pl.ds(r0, rows_per_dot), :] = _dot(ub, bim_ref[...])
        return 0
    lax.fori_loop(0, n // rows_per_dot, drive, 0)

    def step_fn(store):
        def step(t, c):
            hr, hi = c
            rows = pl.ds(pl.multiple_of(t * nr, nr), nr)
            nhr = ar * hr - ai * hi + br_s[rows, :]
            nhi = ar * hi + ai * hr + bi_s[rows, :]
            if store:
                br_s[rows, :] = nhr
                bi_s[rows, :] = nhi
            return nhr, nhi
        return step

    if chained:
        zeros = jnp.zeros((nr, LANES), F32)
        er, ei = lax.fori_loop(0, nl, step_fn(False), (zeros, zeros))
        er_s[...] = er
        ei_s[...] = ei

        def power(_, c):
            pr, pi = c
            return ar * pr - ai * pi, ar * pi + ai * pr
        alr, ali = lax.fori_loop(0, nl - 1, power, (ar, ai))
        hr_s[0:1, :] = h0r_ref[...]
        hi_s[0:1, :] = h0i_ref[...]

        def carry(c, _):
            prev = pl.ds(c - 1, 1)
            pr = hr_s[prev, :]
            pi = hi_s[prev, :]
            hr_s[pl.ds(c, 1), :] = alr * pr - ali * pi + er_s[prev, :]
            hi_s[pl.ds(c, 1), :] = alr * pi + ali * pr + ei_s[prev, :]
            return 0
        lax.fori_loop(1, nr, carry, 0)
        init = (hr_s[...], hi_s[...])
    else:
        init = (h0r_ref[...], h0i_ref[...])

    lr, li = lax.fori_loop(0, nl, step_fn(True), init)
    if chained:
        hlr_ref[...] = lr[nr - 1:nr, :]
        hli_ref[...] = li[nr - 1:nr, :]
    else:
        hlr_ref[...] = lr
        hli_ref[...] = li

    def project(c, _):
        r0 = pl.multiple_of(c * rows_per_dot, rows_per_dot)
        rows = pl.ds(r0, rows_per_dot)
        hcat = jnp.concatenate([br_s[rows, :], bi_s[rows, :]], axis=1).astype(BF16)
        y = _dot(hcat, cc_ref[...]).reshape(rows_per_dot // nr, nr, ys_ref.shape[-1])
        tsl = pl.ds(c * (rows_per_dot // nr), rows_per_dot // nr)

        @pl.when(g == 0)
        def _():
            ys_ref[tsl] = y

        @pl.when(g > 0)
        def _():
            ys_ref[tsl] = ys_ref[tsl] + y
        return 0
    lax.fori_loop(0, n // rows_per_dot, project, 0)


def _s5_scan(u4, bre, bim, are, aim, ccat, h0r, h0i, chained):
    nb, nl, nr, w = u4.shape
    ng = bre.shape[0]
    r0 = h0r.shape[1]
    n = nl * nr
    rows_per_dot = math.gcd(n, 1024)
    rows_per_dot = max(rows_per_dot, nr)
    seq = pl.BlockSpec((None, nl, nr, w), lambda b, g: (b, 0, 0, 0))
    per_g = lambda a: pl.BlockSpec((None,) + a.shape[1:], lambda b, g: (g, 0, 0))
    st = pl.BlockSpec((None, r0, LANES), lambda b, g: (b, 0, g))
    small = pltpu.VMEM((nr, LANES), F32)
    return pl.pallas_call(
        functools.partial(_s5_scan_kernel, chained=chained, rows_per_dot=rows_per_dot),
        grid=(nb, ng),
        in_specs=[seq, per_g(bre), per_g(bim), per_g(are), per_g(aim), per_g(ccat), st, st],
        out_specs=[seq, st, st],
        out_shape=[jax.ShapeDtypeStruct(u4.shape, F32), jax.ShapeDtypeStruct(h0r.shape, F32),
                   jax.ShapeDtypeStruct(h0i.shape, F32)],
        scratch_shapes=[pltpu.VMEM((n, LANES), F32), pltpu.VMEM((n, LANES), F32), small, small, small, small],
        compiler_params=_params("parallel", "arbitrary"),
        name="s5_scan",
    )(u4, bre, bim, are, aim, ccat, h0r, h0i)


BISECT_ITERS = 24
PROBE_QUERIES = 12
PROBE_GAP = 3


def _select_threshold(count_ge, max_below, mn, mx, kq, all_sel, probe_queries):
    inf = jnp.full(mn.shape, jnp.inf, F32)
    zero = jnp.zeros(mn.shape, F32)
    state = (mn, inf, mn, jnp.where(all_sel, 1.0, zero), zero)

    def unfinished(st):
        return jnp.min(st[3]) == 0.0

    def bisect(c):
        it, (lo, hi, theta, done, tie) = c
        mid = jnp.where(hi == jnp.inf, mx, 0.5 * lo + 0.5 * hi)
        cnt = count_ge(mid)
        live = done == 0.0
        collapsed = jnp.logical_and(live, jnp.logical_or(mid == lo, mid == hi))
        hit = jnp.logical_and(live, cnt == kq)
        move = jnp.logical_and(live, jnp.logical_not(collapsed))
        theta = jnp.where(hit, mid, jnp.where(collapsed, hi, theta))
        tie = jnp.where(collapsed, 1.0, tie)
        done = jnp.where(jnp.logical_or(hit, collapsed), 1.0, done)
        lo = jnp.where(jnp.logical_and(move, cnt > kq), mid, lo)
        hi = jnp.where(jnp.logical_and(move, cnt < kq), mid, hi)
        return it + 1, (lo, hi, theta, done, tie)

    _, state = lax.while_loop(
        lambda c: jnp.logical_and(c[0] < BISECT_ITERS, jnp.sum(1.0 - c[1][3]) > float(probe_queries)),
        bisect, (jnp.int32(0), state))

    def exact_step(st):
        lo, hi, theta, done, tie = st
        live = done == 0.0
        tau = max_below(hi)
        cnt = count_ge(tau)
        found = jnp.logical_and(live, cnt >= kq)
        clean = jnp.logical_and(found, cnt == kq)
        theta = jnp.where(clean, tau, jnp.where(found, hi, theta))
        tie = jnp.where(jnp.logical_and(found, jnp.logical_not(clean)), 1.0, tie)
        lo = jnp.where(found, tau, lo)
        hi = jnp.where(jnp.logical_and(live, jnp.logical_not(found)), tau, hi)
        done = jnp.where(found, 1.0, done)
        return lo, hi, theta, done, tie

    def settle(st):
        st = exact_step(st)
        _, st = lax.while_loop(lambda c: jnp.logical_and(c[0] < PROBE_GAP, unfinished(c[1])),
                               bisect, (jnp.int32(0), st))
        return st

    lo, hi, theta, done, tie = lax.while_loop(unfinished, settle, state)
    return theta, lo, tie


COUNT_ROWS = 64


def _dsa_prompt_kernel(qt_ref, qih_ref, qil_ref, wit_ref, kidx_ref, k_ref, vt_ref, o_ref,
                       s_scr, lg_a, lg_b, qp_s, qip_s, *, topk, kc):
    i = pl.program_id(1)
    qb = Q_BLOCK
    n_pairs = qp_s.shape[0]
    pw = 2 * HEAD_DIM

    zq = jnp.zeros((HEAD_DIM, qb), BF16)
    for p in range(n_pairs):
        top = jnp.concatenate([qt_ref[p * pw:p * pw + HEAD_DIM, :], zq], axis=1)
        bot = jnp.concatenate([zq, qt_ref[p * pw + HEAD_DIM:(p + 1) * pw, :]], axis=1)
        qp_s[p] = jnp.concatenate([top, bot], axis=0)
        cols = []
        for h in (2 * p, 2 * p + 1):
            hi = qih_ref[h * IDX_DIM:(h + 1) * IDX_DIM, :]
            cols.append(jnp.concatenate([hi, hi, qil_ref[h * IDX_DIM:(h + 1) * IDX_DIM, :], zq], axis=0))
        qip_s[p] = jnp.concatenate(cols, axis=1)
    nch = ((i + 1) * qb + kc - 1) // kc
    qpos = i * qb + lax.broadcasted_iota(jnp.int32, (1, qb), 1)
    kiota = lax.broadcasted_iota(jnp.int32, (kc, qb), 0)

    def chunk_rows(c):
        return pl.ds(pl.multiple_of(c * kc, kc), kc)

    def score_chunk(c, carry):
        mn, mx = carry
        kk = kidx_ref[chunk_rows(c), :]
        s = jnp.zeros((kc, qb), F32)
        for p in range(n_pairs):
            sp = _dot(kk, qip_s[p])
            for hh in range(2):
                h = 2 * p + hh
                s = s + jnp.maximum(sp[:, hh * qb:(hh + 1) * qb], 0.0) * wit_ref[h:h + 1, :]
        s = s * IDX_SCALE
        vis = (c * kc + kiota) <= qpos
        s_scr[chunk_rows(c), :] = jnp.where(vis, s, -jnp.inf)
        mn = jnp.minimum(mn, jnp.min(jnp.where(vis, s, jnp.inf), axis=0, keepdims=True))
        mx = jnp.maximum(mx, jnp.max(jnp.where(vis, s, -jnp.inf), axis=0, keepdims=True))
        return mn, mx
    nch2 = (nch + 1) // 2
    mn, mx = lax.fori_loop(0, 2 * nch2, score_chunk,
                           (jnp.full((1, qb), jnp.inf, F32), jnp.full((1, qb), -jnp.inf, F32)))

    def fold(x, op):
        return op(x.reshape(kc // COUNT_ROWS, COUNT_ROWS, qb), axis=0)

    def count(pred_fn):
        def body(c, acc):
            hit = jnp.where(pred_fn(s_scr[chunk_rows(c), :], c * kc + kiota), 1.0, 0.0)
            return acc + fold(hit, jnp.sum)
        acc = lax.fori_loop(0, nch, body, jnp.zeros((COUNT_ROWS, qb), F32))
        return jnp.sum(acc, axis=0, keepdims=True)

    def max_below(x):
        def body(c, acc):
            s = s_scr[chunk_rows(c), :]
            return jnp.maximum(acc, fold(jnp.where(s < x, s, -jnp.inf), jnp.max))
        acc = lax.fori_loop(0, nch, body, jnp.full((COUNT_ROWS, qb), -jnp.inf, F32))
        return jnp.max(acc, axis=0, keepdims=True)

    n_vis = (qpos + 1).astype(F32)
    kq = jnp.minimum(n_vis, float(topk))
    theta, lo, tie = _select_threshold(lambda x: count(lambda s, kpos: s >= x), max_below,
                                       mn, mx, kq, n_vis <= float(topk), PROBE_QUERIES)

    @pl.when(jnp.max(tie) > 0.0)
    def _():
        need = kq - count(lambda s, kpos: s >= theta)
        tied = lambda s: jnp.logical_and(tie > 0.0, s == lo)
        tri = jnp.where(lax.broadcasted_iota(jnp.int32, (kc, kc), 0) >= lax.broadcasted_iota(jnp.int32, (kc, kc), 1),
                        1.0, 0.0).astype(BF16)

        def promote(c, seen):
            s = s_scr[chunk_rows(c), :]
            is_tied = tied(s)
            rank = seen + _dot(tri, jnp.where(is_tied, 1.0, 0.0).astype(BF16))
            keep = jnp.logical_and(is_tied, rank <= need)
            s_scr[chunk_rows(c), :] = jnp.where(keep, theta, s)
            return rank[kc - 1:kc, :]
        lax.fori_loop(0, nch, promote, jnp.zeros((1, qb), F32))

    n_heads = 2 * n_pairs
    kc2 = 2 * kc

    def chunk2_rows(c):
        return pl.ds(pl.multiple_of(jnp.minimum(c, nch2 - 1) * kc2, kc2), kc2)

    def put_logits(dst, c):
        for p in range(n_pairs):
            dst[p] = _dot(k_ref[chunk2_rows(c), p * pw:(p + 1) * pw], qp_s[p])

    def absorb(src, c, carry):
        ms, ls, accs = carry
        rows2 = chunk2_rows(c)
        sel = jnp.logical_and(s_scr[rows2, :] >= theta, c < nch2)
        new_m, new_l, new_acc = [], [], []
        for p in range(n_pairs):
            for hh in range(2):
                h = 2 * p + hh
                logit = jnp.where(sel, src[p, :, hh * qb:(hh + 1) * qb], NEG_BIG)
                m_new = jnp.maximum(ms[h], jnp.max(logit, axis=0, keepdims=True))
                alpha = jnp.exp(ms[h] - m_new)
                pr = jnp.exp(logit - m_new)
                new_m.append(m_new)
                new_l.append(alpha * ls[h] + jnp.sum(pr, axis=0, keepdims=True))
                pv = _dot(vt_ref[h * HEAD_DIM:(h + 1) * HEAD_DIM, rows2], pr.astype(BF16))
                new_acc.append(alpha * accs[h] + pv)
        return tuple(new_m), tuple(new_l), tuple(new_acc)

    def att_pair(j, carry):
        c = 2 * j
        put_logits(lg_b, c + 1)
        carry = absorb(lg_a, c, carry)
        put_logits(lg_a, c + 2)
        return absorb(lg_b, c + 1, carry)

    init = (tuple(jnp.full((1, qb), NEG_BIG, F32) for _ in range(n_heads)),
            tuple(jnp.zeros((1, qb), F32) for _ in range(n_heads)),
            tuple(jnp.zeros((HEAD_DIM, qb), F32) for _ in range(n_heads)))
    put_logits(lg_a, 0)
    _, ls, accs = lax.fori_loop(0, (nch2 + 1) // 2, att_pair, init)
    for h in range(n_heads):
        o_ref[h * HEAD_DIM:(h + 1) * HEAD_DIM, :] = accs[h] / ls[h]


def _dsa_prompt(qt, qiht, qilt, wit, kidx3, kb, vt, topk, kc):
    nb, w, t = vt.shape
    n_pairs = w // (2 * HEAD_DIM)
    per_b = lambda a: pl.BlockSpec((None,) + a.shape[1:], lambda b, i: (b, 0, 0))
    qcol = lambda a: pl.BlockSpec((None, a.shape[1], Q_BLOCK), lambda b, i: (b, 0, i))
    return pl.pallas_call(
        functools.partial(_dsa_prompt_kernel, topk=topk, kc=kc),
        grid=(nb, t // Q_BLOCK),
        in_specs=[qcol(qt), qcol(qiht), qcol(qilt), qcol(wit), per_b(kidx3), per_b(kb), per_b(vt)],
        out_specs=pl.BlockSpec((None, w, Q_BLOCK), lambda b, i: (b, 0, i)),
        out_shape=jax.ShapeDtypeStruct((nb, w, t), F32),
        scratch_shapes=[pltpu.VMEM((t, Q_BLOCK), F32),
                        pltpu.VMEM((n_pairs, 2 * kc, 2 * Q_BLOCK), F32),
                        pltpu.VMEM((n_pairs, 2 * kc, 2 * Q_BLOCK), F32),
                        pltpu.VMEM((n_pairs, 2 * HEAD_DIM, 2 * Q_BLOCK), BF16),
                        pltpu.VMEM((n_pairs, kidx3.shape[2], 2 * Q_BLOCK), BF16)],
        compiler_params=_params("parallel", "arbitrary"),
        name="dsa_prompt",
    )(qt, qiht, qilt, wit, kidx3, kb, vt)


SAMPLE_PAGES_PER_STEP = 16
QROWS = 8


def _dsa_sample_select_kernel(pt_ref, qhi_ref, qlo_ref, wi_ref, knew_ref, *rest, topk, n_new, gp):
    pages = rest[:gp]
    bias_ref, s_scr = rest[gp], rest[gp + 1]
    j = pl.program_id(1)
    nj = pl.num_programs(1)
    psz = pages[0].shape[1]
    past = nj * gp * psz
    qhi = qhi_ref[...]
    qlo = qlo_ref[...]
    wi = wi_ref[...]

    def scores(kpage_t):
        khi, klo = _split3(kpage_t)
        s = _dot(qhi, khi) + _dot(qhi, klo) + _dot(qlo, khi)
        s = jnp.maximum(s, 0.0) * wi
        tot = s[0:QROWS, :]
        for h in range(1, IDX_HEADS):
            tot = tot + s[h * QROWS:(h + 1) * QROWS, :]
        return tot * IDX_SCALE

    for g in range(gp):
        off = pl.multiple_of((j * gp + g) * psz, psz)
        s_scr[:, pl.ds(off, psz)] = scores(pages[g][...])

    @pl.when(j == nj - 1)
    def _():
        qrow = lax.broadcasted_iota(jnp.int32, (QROWS, psz), 0)
        kcol = lax.broadcasted_iota(jnp.int32, (QROWS, psz), 1)
        vis_new = jnp.logical_and(kcol <= qrow, kcol < n_new)
        s_scr[:, past:past + psz] = jnp.where(vis_new, scores(knew_ref[...]), -jnp.inf)
        s = s_scr[...]
        nk = s.shape[1]
        kpos = lax.broadcasted_iota(jnp.int32, s.shape, 1)
        qrow1 = lax.broadcasted_iota(jnp.int32, (QROWS, 1), 0)
        n_vis = (past + jnp.minimum(qrow1 + 1, n_new)).astype(F32)
        kq = jnp.minimum(n_vis, float(topk))
        fin = s > -jnp.inf
        mn = jnp.min(jnp.where(fin, s, jnp.inf), axis=1, keepdims=True)
        mx = jnp.max(s, axis=1, keepdims=True)
        count = lambda pred: jnp.sum(jnp.where(pred, 1.0, 0.0), axis=1, keepdims=True)
        theta, lo, tie = _select_threshold(
            lambda x: count(s >= x),
            lambda x: jnp.max(jnp.where(s < x, s, -jnp.inf), axis=1, keepdims=True),
            mn, mx, kq, jnp.logical_or(n_vis <= float(topk), qrow1 >= n_new), 0)

        sel = s >= theta
        bias_ref[...] = jnp.where(sel, 0.0, NEG_BIG)

        @pl.when(jnp.max(tie) > 0.0)
        def _():
            need = kq - count(sel)
            tied = jnp.logical_and(tie > 0.0, s == lo)
            nbits = max(1, (nk - 1).bit_length())

            def jstep(b, jl):
                cand = jl + (jnp.int32(1) << (nbits - 1 - b))
                c_lt = count(jnp.logical_and(tied, kpos < cand))
                return jnp.where(c_lt < need, cand, jl)
            jlast = lax.fori_loop(0, nbits, jstep, jnp.zeros((QROWS, 1), jnp.int32))
            keep = jnp.logical_or(sel, jnp.logical_and(tied, kpos <= jlast))
            bias_ref[...] = jnp.where(keep, 0.0, NEG_BIG)


def _dsa_sample_select(layer, page_table, qhi, qlo, wi_b, knew, pool_kidx, topk, n_new):
    nb, n_pages = page_table.shape
    gp = math.gcd(n_pages, SAMPLE_PAGES_PER_STEP)
    di, psz = pool_kidx.shape[2], pool_kidx.shape[3]
    nk = (n_pages + 1) * psz
    per_b = lambda a: pl.BlockSpec((None,) + a.shape[1:], lambda b, j, pt: (b, 0, 0))
    page = lambda g: pl.BlockSpec((None, None, di, psz), lambda b, j, pt, g=g: (layer, pt[b, j * gp + g], 0, 0))
    grid_spec = pltpu.PrefetchScalarGridSpec(
        num_scalar_prefetch=1,
        grid=(nb, n_pages // gp),
        in_specs=[per_b(qhi), per_b(qlo), per_b(wi_b), per_b(knew)] + [page(g) for g in range(gp)],
        out_specs=pl.BlockSpec((None, QROWS, nk), lambda b, j, pt: (b, 0, 0)),
        scratch_shapes=[pltpu.VMEM((QROWS, nk), F32)],
    )
    return pl.pallas_call(
        functools.partial(_dsa_sample_select_kernel, topk=topk, n_new=n_new, gp=gp),
        grid_spec=grid_spec,
        out_shape=jax.ShapeDtypeStruct((nb, QROWS, nk), F32),
        compiler_params=_params("parallel", "arbitrary"),
        name="dsa_sample_select",
    )(page_table, qhi, qlo, wi_b, knew, *([pool_kidx] * gp))


def _dsa_sample_attend_kernel(pt_ref, q_ref, knew_ref, vnew_ref, *rest, gp):
    bias_refs = rest[:gp]
    kpages = rest[gp:2 * gp]
    vpages = rest[2 * gp:3 * gp]
    bnew_ref = rest[3 * gp]
    o_ref, m_s, l_s, acc_s = rest[3 * gp + 1:]
    j = pl.program_id(1)
    nj = pl.num_programs(1)
    q = q_ref[...]
    n_heads = q.shape[0] // QROWS

    @pl.when(j == 0)
    def _():
        m_s[...] = jnp.full(m_s.shape, NEG_BIG, F32)
        l_s[...] = jnp.zeros(l_s.shape, F32)
        acc_s[...] = jnp.zeros(acc_s.shape, F32)

    def absorb(kts, vts, biases):
        psz = kts[0].shape[1]
        logit = jnp.concatenate(
            [_dot(q, kt[...].astype(BF16)) + jnp.concatenate([b[...]] * n_heads, axis=0)
             for kt, b in zip(kts, biases)], axis=1)
        m = m_s[...]
        m_new = jnp.maximum(m, jnp.max(logit, axis=1, keepdims=True))
        alpha = jnp.exp(m - m_new)
        p = jnp.exp(logit - m_new)
        l_s[...] = alpha * l_s[...] + jnp.sum(p, axis=1, keepdims=True)
        pv = _dot_t(p[:, 0:psz].astype(BF16), vts[0][...].astype(BF16))
        for g in range(1, len(vts)):
            pv = pv + _dot_t(p[:, g * psz:(g + 1) * psz].astype(BF16), vts[g][...].astype(BF16))
        acc_s[...] = alpha * acc_s[...] + pv
        m_s[...] = m_new

    absorb(kpages, vpages, bias_refs)

    @pl.when(j == nj - 1)
    def _():
        absorb([knew_ref], [vnew_ref], [bnew_ref])
        out = acc_s[...] / l_s[...]
        lane = lax.broadcasted_iota(jnp.int32, (QROWS, out.shape[1]), 1)
        y = jnp.zeros((QROWS, out.shape[1]), F32)
        for h in range(n_heads):
            in_head = jnp.logical_and(lane >= h * HEAD_DIM, lane < (h + 1) * HEAD_DIM)
            y = y + jnp.where(in_head, out[h * QROWS:(h + 1) * QROWS, :], 0.0)
        o_ref[...] = y


def _dsa_sample_attend(layer, page_table, q32, knew, vnew, bias, pool_k, pool_v):
    nb, n_pages = page_table.shape
    gp = math.gcd(n_pages, SAMPLE_PAGES_PER_STEP)
    w, psz = pool_k.shape[2], pool_k.shape[3]
    per_b = lambda a: pl.BlockSpec((None,) + a.shape[1:], lambda b, j, pt: (b, 0, 0))
    page = lambda g: pl.BlockSpec((None, None, w, psz), lambda b, j, pt, g=g: (layer, pt[b, j * gp + g], 0, 0))
    bias_pg = lambda g: pl.BlockSpec((None, QROWS, psz), lambda b, j, pt, g=g: (b, 0, j * gp + g))
    bias_new = pl.BlockSpec((None, QROWS, psz), lambda b, j, pt: (b, 0, n_pages))
    grid_spec = pltpu.PrefetchScalarGridSpec(
        num_scalar_prefetch=1,
        grid=(nb, n_pages // gp),
        in_specs=([per_b(q32), per_b(knew), per_b(vnew)] + [bias_pg(g) for g in range(gp)]
                  + [page(g) for g in range(gp)] + [page(g) for g in range(gp)] + [bias_new]),
        out_specs=pl.BlockSpec((None, QROWS, w), lambda b, j, pt: (b, 0, 0)),
        scratch_shapes=[pltpu.VMEM((q32.shape[1], 1), F32), pltpu.VMEM((q32.shape[1], 1), F32),
                        pltpu.VMEM((q32.shape[1], w), F32)],
    )
    return pl.pallas_call(
        functools.partial(_dsa_sample_attend_kernel, gp=gp),
        grid_spec=grid_spec,
        out_shape=jax.ShapeDtypeStruct((nb, QROWS, w), F32),
        compiler_params=_params("parallel", "arbitrary"),
        name="dsa_sample_attend",
    )(page_table, q32, knew, vnew, *([bias] * gp), *([pool_k] * gp), *([pool_v] * gp), bias)


def _mixout_kernel(x_ref, u_ref, g_ref, ys_ref, h_ref, ycc_ref, yatt_ref, d_ref, wglu_ref, bglu_ref,
                   wout_ref, g1_ref, b1_ref, o_ref, *, alpha):
    w = u_ref.shape[-1]
    y = ys_ref[...] + d_ref[...] * u_ref[...]
    z = _dot(jax.nn.gelu(y).astype(BF16), wglu_ref[...]) + bglu_ref[...]
    y_s5 = z[:, :w] * jax.nn.sigmoid(z[:, w:])
    y_lru = h_ref[...] * jax.nn.gelu(g_ref[...])
    acc = _dot(y_s5.astype(BF16), wout_ref[0:w, :])
    acc = acc + _dot(y_lru.astype(BF16), wout_ref[w:2 * w, :])
    acc = acc + _dot(ycc_ref[...].astype(BF16), wout_ref[2 * w:3 * w, :])
    acc = acc + _dot(yatt_ref[...].astype(BF16), wout_ref[3 * w:4 * w, :])
    o_ref[...] = _layer_norm(alpha * x_ref[...] + acc, g1_ref[...], b1_ref[...])


def _mixout(x2, zmix, ys, h, ycc, yatt, d, wglu, bglu, wout, g1, b1, tm, alpha):
    m, dm = x2.shape
    w = ys.shape[1]
    row = lambda a: pl.BlockSpec((tm, a.shape[1]), lambda i: (i, 0))
    full = lambda a: pl.BlockSpec(a.shape, lambda i: (0, 0))
    zcol = lambda c: pl.BlockSpec((tm, w), lambda i, c=c: (i, c))
    return pl.pallas_call(
        functools.partial(_mixout_kernel, alpha=alpha),
        grid=(m // tm,),
        in_specs=[row(x2), zcol(0), zcol(2), row(ys), row(h), row(ycc), row(yatt), full(d), full(wglu),
                  full(bglu), full(wout), full(g1), full(b1)],
        out_specs=pl.BlockSpec((tm, dm), lambda i: (i, 0)),
        out_shape=jax.ShapeDtypeStruct((m, dm), F32),
        compiler_params=_params("parallel"),
        name="mixout",
    )(x2, zmix, zmix, ys, h, ycc, yatt, d, wglu, bglu, wout, g1, b1)


def _ffn_kernel(h_ref, wg_ref, wu_ref, wd_ref, g2_ref, b2_ref, o_ref, acc_s, hb_s, *, alpha):
    f = pl.program_id(1)

    @pl.when(f == 0)
    def _():
        acc_s[...] = jnp.zeros(acc_s.shape, F32)
        hb_s[...] = h_ref[...].astype(BF16)

    hb = hb_s[...]
    a = _dot(hb, wg_ref[...])
    act = (a * jax.nn.sigmoid(a)) * _dot(hb, wu_ref[...])
    acc_s[...] += _dot(act.astype(BF16), wd_ref[...])

    @pl.when(f == pl.num_programs(1) - 1)
    def _():
        o_ref[...] = _layer_norm(alpha * h_ref[...] + acc_s[...], g2_ref[...], b2_ref[...])


def _ffn(h, wg, wu, wd, g2, b2, tm, tf, alpha):
    m, dm = h.shape
    nf = wg.shape[1] // tf
    return pl.pallas_call(
        functools.partial(_ffn_kernel, alpha=alpha),
        grid=(m // tm, nf),
        in_specs=[pl.BlockSpec((tm, dm), lambda i, f: (i, 0)),
                  pl.BlockSpec((dm, tf), lambda i, f: (0, f)),
                  pl.BlockSpec((dm, tf), lambda i, f: (0, f)),
                  pl.BlockSpec((tf, dm), lambda i, f: (f, 0)),
                  pl.BlockSpec(g2.shape, lambda i, f: (0, 0)),
                  pl.BlockSpec(b2.shape, lambda i, f: (0, 0))],
        out_specs=pl.BlockSpec((tm, dm), lambda i, f: (i, 0)),
        out_shape=jax.ShapeDtypeStruct((m, dm), F32),
        scratch_shapes=[pltpu.VMEM((tm, dm), F32), pltpu.VMEM((tm, dm), BF16)],
        compiler_params=_params("parallel", "arbitrary"),
        name="ffn",
    )(h, wg, wu, wd, g2, b2)


MOE_CHUNK_ROWS = 288
MOE_COL_TILE = 896
FFN_COL_TILE = 256


def _moe_kernel(h_ref, wrh_ref, wrl_ref, br_ref, wg_ref, wu_ref, wd_ref, g2_ref, b2_ref, o_ref,
                hb_s, xg_s, y_s, gate_s, slot_s, slot_t_s, *, alpha, n_exp, cr):
    e = pl.program_id(1)
    f = pl.program_id(2)
    nf = pl.num_programs(2)
    tm, dm = h_ref.shape
    first = jnp.logical_and(e == 0, f == 0)
    last = jnp.logical_and(e == pl.num_programs(1) - 1, f == nf - 1)

    @pl.when(first)
    def _():
        hhi, hlo = _split3(h_ref[...])
        hb_s[...] = hhi
        logits = _dot(hhi, wrh_ref[...]) + _dot(hhi, wrl_ref[...]) + _dot(hlo, wrh_ref[...]) + br_ref[...]
        lane = lax.broadcasted_iota(jnp.int32, logits.shape, 1)
        m1 = jnp.max(logits, axis=1, keepdims=True)
        i1 = jnp.min(jnp.where(logits == m1, lane, LANES), axis=1, keepdims=True)
        rest = jnp.where(lane == i1, -jnp.inf, logits)
        m2 = jnp.max(rest, axis=1, keepdims=True)
        i2 = jnp.min(jnp.where(rest == m2, lane, LANES), axis=1, keepdims=True)
        e2 = jnp.exp(m2 - m1)
        den = 1.0 + e2
        gate = jnp.where(lane == i1, 1.0 / den, 0.0) + jnp.where(lane == i2, e2 / den, 0.0)
        routed = jnp.logical_or(lane == i1, lane == i2)
        ind = jnp.where(routed, 1.0, 0.0).astype(BF16)
        tri = jnp.where(lax.broadcasted_iota(jnp.int32, (tm, tm), 0) >= lax.broadcasted_iota(jnp.int32, (tm, tm), 1),
                        1.0, 0.0).astype(BF16)
        slot = jnp.where(routed, _dot(tri, ind) - 1.0, -1.0)
        eye = jnp.where(lax.broadcasted_iota(jnp.int32, (LANES, LANES), 0)
                        == lax.broadcasted_iota(jnp.int32, (LANES, LANES), 1), 1.0, 0.0).astype(BF16)
        ind_t = _dot_t(eye, ind)
        slot_t = jnp.where(ind_t > 0.0, _dot_t(ind_t.astype(BF16), tri) - 1.0, -1.0)
        slot_t_s[...] = slot_t[0:slot_t_s.shape[0], :]
        for ex in range(n_exp):
            pick = lambda x: jnp.sum(jnp.where(lane == ex, x, 0.0), axis=1, keepdims=True)
            gate_s[ex] = jnp.broadcast_to(pick(gate), gate_s.shape[1:])
            slot_s[ex] = jnp.broadcast_to(pick(slot), slot_s.shape[1:])
        o_ref[...] = jnp.zeros(o_ref.shape, F32)

    slot_b = slot_s[e]
    n_pass = ((jnp.max(slot_b) + 1.0).astype(jnp.int32) + cr - 1) // cr

    def pass_rows(r):
        return pl.ds(pl.multiple_of(r * cr, 16), cr)

    @pl.when(f == 0)
    def _():
        srow = slot_t_s[pl.ds(e, 1), :]

        def pack(r, _):
            want = (r * cr + lax.broadcasted_iota(jnp.int32, (cr, tm), 0)).astype(F32)
            onehot = jnp.where(srow == want, 1.0, 0.0).astype(BF16)
            xg_s[pass_rows(r), :] = _dot(onehot, hb_s[...]).astype(BF16)
            return 0
        lax.fori_loop(0, n_pass, pack, 0)

    def expert_pass(r, _):
        x = xg_s[pass_rows(r), :]
        a = _dot(x, wg_ref[...])
        act = (a * jax.nn.sigmoid(a)) * _dot(x, wu_ref[...])
        y = _dot(act.astype(BF16), wd_ref[...])

        @pl.when(f == 0)
        def _():
            y_s[pass_rows(r), :] = y

        @pl.when(f > 0)
        def _():
            y_s[pass_rows(r), :] = y_s[pass_rows(r), :] + y
        return 0
    lax.fori_loop(0, n_pass, expert_pass, 0)

    @pl.when(f == nf - 1)
    def _():
        reps = (cr + LANES - 1) // LANES
        slot_w = jnp.concatenate([slot_b] * reps, axis=1)[:, :cr]
        gate_w = jnp.concatenate([gate_s[e]] * (dm // LANES), axis=1)

        def unpack(r, _):
            want = (r * cr + lax.broadcasted_iota(jnp.int32, (tm, cr), 1)).astype(F32)
            onehot = jnp.where(slot_w == want, 1.0, 0.0).astype(BF16)
            o_ref[...] += gate_w * _dot(onehot, y_s[pass_rows(r), :].astype(BF16))
            return 0
        lax.fori_loop(0, n_pass, unpack, 0)

    @pl.when(last)
    def _():
        o_ref[...] = _layer_norm(alpha * h_ref[...] + o_ref[...], g2_ref[...], b2_ref[...])


def _moe(h, wrh, wrl, br, wg, wu, wd, g2, b2, tm, tf, alpha):
    m, dm = h.shape
    n_exp, _, dff = wg.shape
    nf = dff // tf
    cr = min(MOE_CHUNK_ROWS, tm)
    cap = -(-tm // cr) * cr
    c2 = lambda a: pl.BlockSpec(a.shape, lambda i, e, f: (0, 0))
    return pl.pallas_call(
        functools.partial(_moe_kernel, alpha=alpha, n_exp=n_exp, cr=cr),
        grid=(m // tm, n_exp, nf),
        in_specs=[pl.BlockSpec((tm, dm), lambda i, e, f: (i, 0)),
                  c2(wrh), c2(wrl), c2(br),
                  pl.BlockSpec((None, dm, tf), lambda i, e, f: (e, 0, f)),
                  pl.BlockSpec((None, dm, tf), lambda i, e, f: (e, 0, f)),
                  pl.BlockSpec((None, tf, dm), lambda i, e, f: (e, f, 0)),
                  c2(g2), c2(b2)],
        out_specs=pl.BlockSpec((tm, dm), lambda i, e, f: (i, 0)),
        out_shape=jax.ShapeDtypeStruct((m, dm), F32),
        scratch_shapes=[pltpu.VMEM((tm, dm), BF16),
                        pltpu.VMEM((cap, dm), BF16),
                        pltpu.VMEM((cap, dm), F32),
                        pltpu.VMEM((n_exp, tm, LANES), F32),
                        pltpu.VMEM((n_exp, tm, LANES), F32),
                        pltpu.VMEM((max(8, n_exp), tm), F32)],
        compiler_params=_params("parallel", "arbitrary", "arbitrary"),
        name="moe",
    )(h, wrh, wrl, br, wg, wu, wd, g2, b2)


def _row_tile(m, target):
    t = math.gcd(m, target)
    return t if t % 8 == 0 else m


def _chunk_len(t):
    return math.gcd(t, 128)


def _rope_tables(pos, reps):
    half = HEAD_DIM // 2
    inv = ROPE_THETA ** (-(jnp.arange(half, dtype=F32) / half))
    ang = pos.astype(F32)[:, None] * inv[None, :]
    cos, sin = jnp.cos(ang), jnp.sin(ang)
    cos_h = jnp.concatenate([cos, cos], axis=-1)
    sin_h = jnp.concatenate([-sin, sin], axis=-1)
    return jnp.tile(cos_h, (1, reps)), jnp.tile(sin_h, (1, reps))


def _s5_params(a_re, a_im, b_re, b_im, c_re, c_im, log_dt):
    ng, ns = a_re.shape
    lr, li = a_re.astype(F32), a_im.astype(F32)
    dt = jnp.exp(log_dt.astype(F32))[:, None]
    mag = jnp.exp(lr * dt)
    ar, ai = mag * jnp.cos(li * dt), mag * jnp.sin(li * dt)
    den = lr * lr + li * li
    qr = ((ar - 1.0) * lr + ai * li) / den
    qi = (ai * lr - (ar - 1.0) * li) / den
    br, bi = b_re.astype(F32), b_im.astype(F32)
    bbr = qr[..., None] * br - qi[..., None] * bi
    bbi = qr[..., None] * bi + qi[..., None] * br
    eye = jnp.eye(ng, dtype=F32)
    gpl = LANES // ns
    nslab = ng // gpl

    def b_mat(x):
        full = jnp.einsum('gpc,gh->gchp', x, eye).reshape(ng * S5_CH, ng * ns)
        return jnp.transpose(full.reshape(ng * S5_CH, nslab, LANES), (1, 0, 2)).astype(BF16)

    def c_mat(x):
        return jnp.einsum('gcp,gh->gphc', x, eye).reshape(nslab, LANES, ng * S5_CH)

    ccat = jnp.concatenate([c_mat(c_re.astype(F32)), -c_mat(c_im.astype(F32))], axis=1).astype(BF16)
    return b_mat(bbr), b_mat(bbi), ar.reshape(nslab, 1, LANES), ai.reshape(nslab, 1, LANES), ccat


def _to_chunks(x, nl):
    nb, t, w = x.shape
    return jnp.swapaxes(x.reshape(nb, t // nl, nl, w), 1, 2)


def _from_chunks(x):
    nb, nl, nr, w = x.shape
    return jnp.swapaxes(x, 1, 2).reshape(nb, nr * nl, w)


def _mixers(lp, zmix, nb, t, s5_re0, s5_im0, lru_h0, lru_buf0, cc_buf0, chained):
    w = lp['gw']
    zmix3 = zmix.reshape(nb, t, zmix.shape[-1])
    tt = math.gcd(t, 512)
    a, b, ycc, lru_buf, cc_buf = _convmix(zmix3, lru_buf0, cc_buf0, lp['lru_conv_w'], lp['lru_conv_b'],
                                          lp['lru_wg'], lp['lru_bg'], lp['lru_sp'], lp['cc_dw_w'], lp['cc_dw_b'],
                                          lp['cc_ln_g'], lp['cc_ln_b'], tt)
    u = zmix3[:, :, :w].astype(BF16)
    ns = s5_re0.shape[-2] * s5_re0.shape[-1]
    if chained:
        nl = _chunk_len(t)
        a4, b4, u4 = _to_chunks(a, nl), _to_chunks(b, nl), _to_chunks(u, nl)
        h0 = lru_h0.reshape(nb, 1, w)
        s0r, s0i = s5_re0.reshape(nb, 1, ns), s5_im0.reshape(nb, 1, ns)
    else:
        to_rows = lambda x: jnp.swapaxes(x, 0, 1)[None]
        a4, b4, u4 = to_rows(a), to_rows(b), to_rows(u)
        h0 = lru_h0.reshape(1, nb, w)
        s0r, s0i = s5_re0.reshape(1, nb, ns), s5_im0.reshape(1, nb, ns)
    h4, lru_h = _lru_scan(a4, b4, h0, chained)
    ys4, s5_re, s5_im = _s5_scan(u4, lp['s5_bre'], lp['s5_bim'], lp['s5_are'], lp['s5_aim'], lp['s5_ccat'],
                                 s0r, s0i, chained)
    if chained:
        h, ys = _from_chunks(h4), _from_chunks(ys4)
    else:
        h, ys = jnp.swapaxes(h4[0], 0, 1), jnp.swapaxes(ys4[0], 0, 1)
    states = (s5_re.reshape(s5_re0.shape), s5_im.reshape(s5_im0.shape), lru_h.reshape(lru_h0.shape),
              lru_buf, cc_buf)
    return ys.reshape(nb * t, w), h.reshape(nb * t, w), ycc.reshape(nb * t, w), states


def _idx3(x, order):
    hi, lo = _split3(x)
    parts = [hi if o == 'h' else lo for o in order]
    pad = jnp.zeros(x.shape[:-1] + (2 * LANES - len(order) * x.shape[-1],), BF16)
    return jnp.concatenate(parts + [pad], axis=-1)


def _attend_prompt(q, qi, k, v, kw, nb, t):
    w = q.shape[-1]
    topk = min(TOPK_MAX, t // 4)
    to_cols = lambda x: jnp.swapaxes(x.reshape(nb, t, -1), 1, 2)
    qt = to_cols(q)
    qih, qil = _split3(qi)
    wi = kw[:, IDX_DIM:IDX_DIM + IDX_HEADS].reshape(nb, t, IDX_HEADS)
    wit = jnp.pad(jnp.swapaxes(wi, 1, 2), ((0, 0), (0, 8 - IDX_HEADS), (0, 0)))
    kidx3 = _idx3(kw[:, :IDX_DIM].reshape(nb, t, IDX_DIM), 'hlh')
    kb = k.reshape(nb, t, w).astype(BF16)
    vt = jnp.swapaxes(v.reshape(nb, t, w), 1, 2).astype(BF16)
    kc = math.gcd(t // 2, 512)
    yt = _dsa_prompt(qt, to_cols(qih), to_cols(qil), wit, kidx3, kb, vt, topk, kc)
    return jnp.swapaxes(yt, 1, 2).reshape(nb * t, w)


def _attend_sample(layer, q, qi, k, v, kw, nb, t, pool_k, pool_v, pool_kidx, page_table):
    w = q.shape[-1]
    n_heads = w // HEAD_DIM
    psz = pool_kidx.shape[2]
    past = page_table.shape[1] * psz
    topk = min(TOPK_MAX, (past + t) // 4)
    pad_q = lambda x: jnp.pad(x, ((0, 0), (0, 0), (0, QROWS - t), (0, 0)))

    qi_h = jnp.swapaxes(qi.reshape(nb, t, IDX_HEADS, IDX_DIM), 1, 2)
    qhi, qlo = _split3(pad_q(qi_h).reshape(nb, IDX_HEADS * QROWS, IDX_DIM))
    wi = kw[:, IDX_DIM:IDX_DIM + IDX_HEADS].reshape(nb, t, IDX_HEADS)
    wi_b = jnp.broadcast_to(pad_q(jnp.swapaxes(wi, 1, 2)[..., None]).reshape(nb, IDX_HEADS * QROWS, 1),
                            (nb, IDX_HEADS * QROWS, psz))
    new_page_t = lambda x: jnp.swapaxes(jnp.pad(x, ((0, 0), (0, psz - t), (0, 0))), 1, 2)
    kidx_new = new_page_t(kw[:, :IDX_DIM].reshape(nb, t, IDX_DIM))
    bias = _dsa_sample_select(layer, page_table, qhi, qlo, wi_b, kidx_new, jnp.swapaxes(pool_kidx, 2, 3),
                              topk, t)

    head_of_lane = jnp.arange(w) // HEAD_DIM
    q_rows = jnp.broadcast_to(pad_q(q.reshape(nb, 1, t, w)), (nb, n_heads, QROWS, w))
    q32 = jnp.where(head_of_lane[None, None, None, :] == jnp.arange(n_heads)[None, :, None, None],
                    q_rows, jnp.zeros_like(q_rows)).reshape(nb, n_heads * QROWS, w)
    k_new = new_page_t(k.reshape(nb, t, w))
    v_new = new_page_t(v.reshape(nb, t, w))
    pool_t = lambda x: jnp.transpose(x, (0, 1, 3, 4, 2)).reshape(x.shape[:2] + (w, psz))
    y = _dsa_sample_attend(layer, page_table, q32, k_new, v_new, bias, pool_t(pool_k), pool_t(pool_v))
    return y[:, :t, :].reshape(nb * t, w)


def _prep_layer(l, w_in, s5_a_re, s5_a_im, s5_b_re, s5_b_im, s5_c_re, s5_c_im, s5_d, s5_log_dt, s5_w_glu,
                s5_b_glu, lru_conv_w, lru_conv_b, lru_w_a, lru_b_a, lru_w_x, lru_b_x, lru_lambda, cc_dw_w,
                cc_dw_b, cc_ln_g, cc_ln_b, w_out, ln1_g, ln1_b, ln2_g, ln2_b):
    gw = s5_d.shape[1]
    d_in = w_in.shape[2]
    mix_w = 5 * gw
    pad_to = mix_w + 4 * gw + LANES
    assert d_in == mix_w + 4 * gw + IDX_DIM + IDX_HEADS and gw == IDX_HEADS * IDX_DIM
    row = lambda x: x[l].reshape(1, -1).astype(F32)
    nh, hd, _ = lru_w_a.shape[1:]
    eye = jnp.eye(nh, dtype=F32)
    bd = lambda wt: jnp.einsum('hij,hk->hikj', wt[l].astype(F32), eye).reshape(nh * hd, nh * hd)
    s5_bre, s5_bim, s5_are, s5_aim, s5_ccat = _s5_params(s5_a_re[l], s5_a_im[l], s5_b_re[l], s5_b_im[l],
                                                         s5_c_re[l], s5_c_im[l], s5_log_dt[l])
    return dict(
        gw=gw, mix_w=mix_w,
        w_in=jnp.pad(w_in[l], ((0, 0), (0, pad_to - d_in))).astype(BF16),
        s5_bre=s5_bre, s5_bim=s5_bim, s5_are=s5_are, s5_aim=s5_aim, s5_ccat=s5_ccat,
        s5_d=row(s5_d), s5_w_glu=s5_w_glu[l].astype(BF16), s5_b_glu=row(s5_b_glu),
        lru_conv_w=lru_conv_w[l].astype(F32), lru_conv_b=row(lru_conv_b),
        lru_wg=jnp.concatenate([bd(lru_w_a), bd(lru_w_x)], axis=1).astype(BF16),
        lru_bg=jnp.concatenate([row(lru_b_a), row(lru_b_x)], axis=1),
        lru_sp=jax.nn.softplus(-row(lru_lambda)),
        cc_dw_w=cc_dw_w[l].astype(F32), cc_dw_b=row(cc_dw_b), cc_ln_g=row(cc_ln_g), cc_ln_b=row(cc_ln_b),
        w_out=w_out[l].astype(BF16), ln1_g=row(ln1_g), ln1_b=row(ln1_b), ln2_g=row(ln2_g), ln2_b=row(ln2_b),
    )


def kernel(x_prompt, x_sample, cache_k, cache_v, cache_kidx, state_s5_re, state_s5_im, state_lru_h, state_lru_conv, state_cc_conv, page_table, w_in, s5_a_re, s5_a_im, s5_b_re, s5_b_im, s5_c_re, s5_c_im, s5_d, s5_log_dt, s5_w_glu, s5_b_glu, lru_conv_w, lru_conv_b, lru_w_a, lru_b_a, lru_w_x, lru_b_x, lru_lambda, cc_dw_w, cc_dw_b, cc_ln_g, cc_ln_b, w_out, ln1_g, ln1_b, ln2_g, ln2_b, ffn_w_gate, ffn_w_up, ffn_w_down, moe_w_router, moe_b_router, moe_w_gate, moe_w_up, moe_w_down):
    bp, tp, dm = x_prompt.shape
    bs, ts, _ = x_sample.shape
    depth = w_in.shape[0]
    past = page_table.shape[1] * cache_k.shape[2]
    alpha = (2.0 * depth) ** 0.25
    n_heads = cache_k.shape[3]
    n_exp = moe_w_router.shape[-1]
    gw = s5_d.shape[1]

    cos_p, sin_p = _rope_tables(jnp.arange(tp, dtype=jnp.int32), gw // HEAD_DIM)
    cos_s, sin_s = _rope_tables(past + jnp.arange(ts, dtype=jnp.int32), gw // HEAD_DIM)
    cos_s, sin_s = jnp.tile(cos_s, (bs, 1)), jnp.tile(sin_s, (bs, 1))

    tm_p = _row_tile(tp, 512)
    xp = x_prompt.reshape(bp * tp, dm)
    xs = x_sample.reshape(bs * ts, dm)
    zeros_p = lambda *shape: jnp.zeros((bp,) + shape, x_prompt.dtype)
    states_p, states_s = [], []

    for l in range(depth):
        lp = _prep_layer(l, w_in, s5_a_re, s5_a_im, s5_b_re, s5_b_im, s5_c_re, s5_c_im, s5_d, s5_log_dt,
                         s5_w_glu, s5_b_glu, lru_conv_w, lru_conv_b, lru_w_a, lru_b_a, lru_w_x, lru_b_x,
                         lru_lambda, cc_dw_w, cc_dw_b, cc_ln_g, cc_ln_b, w_out, ln1_g, ln1_b, ln2_g, ln2_b)
        mix_w = lp['mix_w']
        j = l // 2
        if l % 2 == 0:
            tf = math.gcd(ffn_w_gate.shape[2], FFN_COL_TILE)
            cw = (ffn_w_gate[j].astype(BF16), ffn_w_up[j].astype(BF16), ffn_w_down[j].astype(BF16))
        else:
            wr = jnp.pad(moe_w_router[j].astype(F32), ((0, 0), (0, LANES - n_exp)))
            wrh, wrl = _split3(wr)
            br = jnp.pad(moe_b_router[j].astype(F32), (0, LANES - n_exp), constant_values=NEG_BIG).reshape(1, LANES)
            tf = math.gcd(moe_w_gate.shape[3], MOE_COL_TILE)
            cw = (wrh, wrl, br, moe_w_gate[j].astype(BF16), moe_w_up[j].astype(BF16), moe_w_down[j].astype(BF16))

        def channel(h, tm):
            if l % 2 == 0:
                return _ffn(h, *cw, lp['ln2_g'], lp['ln2_b'], tm, tf, alpha)
            return _moe(h, *cw, lp['ln2_g'], lp['ln2_b'], tm, tf, alpha)

        zmix, q, qi, k, v, kw = _in_proj(xp, lp['w_in'], cos_p, sin_p, tm_p, mix_w, gw)
        ys, h, ycc, st = _mixers(lp, zmix, bp, tp,
                                 zeros_p(*state_s5_re.shape[2:]), zeros_p(*state_s5_im.shape[2:]),
                                 zeros_p(*state_lru_h.shape[2:]), zeros_p(*state_lru_conv.shape[2:]),
                                 zeros_p(*state_cc_conv.shape[2:]), chained=True)
        yatt = _attend_prompt(q, qi, k, v, kw, bp, tp)
        h1 = _mixout(xp, zmix, ys, h, ycc, yatt, lp['s5_d'], lp['s5_w_glu'], lp['s5_b_glu'], lp['w_out'],
                     lp['ln1_g'], lp['ln1_b'], tm_p, alpha)
        xp = channel(h1, _row_tile(bp * tp, 1024))
        states_p.append((k.reshape(bp, tp, n_heads, HEAD_DIM), v.reshape(bp, tp, n_heads, HEAD_DIM),
                         kw[:, :IDX_DIM].reshape(bp, tp, IDX_DIM)) + st)

        ms = bs * ts
        zmix, q, qi, k, v, kw = _in_proj(xs, lp['w_in'], cos_s, sin_s, ms, mix_w, gw)
        ys, h, ycc, st = _mixers(lp, zmix, bs, ts, state_s5_re[l], state_s5_im[l], state_lru_h[l],
                                 state_lru_conv[l], state_cc_conv[l], chained=False)
        yatt = _attend_sample(l, q, qi, k, v, kw, bs, ts, cache_k, cache_v, cache_kidx, page_table)
        h1 = _mixout(xs, zmix, ys, h, ycc, yatt, lp['s5_d'], lp['s5_w_glu'], lp['s5_b_glu'], lp['w_out'],
                     lp['ln1_g'], lp['ln1_b'], ms, alpha)
        xs = channel(h1, ms)
        states_s.append((k.reshape(bs, ts, n_heads, HEAD_DIM), v.reshape(bs, ts, n_heads, HEAD_DIM),
                         kw[:, :IDX_DIM].reshape(bs, ts, IDX_DIM)) + st)

    new_p = [jnp.stack(col) for col in zip(*states_p)]
    new_s = [jnp.stack(col) for col in zip(*states_s)]
    out = [xp.reshape(bp, tp, dm), xs.reshape(bs, ts, dm)]
    for a, b in zip(new_p, new_s):
        out += [a, b]
    return tuple(out)
```

```python
import functools
import math

import jax
import jax.numpy as jnp
from jax import lax
from jax.experimental import pallas as pl
from jax.experimental.pallas import tpu as pltpu

F32 = jnp.float32
BF16 = jnp.bfloat16

S5_CH = 16
LRU_C = 8.0
HEAD_DIM = 64
IDX_DIM = 64
IDX_HEADS = 4
TOPK_MAX = 256
Q_BLOCK = 128
ROPE_THETA = 10000.0
MOE_TOP_K = 2
LN_EPS = 1e-5
ATT_SCALE = HEAD_DIM ** -0.5
IDX_SCALE = (IDX_DIM * IDX_HEADS) ** -0.5

LANES = 128
NEG_BIG = -1e30
VMEM_LIMIT = 56 * 1024 * 1024


def _params(*sem):
    return pltpu.CompilerParams(dimension_semantics=sem, vmem_limit_bytes=VMEM_LIMIT)


def _dot(a, b):
    return jnp.dot(a, b, preferred_element_type=F32)


def _dot_t(a, b):
    return lax.dot_general(a, b, (((1,), (1,)), ((), ())), preferred_element_type=F32)


def _split3(x):
    hi = x.astype(BF16)
    lo = (x - hi.astype(F32)).astype(BF16)
    return hi, lo


def _layer_norm(x, g, b):
    mu = jnp.mean(x, axis=-1, keepdims=True)
    xc = x - mu
    var = jnp.mean(xc * xc, axis=-1, keepdims=True)
    return xc * lax.rsqrt(var + LN_EPS) * g + b


def _rope(z, cos, sin):
    w = z.shape[-1]
    half = HEAD_DIM // 2
    lane = lax.broadcasted_iota(jnp.int32, z.shape, 1)
    first = (lane & (HEAD_DIM - 1)) < half
    partner = jnp.where(first, pltpu.roll(z, w - half, axis=1), pltpu.roll(z, half, axis=1))
    return z * cos + partner * sin


def _in_proj_kernel(x_ref, w_ref, cos_ref, sin_ref, zmix_ref, q_ref, qi_ref, k_ref, v_ref, kw_ref,
                    *, mix_w, gw):
    xb = x_ref[...].astype(BF16)
    cos = cos_ref[...]
    sin = sin_ref[...]

    def proj(lo, width):
        return _dot(xb, w_ref[:, lo:lo + width])

    zmix_ref[...] = proj(0, mix_w)
    o = mix_w
    q_ref[...] = (_rope(proj(o, gw), cos, sin) * ATT_SCALE).astype(BF16)
    k_ref[...] = _rope(proj(o + gw, gw), cos, sin)
    v_ref[...] = proj(o + 2 * gw, gw)
    qi_ref[...] = _rope(proj(o + 3 * gw, gw), cos, sin)
    kw = proj(o + 4 * gw, LANES)
    is_key = lax.broadcasted_iota(jnp.int32, kw.shape, 1) < IDX_DIM
    kw_ref[...] = _rope(kw, jnp.where(is_key, cos[:, :LANES], 1.0), jnp.where(is_key, sin[:, :LANES], 0.0))


def _in_proj(x2, w_pad, cos_t, sin_t, tm, mix_w, gw):
    m, d = x2.shape
    tab_blocks = cos_t.shape[0] // tm
    row = lambda i: (i, 0)
    tab = lambda i: (i % tab_blocks, 0)
    outs = [
        jax.ShapeDtypeStruct((m, mix_w), F32),
        jax.ShapeDtypeStruct((m, gw), BF16),
        jax.ShapeDtypeStruct((m, gw), F32),
        jax.ShapeDtypeStruct((m, gw), F32),
        jax.ShapeDtypeStruct((m, gw), F32),
        jax.ShapeDtypeStruct((m, LANES), F32),
    ]
    return pl.pallas_call(
        functools.partial(_in_proj_kernel, mix_w=mix_w, gw=gw),
        grid=(m // tm,),
        in_specs=[pl.BlockSpec((tm, d), row),
                  pl.BlockSpec(w_pad.shape, lambda i: (0, 0)),
                  pl.BlockSpec((tm, gw), tab),
                  pl.BlockSpec((tm, gw), tab)],
        out_specs=[pl.BlockSpec((tm, s.shape[1]), row) for s in outs],
        out_shape=outs,
        compiler_params=_params("parallel"),
        name="in_proj",
    )(x2, w_pad, cos_t, sin_t)


LRU_HALO = 8
CC_HALO = 32


def _convmix_kernel(xl_ref, ca_ref, cg_ref, lbuf_ref, cbuf_ref, lw_ref, lb_ref, wg_ref, bg_ref, sp_ref,
                    cw_ref, cb_ref, lng_ref, lnb_ref,
                    a_ref, b_ref, ycc_ref, lbuf_o, cbuf_o, xl_s, xc_s, *, tt):
    i = pl.program_id(1)
    kl = lw_ref.shape[0]
    kc = cw_ref.shape[0]
    w = xl_ref.shape[-1]
    l0 = LRU_HALO - (kl - 1)
    c0 = CC_HALO - (kc - 1)

    @pl.when(i == 0)
    def _():
        xl_s[l0:LRU_HALO, :] = lbuf_ref[...]
        xc_s[c0:CC_HALO, :] = cbuf_ref[...]

    xl_s[LRU_HALO:LRU_HALO + tt, :] = xl_ref[...]
    xc_s[CC_HALO:CC_HALO + tt, :] = ca_ref[...] * jax.nn.sigmoid(cg_ref[...])

    conv = xl_s[l0:l0 + tt, :] * lw_ref[0:1, :]
    for j in range(1, kl):
        conv = conv + xl_s[l0 + j:l0 + j + tt, :] * lw_ref[j:j + 1, :]
    conv = conv + lb_ref[...]
    gates = _dot(conv.astype(BF16), wg_ref[...]) + bg_ref[...]
    r = jax.nn.sigmoid(gates[:, :w])
    ig = jax.nn.sigmoid(gates[:, w:])
    log_a = (-LRU_C) * r * sp_ref[...]
    a = jnp.exp(log_a)
    a_ref[...] = a
    b_ref[...] = jnp.sqrt(-jnp.tanh(log_a) * (a * a + 1.0)) * (ig * conv)

    c = xc_s[c0:c0 + tt, :] * cw_ref[0:1, :]
    for j in range(1, kc):
        c = c + xc_s[c0 + j:c0 + j + tt, :] * cw_ref[j:j + 1, :]
    c = _layer_norm(c + cb_ref[...], lng_ref[...], lnb_ref[...])
    ycc_ref[...] = c * jax.nn.sigmoid(c)

    new_l = xl_s[l0 + tt:LRU_HALO + tt, :]
    new_c = xc_s[c0 + tt:CC_HALO + tt, :]
    xl_s[l0:LRU_HALO, :] = new_l
    xc_s[c0:CC_HALO, :] = new_c
    lbuf_o[...] = new_l
    cbuf_o[...] = new_c


def _convmix(zmix3, lbuf, cbuf, lw, lb, wg, bg, sp, cw, cb, lng, lnb, tt):
    nb, t, _ = zmix3.shape
    w = lw.shape[1]
    col = lambda c: pl.BlockSpec((None, tt, w), lambda b, i, c=c: (b, i, c))
    full2 = lambda a: pl.BlockSpec(a.shape, lambda b, i: (0, 0))
    per_b = lambda a: pl.BlockSpec((None,) + a.shape[1:], lambda b, i: (b, 0, 0))
    seq_out = jax.ShapeDtypeStruct((nb, t, w), F32)
    return pl.pallas_call(
        functools.partial(_convmix_kernel, tt=tt),
        grid=(nb, t // tt),
        in_specs=[col(1), col(3), col(4), per_b(lbuf), per_b(cbuf), full2(lw), full2(lb), full2(wg),
                  full2(bg), full2(sp), full2(cw), full2(cb), full2(lng), full2(lnb)],
        out_specs=[pl.BlockSpec((None, tt, w), lambda b, i: (b, i, 0))] * 3 + [per_b(lbuf), per_b(cbuf)],
        out_shape=[seq_out, seq_out, seq_out,
                   jax.ShapeDtypeStruct(lbuf.shape, F32), jax.ShapeDtypeStruct(cbuf.shape, F32)],
        scratch_shapes=[pltpu.VMEM((LRU_HALO + tt, w), F32), pltpu.VMEM((CC_HALO + tt, w), F32)],
        compiler_params=_params("parallel", "arbitrary"),
        name="convmix",
    )(zmix3, zmix3, zmix3, lbuf, cbuf, lw, lb, wg, bg, sp, cw, cb, lng, lnb)


def _lru_scan_kernel(a_ref, b_ref, h0_ref, h_ref, hl_ref, e_s, p_s, hi_s, *, chained):
    nl, nr, _ = a_ref.shape

    def run(init):
        def step(t, h):
            h = a_ref[t] * h + b_ref[t]
            h_ref[t] = h
            return h
        return lax.fori_loop(0, nl, step, init)

    if not chained:
        hl_ref[...] = run(h0_ref[...])
        return

    def local(t, c):
        h, p = c
        at = a_ref[t]
        return at * h + b_ref[t], at * p
    zeros = jnp.zeros(e_s.shape, F32)
    e, p = lax.fori_loop(0, nl, local, (zeros, zeros + 1.0))
    e_s[...] = e
    p_s[...] = p
    hi_s[0:1, :] = h0_ref[...]

    def carry(c, _):
        prev = pl.ds(c - 1, 1)
        hi_s[pl.ds(c, 1), :] = p_s[prev, :] * hi_s[prev, :] + e_s[prev, :]
        return 0
    lax.fori_loop(1, nr, carry, 0)
    last = run(hi_s[...])
    hl_ref[...] = last[nr - 1:nr, :]


def _lru_scan(a4, b4, h0, chained):
    nb, nl, nr, w = a4.shape
    r0 = h0.shape[1]
    nw = w // LANES
    blk = pl.BlockSpec((None, nl, nr, LANES), lambda n, j: (n, 0, 0, j))
    st = pl.BlockSpec((None, r0, LANES), lambda n, j: (n, 0, j))
    return pl.pallas_call(
        functools.partial(_lru_scan_kernel, chained=chained),
        grid=(nb, nw),
        in_specs=[blk, blk, st],
        out_specs=[blk, st],
        out_shape=[jax.ShapeDtypeStruct(a4.shape, F32), jax.ShapeDtypeStruct(h0.shape, F32)],
        scratch_shapes=[pltpu.VMEM((nr, LANES), F32)] * 3,
        compiler_params=_params("parallel", "parallel"),
        name="lru_scan",
    )(a4, b4, h0)


def _s5_scan_kernel(u_ref, bre_ref, bim_ref, are_ref, aim_ref, cc_ref, h0r_ref, h0i_ref,
                    ys_ref, hlr_ref, hli_ref, br_s, bi_s, er_s, ei_s, hr_s, hi_s, *, chained, rows_per_dot):
    g = pl.program_id(1)
    nl, nr, _ = u_ref.shape
    n = nl * nr
    ar = are_ref[...]
    ai = aim_ref[...]

    def drive(c, _):
        r0 = pl.multiple_of(c * rows_per_dot, rows_per_dot)
        ub = u_ref[pl.ds(c * (rows_per_dot // nr), rows_per_dot // nr)].reshape(rows_per_dot, u_ref.shape[-1])
        br_s[pl.ds(r0, rows_per_dot), :] = _dot(ub, bre_ref[...])
        bi_s[pl.ds(r0, rows_per_dot), :] = _dot(ub, bim_ref[...])
        return 0
    lax.fori_loop(0, n // rows_per_dot, drive, 0)

    def step_fn(store):
        def step(t, c):
            hr, hi = c
            rows = pl.ds(pl.multiple_of(t * nr, nr), nr)
            nhr = ar * hr - ai * hi + br_s[rows, :]
            nhi = ar * hi + ai * hr + bi_s[rows, :]
            if store:
                br_s[rows, :] = nhr
                bi_s[rows, :] = nhi
            return nhr, nhi
        return step

    if chained:
        zeros = jnp.zeros((nr, LANES), F32)
        er, ei = lax.fori_loop(0, nl, step_fn(False), (zeros, zeros))
        er_s[...] = er
        ei_s[...] = ei

        def power(_, c):
            pr, pi = c
            return ar * pr - ai * pi, ar * pi + ai * pr
        alr, ali = lax.fori_loop(0, nl - 1, power, (ar, ai))
        hr_s[0:1, :] = h0r_ref[...]
        hi_s[0:1, :] = h0i_ref[...]

        def carry(c, _):
            prev = pl.ds(c - 1, 1)
            pr = hr_s[prev, :]
            pi = hi_s[prev, :]
            hr_s[pl.ds(c, 1), :] = alr * pr - ali * pi + er_s[prev, :]
            hi_s[pl.ds(c, 1), :] = alr * pi + ali * pr + ei_s[prev, :]
            return 0
        lax.fori_loop(1, nr, carry, 0)
        init = (hr_s[...], hi_s[...])
    else:
        init = (h0r_ref[...], h0i_ref[...])

    lr, li = lax.fori_loop(0, nl, step_fn(True), init)
    if chained:
        hlr_ref[...] = lr[nr - 1:nr, :]
        hli_ref[...] = li[nr - 1:nr, :]
    else:
        hlr_ref[...] = lr
        hli_ref[...] = li

    def project(c, _):
        r0 = pl.multiple_of(c * rows_per_dot, rows_per_dot)
        rows = pl.ds(r0, rows_per_dot)
        hcat = jnp.concatenate([br_s[rows, :], bi_s[rows, :]], axis=1).astype(BF16)
        y = _dot(hcat, cc_ref[...]).reshape(rows_per_dot // nr, nr, ys_ref.shape[-1])
        tsl = pl.ds(c * (rows_per_dot // nr), rows_per_dot // nr)

        @pl.when(g == 0)
        def _():
            ys_ref[tsl] = y

        @pl.when(g > 0)
        def _():
            ys_ref[tsl] = ys_ref[tsl] + y
        return 0
    lax.fori_loop(0, n // rows_per_dot, project, 0)


def _s5_scan(u4, bre, bim, are, aim, ccat, h0r, h0i, chained):
    nb, nl, nr, w = u4.shape
    ng = bre.shape[0]
    r0 = h0r.shape[1]
    n = nl * nr
    rows_per_dot = math.gcd(n, 1024)
    rows_per_dot = max(rows_per_dot, nr)
    seq = pl.BlockSpec((None, nl, nr, w), lambda b, g: (b, 0, 0, 0))
    per_g = lambda a: pl.BlockSpec((None,) + a.shape[1:], lambda b, g: (g, 0, 0))
    st = pl.BlockSpec((None, r0, LANES), lambda b, g: (b, 0, g))
    small = pltpu.VMEM((nr, LANES), F32)
    return pl.pallas_call(
        functools.partial(_s5_scan_kernel, chained=chained, rows_per_dot=rows_per_dot),
        grid=(nb, ng),
        in_specs=[seq, per_g(bre), per_g(bim), per_g(are), per_g(aim), per_g(ccat), st, st],
        out_specs=[seq, st, st],
        out_shape=[jax.ShapeDtypeStruct(u4.shape, F32), jax.ShapeDtypeStruct(h0r.shape, F32),
                   jax.ShapeDtypeStruct(h0i.shape, F32)],
        scratch_shapes=[pltpu.VMEM((n, LANES), F32), pltpu.VMEM((n, LANES), F32), small, small, small, small],
        compiler_params=_params("parallel", "arbitrary"),
        name="s5_scan",
    )(u4, bre, bim, are, aim, ccat, h0r, h0i)


BISECT_ITERS = 24
PROBE_QUERIES = 12
PROBE_GAP = 3


def _select_threshold(count_ge, max_below, mn, mx, kq, all_sel, probe_queries):
    inf = jnp.full(mn.shape, jnp.inf, F32)
    zero = jnp.zeros(mn.shape, F32)
    state = (mn, inf, mn, jnp.where(all_sel, 1.0, zero), zero)

    def unfinished(st):
        return jnp.min(st[3]) == 0.0

    def bisect(c):
        it, (lo, hi, theta, done, tie) = c
        mid = jnp.where(hi == jnp.inf, mx, 0.5 * lo + 0.5 * hi)
        cnt = count_ge(mid)
        live = done == 0.0
        collapsed = jnp.logical_and(live, jnp.logical_or(mid == lo, mid == hi))
        hit = jnp.logical_and(live, cnt == kq)
        move = jnp.logical_and(live, jnp.logical_not(collapsed))
        theta = jnp.where(hit, mid, jnp.where(collapsed, hi, theta))
        tie = jnp.where(collapsed, 1.0, tie)
        done = jnp.where(jnp.logical_or(hit, collapsed), 1.0, done)
        lo = jnp.where(jnp.logical_and(move, cnt > kq), mid, lo)
        hi = jnp.where(jnp.logical_and(move, cnt < kq), mid, hi)
        return it + 1, (lo, hi, theta, done, tie)

    _, state = lax.while_loop(
        lambda c: jnp.logical_and(c[0] < BISECT_ITERS, jnp.sum(1.0 - c[1][3]) > float(probe_queries)),
        lambda c: bisect(bisect(c)), (jnp.int32(0), state))

    def exact_step(st):
        lo, hi, theta, done, tie = st
        live = done == 0.0
        tau = max_below(hi)
        cnt = count_ge(tau)
        found = jnp.logical_and(live, cnt >= kq)
        clean = jnp.logical_and(found, cnt == kq)
        theta = jnp.where(clean, tau, jnp.where(found, hi, theta))
        tie = jnp.where(jnp.logical_and(found, jnp.logical_not(clean)), 1.0, tie)
        lo = jnp.where(found, tau, lo)
        hi = jnp.where(jnp.logical_and(live, jnp.logical_not(found)), tau, hi)
        done = jnp.where(found, 1.0, done)
        return lo, hi, theta, done, tie

    def settle(st):
        st = exact_step(st)
        _, st = lax.while_loop(lambda c: jnp.logical_and(c[0] < PROBE_GAP, unfinished(c[1])),
                               bisect, (jnp.int32(0), st))
        return st

    lo, hi, theta, done, tie = lax.while_loop(unfinished, settle, state)
    return theta, lo, tie


COUNT_ROWS = 64


def _dsa_prompt_kernel(qt_ref, qih_ref, qil_ref, wit_ref, kidx_ref, k_ref, vt_ref, o_ref,
                       s_scr, lg_a, lg_b, qp_s, qip_s, *, topk, kc):
    i = pl.program_id(1)
    qb = Q_BLOCK
    n_pairs = qp_s.shape[0]
    pw = 2 * HEAD_DIM

    zq = jnp.zeros((HEAD_DIM, qb), BF16)
    for p in range(n_pairs):
        top = jnp.concatenate([qt_ref[p * pw:p * pw + HEAD_DIM, :], zq], axis=1)
        bot = jnp.concatenate([zq, qt_ref[p * pw + HEAD_DIM:(p + 1) * pw, :]], axis=1)
        qp_s[p] = jnp.concatenate([top, bot], axis=0)
        cols = []
        for h in (2 * p, 2 * p + 1):
            hi = qih_ref[h * IDX_DIM:(h + 1) * IDX_DIM, :]
            cols.append(jnp.concatenate([hi, hi, qil_ref[h * IDX_DIM:(h + 1) * IDX_DIM, :], zq], axis=0))
        qip_s[p] = jnp.concatenate(cols, axis=1)
    nch = ((i + 1) * qb + kc - 1) // kc
    qpos = i * qb + lax.broadcasted_iota(jnp.int32, (1, qb), 1)
    kiota = lax.broadcasted_iota(jnp.int32, (kc, qb), 0)

    def chunk_rows(c):
        return pl.ds(pl.multiple_of(c * kc, kc), kc)

    def score_chunk(c, carry):
        mn, mx = carry
        kk = kidx_ref[chunk_rows(c), :]
        s = jnp.zeros((kc, qb), F32)
        for p in range(n_pairs):
            sp = _dot(kk, qip_s[p])
            for hh in range(2):
                h = 2 * p + hh
                s = s + jnp.maximum(sp[:, hh * qb:(hh + 1) * qb], 0.0) * wit_ref[h:h + 1, :]
        s = s * IDX_SCALE
        vis = (c * kc + kiota) <= qpos
        s_scr[chunk_rows(c), :] = jnp.where(vis, s, -jnp.inf)
        mn = jnp.minimum(mn, jnp.min(jnp.where(vis, s, jnp.inf), axis=0, keepdims=True))
        mx = jnp.maximum(mx, jnp.max(jnp.where(vis, s, -jnp.inf), axis=0, keepdims=True))
        return mn, mx
    nch2 = (nch + 1) // 2
    mn, mx = lax.fori_loop(0, 2 * nch2, score_chunk,
                           (jnp.full((1, qb), jnp.inf, F32), jnp.full((1, qb), -jnp.inf, F32)))

    def fold(x, op):
        return op(x.reshape(kc // COUNT_ROWS, COUNT_ROWS, qb), axis=0)

    def count(pred_fn):
        def body(c, acc):
            hit = jnp.where(pred_fn(s_scr[chunk_rows(c), :], c * kc + kiota), 1.0, 0.0)
            return acc + fold(hit, jnp.sum)
        acc = lax.fori_loop(0, nch, body, jnp.zeros((COUNT_ROWS, qb), F32))
        return jnp.sum(acc, axis=0, keepdims=True)

    def max_below(x):
        def body(c, acc):
            s = s_scr[chunk_rows(c), :]
            return jnp.maximum(acc, fold(jnp.where(s < x, s, -jnp.inf), jnp.max))
        acc = lax.fori_loop(0, nch, body, jnp.full((COUNT_ROWS, qb), -jnp.inf, F32))
        return jnp.max(acc, axis=0, keepdims=True)

    n_vis = (qpos + 1).astype(F32)
    kq = jnp.minimum(n_vis, float(topk))
    theta, lo, tie = _select_threshold(lambda x: count(lambda s, kpos: s >= x), max_below,
                                       mn, mx, kq, n_vis <= float(topk), PROBE_QUERIES)

    @pl.when(jnp.max(tie) > 0.0)
    def _():
        need = kq - count(lambda s, kpos: s >= theta)
        tied = lambda s: jnp.logical_and(tie > 0.0, s == lo)
        tri = jnp.where(lax.broadcasted_iota(jnp.int32, (kc, kc), 0) >= lax.broadcasted_iota(jnp.int32, (kc, kc), 1),
                        1.0, 0.0).astype(BF16)

        def promote(c, seen):
            s = s_scr[chunk_rows(c), :]
            is_tied = tied(s)
            rank = seen + _dot(tri, jnp.where(is_tied, 1.0, 0.0).astype(BF16))
            keep = jnp.logical_and(is_tied, rank <= need)
            s_scr[chunk_rows(c), :] = jnp.where(keep, theta, s)
            return rank[kc - 1:kc, :]
        lax.fori_loop(0, nch, promote, jnp.zeros((1, qb), F32))

    n_heads = 2 * n_pairs
    kc2 = 2 * kc

    def chunk2_rows(c):
        return pl.ds(pl.multiple_of(jnp.minimum(c, nch2 - 1) * kc2, kc2), kc2)

    def put_logits(dst, c):
        for p in range(n_pairs):
            dst[p] = _dot(k_ref[chunk2_rows(c), p * pw:(p + 1) * pw], qp_s[p])

    def absorb(src, c, carry):
        ms, ls, accs = carry
        rows2 = chunk2_rows(c)
        sel = jnp.logical_and(s_scr[rows2, :] >= theta, c < nch2)
        new_m, new_l, new_acc = [], [], []
        for p in range(n_pairs):
            for hh in range(2):
                h = 2 * p + hh
                logit = jnp.where(sel, src[p, :, hh * qb:(hh + 1) * qb], NEG_BIG)
                m_new = jnp.maximum(ms[h], jnp.max(logit, axis=0, keepdims=True))
                alpha = jnp.exp(ms[h] - m_new)
                pr = jnp.exp(logit - m_new)
                new_m.append(m_new)
                new_l.append(alpha * ls[h] + jnp.sum(pr, axis=0, keepdims=True))
                pv = _dot(vt_ref[h * HEAD_DIM:(h + 1) * HEAD_DIM, rows2], pr.astype(BF16))
                new_acc.append(alpha * accs[h] + pv)
        return tuple(new_m), tuple(new_l), tuple(new_acc)

    def att_pair(j, carry):
        c = 2 * j
        put_logits(lg_b, c + 1)
        carry = absorb(lg_a, c, carry)
        put_logits(lg_a, c + 2)
        return absorb(lg_b, c + 1, carry)

    init = (tuple(jnp.full((1, qb), NEG_BIG, F32) for _ in range(n_heads)),
            tuple(jnp.zeros((1, qb), F32) for _ in range(n_heads)),
            tuple(jnp.zeros((HEAD_DIM, qb), F32) for _ in range(n_heads)))
    put_logits(lg_a, 0)
    _, ls, accs = lax.fori_loop(0, (nch2 + 1) // 2, att_pair, init)
    for h in range(n_heads):
        o_ref[h * HEAD_DIM:(h + 1) * HEAD_DIM, :] = accs[h] / ls[h]


def _dsa_prompt(qt, qiht, qilt, wit, kidx3, kb, vt, topk, kc):
    nb, w, t = vt.shape
    n_pairs = w // (2 * HEAD_DIM)
    per_b = lambda a: pl.BlockSpec((None,) + a.shape[1:], lambda b, i: (b, 0, 0))
    qcol = lambda a: pl.BlockSpec((None, a.shape[1], Q_BLOCK), lambda b, i: (b, 0, i))
    return pl.pallas_call(
        functools.partial(_dsa_prompt_kernel, topk=topk, kc=kc),
        grid=(nb, t // Q_BLOCK),
        in_specs=[qcol(qt), qcol(qiht), qcol(qilt), qcol(wit), per_b(kidx3), per_b(kb), per_b(vt)],
        out_specs=pl.BlockSpec((None, w, Q_BLOCK), lambda b, i: (b, 0, i)),
        out_shape=jax.ShapeDtypeStruct((nb, w, t), F32),
        scratch_shapes=[pltpu.VMEM((t, Q_BLOCK), F32),
                        pltpu.VMEM((n_pairs, 2 * kc, 2 * Q_BLOCK), F32),
                        pltpu.VMEM((n_pairs, 2 * kc, 2 * Q_BLOCK), F32),
                        pltpu.VMEM((n_pairs, 2 * HEAD_DIM, 2 * Q_BLOCK), BF16),
                        pltpu.VMEM((n_pairs, kidx3.shape[2], 2 * Q_BLOCK), BF16)],
        compiler_params=_params("parallel", "arbitrary"),
        name="dsa_prompt",
    )(qt, qiht, qilt, wit, kidx3, kb, vt)


SAMPLE_PAGES_PER_STEP = 16
QROWS = 8


SELECT_BATCHES_PER_STEP = 4


def _dsa_sample_select_kernel(pt_ref, qhi_ref, qlo_ref, wi_ref, knew_ref, *rest, topk, n_new, gp, gb):
    pages = rest[:gb * gp]
    bias_ref, s_scr = rest[gb * gp], rest[gb * gp + 1]
    j = pl.program_id(1)
    nj = pl.num_programs(1)
    psz = pages[0].shape[1]
    past = nj * gp * psz
    rows = gb * QROWS

    def scores(g, kpage_t):
        khi, klo = _split3(kpage_t)
        qhi = qhi_ref[g]
        s = _dot(qhi, khi) + _dot(qhi, klo) + _dot(qlo_ref[g], khi)
        s = jnp.maximum(s, 0.0) * wi_ref[g]
        tot = s[0:QROWS, :]
        for h in range(1, IDX_HEADS):
            tot = tot + s[h * QROWS:(h + 1) * QROWS, :]
        return tot * IDX_SCALE

    for g in range(gb):
        for p in range(gp):
            off = pl.multiple_of((j * gp + p) * psz, psz)
            s_scr[g * QROWS:(g + 1) * QROWS, pl.ds(off, psz)] = scores(g, pages[g * gp + p][...])

    @pl.when(j == nj - 1)
    def _():
        qrow = lax.broadcasted_iota(jnp.int32, (QROWS, psz), 0)
        kcol = lax.broadcasted_iota(jnp.int32, (QROWS, psz), 1)
        vis_new = jnp.logical_and(kcol <= qrow, kcol < n_new)
        for g in range(gb):
            s_scr[g * QROWS:(g + 1) * QROWS, past:past + psz] = jnp.where(vis_new, scores(g, knew_ref[g]), -jnp.inf)
        s = s_scr[...]
        nk = s.shape[1]
        kpos = lax.broadcasted_iota(jnp.int32, s.shape, 1)
        qrow1 = lax.broadcasted_iota(jnp.int32, (rows, 1), 0) & (QROWS - 1)
        n_vis = (past + jnp.minimum(qrow1 + 1, n_new)).astype(F32)
        kq = jnp.minimum(n_vis, float(topk))
        fin = s > -jnp.inf
        mn = jnp.min(jnp.where(fin, s, jnp.inf), axis=1, keepdims=True)
        mx = jnp.max(s, axis=1, keepdims=True)
        count = lambda pred: jnp.sum(jnp.where(pred, 1.0, 0.0), axis=1, keepdims=True)
        theta, lo, tie = _select_threshold(
            lambda x: count(s >= x),
            lambda x: jnp.max(jnp.where(s < x, s, -jnp.inf), axis=1, keepdims=True),
            mn, mx, kq, jnp.logical_or(n_vis <= float(topk), qrow1 >= n_new), 0)

        sel = s >= theta
        bias_ref[...] = jnp.where(sel, 0.0, NEG_BIG).reshape(bias_ref.shape)

        @pl.when(jnp.max(tie) > 0.0)
        def _():
            need = kq - count(sel)
            tied = jnp.logical_and(tie > 0.0, s == lo)
            nbits = max(1, (nk - 1).bit_length())

            def jstep(b, jl):
                cand = jl + (jnp.int32(1) << (nbits - 1 - b))
                c_lt = count(jnp.logical_and(tied, kpos < cand))
                return jnp.where(c_lt < need, cand, jl)
            jlast = lax.fori_loop(0, nbits, jstep, jnp.zeros((rows, 1), jnp.int32))
            keep = jnp.logical_or(sel, jnp.logical_and(tied, kpos <= jlast))
            bias_ref[...] = jnp.where(keep, 0.0, NEG_BIG).reshape(bias_ref.shape)


def _dsa_sample_select(layer, page_table, qhi, qlo, wi_b, knew, pool_kidx, topk, n_new):
    nb, n_pages = page_table.shape
    gp = math.gcd(n_pages, SAMPLE_PAGES_PER_STEP)
    gb = math.gcd(nb, SELECT_BATCHES_PER_STEP)
    di, psz = pool_kidx.shape[2], pool_kidx.shape[3]
    nk = (n_pages + 1) * psz
    per_b = lambda a: pl.BlockSpec((gb,) + a.shape[1:], lambda b, j, pt: (b, 0, 0))
    page = lambda g, p: pl.BlockSpec((None, None, di, psz),
                                     lambda b, j, pt: (layer, pt[b * gb + g, j * gp + p], 0, 0))
    grid_spec = pltpu.PrefetchScalarGridSpec(
        num_scalar_prefetch=1,
        grid=(nb // gb, n_pages // gp),
        in_specs=[per_b(qhi), per_b(qlo), per_b(wi_b), per_b(knew)]
        + [page(g, p) for g in range(gb) for p in range(gp)],
        out_specs=pl.BlockSpec((gb, QROWS, nk), lambda b, j, pt: (b, 0, 0)),
        scratch_shapes=[pltpu.VMEM((gb * QROWS, nk), F32)],
    )
    return pl.pallas_call(
        functools.partial(_dsa_sample_select_kernel, topk=topk, n_new=n_new, gp=gp, gb=gb),
        grid_spec=grid_spec,
        out_shape=jax.ShapeDtypeStruct((nb, QROWS, nk), F32),
        compiler_params=_params("parallel", "arbitrary"),
        name="dsa_sample_select",
    )(page_table, qhi, qlo, wi_b, knew, *([pool_kidx] * (gb * gp)))


def _dsa_sample_attend_kernel(pt_ref, q_ref, knew_ref, vnew_ref, *rest, gp):
    bias_refs = rest[:gp]
    kpages = rest[gp:2 * gp]
    vpages = rest[2 * gp:3 * gp]
    bnew_ref = rest[3 * gp]
    o_ref, m_s, l_s, acc_s = rest[3 * gp + 1:]
    j = pl.program_id(1)
    nj = pl.num_programs(1)
    q = q_ref[...]
    n_heads = q.shape[0] // QROWS

    @pl.when(j == 0)
    def _():
        m_s[...] = jnp.full(m_s.shape, NEG_BIG, F32)
        l_s[...] = jnp.zeros(l_s.shape, F32)
        acc_s[...] = jnp.zeros(acc_s.shape, F32)

    def absorb(kts, vts, biases):
        psz = kts[0].shape[1]
        logit = jnp.concatenate(
            [_dot(q, kt[...].astype(BF16)) + jnp.concatenate([b[...]] * n_heads, axis=0)
             for kt, b in zip(kts, biases)], axis=1)
        m = m_s[...]
        m_new = jnp.maximum(m, jnp.max(logit, axis=1, keepdims=True))
        alpha = jnp.exp(m - m_new)
        p = jnp.exp(logit - m_new)
        l_s[...] = alpha * l_s[...] + jnp.sum(p, axis=1, keepdims=True)
        pv = _dot_t(p[:, 0:psz].astype(BF16), vts[0][...].astype(BF16))
        for g in range(1, len(vts)):
            pv = pv + _dot_t(p[:, g * psz:(g + 1) * psz].astype(BF16), vts[g][...].astype(BF16))
        acc_s[...] = alpha * acc_s[...] + pv
        m_s[...] = m_new

    absorb(kpages, vpages, bias_refs)

    @pl.when(j == nj - 1)
    def _():
        absorb([knew_ref], [vnew_ref], [bnew_ref])
        out = acc_s[...] / l_s[...]
        lane = lax.broadcasted_iota(jnp.int32, (QROWS, out.shape[1]), 1)
        y = jnp.zeros((QROWS, out.shape[1]), F32)
        for h in range(n_heads):
            in_head = jnp.logical_and(lane >= h * HEAD_DIM, lane < (h + 1) * HEAD_DIM)
            y = y + jnp.where(in_head, out[h * QROWS:(h + 1) * QROWS, :], 0.0)
        o_ref[...] = y


def _dsa_sample_attend(layer, page_table, q32, knew, vnew, bias, pool_k, pool_v):
    nb, n_pages = page_table.shape
    gp = math.gcd(n_pages, SAMPLE_PAGES_PER_STEP)
    w, psz = pool_k.shape[2], pool_k.shape[3]
    per_b = lambda a: pl.BlockSpec((None,) + a.shape[1:], lambda b, j, pt: (b, 0, 0))
    page = lambda g: pl.BlockSpec((None, None, w, psz), lambda b, j, pt, g=g: (layer, pt[b, j * gp + g], 0, 0))
    bias_pg = lambda g: pl.BlockSpec((None, QROWS, psz), lambda b, j, pt, g=g: (b, 0, j * gp + g))
    bias_new = pl.BlockSpec((None, QROWS, psz), lambda b, j, pt: (b, 0, n_pages))
    grid_spec = pltpu.PrefetchScalarGridSpec(
        num_scalar_prefetch=1,
        grid=(nb, n_pages // gp),
        in_specs=([per_b(q32), per_b(knew), per_b(vnew)] + [bias_pg(g) for g in range(gp)]
                  + [page(g) for g in range(gp)] + [page(g) for g in range(gp)] + [bias_new]),
        out_specs=pl.BlockSpec((None, QROWS, w), lambda b, j, pt: (b, 0, 0)),
        scratch_shapes=[pltpu.VMEM((q32.shape[1], 1), F32), pltpu.VMEM((q32.shape[1], 1), F32),
                        pltpu.VMEM((q32.shape[1], w), F32)],
    )
    return pl.pallas_call(
        functools.partial(_dsa_sample_attend_kernel, gp=gp),
        grid_spec=grid_spec,
        out_shape=jax.ShapeDtypeStruct((nb, QROWS, w), F32),
        compiler_params=_params("parallel", "arbitrary"),
        name="dsa_sample_attend",
    )(page_table, q32, knew, vnew, *([bias] * gp), *([pool_k] * gp), *([pool_v] * gp), bias)


def _mixout_kernel(x_ref, u_ref, g_ref, ys_ref, h_ref, ycc_ref, yatt_ref, d_ref, wglu_ref, bglu_ref,
                   wout_ref, g1_ref, b1_ref, o_ref, *, alpha):
    w = u_ref.shape[-1]
    y = ys_ref[...] + d_ref[...] * u_ref[...]
    z = _dot(jax.nn.gelu(y).astype(BF16), wglu_ref[...]) + bglu_ref[...]
    y_s5 = z[:, :w] * jax.nn.sigmoid(z[:, w:])
    y_lru = h_ref[...] * jax.nn.gelu(g_ref[...])
    acc = _dot(y_s5.astype(BF16), wout_ref[0:w, :])
    acc = acc + _dot(y_lru.astype(BF16), wout_ref[w:2 * w, :])
    acc = acc + _dot(ycc_ref[...].astype(BF16), wout_ref[2 * w:3 * w, :])
    acc = acc + _dot(yatt_ref[...].astype(BF16), wout_ref[3 * w:4 * w, :])
    o_ref[...] = _layer_norm(alpha * x_ref[...] + acc, g1_ref[...], b1_ref[...])


def _mixout(x2, zmix, ys, h, ycc, yatt, d, wglu, bglu, wout, g1, b1, tm, alpha):
    m, dm = x2.shape
    w = ys.shape[1]
    row = lambda a: pl.BlockSpec((tm, a.shape[1]), lambda i: (i, 0))
    full = lambda a: pl.BlockSpec(a.shape, lambda i: (0, 0))
    zcol = lambda c: pl.BlockSpec((tm, w), lambda i, c=c: (i, c))
    return pl.pallas_call(
        functools.partial(_mixout_kernel, alpha=alpha),
        grid=(m // tm,),
        in_specs=[row(x2), zcol(0), zcol(2), row(ys), row(h), row(ycc), row(yatt), full(d), full(wglu),
                  full(bglu), full(wout), full(g1), full(b1)],
        out_specs=pl.BlockSpec((tm, dm), lambda i: (i, 0)),
        out_shape=jax.ShapeDtypeStruct((m, dm), F32),
        compiler_params=_params("parallel"),
        name="mixout",
    )(x2, zmix, zmix, ys, h, ycc, yatt, d, wglu, bglu, wout, g1, b1)


def _ffn_kernel(h_ref, wg_ref, wu_ref, wd_ref, g2_ref, b2_ref, o_ref, acc_s, hb_s, *, alpha):
    f = pl.program_id(1)

    @pl.when(f == 0)
    def _():
        acc_s[...] = jnp.zeros(acc_s.shape, F32)
        hb_s[...] = h_ref[...].astype(BF16)

    hb = hb_s[...]
    a = _dot(hb, wg_ref[...])
    act = (a * jax.nn.sigmoid(a)) * _dot(hb, wu_ref[...])
    acc_s[...] += _dot(act.astype(BF16), wd_ref[...])

    @pl.when(f == pl.num_programs(1) - 1)
    def _():
        o_ref[...] = _layer_norm(alpha * h_ref[...] + acc_s[...], g2_ref[...], b2_ref[...])


def _ffn(h, wg, wu, wd, g2, b2, tm, tf, alpha):
    m, dm = h.shape
    nf = wg.shape[1] // tf
    return pl.pallas_call(
        functools.partial(_ffn_kernel, alpha=alpha),
        grid=(m // tm, nf),
        in_specs=[pl.BlockSpec((tm, dm), lambda i, f: (i, 0)),
                  pl.BlockSpec((dm, tf), lambda i, f: (0, f)),
                  pl.BlockSpec((dm, tf), lambda i, f: (0, f)),
                  pl.BlockSpec((tf, dm), lambda i, f: (f, 0)),
                  pl.BlockSpec(g2.shape, lambda i, f: (0, 0)),
                  pl.BlockSpec(b2.shape, lambda i, f: (0, 0))],
        out_specs=pl.BlockSpec((tm, dm), lambda i, f: (i, 0)),
        out_shape=jax.ShapeDtypeStruct((m, dm), F32),
        scratch_shapes=[pltpu.VMEM((tm, dm), F32), pltpu.VMEM((tm, dm), BF16)],
        compiler_params=_params("parallel", "arbitrary"),
        name="ffn",
    )(h, wg, wu, wd, g2, b2)


MOE_CHUNK_ROWS = 288
MOE_COL_TILE = 896
MOE_ROW_TILE = 1024
FFN_COL_TILE = 256
FFN_ROW_TILE = 1024


def _moe_kernel(h_ref, wrh_ref, wrl_ref, br_ref, wg_ref, wu_ref, wd_ref, g2_ref, b2_ref, o_ref,
                hb_s, xg_s, y_s, gate_s, slot_s, slot_t_s, *, alpha, n_exp, cr):
    e = pl.program_id(1)
    f = pl.program_id(2)
    nf = pl.num_programs(2)
    tm, dm = h_ref.shape
    first = jnp.logical_and(e == 0, f == 0)
    last = jnp.logical_and(e == pl.num_programs(1) - 1, f == nf - 1)

    @pl.when(first)
    def _():
        hhi, hlo = _split3(h_ref[...])
        hb_s[...] = hhi
        logits = _dot(hhi, wrh_ref[...]) + _dot(hhi, wrl_ref[...]) + _dot(hlo, wrh_ref[...]) + br_ref[...]
        lane = lax.broadcasted_iota(jnp.int32, logits.shape, 1)
        m1 = jnp.max(logits, axis=1, keepdims=True)
        i1 = jnp.min(jnp.where(logits == m1, lane, LANES), axis=1, keepdims=True)
        rest = jnp.where(lane == i1, -jnp.inf, logits)
        m2 = jnp.max(rest, axis=1, keepdims=True)
        i2 = jnp.min(jnp.where(rest == m2, lane, LANES), axis=1, keepdims=True)
        e2 = jnp.exp(m2 - m1)
        den = 1.0 + e2
        gate = jnp.where(lane == i1, 1.0 / den, 0.0) + jnp.where(lane == i2, e2 / den, 0.0)
        routed = jnp.logical_or(lane == i1, lane == i2)
        ind = jnp.where(routed, 1.0, 0.0).astype(BF16)
        tri = jnp.where(lax.broadcasted_iota(jnp.int32, (tm, tm), 0) >= lax.broadcasted_iota(jnp.int32, (tm, tm), 1),
                        1.0, 0.0).astype(BF16)
        slot = jnp.where(routed, _dot(tri, ind) - 1.0, -1.0)
        eye = jnp.where(lax.broadcasted_iota(jnp.int32, (LANES, LANES), 0)
                        == lax.broadcasted_iota(jnp.int32, (LANES, LANES), 1), 1.0, 0.0).astype(BF16)
        ind_t = _dot_t(eye, ind)
        slot_t = jnp.where(ind_t > 0.0, _dot_t(ind_t.astype(BF16), tri) - 1.0, -1.0)
        slot_t_s[...] = slot_t[0:slot_t_s.shape[0], :]
        for ex in range(n_exp):
            pick = lambda x: jnp.sum(jnp.where(lane == ex, x, 0.0), axis=1, keepdims=True)
            gate_s[ex] = jnp.broadcast_to(pick(gate), gate_s.shape[1:])
            slot_s[ex] = jnp.broadcast_to(pick(slot), slot_s.shape[1:])
        o_ref[...] = jnp.zeros(o_ref.shape, F32)

    slot_b = slot_s[e]
    n_pass = ((jnp.max(slot_b) + 1.0).astype(jnp.int32) + cr - 1) // cr

    def pass_rows(r):
        return pl.ds(pl.multiple_of(r * cr, 16), cr)

    @pl.when(f == 0)
    def _():
        srow = slot_t_s[pl.ds(e, 1), :]

        def pack(r, _):
            want = (r * cr + lax.broadcasted_iota(jnp.int32, (cr, tm), 0)).astype(F32)
            onehot = jnp.where(srow == want, 1.0, 0.0).astype(BF16)
            xg_s[pass_rows(r), :] = _dot(onehot, hb_s[...]).astype(BF16)
            return 0
        lax.fori_loop(0, n_pass, pack, 0)

    def expert_pass(r, _):
        x = xg_s[pass_rows(r), :]
        a = _dot(x, wg_ref[...])
        act = (a * jax.nn.sigmoid(a)) * _dot(x, wu_ref[...])
        y = _dot(act.astype(BF16), wd_ref[...])

        @pl.when(f == 0)
        def _():
            y_s[pass_rows(r), :] = y

        @pl.when(f > 0)
        def _():
            y_s[pass_rows(r), :] = y_s[pass_rows(r), :] + y
        return 0
    lax.fori_loop(0, n_pass, expert_pass, 0)

    @pl.when(f == nf - 1)
    def _():
        reps = (cr + LANES - 1) // LANES
        slot_w = jnp.concatenate([slot_b] * reps, axis=1)[:, :cr]
        gate_w = jnp.concatenate([gate_s[e]] * (dm // LANES), axis=1)

        def unpack(r, _):
            want = (r * cr + lax.broadcasted_iota(jnp.int32, (tm, cr), 1)).astype(F32)
            onehot = jnp.where(slot_w == want, 1.0, 0.0).astype(BF16)
            o_ref[...] += gate_w * _dot(onehot, y_s[pass_rows(r), :].astype(BF16))
            return 0
        lax.fori_loop(0, n_pass, unpack, 0)

    @pl.when(last)
    def _():
        o_ref[...] = _layer_norm(alpha * h_ref[...] + o_ref[...], g2_ref[...], b2_ref[...])


def _moe(h, wrh, wrl, br, wg, wu, wd, g2, b2, tm, tf, alpha):
    m, dm = h.shape
    n_exp, _, dff = wg.shape
    nf = dff // tf
    cr = min(MOE_CHUNK_ROWS, tm)
    cap = -(-tm // cr) * cr
    c2 = lambda a: pl.BlockSpec(a.shape, lambda i, e, f: (0, 0))
    return pl.pallas_call(
        functools.partial(_moe_kernel, alpha=alpha, n_exp=n_exp, cr=cr),
        grid=(m // tm, n_exp, nf),
        in_specs=[pl.BlockSpec((tm, dm), lambda i, e, f: (i, 0)),
                  c2(wrh), c2(wrl), c2(br),
                  pl.BlockSpec((None, dm, tf), lambda i, e, f: (e, 0, f)),
                  pl.BlockSpec((None, dm, tf), lambda i, e, f: (e, 0, f)),
                  pl.BlockSpec((None, tf, dm), lambda i, e, f: (e, f, 0)),
                  c2(g2), c2(b2)],
        out_specs=pl.BlockSpec((tm, dm), lambda i, e, f: (i, 0)),
        out_shape=jax.ShapeDtypeStruct((m, dm), F32),
        scratch_shapes=[pltpu.VMEM((tm, dm), BF16),
                        pltpu.VMEM((cap, dm), BF16),
                        pltpu.VMEM((cap, dm), F32),
                        pltpu.VMEM((n_exp, tm, LANES), F32),
                        pltpu.VMEM((n_exp, tm, LANES), F32),
                        pltpu.VMEM((max(8, n_exp), tm), F32)],
        compiler_params=_params("parallel", "arbitrary", "arbitrary"),
        name="moe",
    )(h, wrh, wrl, br, wg, wu, wd, g2, b2)


def _row_tile(m, target):
    t = math.gcd(m, target)
    return t if t % 8 == 0 else m


def _chunk_len(t):
    return math.gcd(t, 128)


def _rope_tables(pos, reps):
    half = HEAD_DIM // 2
    inv = ROPE_THETA ** (-(jnp.arange(half, dtype=F32) / half))
    ang = pos.astype(F32)[:, None] * inv[None, :]
    cos, sin = jnp.cos(ang), jnp.sin(ang)
    cos_h = jnp.concatenate([cos, cos], axis=-1)
    sin_h = jnp.concatenate([-sin, sin], axis=-1)
    return jnp.tile(cos_h, (1, reps)), jnp.tile(sin_h, (1, reps))


def _s5_params(a_re, a_im, b_re, b_im, c_re, c_im, log_dt):
    ng, ns = a_re.shape
    lr, li = a_re.astype(F32), a_im.astype(F32)
    dt = jnp.exp(log_dt.astype(F32))[:, None]
    mag = jnp.exp(lr * dt)
    ar, ai = mag * jnp.cos(li * dt), mag * jnp.sin(li * dt)
    den = lr * lr + li * li
    qr = ((ar - 1.0) * lr + ai * li) / den
    qi = (ai * lr - (ar - 1.0) * li) / den
    br, bi = b_re.astype(F32), b_im.astype(F32)
    bbr = qr[..., None] * br - qi[..., None] * bi
    bbi = qr[..., None] * bi + qi[..., None] * br
    eye = jnp.eye(ng, dtype=F32)
    gpl = LANES // ns
    nslab = ng // gpl

    def b_mat(x):
        full = jnp.einsum('gpc,gh->gchp', x, eye).reshape(ng * S5_CH, ng * ns)
        return jnp.transpose(full.reshape(ng * S5_CH, nslab, LANES), (1, 0, 2)).astype(BF16)

    def c_mat(x):
        return jnp.einsum('gcp,gh->gphc', x, eye).reshape(nslab, LANES, ng * S5_CH)

    ccat = jnp.concatenate([c_mat(c_re.astype(F32)), -c_mat(c_im.astype(F32))], axis=1).astype(BF16)
    return b_mat(bbr), b_mat(bbi), ar.reshape(nslab, 1, LANES), ai.reshape(nslab, 1, LANES), ccat


def _to_chunks(x, nl):
    nb, t, w = x.shape
    return jnp.swapaxes(x.reshape(nb, t // nl, nl, w), 1, 2)


def _from_chunks(x):
    nb, nl, nr, w = x.shape
    return jnp.swapaxes(x, 1, 2).reshape(nb, nr * nl, w)


def _mixers(lp, zmix, nb, t, s5_re0, s5_im0, lru_h0, lru_buf0, cc_buf0, chained):
    w = lp['gw']
    zmix3 = zmix.reshape(nb, t, zmix.shape[-1])
    tt = math.gcd(t, 512)
    a, b, ycc, lru_buf, cc_buf = _convmix(zmix3, lru_buf0, cc_buf0, lp['lru_conv_w'], lp['lru_conv_b'],
                                          lp['lru_wg'], lp['lru_bg'], lp['lru_sp'], lp['cc_dw_w'], lp['cc_dw_b'],
                                          lp['cc_ln_g'], lp['cc_ln_b'], tt)
    u = zmix3[:, :, :w].astype(BF16)
    ns = s5_re0.shape[-2] * s5_re0.shape[-1]
    if chained:
        nl = _chunk_len(t)
        a4, b4, u4 = _to_chunks(a, nl), _to_chunks(b, nl), _to_chunks(u, nl)
        h0 = lru_h0.reshape(nb, 1, w)
        s0r, s0i = s5_re0.reshape(nb, 1, ns), s5_im0.reshape(nb, 1, ns)
    else:
        to_rows = lambda x: jnp.swapaxes(x, 0, 1)[None]
        a4, b4, u4 = to_rows(a), to_rows(b), to_rows(u)
        h0 = lru_h0.reshape(1, nb, w)
        s0r, s0i = s5_re0.reshape(1, nb, ns), s5_im0.reshape(1, nb, ns)
    h4, lru_h = _lru_scan(a4, b4, h0, chained)
    ys4, s5_re, s5_im = _s5_scan(u4, lp['s5_bre'], lp['s5_bim'], lp['s5_are'], lp['s5_aim'], lp['s5_ccat'],
                                 s0r, s0i, chained)
    if chained:
        h, ys = _from_chunks(h4), _from_chunks(ys4)
    else:
        h, ys = jnp.swapaxes(h4[0], 0, 1), jnp.swapaxes(ys4[0], 0, 1)
    states = (s5_re.reshape(s5_re0.shape), s5_im.reshape(s5_im0.shape), lru_h.reshape(lru_h0.shape),
              lru_buf, cc_buf)
    return ys.reshape(nb * t, w), h.reshape(nb * t, w), ycc.reshape(nb * t, w), states


def _idx3(x, order):
    hi, lo = _split3(x)
    parts = [hi if o == 'h' else lo for o in order]
    pad = jnp.zeros(x.shape[:-1] + (2 * LANES - len(order) * x.shape[-1],), BF16)
    return jnp.concatenate(parts + [pad], axis=-1)


def _attend_prompt(q, qi, k, v, kw, nb, t):
    w = q.shape[-1]
    topk = min(TOPK_MAX, t // 4)
    to_cols = lambda x: jnp.swapaxes(x.reshape(nb, t, -1), 1, 2)
    qt = to_cols(q)
    qih, qil = _split3(qi)
    wi = kw[:, IDX_DIM:IDX_DIM + IDX_HEADS].reshape(nb, t, IDX_HEADS)
    wit = jnp.pad(jnp.swapaxes(wi, 1, 2), ((0, 0), (0, 8 - IDX_HEADS), (0, 0)))
    kidx3 = _idx3(kw[:, :IDX_DIM].reshape(nb, t, IDX_DIM), 'hlh')
    kb = k.reshape(nb, t, w).astype(BF16)
    vt = jnp.swapaxes(v.reshape(nb, t, w), 1, 2).astype(BF16)
    kc = math.gcd(t // 2, 512)
    yt = _dsa_prompt(qt, to_cols(qih), to_cols(qil), wit, kidx3, kb, vt, topk, kc)
    return jnp.swapaxes(yt, 1, 2).reshape(nb * t, w)


def _attend_sample(layer, q, qi, k, v, kw, nb, t, pool_k, pool_v, pool_kidx, page_table):
    w = q.shape[-1]
    n_heads = w // HEAD_DIM
    psz = pool_kidx.shape[2]
    past = page_table.shape[1] * psz
    topk = min(TOPK_MAX, (past + t) // 4)
    pad_q = lambda x: jnp.pad(x, ((0, 0), (0, 0), (0, QROWS - t), (0, 0)))

    qi_h = jnp.swapaxes(qi.reshape(nb, t, IDX_HEADS, IDX_DIM), 1, 2)
    qhi, qlo = _split3(pad_q(qi_h).reshape(nb, IDX_HEADS * QROWS, IDX_DIM))
    wi = kw[:, IDX_DIM:IDX_DIM + IDX_HEADS].reshape(nb, t, IDX_HEADS)
    wi_b = jnp.broadcast_to(pad_q(jnp.swapaxes(wi, 1, 2)[..., None]).reshape(nb, IDX_HEADS * QROWS, 1),
                            (nb, IDX_HEADS * QROWS, psz))
    new_page_t = lambda x: jnp.swapaxes(jnp.pad(x, ((0, 0), (0, psz - t), (0, 0))), 1, 2)
    kidx_new = new_page_t(kw[:, :IDX_DIM].reshape(nb, t, IDX_DIM))
    bias = _dsa_sample_select(layer, page_table, qhi, qlo, wi_b, kidx_new, jnp.swapaxes(pool_kidx, 2, 3),
                              topk, t)

    head_of_lane = jnp.arange(w) // HEAD_DIM
    q_rows = jnp.broadcast_to(pad_q(q.reshape(nb, 1, t, w)), (nb, n_heads, QROWS, w))
    q32 = jnp.where(head_of_lane[None, None, None, :] == jnp.arange(n_heads)[None, :, None, None],
                    q_rows, jnp.zeros_like(q_rows)).reshape(nb, n_heads * QROWS, w)
    k_new = new_page_t(k.reshape(nb, t, w))
    v_new = new_page_t(v.reshape(nb, t, w))
    pool_t = lambda x: jnp.transpose(x, (0, 1, 3, 4, 2)).reshape(x.shape[:2] + (w, psz))
    y = _dsa_sample_attend(layer, page_table, q32, k_new, v_new, bias, pool_t(pool_k), pool_t(pool_v))
    return y[:, :t, :].reshape(nb * t, w)


def _prep_layer(l, w_in, s5_a_re, s5_a_im, s5_b_re, s5_b_im, s5_c_re, s5_c_im, s5_d, s5_log_dt, s5_w_glu,
                s5_b_glu, lru_conv_w, lru_conv_b, lru_w_a, lru_b_a, lru_w_x, lru_b_x, lru_lambda, cc_dw_w,
                cc_dw_b, cc_ln_g, cc_ln_b, w_out, ln1_g, ln1_b, ln2_g, ln2_b):
    gw = s5_d.shape[1]
    d_in = w_in.shape[2]
    mix_w = 5 * gw
    pad_to = mix_w + 4 * gw + LANES
    assert d_in == mix_w + 4 * gw + IDX_DIM + IDX_HEADS and gw == IDX_HEADS * IDX_DIM
    row = lambda x: x[l].reshape(1, -1).astype(F32)
    nh, hd, _ = lru_w_a.shape[1:]
    eye = jnp.eye(nh, dtype=F32)
    bd = lambda wt: jnp.einsum('hij,hk->hikj', wt[l].astype(F32), eye).reshape(nh * hd, nh * hd)
    s5_bre, s5_bim, s5_are, s5_aim, s5_ccat = _s5_params(s5_a_re[l], s5_a_im[l], s5_b_re[l], s5_b_im[l],
                                                         s5_c_re[l], s5_c_im[l], s5_log_dt[l])
    return dict(
        gw=gw, mix_w=mix_w,
        w_in=jnp.pad(w_in[l], ((0, 0), (0, pad_to - d_in))).astype(BF16),
        s5_bre=s5_bre, s5_bim=s5_bim, s5_are=s5_are, s5_aim=s5_aim, s5_ccat=s5_ccat,
        s5_d=row(s5_d), s5_w_glu=s5_w_glu[l].astype(BF16), s5_b_glu=row(s5_b_glu),
        lru_conv_w=lru_conv_w[l].astype(F32), lru_conv_b=row(lru_conv_b),
        lru_wg=jnp.concatenate([bd(lru_w_a), bd(lru_w_x)], axis=1).astype(BF16),
        lru_bg=jnp.concatenate([row(lru_b_a), row(lru_b_x)], axis=1),
        lru_sp=jax.nn.softplus(-row(lru_lambda)),
        cc_dw_w=cc_dw_w[l].astype(F32), cc_dw_b=row(cc_dw_b), cc_ln_g=row(cc_ln_g), cc_ln_b=row(cc_ln_b),
        w_out=w_out[l].astype(BF16), ln1_g=row(ln1_g), ln1_b=row(ln1_b), ln2_g=row(ln2_g), ln2_b=row(ln2_b),
    )


def kernel(x_prompt, x_sample, cache_k, cache_v, cache_kidx, state_s5_re, state_s5_im, state_lru_h, state_lru_conv, state_cc_conv, page_table, w_in, s5_a_re, s5_a_im, s5_b_re, s5_b_im, s5_c_re, s5_c_im, s5_d, s5_log_dt, s5_w_glu, s5_b_glu, lru_conv_w, lru_conv_b, lru_w_a, lru_b_a, lru_w_x, lru_b_x, lru_lambda, cc_dw_w, cc_dw_b, cc_ln_g, cc_ln_b, w_out, ln1_g, ln1_b, ln2_g, ln2_b, ffn_w_gate, ffn_w_up, ffn_w_down, moe_w_router, moe_b_router, moe_w_gate, moe_w_up, moe_w_down):
    bp, tp, dm = x_prompt.shape
    bs, ts, _ = x_sample.shape
    depth = w_in.shape[0]
    past = page_table.shape[1] * cache_k.shape[2]
    alpha = (2.0 * depth) ** 0.25
    n_heads = cache_k.shape[3]
    n_exp = moe_w_router.shape[-1]
    gw = s5_d.shape[1]

    cos_p, sin_p = _rope_tables(jnp.arange(tp, dtype=jnp.int32), gw // HEAD_DIM)
    cos_s, sin_s = _rope_tables(past + jnp.arange(ts, dtype=jnp.int32), gw // HEAD_DIM)
    cos_s, sin_s = jnp.tile(cos_s, (bs, 1)), jnp.tile(sin_s, (bs, 1))

    tm_p = _row_tile(tp, 512)
    xp = x_prompt.reshape(bp * tp, dm)
    xs = x_sample.reshape(bs * ts, dm)
    zeros_p = lambda *shape: jnp.zeros((bp,) + shape, x_prompt.dtype)
    states_p, states_s = [], []

    for l in range(depth):
        lp = _prep_layer(l, w_in, s5_a_re, s5_a_im, s5_b_re, s5_b_im, s5_c_re, s5_c_im, s5_d, s5_log_dt,
                         s5_w_glu, s5_b_glu, lru_conv_w, lru_conv_b, lru_w_a, lru_b_a, lru_w_x, lru_b_x,
                         lru_lambda, cc_dw_w, cc_dw_b, cc_ln_g, cc_ln_b, w_out, ln1_g, ln1_b, ln2_g, ln2_b)
        mix_w = lp['mix_w']
        j = l // 2
        if l % 2 == 0:
            tf = math.gcd(ffn_w_gate.shape[2], FFN_COL_TILE)
            cw = (ffn_w_gate[j].astype(BF16), ffn_w_up[j].astype(BF16), ffn_w_down[j].astype(BF16))
        else:
            wr = jnp.pad(moe_w_router[j].astype(F32), ((0, 0), (0, LANES - n_exp)))
            wrh, wrl = _split3(wr)
            br = jnp.pad(moe_b_router[j].astype(F32), (0, LANES - n_exp), constant_values=NEG_BIG).reshape(1, LANES)
            tf = math.gcd(moe_w_gate.shape[3], MOE_COL_TILE)
            cw = (wrh, wrl, br, moe_w_gate[j].astype(BF16), moe_w_up[j].astype(BF16), moe_w_down[j].astype(BF16))

        def channel(h):
            if l % 2 == 0:
                return _ffn(h, *cw, lp['ln2_g'], lp['ln2_b'], _row_tile(h.shape[0], FFN_ROW_TILE), tf, alpha)
            return _moe(h, *cw, lp['ln2_g'], lp['ln2_b'], _row_tile(h.shape[0], MOE_ROW_TILE), tf, alpha)

        zmix, q, qi, k, v, kw = _in_proj(xp, lp['w_in'], cos_p, sin_p, tm_p, mix_w, gw)
        ys, h, ycc, st = _mixers(lp, zmix, bp, tp,
                                 zeros_p(*state_s5_re.shape[2:]), zeros_p(*state_s5_im.shape[2:]),
                                 zeros_p(*state_lru_h.shape[2:]), zeros_p(*state_lru_conv.shape[2:]),
                                 zeros_p(*state_cc_conv.shape[2:]), chained=True)
        yatt = _attend_prompt(q, qi, k, v, kw, bp, tp)
        h1 = _mixout(xp, zmix, ys, h, ycc, yatt, lp['s5_d'], lp['s5_w_glu'], lp['s5_b_glu'], lp['w_out'],
                     lp['ln1_g'], lp['ln1_b'], tm_p, alpha)
        xp = channel(h1)
        states_p.append((k.reshape(bp, tp, n_heads, HEAD_DIM), v.reshape(bp, tp, n_heads, HEAD_DIM),
                         kw[:, :IDX_DIM].reshape(bp, tp, IDX_DIM)) + st)

        ms = bs * ts
        zmix, q, qi, k, v, kw = _in_proj(xs, lp['w_in'], cos_s, sin_s, ms, mix_w, gw)
        ys, h, ycc, st = _mixers(lp, zmix, bs, ts, state_s5_re[l], state_s5_im[l], state_lru_h[l],
                                 state_lru_conv[l], state_cc_conv[l], chained=False)
        yatt = _attend_sample(l, q, qi, k, v, kw, bs, ts, cache_k, cache_v, cache_kidx, page_table)
        h1 = _mixout(xs, zmix, ys, h, ycc, yatt, lp['s5_d'], lp['s5_w_glu'], lp['s5_b_glu'], lp['w_out'],
                     lp['ln1_g'], lp['ln1_b'], ms, alpha)
        xs = channel(h1)
        states_s.append((k.reshape(bs, ts, n_heads, HEAD_DIM), v.reshape(bs, ts, n_heads, HEAD_DIM),
                         kw[:, :IDX_DIM].reshape(bs, ts, IDX_DIM)) + st)

    new_p = [jnp.stack(col) for col in zip(*states_p)]
    new_s = [jnp.stack(col) for col in zip(*states_s)]
    out = [xp.reshape(bp, tp, dm), xs.reshape(bs, ts, dm)]
    for a, b in zip(new_p, new_s):
        out += [a, b]
    return tuple(out)
```

```python
import functools
import math

import jax
import jax.numpy as jnp
from jax import lax
from jax.experimental import pallas as pl
from jax.experimental.pallas import tpu as pltpu

F32 = jnp.float32
BF16 = jnp.bfloat16

S5_CH = 16
LRU_C = 8.0
HEAD_DIM = 64
IDX_DIM = 64
IDX_HEADS = 4
TOPK_MAX = 256
Q_BLOCK = 128
ROPE_THETA = 10000.0
MOE_TOP_K = 2
LN_EPS = 1e-5
ATT_SCALE = HEAD_DIM ** -0.5
IDX_SCALE = (IDX_DIM * IDX_HEADS) ** -0.5

LANES = 128
NEG_BIG = -1e30
VMEM_LIMIT = 56 * 1024 * 1024


def _params(*sem):
    return pltpu.CompilerParams(dimension_semantics=sem, vmem_limit_bytes=VMEM_LIMIT)


def _dot(a, b):
    return jnp.dot(a, b, preferred_element_type=F32)


def _dot_t(a, b):
    return lax.dot_general(a, b, (((1,), (1,)), ((), ())), preferred_element_type=F32)


def _split3(x):
    hi = x.astype(BF16)
    lo = (x - hi.astype(F32)).astype(BF16)
    return hi, lo


def _layer_norm(x, g, b):
    mu = jnp.mean(x, axis=-1, keepdims=True)
    xc = x - mu
    var = jnp.mean(xc * xc, axis=-1, keepdims=True)
    return xc * lax.rsqrt(var + LN_EPS) * g + b


def _rope(z, cos, sin):
    w = z.shape[-1]
    half = HEAD_DIM // 2
    lane = lax.broadcasted_iota(jnp.int32, z.shape, 1)
    first = (lane & (HEAD_DIM - 1)) < half
    partner = jnp.where(first, pltpu.roll(z, w - half, axis=1), pltpu.roll(z, half, axis=1))
    return z * cos + partner * sin


def _in_proj_kernel(x_ref, w_ref, cos_ref, sin_ref, zmix_ref, q_ref, qi_ref, k_ref, v_ref, kw_ref,
                    *, mix_w, gw):
    xb = x_ref[...].astype(BF16)
    cos = cos_ref[...]
    sin = sin_ref[...]

    def proj(lo, width):
        return _dot(xb, w_ref[:, lo:lo + width])

    zmix_ref[...] = proj(0, mix_w)
    o = mix_w
    q_ref[...] = (_rope(proj(o, gw), cos, sin) * ATT_SCALE).astype(BF16)
    k_ref[...] = _rope(proj(o + gw, gw), cos, sin)
    v_ref[...] = proj(o + 2 * gw, gw)
    qi_ref[...] = _rope(proj(o + 3 * gw, gw), cos, sin)
    kw = proj(o + 4 * gw, LANES)
    is_key = lax.broadcasted_iota(jnp.int32, kw.shape, 1) < IDX_DIM
    kw_ref[...] = _rope(kw, jnp.where(is_key, cos[:, :LANES], 1.0), jnp.where(is_key, sin[:, :LANES], 0.0))


def _in_proj(x2, w_pad, cos_t, sin_t, tm, mix_w, gw):
    m, d = x2.shape
    tab_blocks = cos_t.shape[0] // tm
    row = lambda i: (i, 0)
    tab = lambda i: (i % tab_blocks, 0)
    outs = [
        jax.ShapeDtypeStruct((m, mix_w), F32),
        jax.ShapeDtypeStruct((m, gw), BF16),
        jax.ShapeDtypeStruct((m, gw), F32),
        jax.ShapeDtypeStruct((m, gw), F32),
        jax.ShapeDtypeStruct((m, gw), F32),
        jax.ShapeDtypeStruct((m, LANES), F32),
    ]
    return pl.pallas_call(
        functools.partial(_in_proj_kernel, mix_w=mix_w, gw=gw),
        grid=(m // tm,),
        in_specs=[pl.BlockSpec((tm, d), row),
                  pl.BlockSpec(w_pad.shape, lambda i: (0, 0)),
                  pl.BlockSpec((tm, gw), tab),
                  pl.BlockSpec((tm, gw), tab)],
        out_specs=[pl.BlockSpec((tm, s.shape[1]), row) for s in outs],
        out_shape=outs,
        compiler_params=_params("parallel"),
        name="in_proj",
    )(x2, w_pad, cos_t, sin_t)


LRU_HALO = 8
CC_HALO = 32


def _convmix_kernel(xl_ref, ca_ref, cg_ref, lbuf_ref, cbuf_ref, lw_ref, lb_ref, wg_ref, bg_ref, sp_ref,
                    cw_ref, cb_ref, lng_ref, lnb_ref,
                    a_ref, b_ref, ycc_ref, lbuf_o, cbuf_o, xl_s, xc_s, *, tt):
    i = pl.program_id(1)
    kl = lw_ref.shape[0]
    kc = cw_ref.shape[0]
    w = xl_ref.shape[-1]
    l0 = LRU_HALO - (kl - 1)
    c0 = CC_HALO - (kc - 1)

    @pl.when(i == 0)
    def _():
        xl_s[l0:LRU_HALO, :] = lbuf_ref[...]
        xc_s[c0:CC_HALO, :] = cbuf_ref[...]

    xl_s[LRU_HALO:LRU_HALO + tt, :] = xl_ref[...]
    xc_s[CC_HALO:CC_HALO + tt, :] = ca_ref[...] * jax.nn.sigmoid(cg_ref[...])

    conv = xl_s[l0:l0 + tt, :] * lw_ref[0:1, :]
    for j in range(1, kl):
        conv = conv + xl_s[l0 + j:l0 + j + tt, :] * lw_ref[j:j + 1, :]
    conv = conv + lb_ref[...]
    gates = _dot(conv.astype(BF16), wg_ref[...]) + bg_ref[...]
    r = jax.nn.sigmoid(gates[:, :w])
    ig = jax.nn.sigmoid(gates[:, w:])
    log_a = (-LRU_C) * r * sp_ref[...]
    a = jnp.exp(log_a)
    a_ref[...] = a
    b_ref[...] = jnp.sqrt(-jnp.tanh(log_a) * (a * a + 1.0)) * (ig * conv)

    c = xc_s[c0:c0 + tt, :] * cw_ref[0:1, :]
    for j in range(1, kc):
        c = c + xc_s[c0 + j:c0 + j + tt, :] * cw_ref[j:j + 1, :]
    c = _layer_norm(c + cb_ref[...], lng_ref[...], lnb_ref[...])
    ycc_ref[...] = c * jax.nn.sigmoid(c)

    new_l = xl_s[l0 + tt:LRU_HALO + tt, :]
    new_c = xc_s[c0 + tt:CC_HALO + tt, :]
    xl_s[l0:LRU_HALO, :] = new_l
    xc_s[c0:CC_HALO, :] = new_c
    lbuf_o[...] = new_l
    cbuf_o[...] = new_c


def _convmix(zmix3, lbuf, cbuf, lw, lb, wg, bg, sp, cw, cb, lng, lnb, tt):
    nb, t, _ = zmix3.shape
    w = lw.shape[1]
    col = lambda c: pl.BlockSpec((None, tt, w), lambda b, i, c=c: (b, i, c))
    full2 = lambda a: pl.BlockSpec(a.shape, lambda b, i: (0, 0))
    per_b = lambda a: pl.BlockSpec((None,) + a.shape[1:], lambda b, i: (b, 0, 0))
    seq_out = jax.ShapeDtypeStruct((nb, t, w), F32)
    return pl.pallas_call(
        functools.partial(_convmix_kernel, tt=tt),
        grid=(nb, t // tt),
        in_specs=[col(1), col(3), col(4), per_b(lbuf), per_b(cbuf), full2(lw), full2(lb), full2(wg),
                  full2(bg), full2(sp), full2(cw), full2(cb), full2(lng), full2(lnb)],
        out_specs=[pl.BlockSpec((None, tt, w), lambda b, i: (b, i, 0))] * 3 + [per_b(lbuf), per_b(cbuf)],
        out_shape=[seq_out, seq_out, seq_out,
                   jax.ShapeDtypeStruct(lbuf.shape, F32), jax.ShapeDtypeStruct(cbuf.shape, F32)],
        scratch_shapes=[pltpu.VMEM((LRU_HALO + tt, w), F32), pltpu.VMEM((CC_HALO + tt, w), F32)],
        compiler_params=_params("parallel", "arbitrary"),
        name="convmix",
    )(zmix3, zmix3, zmix3, lbuf, cbuf, lw, lb, wg, bg, sp, cw, cb, lng, lnb)


def _lru_scan_kernel(a_ref, b_ref, h0_ref, h_ref, hl_ref, e_s, p_s, hi_s, *, chained):
    nl, nr, _ = a_ref.shape

    def run(init):
        def step(t, h):
            h = a_ref[t] * h + b_ref[t]
            h_ref[t] = h
            return h
        return lax.fori_loop(0, nl, step, init)

    if not chained:
        hl_ref[...] = run(h0_ref[...])
        return

    def local(t, c):
        h, p = c
        at = a_ref[t]
        return at * h + b_ref[t], at * p
    zeros = jnp.zeros(e_s.shape, F32)
    e, p = lax.fori_loop(0, nl, local, (zeros, zeros + 1.0))
    e_s[...] = e
    p_s[...] = p
    hi_s[0:1, :] = h0_ref[...]

    def carry(c, _):
        prev = pl.ds(c - 1, 1)
        hi_s[pl.ds(c, 1), :] = p_s[prev, :] * hi_s[prev, :] + e_s[prev, :]
        return 0
    lax.fori_loop(1, nr, carry, 0)
    last = run(hi_s[...])
    hl_ref[...] = last[nr - 1:nr, :]


def _lru_scan(a4, b4, h0, chained):
    nb, nl, nr, w = a4.shape
    r0 = h0.shape[1]
    nw = w // LANES
    blk = pl.BlockSpec((None, nl, nr, LANES), lambda n, j: (n, 0, 0, j))
    st = pl.BlockSpec((None, r0, LANES), lambda n, j: (n, 0, j))
    return pl.pallas_call(
        functools.partial(_lru_scan_kernel, chained=chained),
        grid=(nb, nw),
        in_specs=[blk, blk, st],
        out_specs=[blk, st],
        out_shape=[jax.ShapeDtypeStruct(a4.shape, F32), jax.ShapeDtypeStruct(h0.shape, F32)],
        scratch_shapes=[pltpu.VMEM((nr, LANES), F32)] * 3,
        compiler_params=_params("parallel", "parallel"),
        name="lru_scan",
    )(a4, b4, h0)


def _s5_scan_kernel(u_ref, bre_ref, bim_ref, are_ref, aim_ref, cc_ref, h0r_ref, h0i_ref,
                    ys_ref, hlr_ref, hli_ref, br_s, bi_s, er_s, ei_s, hr_s, hi_s, *, chained, rows_per_dot):
    g = pl.program_id(1)
    nl, nr, _ = u_ref.shape
    n = nl * nr
    ar = are_ref[...]
    ai = aim_ref[...]

    def drive(c, _):
        r0 = pl.multiple_of(c * rows_per_dot, rows_per_dot)
        ub = u_ref[pl.ds(c * (rows_per_dot // nr), rows_per_dot // nr)].reshape(rows_per_dot, u_ref.shape[-1])
        br_s[pl.ds(r0, rows_per_dot), :] = _dot(ub, bre_ref[...])
        bi_s[pl.ds(r0, rows_per_dot), :] = _dot(ub, bim_ref[...])
        return 0
    lax.fori_loop(0, n // rows_per_dot, drive, 0)

    def step_fn(store):
        def step(t, c):
            hr, hi = c
            rows = pl.ds(pl.multiple_of(t * nr, nr), nr)
            nhr = ar * hr - ai * hi + br_s[rows, :]
            nhi = ar * hi + ai * hr + bi_s[rows, :]
            if store:
                br_s[rows, :] = nhr
                bi_s[rows, :] = nhi
            return nhr, nhi
        return step

    if chained:
        zeros = jnp.zeros((nr, LANES), F32)
        er, ei = lax.fori_loop(0, nl, step_fn(False), (zeros, zeros))
        er_s[...] = er
        ei_s[...] = ei

        def power(_, c):
            pr, pi = c
            return ar * pr - ai * pi, ar * pi + ai * pr
        alr, ali = lax.fori_loop(0, nl - 1, power, (ar, ai))
        hr_s[0:1, :] = h0r_ref[...]
        hi_s[0:1, :] = h0i_ref[...]

        def carry(c, _):
            prev = pl.ds(c - 1, 1)
            pr = hr_s[prev, :]
            pi = hi_s[prev, :]
            hr_s[pl.ds(c, 1), :] = alr * pr - ali * pi + er_s[prev, :]
            hi_s[pl.ds(c, 1), :] = alr * pi + ali * pr + ei_s[prev, :]
            return 0
        lax.fori_loop(1, nr, carry, 0)
        init = (hr_s[...], hi_s[...])
    else:
        init = (h0r_ref[...], h0i_ref[...])

    lr, li = lax.fori_loop(0, nl, step_fn(True), init)
    if chained:
        hlr_ref[...] = lr[nr - 1:nr, :]
        hli_ref[...] = li[nr - 1:nr, :]
    else:
        hlr_ref[...] = lr
        hli_ref[...] = li

    def project(c, _):
        r0 = pl.multiple_of(c * rows_per_dot, rows_per_dot)
        rows = pl.ds(r0, rows_per_dot)
        hcat = jnp.concatenate([br_s[rows, :], bi_s[rows, :]], axis=1).astype(BF16)
        y = _dot(hcat, cc_ref[...]).reshape(rows_per_dot // nr, nr, ys_ref.shape[-1])
        tsl = pl.ds(c * (rows_per_dot // nr), rows_per_dot // nr)

        @pl.when(g == 0)
        def _():
            ys_ref[tsl] = y

        @pl.when(g > 0)
        def _():
            ys_ref[tsl] = ys_ref[tsl] + y
        return 0
    lax.fori_loop(0, n // rows_per_dot, project, 0)


def _s5_scan(u4, bre, bim, are, aim, ccat, h0r, h0i, chained):
    nb, nl, nr, w = u4.shape
    ng = bre.shape[0]
    r0 = h0r.shape[1]
    n = nl * nr
    rows_per_dot = math.gcd(n, 1024)
    rows_per_dot = max(rows_per_dot, nr)
    seq = pl.BlockSpec((None, nl, nr, w), lambda b, g: (b, 0, 0, 0))
    per_g = lambda a: pl.BlockSpec((None,) + a.shape[1:], lambda b, g: (g, 0, 0))
    st = pl.BlockSpec((None, r0, LANES), lambda b, g: (b, 0, g))
    small = pltpu.VMEM((nr, LANES), F32)
    return pl.pallas_call(
        functools.partial(_s5_scan_kernel, chained=chained, rows_per_dot=rows_per_dot),
        grid=(nb, ng),
        in_specs=[seq, per_g(bre), per_g(bim), per_g(are), per_g(aim), per_g(ccat), st, st],
        out_specs=[seq, st, st],
        out_shape=[jax.ShapeDtypeStruct(u4.shape, F32), jax.ShapeDtypeStruct(h0r.shape, F32),
                   jax.ShapeDtypeStruct(h0i.shape, F32)],
        scratch_shapes=[pltpu.VMEM((n, LANES), F32), pltpu.VMEM((n, LANES), F32), small, small, small, small],
        compiler_params=_params("parallel", "arbitrary"),
        name="s5_scan",
    )(u4, bre, bim, are, aim, ccat, h0r, h0i)


BISECT_ITERS = 24
PROBE_QUERIES = 12
PROBE_GAP = 3


def _select_threshold(count_ge, max_below, mn, mx, kq, all_sel, probe_queries):
    inf = jnp.full(mn.shape, jnp.inf, F32)
    zero = jnp.zeros(mn.shape, F32)
    state = (mn, inf, mn, jnp.where(all_sel, 1.0, zero), zero)

    def unfinished(st):
        return jnp.min(st[3]) == 0.0

    def bisect(c):
        it, (lo, hi, theta, done, tie) = c
        mid = jnp.where(hi == jnp.inf, mx, 0.5 * lo + 0.5 * hi)
        cnt = count_ge(mid)
        live = done == 0.0
        collapsed = jnp.logical_and(live, jnp.logical_or(mid == lo, mid == hi))
        hit = jnp.logical_and(live, cnt == kq)
        move = jnp.logical_and(live, jnp.logical_not(collapsed))
        theta = jnp.where(hit, mid, jnp.where(collapsed, hi, theta))
        tie = jnp.where(collapsed, 1.0, tie)
        done = jnp.where(jnp.logical_or(hit, collapsed), 1.0, done)
        lo = jnp.where(jnp.logical_and(move, cnt > kq), mid, lo)
        hi = jnp.where(jnp.logical_and(move, cnt < kq), mid, hi)
        return it + 1, (lo, hi, theta, done, tie)

    _, state = lax.while_loop(
        lambda c: jnp.logical_and(c[0] < BISECT_ITERS, jnp.sum(1.0 - c[1][3]) > float(probe_queries)),
        lambda c: bisect(bisect(c)), (jnp.int32(0), state))

    def exact_step(st):
        lo, hi, theta, done, tie = st
        live = done == 0.0
        tau = max_below(hi)
        cnt = count_ge(tau)
        found = jnp.logical_and(live, cnt >= kq)
        clean = jnp.logical_and(found, cnt == kq)
        theta = jnp.where(clean, tau, jnp.where(found, hi, theta))
        tie = jnp.where(jnp.logical_and(found, jnp.logical_not(clean)), 1.0, tie)
        lo = jnp.where(found, tau, lo)
        hi = jnp.where(jnp.logical_and(live, jnp.logical_not(found)), tau, hi)
        done = jnp.where(found, 1.0, done)
        return lo, hi, theta, done, tie

    def settle(st):
        st = exact_step(st)
        _, st = lax.while_loop(lambda c: jnp.logical_and(c[0] < PROBE_GAP, unfinished(c[1])),
                               bisect, (jnp.int32(0), st))
        return st

    lo, hi, theta, done, tie = lax.while_loop(unfinished, settle, state)
    return theta, lo, tie


COUNT_ROWS = 64


def _dsa_prompt_kernel(qt_ref, qih_ref, qil_ref, wit_ref, kidx_ref, k_ref, vt_ref, o_ref,
                       s_scr, lg_a, lg_b, qp_s, qip_s, *, topk, kc):
    i = pl.program_id(1)
    qb = Q_BLOCK
    n_pairs = qp_s.shape[0]
    pw = 2 * HEAD_DIM

    zq = jnp.zeros((HEAD_DIM, qb), BF16)
    for p in range(n_pairs):
        top = jnp.concatenate([qt_ref[p * pw:p * pw + HEAD_DIM, :], zq], axis=1)
        bot = jnp.concatenate([zq, qt_ref[p * pw + HEAD_DIM:(p + 1) * pw, :]], axis=1)
        qp_s[p] = jnp.concatenate([top, bot], axis=0)
        cols = []
        for h in (2 * p, 2 * p + 1):
            hi = qih_ref[h * IDX_DIM:(h + 1) * IDX_DIM, :]
            cols.append(jnp.concatenate([hi, hi, qil_ref[h * IDX_DIM:(h + 1) * IDX_DIM, :], zq], axis=0))
        qip_s[p] = jnp.concatenate(cols, axis=1)
    nch = ((i + 1) * qb + kc - 1) // kc
    qpos = i * qb + lax.broadcasted_iota(jnp.int32, (1, qb), 1)
    kiota = lax.broadcasted_iota(jnp.int32, (kc, qb), 0)

    def chunk_rows(c):
        return pl.ds(pl.multiple_of(c * kc, kc), kc)

    def score_chunk(c, carry):
        mn, mx = carry
        kk = kidx_ref[chunk_rows(c), :]
        s = jnp.zeros((kc, qb), F32)
        for p in range(n_pairs):
            sp = _dot(kk, qip_s[p])
            for hh in range(2):
                h = 2 * p + hh
                s = s + jnp.maximum(sp[:, hh * qb:(hh + 1) * qb], 0.0) * wit_ref[h:h + 1, :]
        s = s * IDX_SCALE
        vis = (c * kc + kiota) <= qpos
        s_scr[chunk_rows(c), :] = jnp.where(vis, s, -jnp.inf)
        mn = jnp.minimum(mn, jnp.min(jnp.where(vis, s, jnp.inf), axis=0, keepdims=True))
        mx = jnp.maximum(mx, jnp.max(jnp.where(vis, s, -jnp.inf), axis=0, keepdims=True))
        return mn, mx
    nch2 = (nch + 1) // 2
    mn, mx = lax.fori_loop(0, 2 * nch2, score_chunk,
                           (jnp.full((1, qb), jnp.inf, F32), jnp.full((1, qb), -jnp.inf, F32)))

    def fold(x, op):
        return op(x.reshape(kc // COUNT_ROWS, COUNT_ROWS, qb), axis=0)

    def count(pred_fn):
        def body(c, acc):
            hit = jnp.where(pred_fn(s_scr[chunk_rows(c), :], c * kc + kiota), 1.0, 0.0)
            return acc + fold(hit, jnp.sum)
        acc = lax.fori_loop(0, nch, body, jnp.zeros((COUNT_ROWS, qb), F32))
        return jnp.sum(acc, axis=0, keepdims=True)

    def max_below(x):
        def body(c, acc):
            s = s_scr[chunk_rows(c), :]
            return jnp.maximum(acc, fold(jnp.where(s < x, s, -jnp.inf), jnp.max))
        acc = lax.fori_loop(0, nch, body, jnp.full((COUNT_ROWS, qb), -jnp.inf, F32))
        return jnp.max(acc, axis=0, keepdims=True)

    n_vis = (qpos + 1).astype(F32)
    kq = jnp.minimum(n_vis, float(topk))
    theta, lo, tie = _select_threshold(lambda x: count(lambda s, kpos: s >= x), max_below,
                                       mn, mx, kq, n_vis <= float(topk), PROBE_QUERIES)

    @pl.when(jnp.max(tie) > 0.0)
    def _():
        need = kq - count(lambda s, kpos: s >= theta)
        tied = lambda s: jnp.logical_and(tie > 0.0, s == lo)
        tri = jnp.where(lax.broadcasted_iota(jnp.int32, (kc, kc), 0) >= lax.broadcasted_iota(jnp.int32, (kc, kc), 1),
                        1.0, 0.0).astype(BF16)

        def promote(c, seen):
            s = s_scr[chunk_rows(c), :]
            is_tied = tied(s)
            rank = seen + _dot(tri, jnp.where(is_tied, 1.0, 0.0).astype(BF16))
            keep = jnp.logical_and(is_tied, rank <= need)
            s_scr[chunk_rows(c), :] = jnp.where(keep, theta, s)
            return rank[kc - 1:kc, :]
        lax.fori_loop(0, nch, promote, jnp.zeros((1, qb), F32))

    n_heads = 2 * n_pairs
    kc2 = 2 * kc

    def chunk2_rows(c):
        return pl.ds(pl.multiple_of(jnp.minimum(c, nch2 - 1) * kc2, kc2), kc2)

    def put_logits(dst, c):
        for p in range(n_pairs):
            dst[p] = _dot(k_ref[chunk2_rows(c), p * pw:(p + 1) * pw], qp_s[p])

    def absorb(src, c, carry):
        ms, ls, accs = carry
        rows2 = chunk2_rows(c)
        sel = jnp.logical_and(s_scr[rows2, :] >= theta, c < nch2)
        new_m, new_l, new_acc = [], [], []
        for p in range(n_pairs):
            for hh in range(2):
                h = 2 * p + hh
                logit = jnp.where(sel, src[p, :, hh * qb:(hh + 1) * qb], NEG_BIG)
                m_new = jnp.maximum(ms[h], jnp.max(logit, axis=0, keepdims=True))
                alpha = jnp.exp(ms[h] - m_new)
                pr = jnp.exp(logit - m_new)
                new_m.append(m_new)
                new_l.append(alpha * ls[h] + jnp.sum(pr, axis=0, keepdims=True))
                pv = _dot(vt_ref[h * HEAD_DIM:(h + 1) * HEAD_DIM, rows2], pr.astype(BF16))
                new_acc.append(alpha * accs[h] + pv)
        return tuple(new_m), tuple(new_l), tuple(new_acc)

    def att_pair(j, carry):
        c = 2 * j
        put_logits(lg_b, c + 1)
        carry = absorb(lg_a, c, carry)
        put_logits(lg_a, c + 2)
        return absorb(lg_b, c + 1, carry)

    init = (tuple(jnp.full((1, qb), NEG_BIG, F32) for _ in range(n_heads)),
            tuple(jnp.zeros((1, qb), F32) for _ in range(n_heads)),
            tuple(jnp.zeros((HEAD_DIM, qb), F32) for _ in range(n_heads)))
    put_logits(lg_a, 0)
    _, ls, accs = lax.fori_loop(0, (nch2 + 1) // 2, att_pair, init)
    for h in range(n_heads):
        o_ref[h * HEAD_DIM:(h + 1) * HEAD_DIM, :] = accs[h] / ls[h]


def _dsa_prompt(qt, qiht, qilt, wit, kidx3, kb, vt, topk, kc):
    nb, w, t = vt.shape
    n_pairs = w // (2 * HEAD_DIM)
    per_b = lambda a: pl.BlockSpec((None,) + a.shape[1:], lambda b, i: (b, 0, 0))
    qcol = lambda a: pl.BlockSpec((None, a.shape[1], Q_BLOCK), lambda b, i: (b, 0, i))
    return pl.pallas_call(
        functools.partial(_dsa_prompt_kernel, topk=topk, kc=kc),
        grid=(nb, t // Q_BLOCK),
        in_specs=[qcol(qt), qcol(qiht), qcol(qilt), qcol(wit), per_b(kidx3), per_b(kb), per_b(vt)],
        out_specs=pl.BlockSpec((None, w, Q_BLOCK), lambda b, i: (b, 0, i)),
        out_shape=jax.ShapeDtypeStruct((nb, w, t), F32),
        scratch_shapes=[pltpu.VMEM((t, Q_BLOCK), F32),
                        pltpu.VMEM((n_pairs, 2 * kc, 2 * Q_BLOCK), F32),
                        pltpu.VMEM((n_pairs, 2 * kc, 2 * Q_BLOCK), F32),
                        pltpu.VMEM((n_pairs, 2 * HEAD_DIM, 2 * Q_BLOCK), BF16),
                        pltpu.VMEM((n_pairs, kidx3.shape[2], 2 * Q_BLOCK), BF16)],
        compiler_params=_params("parallel", "arbitrary"),
        name="dsa_prompt",
    )(qt, qiht, qilt, wit, kidx3, kb, vt)


SAMPLE_PAGES_PER_STEP = 16
QROWS = 8


SELECT_BATCHES_PER_STEP = 4


def _dsa_sample_select_kernel(pt_ref, qhi_ref, qlo_ref, wi_ref, knew_ref, *rest, topk, n_new, gp, gb):
    pages = rest[:gb * gp]
    bias_ref, s_scr = rest[gb * gp], rest[gb * gp + 1]
    j = pl.program_id(1)
    nj = pl.num_programs(1)
    psz = pages[0].shape[1]
    past = nj * gp * psz
    rows = gb * QROWS

    def scores(g, kpage_t):
        khi, klo = _split3(kpage_t)
        qhi = qhi_ref[g]
        s = _dot(qhi, khi) + _dot(qhi, klo) + _dot(qlo_ref[g], khi)
        s = jnp.maximum(s, 0.0) * wi_ref[g]
        tot = s[0:QROWS, :]
        for h in range(1, IDX_HEADS):
            tot = tot + s[h * QROWS:(h + 1) * QROWS, :]
        return tot * IDX_SCALE

    for g in range(gb):
        for p in range(gp):
            off = pl.multiple_of((j * gp + p) * psz, psz)
            s_scr[g * QROWS:(g + 1) * QROWS, pl.ds(off, psz)] = scores(g, pages[g * gp + p][...])

    @pl.when(j == nj - 1)
    def _():
        qrow = lax.broadcasted_iota(jnp.int32, (QROWS, psz), 0)
        kcol = lax.broadcasted_iota(jnp.int32, (QROWS, psz), 1)
        vis_new = jnp.logical_and(kcol <= qrow, kcol < n_new)
        for g in range(gb):
            s_scr[g * QROWS:(g + 1) * QROWS, past:past + psz] = jnp.where(vis_new, scores(g, knew_ref[g]), -jnp.inf)
        s = s_scr[...]
        nk = s.shape[1]
        kpos = lax.broadcasted_iota(jnp.int32, s.shape, 1)
        qrow1 = lax.broadcasted_iota(jnp.int32, (rows, 1), 0) & (QROWS - 1)
        n_vis = (past + jnp.minimum(qrow1 + 1, n_new)).astype(F32)
        kq = jnp.minimum(n_vis, float(topk))
        fin = s > -jnp.inf
        mn = jnp.min(jnp.where(fin, s, jnp.inf), axis=1, keepdims=True)
        mx = jnp.max(s, axis=1, keepdims=True)
        count = lambda pred: jnp.sum(jnp.where(pred, 1.0, 0.0), axis=1, keepdims=True)
        theta, lo, tie = _select_threshold(
            lambda x: count(s >= x),
            lambda x: jnp.max(jnp.where(s < x, s, -jnp.inf), axis=1, keepdims=True),
            mn, mx, kq, jnp.logical_or(n_vis <= float(topk), qrow1 >= n_new), 0)

        sel = s >= theta
        bias_ref[...] = jnp.where(sel, 0.0, NEG_BIG).reshape(bias_ref.shape)

        @pl.when(jnp.max(tie) > 0.0)
        def _():
            need = kq - count(sel)
            tied = jnp.logical_and(tie > 0.0, s == lo)
            nbits = max(1, (nk - 1).bit_length())

            def jstep(b, jl):
                cand = jl + (jnp.int32(1) << (nbits - 1 - b))
                c_lt = count(jnp.logical_and(tied, kpos < cand))
                return jnp.where(c_lt < need, cand, jl)
            jlast = lax.fori_loop(0, nbits, jstep, jnp.zeros((rows, 1), jnp.int32))
            keep = jnp.logical_or(sel, jnp.logical_and(tied, kpos <= jlast))
            bias_ref[...] = jnp.where(keep, 0.0, NEG_BIG).reshape(bias_ref.shape)


def _dsa_sample_select(layer, page_table, qhi, qlo, wi_b, knew, pool_kidx, topk, n_new):
    nb, n_pages = page_table.shape
    gp = math.gcd(n_pages, SAMPLE_PAGES_PER_STEP)
    gb = math.gcd(nb, SELECT_BATCHES_PER_STEP)
    di, psz = pool_kidx.shape[2], pool_kidx.shape[3]
    nk = (n_pages + 1) * psz
    per_b = lambda a: pl.BlockSpec((gb,) + a.shape[1:], lambda b, j, pt: (b, 0, 0))
    page = lambda g, p: pl.BlockSpec((None, None, di, psz),
                                     lambda b, j, pt: (layer, pt[b * gb + g, j * gp + p], 0, 0))
    grid_spec = pltpu.PrefetchScalarGridSpec(
        num_scalar_prefetch=1,
        grid=(nb // gb, n_pages // gp),
        in_specs=[per_b(qhi), per_b(qlo), per_b(wi_b), per_b(knew)]
        + [page(g, p) for g in range(gb) for p in range(gp)],
        out_specs=pl.BlockSpec((gb, QROWS, nk), lambda b, j, pt: (b, 0, 0)),
        scratch_shapes=[pltpu.VMEM((gb * QROWS, nk), F32)],
    )
    return pl.pallas_call(
        functools.partial(_dsa_sample_select_kernel, topk=topk, n_new=n_new, gp=gp, gb=gb),
        grid_spec=grid_spec,
        out_shape=jax.ShapeDtypeStruct((nb, QROWS, nk), F32),
        compiler_params=_params("parallel", "arbitrary"),
        name="dsa_sample_select",
    )(page_table, qhi, qlo, wi_b, knew, *([pool_kidx] * (gb * gp)))


def _dsa_sample_attend_kernel(pt_ref, q_ref, knew_ref, vnew_ref, *rest, gp):
    bias_refs = rest[:gp]
    kpages = rest[gp:2 * gp]
    vpages = rest[2 * gp:3 * gp]
    bnew_ref = rest[3 * gp]
    o_ref, m_s, l_s, acc_s = rest[3 * gp + 1:]
    j = pl.program_id(1)
    nj = pl.num_programs(1)
    q = q_ref[...]
    n_heads = q.shape[0] // QROWS

    @pl.when(j == 0)
    def _():
        m_s[...] = jnp.full(m_s.shape, NEG_BIG, F32)
        l_s[...] = jnp.zeros(l_s.shape, F32)
        acc_s[...] = jnp.zeros(acc_s.shape, F32)

    def absorb(kts, vts, biases):
        psz = kts[0].shape[1]
        logit = jnp.concatenate(
            [_dot(q, kt[...].astype(BF16)) + jnp.concatenate([b[...]] * n_heads, axis=0)
             for kt, b in zip(kts, biases)], axis=1)
        m = m_s[...]
        m_new = jnp.maximum(m, jnp.max(logit, axis=1, keepdims=True))
        alpha = jnp.exp(m - m_new)
        p = jnp.exp(logit - m_new)
        l_s[...] = alpha * l_s[...] + jnp.sum(p, axis=1, keepdims=True)
        pv = _dot_t(p[:, 0:psz].astype(BF16), vts[0][...].astype(BF16))
        for g in range(1, len(vts)):
            pv = pv + _dot_t(p[:, g * psz:(g + 1) * psz].astype(BF16), vts[g][...].astype(BF16))
        acc_s[...] = alpha * acc_s[...] + pv
        m_s[...] = m_new

    absorb(kpages, vpages, bias_refs)

    @pl.when(j == nj - 1)
    def _():
        absorb([knew_ref], [vnew_ref], [bnew_ref])
        out = acc_s[...] / l_s[...]
        lane = lax.broadcasted_iota(jnp.int32, (QROWS, out.shape[1]), 1)
        y = jnp.zeros((QROWS, out.shape[1]), F32)
        for h in range(n_heads):
            in_head = jnp.logical_and(lane >= h * HEAD_DIM, lane < (h + 1) * HEAD_DIM)
            y = y + jnp.where(in_head, out[h * QROWS:(h + 1) * QROWS, :], 0.0)
        o_ref[...] = y


def _dsa_sample_attend(layer, page_table, q32, knew, vnew, bias, pool_k, pool_v):
    nb, n_pages = page_table.shape
    gp = math.gcd(n_pages, 2 * SAMPLE_PAGES_PER_STEP)
    w, psz = pool_k.shape[2], pool_k.shape[3]
    per_b = lambda a: pl.BlockSpec((None,) + a.shape[1:], lambda b, j, pt: (b, 0, 0))
    page = lambda g: pl.BlockSpec((None, None, w, psz), lambda b, j, pt, g=g: (layer, pt[b, j * gp + g], 0, 0))
    bias_pg = lambda g: pl.BlockSpec((None, QROWS, psz), lambda b, j, pt, g=g: (b, 0, j * gp + g))
    bias_new = pl.BlockSpec((None, QROWS, psz), lambda b, j, pt: (b, 0, n_pages))
    grid_spec = pltpu.PrefetchScalarGridSpec(
        num_scalar_prefetch=1,
        grid=(nb, n_pages // gp),
        in_specs=([per_b(q32), per_b(knew), per_b(vnew)] + [bias_pg(g) for g in range(gp)]
                  + [page(g) for g in range(gp)] + [page(g) for g in range(gp)] + [bias_new]),
        out_specs=pl.BlockSpec((None, QROWS, w), lambda b, j, pt: (b, 0, 0)),
        scratch_shapes=[pltpu.VMEM((q32.shape[1], 1), F32), pltpu.VMEM((q32.shape[1], 1), F32),
                        pltpu.VMEM((q32.shape[1], w), F32)],
    )
    return pl.pallas_call(
        functools.partial(_dsa_sample_attend_kernel, gp=gp),
        grid_spec=grid_spec,
        out_shape=jax.ShapeDtypeStruct((nb, QROWS, w), F32),
        compiler_params=_params("parallel", "arbitrary"),
        name="dsa_sample_attend",
    )(page_table, q32, knew, vnew, *([bias] * gp), *([pool_k] * gp), *([pool_v] * gp), bias)


def _mixout_kernel(x_ref, u_ref, g_ref, ys_ref, h_ref, ycc_ref, yatt_ref, d_ref, wglu_ref, bglu_ref,
                   wout_ref, g1_ref, b1_ref, o_ref, *, alpha):
    w = u_ref.shape[-1]
    y = ys_ref[...] + d_ref[...] * u_ref[...]
    z = _dot(jax.nn.gelu(y).astype(BF16), wglu_ref[...]) + bglu_ref[...]
    y_s5 = z[:, :w] * jax.nn.sigmoid(z[:, w:])
    y_lru = h_ref[...] * jax.nn.gelu(g_ref[...])
    acc = _dot(y_s5.astype(BF16), wout_ref[0:w, :])
    acc = acc + _dot(y_lru.astype(BF16), wout_ref[w:2 * w, :])
    acc = acc + _dot(ycc_ref[...].astype(BF16), wout_ref[2 * w:3 * w, :])
    acc = acc + _dot(yatt_ref[...].astype(BF16), wout_ref[3 * w:4 * w, :])
    o_ref[...] = _layer_norm(alpha * x_ref[...] + acc, g1_ref[...], b1_ref[...])


def _mixout(x2, zmix, ys, h, ycc, yatt, d, wglu, bglu, wout, g1, b1, tm, alpha):
    m, dm = x2.shape
    w = ys.shape[1]
    row = lambda a: pl.BlockSpec((tm, a.shape[1]), lambda i: (i, 0))
    full = lambda a: pl.BlockSpec(a.shape, lambda i: (0, 0))
    zcol = lambda c: pl.BlockSpec((tm, w), lambda i, c=c: (i, c))
    return pl.pallas_call(
        functools.partial(_mixout_kernel, alpha=alpha),
        grid=(m // tm,),
        in_specs=[row(x2), zcol(0), zcol(2), row(ys), row(h), row(ycc), row(yatt), full(d), full(wglu),
                  full(bglu), full(wout), full(g1), full(b1)],
        out_specs=pl.BlockSpec((tm, dm), lambda i: (i, 0)),
        out_shape=jax.ShapeDtypeStruct((m, dm), F32),
        compiler_params=_params("parallel"),
        name="mixout",
    )(x2, zmix, zmix, ys, h, ycc, yatt, d, wglu, bglu, wout, g1, b1)


def _ffn_kernel(h_ref, wg_ref, wu_ref, wd_ref, g2_ref, b2_ref, o_ref, acc_s, hb_s, *, alpha):
    f = pl.program_id(1)

    @pl.when(f == 0)
    def _():
        acc_s[...] = jnp.zeros(acc_s.shape, F32)
        hb_s[...] = h_ref[...].astype(BF16)

    hb = hb_s[...]
    a = _dot(hb, wg_ref[...])
    act = (a * jax.nn.sigmoid(a)) * _dot(hb, wu_ref[...])
    acc_s[...] += _dot(act.astype(BF16), wd_ref[...])

    @pl.when(f == pl.num_programs(1) - 1)
    def _():
        o_ref[...] = _layer_norm(alpha * h_ref[...] + acc_s[...], g2_ref[...], b2_ref[...])


def _ffn(h, wg, wu, wd, g2, b2, tm, tf, alpha):
    m, dm = h.shape
    nf = wg.shape[1] // tf
    return pl.pallas_call(
        functools.partial(_ffn_kernel, alpha=alpha),
        grid=(m // tm, nf),
        in_specs=[pl.BlockSpec((tm, dm), lambda i, f: (i, 0)),
                  pl.BlockSpec((dm, tf), lambda i, f: (0, f)),
                  pl.BlockSpec((dm, tf), lambda i, f: (0, f)),
                  pl.BlockSpec((tf, dm), lambda i, f: (f, 0)),
                  pl.BlockSpec(g2.shape, lambda i, f: (0, 0)),
                  pl.BlockSpec(b2.shape, lambda i, f: (0, 0))],
        out_specs=pl.BlockSpec((tm, dm), lambda i, f: (i, 0)),
        out_shape=jax.ShapeDtypeStruct((m, dm), F32),
        scratch_shapes=[pltpu.VMEM((tm, dm), F32), pltpu.VMEM((tm, dm), BF16)],
        compiler_params=_params("parallel", "arbitrary"),
        name="ffn",
    )(h, wg, wu, wd, g2, b2)


MOE_CHUNK_ROWS = 288
MOE_COL_TILE = 1792
MOE_ROW_TILE = 1024
FFN_COL_TILE = 256
FFN_ROW_TILE = 1024


def _moe_kernel(h_ref, wrh_ref, wrl_ref, br_ref, wg_ref, wu_ref, wd_ref, g2_ref, b2_ref, o_ref,
                hb_s, xg_s, y_s, gate_c, slot_c, gate_b, slot_b_s, slot_t_s, *, alpha, n_exp, cr):
    e = pl.program_id(1)
    f = pl.program_id(2)
    nf = pl.num_programs(2)
    tm, dm = h_ref.shape
    first = jnp.logical_and(e == 0, f == 0)
    last = jnp.logical_and(e == pl.num_programs(1) - 1, f == nf - 1)

    @pl.when(first)
    def _():
        hhi, hlo = _split3(h_ref[...])
        hb_s[...] = hhi
        logits = _dot(hhi, wrh_ref[...]) + _dot(hhi, wrl_ref[...]) + _dot(hlo, wrh_ref[...]) + br_ref[...]
        lane = lax.broadcasted_iota(jnp.int32, logits.shape, 1)
        m1 = jnp.max(logits, axis=1, keepdims=True)
        i1 = jnp.min(jnp.where(logits == m1, lane, LANES), axis=1, keepdims=True)
        rest = jnp.where(lane == i1, -jnp.inf, logits)
        m2 = jnp.max(rest, axis=1, keepdims=True)
        i2 = jnp.min(jnp.where(rest == m2, lane, LANES), axis=1, keepdims=True)
        e2 = jnp.exp(m2 - m1)
        den = 1.0 + e2
        gate = jnp.where(lane == i1, 1.0 / den, 0.0) + jnp.where(lane == i2, e2 / den, 0.0)
        routed = jnp.logical_or(lane == i1, lane == i2)
        ind = jnp.where(routed, 1.0, 0.0).astype(BF16)
        tri = jnp.where(lax.broadcasted_iota(jnp.int32, (tm, tm), 0) >= lax.broadcasted_iota(jnp.int32, (tm, tm), 1),
                        1.0, 0.0).astype(BF16)
        slot = jnp.where(routed, _dot(tri, ind) - 1.0, -1.0)
        eye = jnp.where(lax.broadcasted_iota(jnp.int32, (LANES, LANES), 0)
                        == lax.broadcasted_iota(jnp.int32, (LANES, LANES), 1), 1.0, 0.0).astype(BF16)
        ind_t = _dot_t(eye, ind)
        slot_t = jnp.where(ind_t > 0.0, _dot_t(ind_t.astype(BF16), tri) - 1.0, -1.0)
        slot_t_s[...] = slot_t[0:slot_t_s.shape[0], :]
        gate_c[...] = gate
        slot_c[...] = slot
        o_ref[...] = jnp.zeros(o_ref.shape, F32)

    @pl.when(f == 0)
    def _():
        lane = lax.broadcasted_iota(jnp.int32, gate_c.shape, 1)
        pick = lambda x: jnp.sum(jnp.where(lane == e, x, 0.0), axis=1, keepdims=True)
        gate_b[...] = jnp.broadcast_to(pick(gate_c[...]), gate_b.shape)
        slot_b_s[...] = jnp.broadcast_to(pick(slot_c[...]), slot_b_s.shape)

    slot_b = slot_b_s[...]
    n_pass = ((jnp.max(slot_b) + 1.0).astype(jnp.int32) + cr - 1) // cr

    def pass_rows(r):
        return pl.ds(pl.multiple_of(r * cr, 16), cr)

    @pl.when(f == 0)
    def _():
        srow = slot_t_s[pl.ds(e, 1), :]

        def pack(r, _):
            want = (r * cr + lax.broadcasted_iota(jnp.int32, (cr, tm), 0)).astype(F32)
            onehot = jnp.where(srow == want, 1.0, 0.0).astype(BF16)
            xg_s[pass_rows(r), :] = _dot(onehot, hb_s[...]).astype(BF16)
            return 0
        lax.fori_loop(0, n_pass, pack, 0)

    def expert_pass(r, _):
        x = xg_s[pass_rows(r), :]
        a = _dot(x, wg_ref[...])
        act = (a * jax.nn.sigmoid(a)) * _dot(x, wu_ref[...])
        y = _dot(act.astype(BF16), wd_ref[...])

        @pl.when(f == 0)
        def _():
            y_s[pass_rows(r), :] = y

        @pl.when(f > 0)
        def _():
            y_s[pass_rows(r), :] = y_s[pass_rows(r), :] + y
        return 0
    lax.fori_loop(0, n_pass, expert_pass, 0)

    @pl.when(f == nf - 1)
    def _():
        reps = (cr + LANES - 1) // LANES
        slot_w = jnp.concatenate([slot_b] * reps, axis=1)[:, :cr]
        gate_w = jnp.concatenate([gate_b[...]] * (dm // LANES), axis=1)

        def unpack(r, _):
            want = (r * cr + lax.broadcasted_iota(jnp.int32, (tm, cr), 1)).astype(F32)
            onehot = jnp.where(slot_w == want, 1.0, 0.0).astype(BF16)
            o_ref[...] += gate_w * _dot(onehot, y_s[pass_rows(r), :].astype(BF16))
            return 0
        lax.fori_loop(0, n_pass, unpack, 0)

    @pl.when(last)
    def _():
        o_ref[...] = _layer_norm(alpha * h_ref[...] + o_ref[...], g2_ref[...], b2_ref[...])


def _moe(h, wrh, wrl, br, wg, wu, wd, g2, b2, tm, tf, alpha):
    m, dm = h.shape
    n_exp, _, dff = wg.shape
    nf = dff // tf
    cr = min(MOE_CHUNK_ROWS, tm)
    cap = -(-tm // cr) * cr
    c2 = lambda a: pl.BlockSpec(a.shape, lambda i, e, f: (0, 0))
    return pl.pallas_call(
        functools.partial(_moe_kernel, alpha=alpha, n_exp=n_exp, cr=cr),
        grid=(m // tm, n_exp, nf),
        in_specs=[pl.BlockSpec((tm, dm), lambda i, e, f: (i, 0), pipeline_mode=pl.Buffered(1)),
                  c2(wrh), c2(wrl), c2(br),
                  pl.BlockSpec((None, dm, tf), lambda i, e, f: (e, 0, f)),
                  pl.BlockSpec((None, dm, tf), lambda i, e, f: (e, 0, f)),
                  pl.BlockSpec((None, tf, dm), lambda i, e, f: (e, f, 0)),
                  c2(g2), c2(b2)],
        out_specs=pl.BlockSpec((tm, dm), lambda i, e, f: (i, 0)),
        out_shape=jax.ShapeDtypeStruct((m, dm), F32),
        scratch_shapes=[pltpu.VMEM((tm, dm), BF16),
                        pltpu.VMEM((cap, dm), BF16),
                        pltpu.VMEM((cap, dm), F32),
                        pltpu.VMEM((tm, LANES), F32),
                        pltpu.VMEM((tm, LANES), F32),
                        pltpu.VMEM((tm, LANES), F32),
                        pltpu.VMEM((tm, LANES), F32),
                        pltpu.VMEM((max(8, n_exp), tm), F32)],
        compiler_params=_params("parallel", "arbitrary", "arbitrary"),
        name="moe",
    )(h, wrh, wrl, br, wg, wu, wd, g2, b2)


def _row_tile(m, target):
    t = math.gcd(m, target)
    return t if t % 8 == 0 else m


def _chunk_len(t):
    return math.gcd(t, 128)


def _rope_tables(pos, reps):
    half = HEAD_DIM // 2
    inv = ROPE_THETA ** (-(jnp.arange(half, dtype=F32) / half))
    ang = pos.astype(F32)[:, None] * inv[None, :]
    cos, sin = jnp.cos(ang), jnp.sin(ang)
    cos_h = jnp.concatenate([cos, cos], axis=-1)
    sin_h = jnp.concatenate([-sin, sin], axis=-1)
    return jnp.tile(cos_h, (1, reps)), jnp.tile(sin_h, (1, reps))


def _s5_params(a_re, a_im, b_re, b_im, c_re, c_im, log_dt):
    ng, ns = a_re.shape
    lr, li = a_re.astype(F32), a_im.astype(F32)
    dt = jnp.exp(log_dt.astype(F32))[:, None]
    mag = jnp.exp(lr * dt)
    ar, ai = mag * jnp.cos(li * dt), mag * jnp.sin(li * dt)
    den = lr * lr + li * li
    qr = ((ar - 1.0) * lr + ai * li) / den
    qi = (ai * lr - (ar - 1.0) * li) / den
    br, bi = b_re.astype(F32), b_im.astype(F32)
    bbr = qr[..., None] * br - qi[..., None] * bi
    bbi = qr[..., None] * bi + qi[..., None] * br
    eye = jnp.eye(ng, dtype=F32)
    gpl = LANES // ns
    nslab = ng // gpl

    def b_mat(x):
        full = jnp.einsum('gpc,gh->gchp', x, eye).reshape(ng * S5_CH, ng * ns)
        return jnp.transpose(full.reshape(ng * S5_CH, nslab, LANES), (1, 0, 2)).astype(BF16)

    def c_mat(x):
        return jnp.einsum('gcp,gh->gphc', x, eye).reshape(nslab, LANES, ng * S5_CH)

    ccat = jnp.concatenate([c_mat(c_re.astype(F32)), -c_mat(c_im.astype(F32))], axis=1).astype(BF16)
    return b_mat(bbr), b_mat(bbi), ar.reshape(nslab, 1, LANES), ai.reshape(nslab, 1, LANES), ccat


def _to_chunks(x, nl):
    nb, t, w = x.shape
    return jnp.swapaxes(x.reshape(nb, t // nl, nl, w), 1, 2)


def _from_chunks(x):
    nb, nl, nr, w = x.shape
    return jnp.swapaxes(x, 1, 2).reshape(nb, nr * nl, w)


def _mixers(lp, zmix, nb, t, s5_re0, s5_im0, lru_h0, lru_buf0, cc_buf0, chained):
    w = lp['gw']
    zmix3 = zmix.reshape(nb, t, zmix.shape[-1])
    tt = math.gcd(t, 512)
    a, b, ycc, lru_buf, cc_buf = _convmix(zmix3, lru_buf0, cc_buf0, lp['lru_conv_w'], lp['lru_conv_b'],
                                          lp['lru_wg'], lp['lru_bg'], lp['lru_sp'], lp['cc_dw_w'], lp['cc_dw_b'],
                                          lp['cc_ln_g'], lp['cc_ln_b'], tt)
    u = zmix3[:, :, :w].astype(BF16)
    ns = s5_re0.shape[-2] * s5_re0.shape[-1]
    if chained:
        nl = _chunk_len(t)
        a4, b4, u4 = _to_chunks(a, nl), _to_chunks(b, nl), _to_chunks(u, nl)
        h0 = lru_h0.reshape(nb, 1, w)
        s0r, s0i = s5_re0.reshape(nb, 1, ns), s5_im0.reshape(nb, 1, ns)
    else:
        to_rows = lambda x: jnp.swapaxes(x, 0, 1)[None]
        a4, b4, u4 = to_rows(a), to_rows(b), to_rows(u)
        h0 = lru_h0.reshape(1, nb, w)
        s0r, s0i = s5_re0.reshape(1, nb, ns), s5_im0.reshape(1, nb, ns)
    h4, lru_h = _lru_scan(a4, b4, h0, chained)
    ys4, s5_re, s5_im = _s5_scan(u4, lp['s5_bre'], lp['s5_bim'], lp['s5_are'], lp['s5_aim'], lp['s5_ccat'],
                                 s0r, s0i, chained)
    if chained:
        h, ys = _from_chunks(h4), _from_chunks(ys4)
    else:
        h, ys = jnp.swapaxes(h4[0], 0, 1), jnp.swapaxes(ys4[0], 0, 1)
    states = (s5_re.reshape(s5_re0.shape), s5_im.reshape(s5_im0.shape), lru_h.reshape(lru_h0.shape),
              lru_buf, cc_buf)
    return ys.reshape(nb * t, w), h.reshape(nb * t, w), ycc.reshape(nb * t, w), states


def _idx3(x, order):
    hi, lo = _split3(x)
    parts = [hi if o == 'h' else lo for o in order]
    pad = jnp.zeros(x.shape[:-1] + (2 * LANES - len(order) * x.shape[-1],), BF16)
    return jnp.concatenate(parts + [pad], axis=-1)


def _attend_prompt(q, qi, k, v, kw, nb, t):
    w = q.shape[-1]
    topk = min(TOPK_MAX, t // 4)
    to_cols = lambda x: jnp.swapaxes(x.reshape(nb, t, -1), 1, 2)
    qt = to_cols(q)
    qih, qil = _split3(qi)
    wi = kw[:, IDX_DIM:IDX_DIM + IDX_HEADS].reshape(nb, t, IDX_HEADS)
    wit = jnp.pad(jnp.swapaxes(wi, 1, 2), ((0, 0), (0, 8 - IDX_HEADS), (0, 0)))
    kidx3 = _idx3(kw[:, :IDX_DIM].reshape(nb, t, IDX_DIM), 'hlh')
    kb = k.reshape(nb, t, w).astype(BF16)
    vt = jnp.swapaxes(v.reshape(nb, t, w), 1, 2).astype(BF16)
    kc = math.gcd(t // 2, 512)
    yt = _dsa_prompt(qt, to_cols(qih), to_cols(qil), wit, kidx3, kb, vt, topk, kc)
    return jnp.swapaxes(yt, 1, 2).reshape(nb * t, w)


def _attend_sample(layer, q, qi, k, v, kw, nb, t, pool_k, pool_v, pool_kidx, page_table):
    w = q.shape[-1]
    n_heads = w // HEAD_DIM
    psz = pool_kidx.shape[2]
    past = page_table.shape[1] * psz
    topk = min(TOPK_MAX, (past + t) // 4)
    pad_q = lambda x: jnp.pad(x, ((0, 0), (0, 0), (0, QROWS - t), (0, 0)))

    qi_h = jnp.swapaxes(qi.reshape(nb, t, IDX_HEADS, IDX_DIM), 1, 2)
    qhi, qlo = _split3(pad_q(qi_h).reshape(nb, IDX_HEADS * QROWS, IDX_DIM))
    wi = kw[:, IDX_DIM:IDX_DIM + IDX_HEADS].reshape(nb, t, IDX_HEADS)
    wi_b = jnp.broadcast_to(pad_q(jnp.swapaxes(wi, 1, 2)[..., None]).reshape(nb, IDX_HEADS * QROWS, 1),
                            (nb, IDX_HEADS * QROWS, psz))
    new_page_t = lambda x: jnp.swapaxes(jnp.pad(x, ((0, 0), (0, psz - t), (0, 0))), 1, 2)
    kidx_new = new_page_t(kw[:, :IDX_DIM].reshape(nb, t, IDX_DIM))
    bias = _dsa_sample_select(layer, page_table, qhi, qlo, wi_b, kidx_new, jnp.swapaxes(pool_kidx, 2, 3),
                              topk, t)

    head_of_lane = jnp.arange(w) // HEAD_DIM
    q_rows = jnp.broadcast_to(pad_q(q.reshape(nb, 1, t, w)), (nb, n_heads, QROWS, w))
    q32 = jnp.where(head_of_lane[None, None, None, :] == jnp.arange(n_heads)[None, :, None, None],
                    q_rows, jnp.zeros_like(q_rows)).reshape(nb, n_heads * QROWS, w)
    k_new = new_page_t(k.reshape(nb, t, w))
    v_new = new_page_t(v.reshape(nb, t, w))
    pool_t = lambda x: jnp.transpose(x, (0, 1, 3, 4, 2)).reshape(x.shape[:2] + (w, psz))
    y = _dsa_sample_attend(layer, page_table, q32, k_new, v_new, bias, pool_t(pool_k), pool_t(pool_v))
    return y[:, :t, :].reshape(nb * t, w)


def _prep_layer(l, w_in, s5_a_re, s5_a_im, s5_b_re, s5_b_im, s5_c_re, s5_c_im, s5_d, s5_log_dt, s5_w_glu,
                s5_b_glu, lru_conv_w, lru_conv_b, lru_w_a, lru_b_a, lru_w_x, lru_b_x, lru_lambda, cc_dw_w,
                cc_dw_b, cc_ln_g, cc_ln_b, w_out, ln1_g, ln1_b, ln2_g, ln2_b):
    gw = s5_d.shape[1]
    d_in = w_in.shape[2]
    mix_w = 5 * gw
    pad_to = mix_w + 4 * gw + LANES
    assert d_in == mix_w + 4 * gw + IDX_DIM + IDX_HEADS and gw == IDX_HEADS * IDX_DIM
    row = lambda x: x[l].reshape(1, -1).astype(F32)
    nh, hd, _ = lru_w_a.shape[1:]
    eye = jnp.eye(nh, dtype=F32)
    bd = lambda wt: jnp.einsum('hij,hk->hikj', wt[l].astype(F32), eye).reshape(nh * hd, nh * hd)
    s5_bre, s5_bim, s5_are, s5_aim, s5_ccat = _s5_params(s5_a_re[l], s5_a_im[l], s5_b_re[l], s5_b_im[l],
                                                         s5_c_re[l], s5_c_im[l], s5_log_dt[l])
    return dict(
        gw=gw, mix_w=mix_w,
        w_in=jnp.pad(w_in[l], ((0, 0), (0, pad_to - d_in))).astype(BF16),
        s5_bre=s5_bre, s5_bim=s5_bim, s5_are=s5_are, s5_aim=s5_aim, s5_ccat=s5_ccat,
        s5_d=row(s5_d), s5_w_glu=s5_w_glu[l].astype(BF16), s5_b_glu=row(s5_b_glu),
        lru_conv_w=lru_conv_w[l].astype(F32), lru_conv_b=row(lru_conv_b),
        lru_wg=jnp.concatenate([bd(lru_w_a), bd(lru_w_x)], axis=1).astype(BF16),
        lru_bg=jnp.concatenate([row(lru_b_a), row(lru_b_x)], axis=1),
        lru_sp=jax.nn.softplus(-row(lru_lambda)),
        cc_dw_w=cc_dw_w[l].astype(F32), cc_dw_b=row(cc_dw_b), cc_ln_g=row(cc_ln_g), cc_ln_b=row(cc_ln_b),
        w_out=w_out[l].astype(BF16), ln1_g=row(ln1_g), ln1_b=row(ln1_b), ln2_g=row(ln2_g), ln2_b=row(ln2_b),
    )


def kernel(x_prompt, x_sample, cache_k, cache_v, cache_kidx, state_s5_re, state_s5_im, state_lru_h, state_lru_conv, state_cc_conv, page_table, w_in, s5_a_re, s5_a_im, s5_b_re, s5_b_im, s5_c_re, s5_c_im, s5_d, s5_log_dt, s5_w_glu, s5_b_glu, lru_conv_w, lru_conv_b, lru_w_a, lru_b_a, lru_w_x, lru_b_x, lru_lambda, cc_dw_w, cc_dw_b, cc_ln_g, cc_ln_b, w_out, ln1_g, ln1_b, ln2_g, ln2_b, ffn_w_gate, ffn_w_up, ffn_w_down, moe_w_router, moe_b_router, moe_w_gate, moe_w_up, moe_w_down):
    bp, tp, dm = x_prompt.shape
    bs, ts, _ = x_sample.shape
    depth = w_in.shape[0]
    past = page_table.shape[1] * cache_k.shape[2]
    alpha = (2.0 * depth) ** 0.25
    n_heads = cache_k.shape[3]
    n_exp = moe_w_router.shape[-1]
    gw = s5_d.shape[1]

    cos_p, sin_p = _rope_tables(jnp.arange(tp, dtype=jnp.int32), gw // HEAD_DIM)
    cos_s, sin_s = _rope_tables(past + jnp.arange(ts, dtype=jnp.int32), gw // HEAD_DIM)
    cos_s, sin_s = jnp.tile(cos_s, (bs, 1)), jnp.tile(sin_s, (bs, 1))

    tm_p = _row_tile(tp, 512)
    xp = x_prompt.reshape(bp * tp, dm)
    xs = x_sample.reshape(bs * ts, dm)
    zeros_p = lambda *shape: jnp.zeros((bp,) + shape, x_prompt.dtype)
    states_p, states_s = [], []

    for l in range(depth):
        lp = _prep_layer(l, w_in, s5_a_re, s5_a_im, s5_b_re, s5_b_im, s5_c_re, s5_c_im, s5_d, s5_log_dt,
                         s5_w_glu, s5_b_glu, lru_conv_w, lru_conv_b, lru_w_a, lru_b_a, lru_w_x, lru_b_x,
                         lru_lambda, cc_dw_w, cc_dw_b, cc_ln_g, cc_ln_b, w_out, ln1_g, ln1_b, ln2_g, ln2_b)
        mix_w = lp['mix_w']
        j = l // 2
        if l % 2 == 0:
            tf = math.gcd(ffn_w_gate.shape[2], FFN_COL_TILE)
            cw = (ffn_w_gate[j].astype(BF16), ffn_w_up[j].astype(BF16), ffn_w_down[j].astype(BF16))
        else:
            wr = jnp.pad(moe_w_router[j].astype(F32), ((0, 0), (0, LANES - n_exp)))
            wrh, wrl = _split3(wr)
            br = jnp.pad(moe_b_router[j].astype(F32), (0, LANES - n_exp), constant_values=NEG_BIG).reshape(1, LANES)
            tf = math.gcd(moe_w_gate.shape[3], MOE_COL_TILE)
            cw = (wrh, wrl, br, moe_w_gate[j].astype(BF16), moe_w_up[j].astype(BF16), moe_w_down[j].astype(BF16))

        def channel(h):
            if l % 2 == 0:
                return _ffn(h, *cw, lp['ln2_g'], lp['ln2_b'], _row_tile(h.shape[0], FFN_ROW_TILE), tf, alpha)
            return _moe(h, *cw, lp['ln2_g'], lp['ln2_b'], _row_tile(h.shape[0], MOE_ROW_TILE), tf, alpha)

        zmix, q, qi, k, v, kw = _in_proj(xp, lp['w_in'], cos_p, sin_p, tm_p, mix_w, gw)
        ys, h, ycc, st = _mixers(lp, zmix, bp, tp,
                                 zeros_p(*state_s5_re.shape[2:]), zeros_p(*state_s5_im.shape[2:]),
                                 zeros_p(*state_lru_h.shape[2:]), zeros_p(*state_lru_conv.shape[2:]),
                                 zeros_p(*state_cc_conv.shape[2:]), chained=True)
        yatt = _attend_prompt(q, qi, k, v, kw, bp, tp)
        h1 = _mixout(xp, zmix, ys, h, ycc, yatt, lp['s5_d'], lp['s5_w_glu'], lp['s5_b_glu'], lp['w_out'],
                     lp['ln1_g'], lp['ln1_b'], tm_p, alpha)
        xp = channel(h1)
        states_p.append((k.reshape(bp, tp, n_heads, HEAD_DIM), v.reshape(bp, tp, n_heads, HEAD_DIM),
                         kw[:, :IDX_DIM].reshape(bp, tp, IDX_DIM)) + st)

        ms = bs * ts
        zmix, q, qi, k, v, kw = _in_proj(xs, lp['w_in'], cos_s, sin_s, ms, mix_w, gw)
        ys, h, ycc, st = _mixers(lp, zmix, bs, ts, state_s5_re[l], state_s5_im[l], state_lru_h[l],
                                 state_lru_conv[l], state_cc_conv[l], chained=False)
        yatt = _attend_sample(l, q, qi, k, v, kw, bs, ts, cache_k, cache_v, cache_kidx, page_table)
        h1 = _mixout(xs, zmix, ys, h, ycc, yatt, lp['s5_d'], lp['s5_w_glu'], lp['s5_b_glu'], lp['w_out'],
                     lp['ln1_g'], lp['ln1_b'], ms, alpha)
        xs = channel(h1)
        states_s.append((k.reshape(bs, ts, n_heads, HEAD_DIM), v.reshape(bs, ts, n_heads, HEAD_DIM),
                         kw[:, :IDX_DIM].reshape(bs, ts, IDX_DIM)) + st)

    new_p = [jnp.stack(col) for col in zip(*states_p)]
    new_s = [jnp.stack(col) for col in zip(*states_s)]
    out = [xp.reshape(bp, tp, dm), xs.reshape(bs, ts, dm)]
    for a, b in zip(new_p, new_s):
        out += [a, b]
    return tuple(out)
```

```python
import functools
import math

import jax
import jax.numpy as jnp
from jax import lax
from jax.experimental import pallas as pl
from jax.experimental.pallas import tpu as pltpu

F32 = jnp.float32
BF16 = jnp.bfloat16

S5_CH = 16
LRU_C = 8.0
HEAD_DIM = 64
IDX_DIM = 64
IDX_HEADS = 4
TOPK_MAX = 256
Q_BLOCK = 128
ROPE_THETA = 10000.0
MOE_TOP_K = 2
LN_EPS = 1e-5
ATT_SCALE = HEAD_DIM ** -0.5
LOG2_E = math.log2(math.e)
IDX_SCALE = (IDX_DIM * IDX_HEADS) ** -0.5

LANES = 128
NEG_BIG = -1e30
VMEM_LIMIT = 56 * 1024 * 1024


def _params(*sem):
    return pltpu.CompilerParams(dimension_semantics=sem, vmem_limit_bytes=VMEM_LIMIT)


def _dot(a, b):
    return jnp.dot(a, b, preferred_element_type=F32)


def _dot_t(a, b):
    return lax.dot_general(a, b, (((1,), (1,)), ((), ())), preferred_element_type=F32)


def _split3(x):
    hi = x.astype(BF16)
    lo = (x - hi.astype(F32)).astype(BF16)
    return hi, lo


def _layer_norm(x, g, b):
    mu = jnp.mean(x, axis=-1, keepdims=True)
    xc = x - mu
    var = jnp.mean(xc * xc, axis=-1, keepdims=True)
    return xc * lax.rsqrt(var + LN_EPS) * g + b


def _rope(z, cos, sin):
    w = z.shape[-1]
    half = HEAD_DIM // 2
    lane = lax.broadcasted_iota(jnp.int32, z.shape, 1)
    first = (lane & (HEAD_DIM - 1)) < half
    partner = jnp.where(first, pltpu.roll(z, w - half, axis=1), pltpu.roll(z, half, axis=1))
    return z * cos + partner * sin


def _in_proj_kernel(x_ref, w_ref, cos_ref, sin_ref, zmix_ref, q_ref, qi_ref, k_ref, v_ref, kw_ref,
                    *, mix_w, gw):
    xb = x_ref[...].astype(BF16)
    cos = cos_ref[...]
    sin = sin_ref[...]

    def proj(lo, width):
        return _dot(xb, w_ref[:, lo:lo + width])

    zmix_ref[...] = proj(0, mix_w)
    o = mix_w
    q_ref[...] = (_rope(proj(o, gw), cos, sin) * (ATT_SCALE * LOG2_E)).astype(BF16)
    k_ref[...] = _rope(proj(o + gw, gw), cos, sin)
    v_ref[...] = proj(o + 2 * gw, gw)
    qi_ref[...] = _rope(proj(o + 3 * gw, gw), cos, sin)
    kw = proj(o + 4 * gw, LANES)
    is_key = lax.broadcasted_iota(jnp.int32, kw.shape, 1) < IDX_DIM
    kw_ref[...] = _rope(kw, jnp.where(is_key, cos[:, :LANES], 1.0), jnp.where(is_key, sin[:, :LANES], 0.0))


def _in_proj(x2, w_pad, cos_t, sin_t, tm, mix_w, gw):
    m, d = x2.shape
    tab_blocks = cos_t.shape[0] // tm
    row = lambda i: (i, 0)
    tab = lambda i: (i % tab_blocks, 0)
    outs = [
        jax.ShapeDtypeStruct((m, mix_w), F32),
        jax.ShapeDtypeStruct((m, gw), BF16),
        jax.ShapeDtypeStruct((m, gw), F32),
        jax.ShapeDtypeStruct((m, gw), F32),
        jax.ShapeDtypeStruct((m, gw), F32),
        jax.ShapeDtypeStruct((m, LANES), F32),
    ]
    return pl.pallas_call(
        functools.partial(_in_proj_kernel, mix_w=mix_w, gw=gw),
        grid=(m // tm,),
        in_specs=[pl.BlockSpec((tm, d), row),
                  pl.BlockSpec(w_pad.shape, lambda i: (0, 0)),
                  pl.BlockSpec((tm, gw), tab),
                  pl.BlockSpec((tm, gw), tab)],
        out_specs=[pl.BlockSpec((tm, s.shape[1]), row) for s in outs],
        out_shape=outs,
        compiler_params=_params("parallel"),
        name="in_proj",
    )(x2, w_pad, cos_t, sin_t)


LRU_HALO = 8
CC_HALO = 32


def _convmix_kernel(xl_ref, ca_ref, cg_ref, lbuf_ref, cbuf_ref, lw_ref, lb_ref, wg_ref, bg_ref, sp_ref,
                    cw_ref, cb_ref, lng_ref, lnb_ref,
                    a_ref, b_ref, ycc_ref, lbuf_o, cbuf_o, xl_s, xc_s, *, tt):
    i = pl.program_id(1)
    kl = lw_ref.shape[0]
    kc = cw_ref.shape[0]
    w = xl_ref.shape[-1]
    l0 = LRU_HALO - (kl - 1)
    c0 = CC_HALO - (kc - 1)

    @pl.when(i == 0)
    def _():
        xl_s[l0:LRU_HALO, :] = lbuf_ref[...]
        xc_s[c0:CC_HALO, :] = cbuf_ref[...]

    xl_s[LRU_HALO:LRU_HALO + tt, :] = xl_ref[...]
    xc_s[CC_HALO:CC_HALO + tt, :] = ca_ref[...] * jax.nn.sigmoid(cg_ref[...])

    conv = xl_s[l0:l0 + tt, :] * lw_ref[0:1, :]
    for j in range(1, kl):
        conv = conv + xl_s[l0 + j:l0 + j + tt, :] * lw_ref[j:j + 1, :]
    conv = conv + lb_ref[...]
    gates = _dot(conv.astype(BF16), wg_ref[...]) + bg_ref[...]
    r = jax.nn.sigmoid(gates[:, :w])
    ig = jax.nn.sigmoid(gates[:, w:])
    log_a = (-LRU_C) * r * sp_ref[...]
    a = jnp.exp(log_a)
    a_ref[...] = a
    b_ref[...] = jnp.sqrt(-jnp.tanh(log_a) * (a * a + 1.0)) * (ig * conv)

    c = xc_s[c0:c0 + tt, :] * cw_ref[0:1, :]
    for j in range(1, kc):
        c = c + xc_s[c0 + j:c0 + j + tt, :] * cw_ref[j:j + 1, :]
    c = _layer_norm(c + cb_ref[...], lng_ref[...], lnb_ref[...])
    ycc_ref[...] = c * jax.nn.sigmoid(c)

    new_l = xl_s[l0 + tt:LRU_HALO + tt, :]
    new_c = xc_s[c0 + tt:CC_HALO + tt, :]
    xl_s[l0:LRU_HALO, :] = new_l
    xc_s[c0:CC_HALO, :] = new_c
    lbuf_o[...] = new_l
    cbuf_o[...] = new_c


def _convmix(zmix3, lbuf, cbuf, lw, lb, wg, bg, sp, cw, cb, lng, lnb, tt):
    nb, t, _ = zmix3.shape
    w = lw.shape[1]
    col = lambda c: pl.BlockSpec((None, tt, w), lambda b, i, c=c: (b, i, c))
    full2 = lambda a: pl.BlockSpec(a.shape, lambda b, i: (0, 0))
    per_b = lambda a: pl.BlockSpec((None,) + a.shape[1:], lambda b, i: (b, 0, 0))
    seq_out = jax.ShapeDtypeStruct((nb, t, w), F32)
    return pl.pallas_call(
        functools.partial(_convmix_kernel, tt=tt),
        grid=(nb, t // tt),
        in_specs=[col(1), col(3), col(4), per_b(lbuf), per_b(cbuf), full2(lw), full2(lb), full2(wg),
                  full2(bg), full2(sp), full2(cw), full2(cb), full2(lng), full2(lnb)],
        out_specs=[pl.BlockSpec((None, tt, w), lambda b, i: (b, i, 0))] * 3 + [per_b(lbuf), per_b(cbuf)],
        out_shape=[seq_out, seq_out, seq_out,
                   jax.ShapeDtypeStruct(lbuf.shape, F32), jax.ShapeDtypeStruct(cbuf.shape, F32)],
        scratch_shapes=[pltpu.VMEM((LRU_HALO + tt, w), F32), pltpu.VMEM((CC_HALO + tt, w), F32)],
        compiler_params=_params("parallel", "arbitrary"),
        name="convmix",
    )(zmix3, zmix3, zmix3, lbuf, cbuf, lw, lb, wg, bg, sp, cw, cb, lng, lnb)


def _lru_scan_kernel(a_ref, b_ref, h0_ref, h_ref, hl_ref, e_s, p_s, hi_s, *, chained):
    nl, nr, _ = a_ref.shape

    def run(init):
        def step(t, h):
            h = a_ref[t] * h + b_ref[t]
            h_ref[t] = h
            return h
        return lax.fori_loop(0, nl, step, init)

    if not chained:
        hl_ref[...] = run(h0_ref[...])
        return

    def local(t, c):
        h, p = c
        at = a_ref[t]
        return at * h + b_ref[t], at * p
    zeros = jnp.zeros(e_s.shape, F32)
    e, p = lax.fori_loop(0, nl, local, (zeros, zeros + 1.0))
    e_s[...] = e
    p_s[...] = p
    hi_s[0:1, :] = h0_ref[...]

    def carry(c, _):
        prev = pl.ds(c - 1, 1)
        hi_s[pl.ds(c, 1), :] = p_s[prev, :] * hi_s[prev, :] + e_s[prev, :]
        return 0
    lax.fori_loop(1, nr, carry, 0)
    last = run(hi_s[...])
    hl_ref[...] = last[nr - 1:nr, :]


def _lru_scan(a4, b4, h0, chained):
    nb, nl, nr, w = a4.shape
    r0 = h0.shape[1]
    nw = w // LANES
    blk = pl.BlockSpec((None, nl, nr, LANES), lambda n, j: (n, 0, 0, j))
    st = pl.BlockSpec((None, r0, LANES), lambda n, j: (n, 0, j))
    return pl.pallas_call(
        functools.partial(_lru_scan_kernel, chained=chained),
        grid=(nb, nw),
        in_specs=[blk, blk, st],
        out_specs=[blk, st],
        out_shape=[jax.ShapeDtypeStruct(a4.shape, F32), jax.ShapeDtypeStruct(h0.shape, F32)],
        scratch_shapes=[pltpu.VMEM((nr, LANES), F32)] * 3,
        compiler_params=_params("parallel", "parallel"),
        name="lru_scan",
    )(a4, b4, h0)


def _s5_scan_kernel(u_ref, bre_ref, bim_ref, are_ref, aim_ref, cc_ref, h0r_ref, h0i_ref,
                    ys_ref, hlr_ref, hli_ref, br_s, bi_s, er_s, ei_s, hr_s, hi_s, *, chained, rows_per_dot):
    g = pl.program_id(1)
    nl, nr, _ = u_ref.shape
    n = nl * nr
    ar = are_ref[...]
    ai = aim_ref[...]

    def drive(c, _):
        r0 = pl.multiple_of(c * rows_per_dot, rows_per_dot)
        ub = u_ref[pl.ds(c * (rows_per_dot // nr), rows_per_dot // nr)].reshape(rows_per_dot, u_ref.shape[-1])
        br_s[pl.ds(r0, rows_per_dot), :] = _dot(ub, bre_ref[...])
        bi_s[pl.ds(r0, rows_per_dot), :] = _dot(ub, bim_ref[...])
        return 0
    lax.fori_loop(0, n // rows_per_dot, drive, 0)

    def step_fn(store):
        def step(t, c):
            hr, hi = c
            rows = pl.ds(pl.multiple_of(t * nr, nr), nr)
            nhr = ar * hr - ai * hi + br_s[rows, :]
            nhi = ar * hi + ai * hr + bi_s[rows, :]
            if store:
                br_s[rows, :] = nhr
                bi_s[rows, :] = nhi
            return nhr, nhi
        return step

    if chained:
        zeros = jnp.zeros((nr, LANES), F32)
        er, ei = lax.fori_loop(0, nl, step_fn(False), (zeros, zeros))
        er_s[...] = er
        ei_s[...] = ei

        def power(_, c):
            pr, pi = c
            return ar * pr - ai * pi, ar * pi + ai * pr
        alr, ali = lax.fori_loop(0, nl - 1, power, (ar, ai))
        hr_s[0:1, :] = h0r_ref[...]
        hi_s[0:1, :] = h0i_ref[...]

        def carry(c, _):
            prev = pl.ds(c - 1, 1)
            pr = hr_s[prev, :]
            pi = hi_s[prev, :]
            hr_s[pl.ds(c, 1), :] = alr * pr - ali * pi + er_s[prev, :]
            hi_s[pl.ds(c, 1), :] = alr * pi + ali * pr + ei_s[prev, :]
            return 0
        lax.fori_loop(1, nr, carry, 0)
        init = (hr_s[...], hi_s[...])
    else:
        init = (h0r_ref[...], h0i_ref[...])

    lr, li = lax.fori_loop(0, nl, step_fn(True), init)
    if chained:
        hlr_ref[...] = lr[nr - 1:nr, :]
        hli_ref[...] = li[nr - 1:nr, :]
    else:
        hlr_ref[...] = lr
        hli_ref[...] = li

    def project(c, _):
        r0 = pl.multiple_of(c * rows_per_dot, rows_per_dot)
        rows = pl.ds(r0, rows_per_dot)
        hcat = jnp.concatenate([br_s[rows, :], bi_s[rows, :]], axis=1).astype(BF16)
        y = _dot(hcat, cc_ref[...]).reshape(rows_per_dot // nr, nr, ys_ref.shape[-1])
        tsl = pl.ds(c * (rows_per_dot // nr), rows_per_dot // nr)

        @pl.when(g == 0)
        def _():
            ys_ref[tsl] = y

        @pl.when(g > 0)
        def _():
            ys_ref[tsl] = ys_ref[tsl] + y
        return 0
    lax.fori_loop(0, n // rows_per_dot, project, 0)


def _s5_scan(u4, bre, bim, are, aim, ccat, h0r, h0i, chained):
    nb, nl, nr, w = u4.shape
    ng = bre.shape[0]
    r0 = h0r.shape[1]
    n = nl * nr
    rows_per_dot = math.gcd(n, 1024)
    rows_per_dot = max(rows_per_dot, nr)
    seq = pl.BlockSpec((None, nl, nr, w), lambda b, g: (b, 0, 0, 0))
    per_g = lambda a: pl.BlockSpec((None,) + a.shape[1:], lambda b, g: (g, 0, 0))
    st = pl.BlockSpec((None, r0, LANES), lambda b, g: (b, 0, g))
    small = pltpu.VMEM((nr, LANES), F32)
    return pl.pallas_call(
        functools.partial(_s5_scan_kernel, chained=chained, rows_per_dot=rows_per_dot),
        grid=(nb, ng),
        in_specs=[seq, per_g(bre), per_g(bim), per_g(are), per_g(aim), per_g(ccat), st, st],
        out_specs=[seq, st, st],
        out_shape=[jax.ShapeDtypeStruct(u4.shape, F32), jax.ShapeDtypeStruct(h0r.shape, F32),
                   jax.ShapeDtypeStruct(h0i.shape, F32)],
        scratch_shapes=[pltpu.VMEM((n, LANES), F32), pltpu.VMEM((n, LANES), F32), small, small, small, small],
        compiler_params=_params("parallel", "arbitrary"),
        name="s5_scan",
    )(u4, bre, bim, are, aim, ccat, h0r, h0i)


BISECT_ITERS = 24
PROBE_QUERIES = 12
PROBE_GAP = 3


def _select_threshold(count_ge, max_below, mn, mx, kq, all_sel, probe_queries):
    inf = jnp.full(mn.shape, jnp.inf, F32)
    zero = jnp.zeros(mn.shape, F32)
    state = (mn, inf, mn, jnp.where(all_sel, 1.0, zero), zero)

    def unfinished(st):
        return jnp.min(st[3]) == 0.0

    def bisect(c):
        it, (lo, hi, theta, done, tie) = c
        mid = jnp.where(hi == jnp.inf, mx, 0.5 * lo + 0.5 * hi)
        cnt = count_ge(mid)
        live = done == 0.0
        collapsed = jnp.logical_and(live, jnp.logical_or(mid == lo, mid == hi))
        hit = jnp.logical_and(live, cnt == kq)
        move = jnp.logical_and(live, jnp.logical_not(collapsed))
        theta = jnp.where(hit, mid, jnp.where(collapsed, hi, theta))
        tie = jnp.where(collapsed, 1.0, tie)
        done = jnp.where(jnp.logical_or(hit, collapsed), 1.0, done)
        lo = jnp.where(jnp.logical_and(move, cnt > kq), mid, lo)
        hi = jnp.where(jnp.logical_and(move, cnt < kq), mid, hi)
        return it + 1, (lo, hi, theta, done, tie)

    _, state = lax.while_loop(
        lambda c: jnp.logical_and(c[0] < BISECT_ITERS, jnp.sum(1.0 - c[1][3]) > float(probe_queries)),
        lambda c: bisect(bisect(c)), (jnp.int32(0), state))

    def exact_step(st):
        lo, hi, theta, done, tie = st
        live = done == 0.0
        tau = max_below(hi)
        cnt = count_ge(tau)
        found = jnp.logical_and(live, cnt >= kq)
        clean = jnp.logical_and(found, cnt == kq)
        theta = jnp.where(clean, tau, jnp.where(found, hi, theta))
        tie = jnp.where(jnp.logical_and(found, jnp.logical_not(clean)), 1.0, tie)
        lo = jnp.where(found, tau, lo)
        hi = jnp.where(jnp.logical_and(live, jnp.logical_not(found)), tau, hi)
        done = jnp.where(found, 1.0, done)
        return lo, hi, theta, done, tie

    def settle(st):
        st = exact_step(st)
        _, st = lax.while_loop(lambda c: jnp.logical_and(c[0] < PROBE_GAP, unfinished(c[1])),
                               bisect, (jnp.int32(0), st))
        return st

    lo, hi, theta, done, tie = lax.while_loop(unfinished, settle, state)
    return theta, lo, tie


COUNT_ROWS = 64


def _dsa_prompt_kernel(qt_ref, qih_ref, qil_ref, wit_ref, kidx_ref, k_ref, vt_ref, o_ref,
                       s_scr, lg_a, lg_b, qp_s, qip_s, *, topk, kc):
    i = pl.program_id(1)
    qb = Q_BLOCK
    n_pairs = qp_s.shape[0]
    pw = 2 * HEAD_DIM

    zq = jnp.zeros((HEAD_DIM, qb), BF16)
    for p in range(n_pairs):
        top = jnp.concatenate([qt_ref[p * pw:p * pw + HEAD_DIM, :], zq], axis=1)
        bot = jnp.concatenate([zq, qt_ref[p * pw + HEAD_DIM:(p + 1) * pw, :]], axis=1)
        qp_s[p] = jnp.concatenate([top, bot], axis=0)
        cols = []
        for h in (2 * p, 2 * p + 1):
            hi = qih_ref[h * IDX_DIM:(h + 1) * IDX_DIM, :]
            cols.append(jnp.concatenate([hi, hi, qil_ref[h * IDX_DIM:(h + 1) * IDX_DIM, :], zq], axis=0))
        qip_s[p] = jnp.concatenate(cols, axis=1)
    nch = ((i + 1) * qb + kc - 1) // kc
    qpos = i * qb + lax.broadcasted_iota(jnp.int32, (1, qb), 1)
    kiota = lax.broadcasted_iota(jnp.int32, (kc, qb), 0)

    def chunk_rows(c):
        return pl.ds(pl.multiple_of(c * kc, kc), kc)

    def score_chunk(c, carry):
        mn, mx = carry
        kk = kidx_ref[chunk_rows(c), :]
        s = jnp.zeros((kc, qb), F32)
        for p in range(n_pairs):
            sp = _dot(kk, qip_s[p])
            for hh in range(2):
                h = 2 * p + hh
                s = s + jnp.maximum(sp[:, hh * qb:(hh + 1) * qb], 0.0) * wit_ref[h:h + 1, :]
        s = s * IDX_SCALE
        vis = (c * kc + kiota) <= qpos
        s_scr[chunk_rows(c), :] = jnp.where(vis, s, -jnp.inf)
        mn = jnp.minimum(mn, jnp.min(jnp.where(vis, s, jnp.inf), axis=0, keepdims=True))
        mx = jnp.maximum(mx, jnp.max(jnp.where(vis, s, -jnp.inf), axis=0, keepdims=True))
        return mn, mx
    nch2 = (nch + 1) // 2
    mn, mx = lax.fori_loop(0, 2 * nch2, score_chunk,
                           (jnp.full((1, qb), jnp.inf, F32), jnp.full((1, qb), -jnp.inf, F32)))

    def fold(x, op):
        return op(x.reshape(kc // COUNT_ROWS, COUNT_ROWS, qb), axis=0)

    def count(pred_fn):
        def body(c, acc):
            hit = jnp.where(pred_fn(s_scr[chunk_rows(c), :], c * kc + kiota), 1.0, 0.0)
            return acc + fold(hit, jnp.sum)
        acc = lax.fori_loop(0, nch, body, jnp.zeros((COUNT_ROWS, qb), F32))
        return jnp.sum(acc, axis=0, keepdims=True)

    def max_below(x):
        def body(c, acc):
            s = s_scr[chunk_rows(c), :]
            return jnp.maximum(acc, fold(jnp.where(s < x, s, -jnp.inf), jnp.max))
        acc = lax.fori_loop(0, nch, body, jnp.full((COUNT_ROWS, qb), -jnp.inf, F32))
        return jnp.max(acc, axis=0, keepdims=True)

    n_vis = (qpos + 1).astype(F32)
    kq = jnp.minimum(n_vis, float(topk))
    theta, lo, tie = _select_threshold(lambda x: count(lambda s, kpos: s >= x), max_below,
                                       mn, mx, kq, n_vis <= float(topk), PROBE_QUERIES)

    @pl.when(jnp.max(tie) > 0.0)
    def _():
        need = kq - count(lambda s, kpos: s >= theta)
        tied = lambda s: jnp.logical_and(tie > 0.0, s == lo)
        tri = jnp.where(lax.broadcasted_iota(jnp.int32, (kc, kc), 0) >= lax.broadcasted_iota(jnp.int32, (kc, kc), 1),
                        1.0, 0.0).astype(BF16)

        def promote(c, seen):
            s = s_scr[chunk_rows(c), :]
            is_tied = tied(s)
            rank = seen + _dot(tri, jnp.where(is_tied, 1.0, 0.0).astype(BF16))
            keep = jnp.logical_and(is_tied, rank <= need)
            s_scr[chunk_rows(c), :] = jnp.where(keep, theta, s)
            return rank[kc - 1:kc, :]
        lax.fori_loop(0, nch, promote, jnp.zeros((1, qb), F32))

    n_heads = 2 * n_pairs
    kc2 = 2 * kc

    def chunk2_rows(c):
        return pl.ds(pl.multiple_of(jnp.minimum(c, nch2 - 1) * kc2, kc2), kc2)

    def put_logits(dst, c):
        for p in range(n_pairs):
            dst[p] = _dot(k_ref[chunk2_rows(c), p * pw:(p + 1) * pw], qp_s[p])

    def absorb(src, c, carry):
        ms, ls, accs = carry
        rows2 = chunk2_rows(c)
        sel = s_scr[rows2, :] >= jnp.where(c < nch2, theta, jnp.nan)
        new_m, new_l, new_acc = [], [], []
        for p in range(n_pairs):
            for hh in range(2):
                h = 2 * p + hh
                logit = jnp.where(sel, src[p, :, hh * qb:(hh + 1) * qb], NEG_BIG)
                m_new = jnp.maximum(ms[h], jnp.max(logit, axis=0, keepdims=True))
                alpha = jnp.exp2(ms[h] - m_new)
                pr = jnp.exp2(logit - m_new)
                new_m.append(m_new)
                new_l.append(alpha * ls[h] + jnp.sum(pr, axis=0, keepdims=True))
                pv = _dot(vt_ref[h * HEAD_DIM:(h + 1) * HEAD_DIM, rows2], pr.astype(BF16))
                new_acc.append(alpha * accs[h] + pv)
        return tuple(new_m), tuple(new_l), tuple(new_acc)

    def att_pair(j, carry):
        c = 2 * j
        put_logits(lg_b, c + 1)
        carry = absorb(lg_a, c, carry)
        put_logits(lg_a, c + 2)
        return absorb(lg_b, c + 1, carry)

    init = (tuple(jnp.full((1, qb), NEG_BIG, F32) for _ in range(n_heads)),
            tuple(jnp.zeros((1, qb), F32) for _ in range(n_heads)),
            tuple(jnp.zeros((HEAD_DIM, qb), F32) for _ in range(n_heads)))
    put_logits(lg_a, 0)
    _, ls, accs = lax.fori_loop(0, (nch2 + 1) // 2, att_pair, init)
    for h in range(n_heads):
        o_ref[h * HEAD_DIM:(h + 1) * HEAD_DIM, :] = accs[h] / ls[h]


def _dsa_prompt(qt, qiht, qilt, wit, kidx3, kb, vt, topk, kc):
    nb, w, t = vt.shape
    n_pairs = w // (2 * HEAD_DIM)
    per_b = lambda a: pl.BlockSpec((None,) + a.shape[1:], lambda b, i: (b, 0, 0))
    qcol = lambda a: pl.BlockSpec((None, a.shape[1], Q_BLOCK), lambda b, i: (b, 0, i))
    return pl.pallas_call(
        functools.partial(_dsa_prompt_kernel, topk=topk, kc=kc),
        grid=(nb, t // Q_BLOCK),
        in_specs=[qcol(qt), qcol(qiht), qcol(qilt), qcol(wit), per_b(kidx3), per_b(kb), per_b(vt)],
        out_specs=pl.BlockSpec((None, w, Q_BLOCK), lambda b, i: (b, 0, i)),
        out_shape=jax.ShapeDtypeStruct((nb, w, t), F32),
        scratch_shapes=[pltpu.VMEM((t, Q_BLOCK), F32),
                        pltpu.VMEM((n_pairs, 2 * kc, 2 * Q_BLOCK), F32),
                        pltpu.VMEM((n_pairs, 2 * kc, 2 * Q_BLOCK), F32),
                        pltpu.VMEM((n_pairs, 2 * HEAD_DIM, 2 * Q_BLOCK), BF16),
                        pltpu.VMEM((n_pairs, kidx3.shape[2], 2 * Q_BLOCK), BF16)],
        compiler_params=_params("parallel", "arbitrary"),
        name="dsa_prompt",
    )(qt, qiht, qilt, wit, kidx3, kb, vt)


SAMPLE_PAGES_PER_STEP = 16
QROWS = 8


SELECT_BATCHES_PER_STEP = 4


def _dsa_sample_select_kernel(pt_ref, qhi_ref, qlo_ref, wi_ref, knew_ref, *rest, topk, n_new, gp, gb):
    pages = rest[:gb * gp]
    bias_ref, s_scr = rest[gb * gp], rest[gb * gp + 1]
    j = pl.program_id(1)
    nj = pl.num_programs(1)
    psz = pages[0].shape[1]
    past = nj * gp * psz
    rows = gb * QROWS

    def scores(g, kpage_t):
        khi, klo = _split3(kpage_t)
        qhi = qhi_ref[g]
        s = _dot(qhi, khi) + _dot(qhi, klo) + _dot(qlo_ref[g], khi)
        s = jnp.maximum(s, 0.0) * wi_ref[g]
        tot = s[0:QROWS, :]
        for h in range(1, IDX_HEADS):
            tot = tot + s[h * QROWS:(h + 1) * QROWS, :]
        return tot * IDX_SCALE

    for g in range(gb):
        for p in range(gp):
            off = pl.multiple_of((j * gp + p) * psz, psz)
            s_scr[g * QROWS:(g + 1) * QROWS, pl.ds(off, psz)] = scores(g, pages[g * gp + p][...])

    @pl.when(j == nj - 1)
    def _():
        qrow = lax.broadcasted_iota(jnp.int32, (QROWS, psz), 0)
        kcol = lax.broadcasted_iota(jnp.int32, (QROWS, psz), 1)
        vis_new = jnp.logical_and(kcol <= qrow, kcol < n_new)
        for g in range(gb):
            s_scr[g * QROWS:(g + 1) * QROWS, past:past + psz] = jnp.where(vis_new, scores(g, knew_ref[g]), -jnp.inf)
        s = s_scr[...]
        nk = s.shape[1]
        kpos = lax.broadcasted_iota(jnp.int32, s.shape, 1)
        qrow1 = lax.broadcasted_iota(jnp.int32, (rows, 1), 0) & (QROWS - 1)
        n_vis = (past + jnp.minimum(qrow1 + 1, n_new)).astype(F32)
        kq = jnp.minimum(n_vis, float(topk))
        fin = s > -jnp.inf
        mn = jnp.min(jnp.where(fin, s, jnp.inf), axis=1, keepdims=True)
        mx = jnp.max(s, axis=1, keepdims=True)
        count = lambda pred: jnp.sum(jnp.where(pred, 1.0, 0.0), axis=1, keepdims=True)
        theta, lo, tie = _select_threshold(
            lambda x: count(s >= x),
            lambda x: jnp.max(jnp.where(s < x, s, -jnp.inf), axis=1, keepdims=True),
            mn, mx, kq, jnp.logical_or(n_vis <= float(topk), qrow1 >= n_new), 0)

        sel = s >= theta
        bias_ref[...] = jnp.where(sel, 0.0, NEG_BIG).reshape(bias_ref.shape)

        @pl.when(jnp.max(tie) > 0.0)
        def _():
            need = kq - count(sel)
            tied = jnp.logical_and(tie > 0.0, s == lo)
            nbits = max(1, (nk - 1).bit_length())

            def jstep(b, jl):
                cand = jl + (jnp.int32(1) << (nbits - 1 - b))
                c_lt = count(jnp.logical_and(tied, kpos < cand))
                return jnp.where(c_lt < need, cand, jl)
            jlast = lax.fori_loop(0, nbits, jstep, jnp.zeros((rows, 1), jnp.int32))
            keep = jnp.logical_or(sel, jnp.logical_and(tied, kpos <= jlast))
            bias_ref[...] = jnp.where(keep, 0.0, NEG_BIG).reshape(bias_ref.shape)


def _dsa_sample_select(layer, page_table, qhi, qlo, wi_b, knew, pool_kidx, topk, n_new):
    nb, n_pages = page_table.shape
    gp = math.gcd(n_pages, SAMPLE_PAGES_PER_STEP)
    gb = math.gcd(nb, SELECT_BATCHES_PER_STEP)
    di, psz = pool_kidx.shape[2], pool_kidx.shape[3]
    nk = (n_pages + 1) * psz
    per_b = lambda a: pl.BlockSpec((gb,) + a.shape[1:], lambda b, j, pt: (b, 0, 0))
    page = lambda g, p: pl.BlockSpec((None, None, di, psz),
                                     lambda b, j, pt: (layer, pt[b * gb + g, j * gp + p], 0, 0))
    grid_spec = pltpu.PrefetchScalarGridSpec(
        num_scalar_prefetch=1,
        grid=(nb // gb, n_pages // gp),
        in_specs=[per_b(qhi), per_b(qlo), per_b(wi_b), per_b(knew)]
        + [page(g, p) for g in range(gb) for p in range(gp)],
        out_specs=pl.BlockSpec((gb, QROWS, nk), lambda b, j, pt: (b, 0, 0)),
        scratch_shapes=[pltpu.VMEM((gb * QROWS, nk), F32)],
    )
    return pl.pallas_call(
        functools.partial(_dsa_sample_select_kernel, topk=topk, n_new=n_new, gp=gp, gb=gb),
        grid_spec=grid_spec,
        out_shape=jax.ShapeDtypeStruct((nb, QROWS, nk), F32),
        compiler_params=_params("parallel", "arbitrary"),
        name="dsa_sample_select",
    )(page_table, qhi, qlo, wi_b, knew, *([pool_kidx] * (gb * gp)))


def _dsa_sample_attend_kernel(pt_ref, q_ref, knew_ref, vnew_ref, *rest, gp):
    bias_refs = rest[:gp]
    kpages = rest[gp:2 * gp]
    vpages = rest[2 * gp:3 * gp]
    bnew_ref = rest[3 * gp]
    o_ref, m_s, l_s, acc_s = rest[3 * gp + 1:]
    j = pl.program_id(1)
    nj = pl.num_programs(1)
    q = q_ref[...]
    n_heads = q.shape[0] // QROWS

    @pl.when(j == 0)
    def _():
        m_s[...] = jnp.full(m_s.shape, NEG_BIG, F32)
        l_s[...] = jnp.zeros(l_s.shape, F32)
        acc_s[...] = jnp.zeros(acc_s.shape, F32)

    def absorb(kts, vts, biases):
        psz = kts[0].shape[1]
        logit = jnp.concatenate(
            [_dot(q, kt[...].astype(BF16)) + jnp.concatenate([b[...]] * n_heads, axis=0)
             for kt, b in zip(kts, biases)], axis=1)
        m = m_s[...]
        m_new = jnp.maximum(m, jnp.max(logit, axis=1, keepdims=True))
        alpha = jnp.exp2(m - m_new)
        p = jnp.exp2(logit - m_new)
        l_s[...] = alpha * l_s[...] + jnp.sum(p, axis=1, keepdims=True)
        pv = _dot_t(p[:, 0:psz].astype(BF16), vts[0][...].astype(BF16))
        for g in range(1, len(vts)):
            pv = pv + _dot_t(p[:, g * psz:(g + 1) * psz].astype(BF16), vts[g][...].astype(BF16))
        acc_s[...] = alpha * acc_s[...] + pv
        m_s[...] = m_new

    absorb(kpages, vpages, bias_refs)

    @pl.when(j == nj - 1)
    def _():
        absorb([knew_ref], [vnew_ref], [bnew_ref])
        out = acc_s[...] / l_s[...]
        lane = lax.broadcasted_iota(jnp.int32, (QROWS, out.shape[1]), 1)
        y = jnp.zeros((QROWS, out.shape[1]), F32)
        for h in range(n_heads):
            in_head = jnp.logical_and(lane >= h * HEAD_DIM, lane < (h + 1) * HEAD_DIM)
            y = y + jnp.where(in_head, out[h * QROWS:(h + 1) * QROWS, :], 0.0)
        o_ref[...] = y


def _dsa_sample_attend(layer, page_table, q32, knew, vnew, bias, pool_k, pool_v):
    nb, n_pages = page_table.shape
    gp = math.gcd(n_pages, 2 * SAMPLE_PAGES_PER_STEP)
    w, psz = pool_k.shape[2], pool_k.shape[3]
    per_b = lambda a: pl.BlockSpec((None,) + a.shape[1:], lambda b, j, pt: (b, 0, 0))
    page = lambda g: pl.BlockSpec((None, None, w, psz), lambda b, j, pt, g=g: (layer, pt[b, j * gp + g], 0, 0))
    bias_pg = lambda g: pl.BlockSpec((None, QROWS, psz), lambda b, j, pt, g=g: (b, 0, j * gp + g))
    bias_new = pl.BlockSpec((None, QROWS, psz), lambda b, j, pt: (b, 0, n_pages))
    grid_spec = pltpu.PrefetchScalarGridSpec(
        num_scalar_prefetch=1,
        grid=(nb, n_pages // gp),
        in_specs=([per_b(q32), per_b(knew), per_b(vnew)] + [bias_pg(g) for g in range(gp)]
                  + [page(g) for g in range(gp)] + [page(g) for g in range(gp)] + [bias_new]),
        out_specs=pl.BlockSpec((None, QROWS, w), lambda b, j, pt: (b, 0, 0)),
        scratch_shapes=[pltpu.VMEM((q32.shape[1], 1), F32), pltpu.VMEM((q32.shape[1], 1), F32),
                        pltpu.VMEM((q32.shape[1], w), F32)],
    )
    return pl.pallas_call(
        functools.partial(_dsa_sample_attend_kernel, gp=gp),
        grid_spec=grid_spec,
        out_shape=jax.ShapeDtypeStruct((nb, QROWS, w), F32),
        compiler_params=_params("parallel", "arbitrary"),
        name="dsa_sample_attend",
    )(page_table, q32, knew, vnew, *([bias] * gp), *([pool_k] * gp), *([pool_v] * gp), bias)


def _mixout_kernel(x_ref, u_ref, g_ref, ys_ref, h_ref, ycc_ref, yatt_ref, d_ref, wglu_ref, bglu_ref,
                   wout_ref, g1_ref, b1_ref, o_ref, *, alpha):
    w = u_ref.shape[-1]
    y = ys_ref[...] + d_ref[...] * u_ref[...]
    z = _dot(jax.nn.gelu(y).astype(BF16), wglu_ref[...]) + bglu_ref[...]
    y_s5 = z[:, :w] * jax.nn.sigmoid(z[:, w:])
    y_lru = h_ref[...] * jax.nn.gelu(g_ref[...])
    acc = _dot(y_s5.astype(BF16), wout_ref[0:w, :])
    acc = acc + _dot(y_lru.astype(BF16), wout_ref[w:2 * w, :])
    acc = acc + _dot(ycc_ref[...].astype(BF16), wout_ref[2 * w:3 * w, :])
    acc = acc + _dot(yatt_ref[...].astype(BF16), wout_ref[3 * w:4 * w, :])
    o_ref[...] = _layer_norm(alpha * x_ref[...] + acc, g1_ref[...], b1_ref[...])


def _mixout(x2, zmix, ys, h, ycc, yatt, d, wglu, bglu, wout, g1, b1, tm, alpha):
    m, dm = x2.shape
    w = ys.shape[1]
    row = lambda a: pl.BlockSpec((tm, a.shape[1]), lambda i: (i, 0))
    full = lambda a: pl.BlockSpec(a.shape, lambda i: (0, 0))
    zcol = lambda c: pl.BlockSpec((tm, w), lambda i, c=c: (i, c))
    return pl.pallas_call(
        functools.partial(_mixout_kernel, alpha=alpha),
        grid=(m // tm,),
        in_specs=[row(x2), zcol(0), zcol(2), row(ys), row(h), row(ycc), row(yatt), full(d), full(wglu),
                  full(bglu), full(wout), full(g1), full(b1)],
        out_specs=pl.BlockSpec((tm, dm), lambda i: (i, 0)),
        out_shape=jax.ShapeDtypeStruct((m, dm), F32),
        compiler_params=_params("parallel"),
        name="mixout",
    )(x2, zmix, zmix, ys, h, ycc, yatt, d, wglu, bglu, wout, g1, b1)


def _ffn_kernel(h_ref, wg_ref, wu_ref, wd_ref, g2_ref, b2_ref, o_ref, acc_s, hb_s, *, alpha):
    f = pl.program_id(1)

    @pl.when(f == 0)
    def _():
        acc_s[...] = jnp.zeros(acc_s.shape, F32)
        hb_s[...] = h_ref[...].astype(BF16)

    hb = hb_s[...]
    a = _dot(hb, wg_ref[...])
    act = (a * jax.nn.sigmoid(a)) * _dot(hb, wu_ref[...])
    acc_s[...] += _dot(act.astype(BF16), wd_ref[...])

    @pl.when(f == pl.num_programs(1) - 1)
    def _():
        o_ref[...] = _layer_norm(alpha * h_ref[...] + acc_s[...], g2_ref[...], b2_ref[...])


def _ffn(h, wg, wu, wd, g2, b2, tm, tf, alpha):
    m, dm = h.shape
    nf = wg.shape[1] // tf
    return pl.pallas_call(
        functools.partial(_ffn_kernel, alpha=alpha),
        grid=(m // tm, nf),
        in_specs=[pl.BlockSpec((tm, dm), lambda i, f: (i, 0)),
                  pl.BlockSpec((dm, tf), lambda i, f: (0, f)),
                  pl.BlockSpec((dm, tf), lambda i, f: (0, f)),
                  pl.BlockSpec((tf, dm), lambda i, f: (f, 0)),
                  pl.BlockSpec(g2.shape, lambda i, f: (0, 0)),
                  pl.BlockSpec(b2.shape, lambda i, f: (0, 0))],
        out_specs=pl.BlockSpec((tm, dm), lambda i, f: (i, 0)),
        out_shape=jax.ShapeDtypeStruct((m, dm), F32),
        scratch_shapes=[pltpu.VMEM((tm, dm), F32), pltpu.VMEM((tm, dm), BF16)],
        compiler_params=_params("parallel", "arbitrary"),
        name="ffn",
    )(h, wg, wu, wd, g2, b2)


MOE_CHUNK_ROWS = 288
MOE_COL_TILE = 1792
MOE_ROW_TILE = 1024
FFN_COL_TILE = 256
FFN_ROW_TILE = 1024


def _moe_kernel(h_ref, wrh_ref, wrl_ref, br_ref, wg_ref, wu_ref, wd_ref, g2_ref, b2_ref, o_ref,
                hb_s, xg_s, y_s, gate_c, slot_c, gate_b, slot_b_s, slot_t_s, *, alpha, n_exp, cr):
    e = pl.program_id(1)
    f = pl.program_id(2)
    nf = pl.num_programs(2)
    tm, dm = h_ref.shape
    first = jnp.logical_and(e == 0, f == 0)
    last = jnp.logical_and(e == pl.num_programs(1) - 1, f == nf - 1)

    @pl.when(first)
    def _():
        hhi, hlo = _split3(h_ref[...])
        hb_s[...] = hhi
        logits = _dot(hhi, wrh_ref[...]) + _dot(hhi, wrl_ref[...]) + _dot(hlo, wrh_ref[...]) + br_ref[...]
        lane = lax.broadcasted_iota(jnp.int32, logits.shape, 1)
        m1 = jnp.max(logits, axis=1, keepdims=True)
        i1 = jnp.min(jnp.where(logits == m1, lane, LANES), axis=1, keepdims=True)
        rest = jnp.where(lane == i1, -jnp.inf, logits)
        m2 = jnp.max(rest, axis=1, keepdims=True)
        i2 = jnp.min(jnp.where(rest == m2, lane, LANES), axis=1, keepdims=True)
        e2 = jnp.exp(m2 - m1)
        den = 1.0 + e2
        gate = jnp.where(lane == i1, 1.0 / den, 0.0) + jnp.where(lane == i2, e2 / den, 0.0)
        routed = jnp.logical_or(lane == i1, lane == i2)
        ind = jnp.where(routed, 1.0, 0.0).astype(BF16)
        tri = jnp.where(lax.broadcasted_iota(jnp.int32, (tm, tm), 0) >= lax.broadcasted_iota(jnp.int32, (tm, tm), 1),
                        1.0, 0.0).astype(BF16)
        slot = jnp.where(routed, _dot(tri, ind) - 1.0, -1.0)
        eye = jnp.where(lax.broadcasted_iota(jnp.int32, (LANES, LANES), 0)
                        == lax.broadcasted_iota(jnp.int32, (LANES, LANES), 1), 1.0, 0.0).astype(BF16)
        ind_t = _dot_t(eye, ind)
        slot_t = jnp.where(ind_t > 0.0, _dot_t(ind_t.astype(BF16), tri) - 1.0, -1.0)
        slot_t_s[...] = slot_t[0:slot_t_s.shape[0], :]
        gate_c[...] = gate
        slot_c[...] = slot
        o_ref[...] = jnp.zeros(o_ref.shape, F32)

    @pl.when(f == 0)
    def _():
        lane = lax.broadcasted_iota(jnp.int32, gate_c.shape, 1)
        pick = lambda x: jnp.sum(jnp.where(lane == e, x, 0.0), axis=1, keepdims=True)
        gate_b[...] = jnp.broadcast_to(pick(gate_c[...]), gate_b.shape)
        slot_b_s[...] = jnp.broadcast_to(pick(slot_c[...]), slot_b_s.shape)

    slot_b = slot_b_s[...]
    n_pass = ((jnp.max(slot_b) + 1.0).astype(jnp.int32) + cr - 1) // cr

    def pass_rows(r):
        return pl.ds(pl.multiple_of(r * cr, 16), cr)

    @pl.when(f == 0)
    def _():
        srow = slot_t_s[pl.ds(e, 1), :]

        def pack(r, _):
            want = (r * cr + lax.broadcasted_iota(jnp.int32, (cr, tm), 0)).astype(F32)
            onehot = jnp.where(srow == want, 1.0, 0.0).astype(BF16)
            xg_s[pass_rows(r), :] = _dot(onehot, hb_s[...]).astype(BF16)
            return 0
        lax.fori_loop(0, n_pass, pack, 0)

    def expert_pass(r, _):
        x = xg_s[pass_rows(r), :]
        a = _dot(x, wg_ref[...])
        act = (a * jax.nn.sigmoid(a)) * _dot(x, wu_ref[...])
        y = _dot(act.astype(BF16), wd_ref[...])

        @pl.when(f == 0)
        def _():
            y_s[pass_rows(r), :] = y

        @pl.when(f > 0)
        def _():
            y_s[pass_rows(r), :] = y_s[pass_rows(r), :] + y
        return 0
    lax.fori_loop(0, n_pass, expert_pass, 0)

    @pl.when(f == nf - 1)
    def _():
        reps = (cr + LANES - 1) // LANES
        slot_w = jnp.concatenate([slot_b] * reps, axis=1)[:, :cr]
        gate_w = jnp.concatenate([gate_b[...]] * (dm // LANES), axis=1)

        def unpack(r, _):
            want = (r * cr + lax.broadcasted_iota(jnp.int32, (tm, cr), 1)).astype(F32)
            onehot = jnp.where(slot_w == want, 1.0, 0.0).astype(BF16)
            o_ref[...] += gate_w * _dot(onehot, y_s[pass_rows(r), :].astype(BF16))
            return 0
        lax.fori_loop(0, n_pass, unpack, 0)

    @pl.when(last)
    def _():
        o_ref[...] = _layer_norm(alpha * h_ref[...] + o_ref[...], g2_ref[...], b2_ref[...])


def _moe(h, wrh, wrl, br, wg, wu, wd, g2, b2, tm, tf, alpha):
    m, dm = h.shape
    n_exp, _, dff = wg.shape
    nf = dff // tf
    cr = min(MOE_CHUNK_ROWS, tm)
    cap = -(-tm // cr) * cr
    c2 = lambda a: pl.BlockSpec(a.shape, lambda i, e, f: (0, 0))
    return pl.pallas_call(
        functools.partial(_moe_kernel, alpha=alpha, n_exp=n_exp, cr=cr),
        grid=(m // tm, n_exp, nf),
        in_specs=[pl.BlockSpec((tm, dm), lambda i, e, f: (i, 0), pipeline_mode=pl.Buffered(1)),
                  c2(wrh), c2(wrl), c2(br),
                  pl.BlockSpec((None, dm, tf), lambda i, e, f: (e, 0, f)),
                  pl.BlockSpec((None, dm, tf), lambda i, e, f: (e, 0, f)),
                  pl.BlockSpec((None, tf, dm), lambda i, e, f: (e, f, 0)),
                  c2(g2), c2(b2)],
        out_specs=pl.BlockSpec((tm, dm), lambda i, e, f: (i, 0)),
        out_shape=jax.ShapeDtypeStruct((m, dm), F32),
        scratch_shapes=[pltpu.VMEM((tm, dm), BF16),
                        pltpu.VMEM((cap, dm), BF16),
                        pltpu.VMEM((cap, dm), F32),
                        pltpu.VMEM((tm, LANES), F32),
                        pltpu.VMEM((tm, LANES), F32),
                        pltpu.VMEM((tm, LANES), F32),
                        pltpu.VMEM((tm, LANES), F32),
                        pltpu.VMEM((max(8, n_exp), tm), F32)],
        compiler_params=_params("parallel", "arbitrary", "arbitrary"),
        name="moe",
    )(h, wrh, wrl, br, wg, wu, wd, g2, b2)


def _row_tile(m, target):
    t = math.gcd(m, target)
    return t if t % 8 == 0 else m


def _chunk_len(t):
    return math.gcd(t, 128)


def _rope_tables(pos, reps):
    half = HEAD_DIM // 2
    inv = ROPE_THETA ** (-(jnp.arange(half, dtype=F32) / half))
    ang = pos.astype(F32)[:, None] * inv[None, :]
    cos, sin = jnp.cos(ang), jnp.sin(ang)
    cos_h = jnp.concatenate([cos, cos], axis=-1)
    sin_h = jnp.concatenate([-sin, sin], axis=-1)
    return jnp.tile(cos_h, (1, reps)), jnp.tile(sin_h, (1, reps))


def _s5_params(a_re, a_im, b_re, b_im, c_re, c_im, log_dt):
    ng, ns = a_re.shape
    lr, li = a_re.astype(F32), a_im.astype(F32)
    dt = jnp.exp(log_dt.astype(F32))[:, None]
    mag = jnp.exp(lr * dt)
    ar, ai = mag * jnp.cos(li * dt), mag * jnp.sin(li * dt)
    den = lr * lr + li * li
    qr = ((ar - 1.0) * lr + ai * li) / den
    qi = (ai * lr - (ar - 1.0) * li) / den
    br, bi = b_re.astype(F32), b_im.astype(F32)
    bbr = qr[..., None] * br - qi[..., None] * bi
    bbi = qr[..., None] * bi + qi[..., None] * br
    eye = jnp.eye(ng, dtype=F32)
    gpl = LANES // ns
    nslab = ng // gpl

    def b_mat(x):
        full = jnp.einsum('gpc,gh->gchp', x, eye).reshape(ng * S5_CH, ng * ns)
        return jnp.transpose(full.reshape(ng * S5_CH, nslab, LANES), (1, 0, 2)).astype(BF16)

    def c_mat(x):
        return jnp.einsum('gcp,gh->gphc', x, eye).reshape(nslab, LANES, ng * S5_CH)

    ccat = jnp.concatenate([c_mat(c_re.astype(F32)), -c_mat(c_im.astype(F32))], axis=1).astype(BF16)
    return b_mat(bbr), b_mat(bbi), ar.reshape(nslab, 1, LANES), ai.reshape(nslab, 1, LANES), ccat


def _to_chunks(x, nl):
    nb, t, w = x.shape
    return jnp.swapaxes(x.reshape(nb, t // nl, nl, w), 1, 2)


def _from_chunks(x):
    nb, nl, nr, w = x.shape
    return jnp.swapaxes(x, 1, 2).reshape(nb, nr * nl, w)


def _mixers(lp, zmix, nb, t, s5_re0, s5_im0, lru_h0, lru_buf0, cc_buf0, chained):
    w = lp['gw']
    zmix3 = zmix.reshape(nb, t, zmix.shape[-1])
    tt = math.gcd(t, 512)
    a, b, ycc, lru_buf, cc_buf = _convmix(zmix3, lru_buf0, cc_buf0, lp['lru_conv_w'], lp['lru_conv_b'],
                                          lp['lru_wg'], lp['lru_bg'], lp['lru_sp'], lp['cc_dw_w'], lp['cc_dw_b'],
                                          lp['cc_ln_g'], lp['cc_ln_b'], tt)
    u = zmix3[:, :, :w].astype(BF16)
    ns = s5_re0.shape[-2] * s5_re0.shape[-1]
    if chained:
        nl = _chunk_len(t)
        a4, b4, u4 = _to_chunks(a, nl), _to_chunks(b, nl), _to_chunks(u, nl)
        h0 = lru_h0.reshape(nb, 1, w)
        s0r, s0i = s5_re0.reshape(nb, 1, ns), s5_im0.reshape(nb, 1, ns)
    else:
        to_rows = lambda x: jnp.swapaxes(x, 0, 1)[None]
        a4, b4, u4 = to_rows(a), to_rows(b), to_rows(u)
        h0 = lru_h0.reshape(1, nb, w)
        s0r, s0i = s5_re0.reshape(1, nb, ns), s5_im0.reshape(1, nb, ns)
    h4, lru_h = _lru_scan(a4, b4, h0, chained)
    ys4, s5_re, s5_im = _s5_scan(u4, lp['s5_bre'], lp['s5_bim'], lp['s5_are'], lp['s5_aim'], lp['s5_ccat'],
                                 s0r, s0i, chained)
    if chained:
        h, ys = _from_chunks(h4), _from_chunks(ys4)
    else:
        h, ys = jnp.swapaxes(h4[0], 0, 1), jnp.swapaxes(ys4[0], 0, 1)
    states = (s5_re.reshape(s5_re0.shape), s5_im.reshape(s5_im0.shape), lru_h.reshape(lru_h0.shape),
              lru_buf, cc_buf)
    return ys.reshape(nb * t, w), h.reshape(nb * t, w), ycc.reshape(nb * t, w), states


def _idx3(x, order):
    hi, lo = _split3(x)
    parts = [hi if o == 'h' else lo for o in order]
    pad = jnp.zeros(x.shape[:-1] + (2 * LANES - len(order) * x.shape[-1],), BF16)
    return jnp.concatenate(parts + [pad], axis=-1)


def _attend_prompt(q, qi, k, v, kw, nb, t):
    w = q.shape[-1]
    topk = min(TOPK_MAX, t // 4)
    to_cols = lambda x: jnp.swapaxes(x.reshape(nb, t, -1), 1, 2)
    qt = to_cols(q)
    qih, qil = _split3(qi)
    wi = kw[:, IDX_DIM:IDX_DIM + IDX_HEADS].reshape(nb, t, IDX_HEADS)
    wit = jnp.pad(jnp.swapaxes(wi, 1, 2), ((0, 0), (0, 8 - IDX_HEADS), (0, 0)))
    kidx3 = _idx3(kw[:, :IDX_DIM].reshape(nb, t, IDX_DIM), 'hlh')
    kb = k.reshape(nb, t, w).astype(BF16)
    vt = jnp.swapaxes(v.reshape(nb, t, w), 1, 2).astype(BF16)
    kc = math.gcd(t // 2, 512)
    yt = _dsa_prompt(qt, to_cols(qih), to_cols(qil), wit, kidx3, kb, vt, topk, kc)
    return jnp.swapaxes(yt, 1, 2).reshape(nb * t, w)


def _attend_sample(layer, q, qi, k, v, kw, nb, t, pool_k, pool_v, pool_kidx, page_table):
    w = q.shape[-1]
    n_heads = w // HEAD_DIM
    psz = pool_kidx.shape[2]
    past = page_table.shape[1] * psz
    topk = min(TOPK_MAX, (past + t) // 4)
    pad_q = lambda x: jnp.pad(x, ((0, 0), (0, 0), (0, QROWS - t), (0, 0)))

    qi_h = jnp.swapaxes(qi.reshape(nb, t, IDX_HEADS, IDX_DIM), 1, 2)
    qhi, qlo = _split3(pad_q(qi_h).reshape(nb, IDX_HEADS * QROWS, IDX_DIM))
    wi = kw[:, IDX_DIM:IDX_DIM + IDX_HEADS].reshape(nb, t, IDX_HEADS)
    wi_b = jnp.broadcast_to(pad_q(jnp.swapaxes(wi, 1, 2)[..., None]).reshape(nb, IDX_HEADS * QROWS, 1),
                            (nb, IDX_HEADS * QROWS, psz))
    new_page_t = lambda x: jnp.swapaxes(jnp.pad(x, ((0, 0), (0, psz - t), (0, 0))), 1, 2)
    kidx_new = new_page_t(kw[:, :IDX_DIM].reshape(nb, t, IDX_DIM))
    bias = _dsa_sample_select(layer, page_table, qhi, qlo, wi_b, kidx_new, jnp.swapaxes(pool_kidx, 2, 3),
                              topk, t)

    head_of_lane = jnp.arange(w) // HEAD_DIM
    q_rows = jnp.broadcast_to(pad_q(q.reshape(nb, 1, t, w)), (nb, n_heads, QROWS, w))
    q32 = jnp.where(head_of_lane[None, None, None, :] == jnp.arange(n_heads)[None, :, None, None],
                    q_rows, jnp.zeros_like(q_rows)).reshape(nb, n_heads * QROWS, w)
    k_new = new_page_t(k.reshape(nb, t, w))
    v_new = new_page_t(v.reshape(nb, t, w))
    pool_t = lambda x: jnp.transpose(x, (0, 1, 3, 4, 2)).reshape(x.shape[:2] + (w, psz))
    y = _dsa_sample_attend(layer, page_table, q32, k_new, v_new, bias, pool_t(pool_k), pool_t(pool_v))
    return y[:, :t, :].reshape(nb * t, w)


def _prep_layer(l, w_in, s5_a_re, s5_a_im, s5_b_re, s5_b_im, s5_c_re, s5_c_im, s5_d, s5_log_dt, s5_w_glu,
                s5_b_glu, lru_conv_w, lru_conv_b, lru_w_a, lru_b_a, lru_w_x, lru_b_x, lru_lambda, cc_dw_w,
                cc_dw_b, cc_ln_g, cc_ln_b, w_out, ln1_g, ln1_b, ln2_g, ln2_b):
    gw = s5_d.shape[1]
    d_in = w_in.shape[2]
    mix_w = 5 * gw
    pad_to = mix_w + 4 * gw + LANES
    assert d_in == mix_w + 4 * gw + IDX_DIM + IDX_HEADS and gw == IDX_HEADS * IDX_DIM
    row = lambda x: x[l].reshape(1, -1).astype(F32)
    nh, hd, _ = lru_w_a.shape[1:]
    eye = jnp.eye(nh, dtype=F32)
    bd = lambda wt: jnp.einsum('hij,hk->hikj', wt[l].astype(F32), eye).reshape(nh * hd, nh * hd)
    s5_bre, s5_bim, s5_are, s5_aim, s5_ccat = _s5_params(s5_a_re[l], s5_a_im[l], s5_b_re[l], s5_b_im[l],
                                                         s5_c_re[l], s5_c_im[l], s5_log_dt[l])
    return dict(
        gw=gw, mix_w=mix_w,
        w_in=jnp.pad(w_in[l], ((0, 0), (0, pad_to - d_in))).astype(BF16),
        s5_bre=s5_bre, s5_bim=s5_bim, s5_are=s5_are, s5_aim=s5_aim, s5_ccat=s5_ccat,
        s5_d=row(s5_d), s5_w_glu=s5_w_glu[l].astype(BF16), s5_b_glu=row(s5_b_glu),
        lru_conv_w=lru_conv_w[l].astype(F32), lru_conv_b=row(lru_conv_b),
        lru_wg=jnp.concatenate([bd(lru_w_a), bd(lru_w_x)], axis=1).astype(BF16),
        lru_bg=jnp.concatenate([row(lru_b_a), row(lru_b_x)], axis=1),
        lru_sp=jax.nn.softplus(-row(lru_lambda)),
        cc_dw_w=cc_dw_w[l].astype(F32), cc_dw_b=row(cc_dw_b), cc_ln_g=row(cc_ln_g), cc_ln_b=row(cc_ln_b),
        w_out=w_out[l].astype(BF16), ln1_g=row(ln1_g), ln1_b=row(ln1_b), ln2_g=row(ln2_g), ln2_b=row(ln2_b),
    )


def kernel(x_prompt, x_sample, cache_k, cache_v, cache_kidx, state_s5_re, state_s5_im, state_lru_h, state_lru_conv, state_cc_conv, page_table, w_in, s5_a_re, s5_a_im, s5_b_re, s5_b_im, s5_c_re, s5_c_im, s5_d, s5_log_dt, s5_w_glu, s5_b_glu, lru_conv_w, lru_conv_b, lru_w_a, lru_b_a, lru_w_x, lru_b_x, lru_lambda, cc_dw_w, cc_dw_b, cc_ln_g, cc_ln_b, w_out, ln1_g, ln1_b, ln2_g, ln2_b, ffn_w_gate, ffn_w_up, ffn_w_down, moe_w_router, moe_b_router, moe_w_gate, moe_w_up, moe_w_down):
    bp, tp, dm = x_prompt.shape
    bs, ts, _ = x_sample.shape
    depth = w_in.shape[0]
    past = page_table.shape[1] * cache_k.shape[2]
    alpha = (2.0 * depth) ** 0.25
    n_heads = cache_k.shape[3]
    n_exp = moe_w_router.shape[-1]
    gw = s5_d.shape[1]

    cos_p, sin_p = _rope_tables(jnp.arange(tp, dtype=jnp.int32), gw // HEAD_DIM)
    cos_s, sin_s = _rope_tables(past + jnp.arange(ts, dtype=jnp.int32), gw // HEAD_DIM)
    cos_s, sin_s = jnp.tile(cos_s, (bs, 1)), jnp.tile(sin_s, (bs, 1))

    tm_p = _row_tile(tp, 512)
    xp = x_prompt.reshape(bp * tp, dm)
    xs = x_sample.reshape(bs * ts, dm)
    zeros_p = lambda *shape: jnp.zeros((bp,) + shape, x_prompt.dtype)
    states_p, states_s = [], []

    for l in range(depth):
        lp = _prep_layer(l, w_in, s5_a_re, s5_a_im, s5_b_re, s5_b_im, s5_c_re, s5_c_im, s5_d, s5_log_dt,
                         s5_w_glu, s5_b_glu, lru_conv_w, lru_conv_b, lru_w_a, lru_b_a, lru_w_x, lru_b_x,
                         lru_lambda, cc_dw_w, cc_dw_b, cc_ln_g, cc_ln_b, w_out, ln1_g, ln1_b, ln2_g, ln2_b)
        mix_w = lp['mix_w']
        j = l // 2
        if l % 2 == 0:
            tf = math.gcd(ffn_w_gate.shape[2], FFN_COL_TILE)
            cw = (ffn_w_gate[j].astype(BF16), ffn_w_up[j].astype(BF16), ffn_w_down[j].astype(BF16))
        else:
            wr = jnp.pad(moe_w_router[j].astype(F32), ((0, 0), (0, LANES - n_exp)))
            wrh, wrl = _split3(wr)
            br = jnp.pad(moe_b_router[j].astype(F32), (0, LANES - n_exp), constant_values=NEG_BIG).reshape(1, LANES)
            tf = math.gcd(moe_w_gate.shape[3], MOE_COL_TILE)
            cw = (wrh, wrl, br, moe_w_gate[j].astype(BF16), moe_w_up[j].astype(BF16), moe_w_down[j].astype(BF16))

        def channel(h):
            if l % 2 == 0:
                return _ffn(h, *cw, lp['ln2_g'], lp['ln2_b'], _row_tile(h.shape[0], FFN_ROW_TILE), tf, alpha)
            return _moe(h, *cw, lp['ln2_g'], lp['ln2_b'], _row_tile(h.shape[0], MOE_ROW_TILE), tf, alpha)

        zmix, q, qi, k, v, kw = _in_proj(xp, lp['w_in'], cos_p, sin_p, tm_p, mix_w, gw)
        ys, h, ycc, st = _mixers(lp, zmix, bp, tp,
                                 zeros_p(*state_s5_re.shape[2:]), zeros_p(*state_s5_im.shape[2:]),
                                 zeros_p(*state_lru_h.shape[2:]), zeros_p(*state_lru_conv.shape[2:]),
                                 zeros_p(*state_cc_conv.shape[2:]), chained=True)
        yatt = _attend_prompt(q, qi, k, v, kw, bp, tp)
        h1 = _mixout(xp, zmix, ys, h, ycc, yatt, lp['s5_d'], lp['s5_w_glu'], lp['s5_b_glu'], lp['w_out'],
                     lp['ln1_g'], lp['ln1_b'], tm_p, alpha)
        xp = channel(h1)
        states_p.append((k.reshape(bp, tp, n_heads, HEAD_DIM), v.reshape(bp, tp, n_heads, HEAD_DIM),
                         kw[:, :IDX_DIM].reshape(bp, tp, IDX_DIM)) + st)

        ms = bs * ts
        zmix, q, qi, k, v, kw = _in_proj(xs, lp['w_in'], cos_s, sin_s, ms, mix_w, gw)
        ys, h, ycc, st = _mixers(lp, zmix, bs, ts, state_s5_re[l], state_s5_im[l], state_lru_h[l],
                                 state_lru_conv[l], state_cc_conv[l], chained=False)
        yatt = _attend_sample(l, q, qi, k, v, kw, bs, ts, cache_k, cache_v, cache_kidx, page_table)
        h1 = _mixout(xs, zmix, ys, h, ycc, yatt, lp['s5_d'], lp['s5_w_glu'], lp['s5_b_glu'], lp['w_out'],
                     lp['ln1_g'], lp['ln1_b'], ms, alpha)
        xs = channel(h1)
        states_s.append((k.reshape(bs, ts, n_heads, HEAD_DIM), v.reshape(bs, ts, n_heads, HEAD_DIM),
                         kw[:, :IDX_DIM].reshape(bs, ts, IDX_DIM)) + st)

    new_p = [jnp.stack(col) for col in zip(*states_p)]
    new_s = [jnp.stack(col) for col in zip(*states_s)]
    out = [xp.reshape(bp, tp, dm), xs.reshape(bs, ts, dm)]
    for a, b in zip(new_p, new_s):
        out += [a, b]
    return tuple(out)
```

```python
import functools
import math

import jax
import jax.numpy as jnp
from jax import lax
from jax.experimental import pallas as pl
from jax.experimental.pallas import tpu as pltpu

F32 = jnp.float32
BF16 = jnp.bfloat16

S5_CH = 16
LRU_C = 8.0
HEAD_DIM = 64
IDX_DIM = 64
IDX_HEADS = 4
TOPK_MAX = 256
Q_BLOCK = 128
ROPE_THETA = 10000.0
MOE_TOP_K = 2
LN_EPS = 1e-5
ATT_SCALE = HEAD_DIM ** -0.5
LOG2_E = math.log2(math.e)
IDX_SCALE = (IDX_DIM * IDX_HEADS) ** -0.5

LANES = 128
NEG_BIG = -1e30
VMEM_LIMIT = 56 * 1024 * 1024


def _params(*sem):
    return pltpu.CompilerParams(dimension_semantics=sem, vmem_limit_bytes=VMEM_LIMIT)


def _dot(a, b):
    return jnp.dot(a, b, preferred_element_type=F32)


def _dot_t(a, b):
    return lax.dot_general(a, b, (((1,), (1,)), ((), ())), preferred_element_type=F32)


def _split3(x):
    hi = x.astype(BF16)
    lo = (x - hi.astype(F32)).astype(BF16)
    return hi, lo


def _layer_norm(x, g, b):
    mu = jnp.mean(x, axis=-1, keepdims=True)
    xc = x - mu
    var = jnp.mean(xc * xc, axis=-1, keepdims=True)
    return xc * lax.rsqrt(var + LN_EPS) * g + b


def _rope(z, cos, sin):
    w = z.shape[-1]
    half = HEAD_DIM // 2
    lane = lax.broadcasted_iota(jnp.int32, z.shape, 1)
    first = (lane & (HEAD_DIM - 1)) < half
    partner = jnp.where(first, pltpu.roll(z, w - half, axis=1), pltpu.roll(z, half, axis=1))
    return z * cos + partner * sin


def _in_proj_kernel(x_ref, w_ref, cos_ref, sin_ref, zmix_ref, q_ref, qi_ref, k_ref, v_ref, kw_ref,
                    *, mix_w, gw):
    xb = x_ref[...].astype(BF16)
    cos = cos_ref[...]
    sin = sin_ref[...]

    def proj(lo, width):
        return _dot(xb, w_ref[:, lo:lo + width])

    zmix_ref[...] = proj(0, mix_w)
    o = mix_w
    q_ref[...] = (_rope(proj(o, gw), cos, sin) * (ATT_SCALE * LOG2_E)).astype(BF16)
    k_ref[...] = _rope(proj(o + gw, gw), cos, sin)
    v_ref[...] = proj(o + 2 * gw, gw)
    qi_ref[...] = _rope(proj(o + 3 * gw, gw), cos, sin)
    kw = proj(o + 4 * gw, LANES)
    is_key = lax.broadcasted_iota(jnp.int32, kw.shape, 1) < IDX_DIM
    kw_ref[...] = _rope(kw, jnp.where(is_key, cos[:, :LANES], 1.0), jnp.where(is_key, sin[:, :LANES], 0.0))


def _in_proj(x2, w_pad, cos_t, sin_t, tm, mix_w, gw):
    m, d = x2.shape
    tab_blocks = cos_t.shape[0] // tm
    row = lambda i: (i, 0)
    tab = lambda i: (i % tab_blocks, 0)
    outs = [
        jax.ShapeDtypeStruct((m, mix_w), F32),
        jax.ShapeDtypeStruct((m, gw), BF16),
        jax.ShapeDtypeStruct((m, gw), F32),
        jax.ShapeDtypeStruct((m, gw), F32),
        jax.ShapeDtypeStruct((m, gw), F32),
        jax.ShapeDtypeStruct((m, LANES), F32),
    ]
    return pl.pallas_call(
        functools.partial(_in_proj_kernel, mix_w=mix_w, gw=gw),
        grid=(m // tm,),
        in_specs=[pl.BlockSpec((tm, d), row),
                  pl.BlockSpec(w_pad.shape, lambda i: (0, 0)),
                  pl.BlockSpec((tm, gw), tab),
                  pl.BlockSpec((tm, gw), tab)],
        out_specs=[pl.BlockSpec((tm, s.shape[1]), row) for s in outs],
        out_shape=outs,
        compiler_params=_params("parallel"),
        name="in_proj",
    )(x2, w_pad, cos_t, sin_t)


LRU_HALO = 8
CC_HALO = 32


SUBLANES = 8


def _tap_sum(x_s, first, taps, tt, w_ref):
    n = x_s.shape[0]
    if n % SUBLANES or tt % SUBLANES:
        acc = x_s[first:first + tt, :] * w_ref[0:1, :]
        for j in range(1, taps):
            acc = acc + x_s[first + j:first + j + tt, :] * w_ref[j:j + 1, :]
        return acc
    full = x_s[...]
    acc = None
    for phase in range(SUBLANES):
        js = [j for j in range(taps) if (first + j) % SUBLANES == phase]
        if not js:
            continue
        z = full if phase == 0 else pltpu.roll(full, n - phase, axis=0)
        for j in js:
            a0 = first + j - phase
            term = z[a0:a0 + tt, :] * w_ref[j:j + 1, :]
            acc = term if acc is None else acc + term
    return acc


def _convmix_kernel(xl_ref, ca_ref, cg_ref, lbuf_ref, cbuf_ref, lw_ref, lb_ref, wg_ref, bg_ref, sp_ref,
                    cw_ref, cb_ref, lng_ref, lnb_ref,
                    a_ref, b_ref, ycc_ref, lbuf_o, cbuf_o, xl_s, xc_s, *, tt):
    i = pl.program_id(1)
    kl = lw_ref.shape[0]
    kc = cw_ref.shape[0]
    w = xl_ref.shape[-1]
    l0 = LRU_HALO - (kl - 1)
    c0 = CC_HALO - (kc - 1)

    @pl.when(i == 0)
    def _():
        xl_s[l0:LRU_HALO, :] = lbuf_ref[...]
        xc_s[c0:CC_HALO, :] = cbuf_ref[...]

    xl_s[LRU_HALO:LRU_HALO + tt, :] = xl_ref[...]
    xc_s[CC_HALO:CC_HALO + tt, :] = ca_ref[...] * jax.nn.sigmoid(cg_ref[...])

    conv = _tap_sum(xl_s, l0, kl, tt, lw_ref) + lb_ref[...]
    gates = _dot(conv.astype(BF16), wg_ref[...]) + bg_ref[...]
    r = jax.nn.sigmoid(gates[:, :w])
    ig = jax.nn.sigmoid(gates[:, w:])
    log_a = (-LRU_C) * r * sp_ref[...]
    a = jnp.exp(log_a)
    a_ref[...] = a
    b_ref[...] = jnp.sqrt(-jnp.tanh(log_a) * (a * a + 1.0)) * (ig * conv)

    c = _tap_sum(xc_s, c0, kc, tt, cw_ref)
    c = _layer_norm(c + cb_ref[...], lng_ref[...], lnb_ref[...])
    ycc_ref[...] = c * jax.nn.sigmoid(c)

    new_l = xl_s[l0 + tt:LRU_HALO + tt, :]
    new_c = xc_s[c0 + tt:CC_HALO + tt, :]
    xl_s[l0:LRU_HALO, :] = new_l
    xc_s[c0:CC_HALO, :] = new_c
    lbuf_o[...] = new_l
    cbuf_o[...] = new_c


def _convmix(zmix3, lbuf, cbuf, lw, lb, wg, bg, sp, cw, cb, lng, lnb, tt):
    nb, t, _ = zmix3.shape
    w = lw.shape[1]
    col = lambda c: pl.BlockSpec((None, tt, w), lambda b, i, c=c: (b, i, c))
    full2 = lambda a: pl.BlockSpec(a.shape, lambda b, i: (0, 0))
    per_b = lambda a: pl.BlockSpec((None,) + a.shape[1:], lambda b, i: (b, 0, 0))
    seq_out = jax.ShapeDtypeStruct((nb, t, w), F32)
    return pl.pallas_call(
        functools.partial(_convmix_kernel, tt=tt),
        grid=(nb, t // tt),
        in_specs=[col(1), col(3), col(4), per_b(lbuf), per_b(cbuf), full2(lw), full2(lb), full2(wg),
                  full2(bg), full2(sp), full2(cw), full2(cb), full2(lng), full2(lnb)],
        out_specs=[pl.BlockSpec((None, tt, w), lambda b, i: (b, i, 0))] * 3 + [per_b(lbuf), per_b(cbuf)],
        out_shape=[seq_out, seq_out, seq_out,
                   jax.ShapeDtypeStruct(lbuf.shape, F32), jax.ShapeDtypeStruct(cbuf.shape, F32)],
        scratch_shapes=[pltpu.VMEM((LRU_HALO + tt, w), F32), pltpu.VMEM((CC_HALO + tt, w), F32)],
        compiler_params=_params("parallel", "arbitrary"),
        name="convmix",
    )(zmix3, zmix3, zmix3, lbuf, cbuf, lw, lb, wg, bg, sp, cw, cb, lng, lnb)


def _lru_scan_kernel(a_ref, b_ref, h0_ref, h_ref, hl_ref, e_s, p_s, hi_s, *, chained):
    nl, nr, _ = a_ref.shape

    def run(init):
        def step(t, h):
            h = a_ref[t] * h + b_ref[t]
            h_ref[t] = h
            return h
        return lax.fori_loop(0, nl, step, init)

    if not chained:
        hl_ref[...] = run(h0_ref[...])
        return

    def local(t, c):
        h, p = c
        at = a_ref[t]
        return at * h + b_ref[t], at * p
    zeros = jnp.zeros(e_s.shape, F32)
    e, p = lax.fori_loop(0, nl, local, (zeros, zeros + 1.0))
    e_s[...] = e
    p_s[...] = p
    hi_s[0:1, :] = h0_ref[...]

    def carry(c, _):
        prev = pl.ds(c - 1, 1)
        hi_s[pl.ds(c, 1), :] = p_s[prev, :] * hi_s[prev, :] + e_s[prev, :]
        return 0
    lax.fori_loop(1, nr, carry, 0)
    last = run(hi_s[...])
    hl_ref[...] = last[nr - 1:nr, :]


def _lru_scan(a4, b4, h0, chained):
    nb, nl, nr, w = a4.shape
    r0 = h0.shape[1]
    nw = w // LANES
    blk = pl.BlockSpec((None, nl, nr, LANES), lambda n, j: (n, 0, 0, j))
    st = pl.BlockSpec((None, r0, LANES), lambda n, j: (n, 0, j))
    return pl.pallas_call(
        functools.partial(_lru_scan_kernel, chained=chained),
        grid=(nb, nw),
        in_specs=[blk, blk, st],
        out_specs=[blk, st],
        out_shape=[jax.ShapeDtypeStruct(a4.shape, F32), jax.ShapeDtypeStruct(h0.shape, F32)],
        scratch_shapes=[pltpu.VMEM((nr, LANES), F32)] * 3,
        compiler_params=_params("parallel", "parallel"),
        name="lru_scan",
    )(a4, b4, h0)


def _s5_scan_kernel(u_ref, bre_ref, bim_ref, are_ref, aim_ref, cc_ref, h0r_ref, h0i_ref,
                    ys_ref, hlr_ref, hli_ref, br_s, bi_s, er_s, ei_s, hr_s, hi_s, *, chained, rows_per_dot):
    g = pl.program_id(1)
    nl, nr, _ = u_ref.shape
    n = nl * nr
    ar = are_ref[...]
    ai = aim_ref[...]

    def drive(c, _):
        r0 = pl.multiple_of(c * rows_per_dot, rows_per_dot)
        ub = u_ref[pl.ds(c * (rows_per_dot // nr), rows_per_dot // nr)].reshape(rows_per_dot, u_ref.shape[-1])
        br_s[pl.ds(r0, rows_per_dot), :] = _dot(ub, bre_ref[...])
        bi_s[pl.ds(r0, rows_per_dot), :] = _dot(ub, bim_ref[...])
        return 0
    lax.fori_loop(0, n // rows_per_dot, drive, 0)

    def step_fn(store):
        def step(t, c):
            hr, hi = c
            rows = pl.ds(pl.multiple_of(t * nr, nr), nr)
            nhr = ar * hr - ai * hi + br_s[rows, :]
            nhi = ar * hi + ai * hr + bi_s[rows, :]
            if store:
                br_s[rows, :] = nhr
                bi_s[rows, :] = nhi
            return nhr, nhi
        return step

    if chained:
        zeros = jnp.zeros((nr, LANES), F32)
        er, ei = lax.fori_loop(0, nl, step_fn(False), (zeros, zeros))
        er_s[...] = er
        ei_s[...] = ei

        def power(_, c):
            pr, pi = c
            return ar * pr - ai * pi, ar * pi + ai * pr
        alr, ali = lax.fori_loop(0, nl - 1, power, (ar, ai))
        hr_s[0:1, :] = h0r_ref[...]
        hi_s[0:1, :] = h0i_ref[...]

        def carry(c, _):
            prev = pl.ds(c - 1, 1)
            pr = hr_s[prev, :]
            pi = hi_s[prev, :]
            hr_s[pl.ds(c, 1), :] = alr * pr - ali * pi + er_s[prev, :]
            hi_s[pl.ds(c, 1), :] = alr * pi + ali * pr + ei_s[prev, :]
            return 0
        lax.fori_loop(1, nr, carry, 0)
        init = (hr_s[...], hi_s[...])
    else:
        init = (h0r_ref[...], h0i_ref[...])

    lr, li = lax.fori_loop(0, nl, step_fn(True), init)
    if chained:
        hlr_ref[...] = lr[nr - 1:nr, :]
        hli_ref[...] = li[nr - 1:nr, :]
    else:
        hlr_ref[...] = lr
        hli_ref[...] = li

    def project(c, _):
        r0 = pl.multiple_of(c * rows_per_dot, rows_per_dot)
        rows = pl.ds(r0, rows_per_dot)
        hcat = jnp.concatenate([br_s[rows, :], bi_s[rows, :]], axis=1).astype(BF16)
        y = _dot(hcat, cc_ref[...]).reshape(rows_per_dot // nr, nr, ys_ref.shape[-1])
        tsl = pl.ds(c * (rows_per_dot // nr), rows_per_dot // nr)

        @pl.when(g == 0)
        def _():
            ys_ref[tsl] = y

        @pl.when(g > 0)
        def _():
            ys_ref[tsl] = ys_ref[tsl] + y
        return 0
    lax.fori_loop(0, n // rows_per_dot, project, 0)


def _s5_scan(u4, bre, bim, are, aim, ccat, h0r, h0i, chained):
    nb, nl, nr, w = u4.shape
    ng = bre.shape[0]
    r0 = h0r.shape[1]
    n = nl * nr
    rows_per_dot = math.gcd(n, 1024)
    rows_per_dot = max(rows_per_dot, nr)
    seq = pl.BlockSpec((None, nl, nr, w), lambda b, g: (b, 0, 0, 0))
    per_g = lambda a: pl.BlockSpec((None,) + a.shape[1:], lambda b, g: (g, 0, 0))
    st = pl.BlockSpec((None, r0, LANES), lambda b, g: (b, 0, g))
    small = pltpu.VMEM((nr, LANES), F32)
    return pl.pallas_call(
        functools.partial(_s5_scan_kernel, chained=chained, rows_per_dot=rows_per_dot),
        grid=(nb, ng),
        in_specs=[seq, per_g(bre), per_g(bim), per_g(are), per_g(aim), per_g(ccat), st, st],
        out_specs=[seq, st, st],
        out_shape=[jax.ShapeDtypeStruct(u4.shape, F32), jax.ShapeDtypeStruct(h0r.shape, F32),
                   jax.ShapeDtypeStruct(h0i.shape, F32)],
        scratch_shapes=[pltpu.VMEM((n, LANES), F32), pltpu.VMEM((n, LANES), F32), small, small, small, small],
        compiler_params=_params("parallel", "arbitrary"),
        name="s5_scan",
    )(u4, bre, bim, are, aim, ccat, h0r, h0i)


BISECT_ITERS = 24
PROBE_QUERIES = 12
PROBE_GAP = 3


def _select_threshold(count_ge, max_below, mn, mx, kq, all_sel, probe_queries):
    inf = jnp.full(mn.shape, jnp.inf, F32)
    zero = jnp.zeros(mn.shape, F32)
    state = (mn, inf, mn, jnp.where(all_sel, 1.0, zero), zero)

    def unfinished(st):
        return jnp.min(st[3]) == 0.0

    def bisect(c):
        it, (lo, hi, theta, done, tie) = c
        mid = jnp.where(hi == jnp.inf, mx, 0.5 * lo + 0.5 * hi)
        cnt = count_ge(mid)
        live = done == 0.0
        collapsed = jnp.logical_and(live, jnp.logical_or(mid == lo, mid == hi))
        hit = jnp.logical_and(live, cnt == kq)
        move = jnp.logical_and(live, jnp.logical_not(collapsed))
        theta = jnp.where(hit, mid, jnp.where(collapsed, hi, theta))
        tie = jnp.where(collapsed, 1.0, tie)
        done = jnp.where(jnp.logical_or(hit, collapsed), 1.0, done)
        lo = jnp.where(jnp.logical_and(move, cnt > kq), mid, lo)
        hi = jnp.where(jnp.logical_and(move, cnt < kq), mid, hi)
        return it + 1, (lo, hi, theta, done, tie)

    _, state = lax.while_loop(
        lambda c: jnp.logical_and(c[0] < BISECT_ITERS, jnp.sum(1.0 - c[1][3]) > float(probe_queries)),
        lambda c: bisect(bisect(c)), (jnp.int32(0), state))

    def exact_step(st):
        lo, hi, theta, done, tie = st
        live = done == 0.0
        tau = max_below(hi)
        cnt = count_ge(tau)
        found = jnp.logical_and(live, cnt >= kq)
        clean = jnp.logical_and(found, cnt == kq)
        theta = jnp.where(clean, tau, jnp.where(found, hi, theta))
        tie = jnp.where(jnp.logical_and(found, jnp.logical_not(clean)), 1.0, tie)
        lo = jnp.where(found, tau, lo)
        hi = jnp.where(jnp.logical_and(live, jnp.logical_not(found)), tau, hi)
        done = jnp.where(found, 1.0, done)
        return lo, hi, theta, done, tie

    def settle(st):
        st = exact_step(st)
        _, st = lax.while_loop(lambda c: jnp.logical_and(c[0] < PROBE_GAP, unfinished(c[1])),
                               bisect, (jnp.int32(0), st))
        return st

    lo, hi, theta, done, tie = lax.while_loop(unfinished, settle, state)
    return theta, lo, tie


COUNT_ROWS = 64


def _dsa_prompt_kernel(qt_ref, qih_ref, qil_ref, wit_ref, kidx_ref, k_ref, vt_ref, o_ref,
                       s_scr, lg_a, lg_b, qp_s, qip_s, *, topk, kc):
    i = pl.program_id(1)
    qb = Q_BLOCK
    n_pairs = qp_s.shape[0]
    pw = 2 * HEAD_DIM

    zq = jnp.zeros((HEAD_DIM, qb), BF16)
    for p in range(n_pairs):
        top = jnp.concatenate([qt_ref[p * pw:p * pw + HEAD_DIM, :], zq], axis=1)
        bot = jnp.concatenate([zq, qt_ref[p * pw + HEAD_DIM:(p + 1) * pw, :]], axis=1)
        qp_s[p] = jnp.concatenate([top, bot], axis=0)
        cols = []
        for h in (2 * p, 2 * p + 1):
            hi = qih_ref[h * IDX_DIM:(h + 1) * IDX_DIM, :]
            cols.append(jnp.concatenate([hi, hi, qil_ref[h * IDX_DIM:(h + 1) * IDX_DIM, :], zq], axis=0))
        qip_s[p] = jnp.concatenate(cols, axis=1)
    nch = ((i + 1) * qb + kc - 1) // kc
    qpos = i * qb + lax.broadcasted_iota(jnp.int32, (1, qb), 1)
    kiota = lax.broadcasted_iota(jnp.int32, (kc, qb), 0)

    def chunk_rows(c):
        return pl.ds(pl.multiple_of(c * kc, kc), kc)

    def score_chunk(c, carry):
        mn, mx = carry
        kk = kidx_ref[chunk_rows(c), :]
        s = jnp.zeros((kc, qb), F32)
        for p in range(n_pairs):
            sp = _dot(kk, qip_s[p])
            for hh in range(2):
                h = 2 * p + hh
                s = s + jnp.maximum(sp[:, hh * qb:(hh + 1) * qb], 0.0) * wit_ref[h:h + 1, :]
        s = s * IDX_SCALE
        vis = (c * kc + kiota) <= qpos
        s_scr[chunk_rows(c), :] = jnp.where(vis, s, -jnp.inf)
        mn = jnp.minimum(mn, jnp.min(jnp.where(vis, s, jnp.inf), axis=0, keepdims=True))
        mx = jnp.maximum(mx, jnp.max(jnp.where(vis, s, -jnp.inf), axis=0, keepdims=True))
        return mn, mx
    nch2 = (nch + 1) // 2
    mn, mx = lax.fori_loop(0, 2 * nch2, score_chunk,
                           (jnp.full((1, qb), jnp.inf, F32), jnp.full((1, qb), -jnp.inf, F32)))

    def fold(x, op):
        return op(x.reshape(kc // COUNT_ROWS, COUNT_ROWS, qb), axis=0)

    def count(pred_fn):
        def body(c, acc):
            hit = jnp.where(pred_fn(s_scr[chunk_rows(c), :], c * kc + kiota), 1.0, 0.0)
            return acc + fold(hit, jnp.sum)
        acc = lax.fori_loop(0, nch, body, jnp.zeros((COUNT_ROWS, qb), F32))
        return jnp.sum(acc, axis=0, keepdims=True)

    def max_below(x):
        def body(c, acc):
            s = s_scr[chunk_rows(c), :]
            return jnp.maximum(acc, fold(jnp.where(s < x, s, -jnp.inf), jnp.max))
        acc = lax.fori_loop(0, nch, body, jnp.full((COUNT_ROWS, qb), -jnp.inf, F32))
        return jnp.max(acc, axis=0, keepdims=True)

    n_vis = (qpos + 1).astype(F32)
    kq = jnp.minimum(n_vis, float(topk))
    theta, lo, tie = _select_threshold(lambda x: count(lambda s, kpos: s >= x), max_below,
                                       mn, mx, kq, n_vis <= float(topk), PROBE_QUERIES)

    @pl.when(jnp.max(tie) > 0.0)
    def _():
        need = kq - count(lambda s, kpos: s >= theta)
        tied = lambda s: jnp.logical_and(tie > 0.0, s == lo)
        tri = jnp.where(lax.broadcasted_iota(jnp.int32, (kc, kc), 0) >= lax.broadcasted_iota(jnp.int32, (kc, kc), 1),
                        1.0, 0.0).astype(BF16)

        def promote(c, seen):
            s = s_scr[chunk_rows(c), :]
            is_tied = tied(s)
            rank = seen + _dot(tri, jnp.where(is_tied, 1.0, 0.0).astype(BF16))
            keep = jnp.logical_and(is_tied, rank <= need)
            s_scr[chunk_rows(c), :] = jnp.where(keep, theta, s)
            return rank[kc - 1:kc, :]
        lax.fori_loop(0, nch, promote, jnp.zeros((1, qb), F32))

    n_heads = 2 * n_pairs
    kc2 = 2 * kc

    def chunk2_rows(c):
        return pl.ds(pl.multiple_of(jnp.minimum(c, nch2 - 1) * kc2, kc2), kc2)

    def put_logits(dst, c):
        for p in range(n_pairs):
            dst[p] = _dot(k_ref[chunk2_rows(c), p * pw:(p + 1) * pw], qp_s[p])

    def absorb(src, c, carry):
        ms, ls, accs = carry
        rows2 = chunk2_rows(c)
        sel = s_scr[rows2, :] >= jnp.where(c < nch2, theta, jnp.nan)
        new_m, new_l, new_acc = [], [], []
        for p in range(n_pairs):
            for hh in range(2):
                h = 2 * p + hh
                logit = jnp.where(sel, src[p, :, hh * qb:(hh + 1) * qb], NEG_BIG)
                m_new = jnp.maximum(ms[h], jnp.max(logit, axis=0, keepdims=True))
                alpha = jnp.exp2(ms[h] - m_new)
                pr = jnp.exp2(logit - m_new)
                new_m.append(m_new)
                new_l.append(alpha * ls[h] + jnp.sum(pr, axis=0, keepdims=True))
                pv = _dot(vt_ref[h * HEAD_DIM:(h + 1) * HEAD_DIM, rows2], pr.astype(BF16))
                new_acc.append(alpha * accs[h] + pv)
        return tuple(new_m), tuple(new_l), tuple(new_acc)

    def att_pair(j, carry):
        c = 2 * j
        put_logits(lg_b, c + 1)
        carry = absorb(lg_a, c, carry)
        put_logits(lg_a, c + 2)
        return absorb(lg_b, c + 1, carry)

    init = (tuple(jnp.full((1, qb), NEG_BIG, F32) for _ in range(n_heads)),
            tuple(jnp.zeros((1, qb), F32) for _ in range(n_heads)),
            tuple(jnp.zeros((HEAD_DIM, qb), F32) for _ in range(n_heads)))
    put_logits(lg_a, 0)
    _, ls, accs = lax.fori_loop(0, (nch2 + 1) // 2, att_pair, init)
    for h in range(n_heads):
        o_ref[h * HEAD_DIM:(h + 1) * HEAD_DIM, :] = accs[h] / ls[h]


def _dsa_prompt(qt, qiht, qilt, wit, kidx3, kb, vt, topk, kc):
    nb, w, t = vt.shape
    n_pairs = w // (2 * HEAD_DIM)
    per_b = lambda a: pl.BlockSpec((None,) + a.shape[1:], lambda b, i: (b, 0, 0))
    qcol = lambda a: pl.BlockSpec((None, a.shape[1], Q_BLOCK), lambda b, i: (b, 0, i))
    return pl.pallas_call(
        functools.partial(_dsa_prompt_kernel, topk=topk, kc=kc),
        grid=(nb, t // Q_BLOCK),
        in_specs=[qcol(qt), qcol(qiht), qcol(qilt), qcol(wit), per_b(kidx3), per_b(kb), per_b(vt)],
        out_specs=pl.BlockSpec((None, w, Q_BLOCK), lambda b, i: (b, 0, i)),
        out_shape=jax.ShapeDtypeStruct((nb, w, t), F32),
        scratch_shapes=[pltpu.VMEM((t, Q_BLOCK), F32),
                        pltpu.VMEM((n_pairs, 2 * kc, 2 * Q_BLOCK), F32),
                        pltpu.VMEM((n_pairs, 2 * kc, 2 * Q_BLOCK), F32),
                        pltpu.VMEM((n_pairs, 2 * HEAD_DIM, 2 * Q_BLOCK), BF16),
                        pltpu.VMEM((n_pairs, kidx3.shape[2], 2 * Q_BLOCK), BF16)],
        compiler_params=_params("parallel", "arbitrary"),
        name="dsa_prompt",
    )(qt, qiht, qilt, wit, kidx3, kb, vt)


SAMPLE_PAGES_PER_STEP = 16
QROWS = 8


SELECT_BATCHES_PER_STEP = 8


def _dsa_sample_select_kernel(pt_ref, qhi_ref, qlo_ref, wi_ref, knew_ref, *rest, topk, n_new, gp, gb):
    pages = rest[:gb * gp]
    bias_ref, s_scr = rest[gb * gp], rest[gb * gp + 1]
    j = pl.program_id(1)
    nj = pl.num_programs(1)
    psz = pages[0].shape[1]
    past = nj * gp * psz
    rows = gb * QROWS

    def scores(g, kpage_t):
        khi, klo = _split3(kpage_t)
        qhi = qhi_ref[g]
        s = _dot(qhi, khi) + _dot(qhi, klo) + _dot(qlo_ref[g], khi)
        s = jnp.maximum(s, 0.0) * wi_ref[g]
        tot = s[0:QROWS, :]
        for h in range(1, IDX_HEADS):
            tot = tot + s[h * QROWS:(h + 1) * QROWS, :]
        return tot * IDX_SCALE

    for g in range(gb):
        for p in range(gp):
            off = pl.multiple_of((j * gp + p) * psz, psz)
            s_scr[g * QROWS:(g + 1) * QROWS, pl.ds(off, psz)] = scores(g, pages[g * gp + p][...])

    @pl.when(j == nj - 1)
    def _():
        qrow = lax.broadcasted_iota(jnp.int32, (QROWS, psz), 0)
        kcol = lax.broadcasted_iota(jnp.int32, (QROWS, psz), 1)
        vis_new = jnp.logical_and(kcol <= qrow, kcol < n_new)
        for g in range(gb):
            s_scr[g * QROWS:(g + 1) * QROWS, past:past + psz] = jnp.where(vis_new, scores(g, knew_ref[g]), -jnp.inf)
        s = s_scr[...]
        nk = s.shape[1]
        kpos = lax.broadcasted_iota(jnp.int32, s.shape, 1)
        qrow1 = lax.broadcasted_iota(jnp.int32, (rows, 1), 0) & (QROWS - 1)
        n_vis = (past + jnp.minimum(qrow1 + 1, n_new)).astype(F32)
        kq = jnp.minimum(n_vis, float(topk))
        fin = s > -jnp.inf
        mn = jnp.min(jnp.where(fin, s, jnp.inf), axis=1, keepdims=True)
        mx = jnp.max(s, axis=1, keepdims=True)
        count = lambda pred: jnp.sum(jnp.where(pred, 1.0, 0.0), axis=1, keepdims=True)
        theta, lo, tie = _select_threshold(
            lambda x: count(s >= x),
            lambda x: jnp.max(jnp.where(s < x, s, -jnp.inf), axis=1, keepdims=True),
            mn, mx, kq, jnp.logical_or(n_vis <= float(topk), qrow1 >= n_new), 0)

        sel = s >= theta
        bias_ref[...] = jnp.where(sel, 0.0, NEG_BIG).reshape(bias_ref.shape)

        @pl.when(jnp.max(tie) > 0.0)
        def _():
            need = kq - count(sel)
            tied = jnp.logical_and(tie > 0.0, s == lo)
            nbits = max(1, (nk - 1).bit_length())

            def jstep(b, jl):
                cand = jl + (jnp.int32(1) << (nbits - 1 - b))
                c_lt = count(jnp.logical_and(tied, kpos < cand))
                return jnp.where(c_lt < need, cand, jl)
            jlast = lax.fori_loop(0, nbits, jstep, jnp.zeros((rows, 1), jnp.int32))
            keep = jnp.logical_or(sel, jnp.logical_and(tied, kpos <= jlast))
            bias_ref[...] = jnp.where(keep, 0.0, NEG_BIG).reshape(bias_ref.shape)


def _dsa_sample_select(layer, page_table, qhi, qlo, wi_b, knew, pool_kidx, topk, n_new):
    nb, n_pages = page_table.shape
    gp = math.gcd(n_pages, SAMPLE_PAGES_PER_STEP)
    gb = math.gcd(nb, SELECT_BATCHES_PER_STEP)
    di, psz = pool_kidx.shape[2], pool_kidx.shape[3]
    nk = (n_pages + 1) * psz
    per_b = lambda a: pl.BlockSpec((gb,) + a.shape[1:], lambda b, j, pt: (b, 0, 0))
    page = lambda g, p: pl.BlockSpec((None, None, di, psz),
                                     lambda b, j, pt: (layer, pt[b * gb + g, j * gp + p], 0, 0))
    grid_spec = pltpu.PrefetchScalarGridSpec(
        num_scalar_prefetch=1,
        grid=(nb // gb, n_pages // gp),
        in_specs=[per_b(qhi), per_b(qlo), per_b(wi_b), per_b(knew)]
        + [page(g, p) for g in range(gb) for p in range(gp)],
        out_specs=pl.BlockSpec((gb, QROWS, nk), lambda b, j, pt: (b, 0, 0)),
        scratch_shapes=[pltpu.VMEM((gb * QROWS, nk), F32)],
    )
    return pl.pallas_call(
        functools.partial(_dsa_sample_select_kernel, topk=topk, n_new=n_new, gp=gp, gb=gb),
        grid_spec=grid_spec,
        out_shape=jax.ShapeDtypeStruct((nb, QROWS, nk), F32),
        compiler_params=_params("parallel", "arbitrary"),
        name="dsa_sample_select",
    )(page_table, qhi, qlo, wi_b, knew, *([pool_kidx] * (gb * gp)))


def _dsa_sample_attend_kernel(pt_ref, q_ref, knew_ref, vnew_ref, *rest, gp):
    bias_refs = rest[:gp]
    kpages = rest[gp:2 * gp]
    vpages = rest[2 * gp:3 * gp]
    bnew_ref = rest[3 * gp]
    o_ref, m_s, l_s, acc_s = rest[3 * gp + 1:]
    j = pl.program_id(1)
    nj = pl.num_programs(1)
    q = q_ref[...]
    n_heads = q.shape[0] // QROWS

    @pl.when(j == 0)
    def _():
        m_s[...] = jnp.full(m_s.shape, NEG_BIG, F32)
        l_s[...] = jnp.zeros(l_s.shape, F32)
        acc_s[...] = jnp.zeros(acc_s.shape, F32)

    def absorb(kts, vts, biases):
        psz = kts[0].shape[1]
        logit = jnp.concatenate(
            [_dot(q, kt[...].astype(BF16)) + jnp.concatenate([b[...]] * n_heads, axis=0)
             for kt, b in zip(kts, biases)], axis=1)
        m = m_s[...]
        m_new = jnp.maximum(m, jnp.max(logit, axis=1, keepdims=True))
        alpha = jnp.exp2(m - m_new)
        p = jnp.exp2(logit - m_new)
        l_s[...] = alpha * l_s[...] + jnp.sum(p, axis=1, keepdims=True)
        pv = _dot_t(p[:, 0:psz].astype(BF16), vts[0][...].astype(BF16))
        for g in range(1, len(vts)):
            pv = pv + _dot_t(p[:, g * psz:(g + 1) * psz].astype(BF16), vts[g][...].astype(BF16))
        acc_s[...] = alpha * acc_s[...] + pv
        m_s[...] = m_new

    absorb(kpages, vpages, bias_refs)

    @pl.when(j == nj - 1)
    def _():
        absorb([knew_ref], [vnew_ref], [bnew_ref])
        out = acc_s[...] / l_s[...]
        lane = lax.broadcasted_iota(jnp.int32, (QROWS, out.shape[1]), 1)
        y = jnp.zeros((QROWS, out.shape[1]), F32)
        for h in range(n_heads):
            in_head = jnp.logical_and(lane >= h * HEAD_DIM, lane < (h + 1) * HEAD_DIM)
            y = y + jnp.where(in_head, out[h * QROWS:(h + 1) * QROWS, :], 0.0)
        o_ref[...] = y


def _dsa_sample_attend(layer, page_table, q32, knew, vnew, bias, pool_k, pool_v):
    nb, n_pages = page_table.shape
    gp = math.gcd(n_pages, 2 * SAMPLE_PAGES_PER_STEP)
    w, psz = pool_k.shape[2], pool_k.shape[3]
    per_b = lambda a: pl.BlockSpec((None,) + a.shape[1:], lambda b, j, pt: (b, 0, 0))
    page = lambda g: pl.BlockSpec((None, None, w, psz), lambda b, j, pt, g=g: (layer, pt[b, j * gp + g], 0, 0))
    bias_pg = lambda g: pl.BlockSpec((None, QROWS, psz), lambda b, j, pt, g=g: (b, 0, j * gp + g))
    bias_new = pl.BlockSpec((None, QROWS, psz), lambda b, j, pt: (b, 0, n_pages))
    grid_spec = pltpu.PrefetchScalarGridSpec(
        num_scalar_prefetch=1,
        grid=(nb, n_pages // gp),
        in_specs=([per_b(q32), per_b(knew), per_b(vnew)] + [bias_pg(g) for g in range(gp)]
                  + [page(g) for g in range(gp)] + [page(g) for g in range(gp)] + [bias_new]),
        out_specs=pl.BlockSpec((None, QROWS, w), lambda b, j, pt: (b, 0, 0)),
        scratch_shapes=[pltpu.VMEM((q32.shape[1], 1), F32), pltpu.VMEM((q32.shape[1], 1), F32),
                        pltpu.VMEM((q32.shape[1], w), F32)],
    )
    return pl.pallas_call(
        functools.partial(_dsa_sample_attend_kernel, gp=gp),
        grid_spec=grid_spec,
        out_shape=jax.ShapeDtypeStruct((nb, QROWS, w), F32),
        compiler_params=_params("parallel", "arbitrary"),
        name="dsa_sample_attend",
    )(page_table, q32, knew, vnew, *([bias] * gp), *([pool_k] * gp), *([pool_v] * gp), bias)


def _mixout_kernel(x_ref, u_ref, g_ref, ys_ref, h_ref, ycc_ref, yatt_ref, d_ref, wglu_ref, bglu_ref,
                   wout_ref, g1_ref, b1_ref, o_ref, *, alpha):
    w = u_ref.shape[-1]
    y = ys_ref[...] + d_ref[...] * u_ref[...]
    z = _dot(jax.nn.gelu(y).astype(BF16), wglu_ref[...]) + bglu_ref[...]
    y_s5 = z[:, :w] * jax.nn.sigmoid(z[:, w:])
    y_lru = h_ref[...] * jax.nn.gelu(g_ref[...])
    acc = _dot(y_s5.astype(BF16), wout_ref[0:w, :])
    acc = acc + _dot(y_lru.astype(BF16), wout_ref[w:2 * w, :])
    acc = acc + _dot(ycc_ref[...].astype(BF16), wout_ref[2 * w:3 * w, :])
    acc = acc + _dot(yatt_ref[...].astype(BF16), wout_ref[3 * w:4 * w, :])
    o_ref[...] = _layer_norm(alpha * x_ref[...] + acc, g1_ref[...], b1_ref[...])


def _mixout(x2, zmix, ys, h, ycc, yatt, d, wglu, bglu, wout, g1, b1, tm, alpha):
    m, dm = x2.shape
    w = ys.shape[1]
    row = lambda a: pl.BlockSpec((tm, a.shape[1]), lambda i: (i, 0))
    full = lambda a: pl.BlockSpec(a.shape, lambda i: (0, 0))
    zcol = lambda c: pl.BlockSpec((tm, w), lambda i, c=c: (i, c))
    return pl.pallas_call(
        functools.partial(_mixout_kernel, alpha=alpha),
        grid=(m // tm,),
        in_specs=[row(x2), zcol(0), zcol(2), row(ys), row(h), row(ycc), row(yatt), full(d), full(wglu),
                  full(bglu), full(wout), full(g1), full(b1)],
        out_specs=pl.BlockSpec((tm, dm), lambda i: (i, 0)),
        out_shape=jax.ShapeDtypeStruct((m, dm), F32),
        compiler_params=_params("parallel"),
        name="mixout",
    )(x2, zmix, zmix, ys, h, ycc, yatt, d, wglu, bglu, wout, g1, b1)


def _ffn_kernel(h_ref, wg_ref, wu_ref, wd_ref, g2_ref, b2_ref, o_ref, acc_s, hb_s, *, alpha):
    f = pl.program_id(1)

    @pl.when(f == 0)
    def _():
        acc_s[...] = jnp.zeros(acc_s.shape, F32)
        hb_s[...] = h_ref[...].astype(BF16)

    hb = hb_s[...]
    a = _dot(hb, wg_ref[...])
    act = (a * jax.nn.sigmoid(a)) * _dot(hb, wu_ref[...])
    acc_s[...] += _dot(act.astype(BF16), wd_ref[...])

    @pl.when(f == pl.num_programs(1) - 1)
    def _():
        o_ref[...] = _layer_norm(alpha * h_ref[...] + acc_s[...], g2_ref[...], b2_ref[...])


def _ffn(h, wg, wu, wd, g2, b2, tm, tf, alpha):
    m, dm = h.shape
    nf = wg.shape[1] // tf
    return pl.pallas_call(
        functools.partial(_ffn_kernel, alpha=alpha),
        grid=(m // tm, nf),
        in_specs=[pl.BlockSpec((tm, dm), lambda i, f: (i, 0)),
                  pl.BlockSpec((dm, tf), lambda i, f: (0, f)),
                  pl.BlockSpec((dm, tf), lambda i, f: (0, f)),
                  pl.BlockSpec((tf, dm), lambda i, f: (f, 0)),
                  pl.BlockSpec(g2.shape, lambda i, f: (0, 0)),
                  pl.BlockSpec(b2.shape, lambda i, f: (0, 0))],
        out_specs=pl.BlockSpec((tm, dm), lambda i, f: (i, 0)),
        out_shape=jax.ShapeDtypeStruct((m, dm), F32),
        scratch_shapes=[pltpu.VMEM((tm, dm), F32), pltpu.VMEM((tm, dm), BF16)],
        compiler_params=_params("parallel", "arbitrary"),
        name="ffn",
    )(h, wg, wu, wd, g2, b2)


MOE_CHUNK_ROWS = 288
MOE_COL_TILE = 1792
MOE_ROW_TILE = 1024
FFN_COL_TILE = 256
FFN_ROW_TILE = 1024


def _moe_kernel(h_ref, wrh_ref, wrl_ref, br_ref, wg_ref, wu_ref, wd_ref, g2_ref, b2_ref, o_ref,
                hb_s, xg_s, y_s, gate_c, slot_c, gate_b, slot_b_s, slot_t_s, *, alpha, n_exp, cr):
    e = pl.program_id(1)
    f = pl.program_id(2)
    nf = pl.num_programs(2)
    tm, dm = h_ref.shape
    first = jnp.logical_and(e == 0, f == 0)
    last = jnp.logical_and(e == pl.num_programs(1) - 1, f == nf - 1)

    @pl.when(first)
    def _():
        hhi, hlo = _split3(h_ref[...])
        hb_s[...] = hhi
        logits = _dot(hhi, wrh_ref[...]) + _dot(hhi, wrl_ref[...]) + _dot(hlo, wrh_ref[...]) + br_ref[...]
        lane = lax.broadcasted_iota(jnp.int32, logits.shape, 1)
        m1 = jnp.max(logits, axis=1, keepdims=True)
        i1 = jnp.min(jnp.where(logits == m1, lane, LANES), axis=1, keepdims=True)
        rest = jnp.where(lane == i1, -jnp.inf, logits)
        m2 = jnp.max(rest, axis=1, keepdims=True)
        i2 = jnp.min(jnp.where(rest == m2, lane, LANES), axis=1, keepdims=True)
        e2 = jnp.exp(m2 - m1)
        den = 1.0 + e2
        gate = jnp.where(lane == i1, 1.0 / den, 0.0) + jnp.where(lane == i2, e2 / den, 0.0)
        routed = jnp.logical_or(lane == i1, lane == i2)
        ind = jnp.where(routed, 1.0, 0.0).astype(BF16)
        tri = jnp.where(lax.broadcasted_iota(jnp.int32, (tm, tm), 0) >= lax.broadcasted_iota(jnp.int32, (tm, tm), 1),
                        1.0, 0.0).astype(BF16)
        slot = jnp.where(routed, _dot(tri, ind) - 1.0, -1.0)
        eye = jnp.where(lax.broadcasted_iota(jnp.int32, (LANES, LANES), 0)
                        == lax.broadcasted_iota(jnp.int32, (LANES, LANES), 1), 1.0, 0.0).astype(BF16)
        ind_t = _dot_t(eye, ind)
        slot_t = jnp.where(ind_t > 0.0, _dot_t(ind_t.astype(BF16), tri) - 1.0, -1.0)
        slot_t_s[...] = slot_t[0:slot_t_s.shape[0], :]
        gate_c[...] = gate
        slot_c[...] = slot
        o_ref[...] = jnp.zeros(o_ref.shape, F32)

    @pl.when(f == 0)
    def _():
        lane = lax.broadcasted_iota(jnp.int32, gate_c.shape, 1)
        pick = lambda x: jnp.sum(jnp.where(lane == e, x, 0.0), axis=1, keepdims=True)
        gate_b[...] = jnp.broadcast_to(pick(gate_c[...]), gate_b.shape)
        slot_b_s[...] = jnp.broadcast_to(pick(slot_c[...]), slot_b_s.shape)

    slot_b = slot_b_s[...]
    n_pass = ((jnp.max(slot_b) + 1.0).astype(jnp.int32) + cr - 1) // cr

    def pass_rows(r):
        return pl.ds(pl.multiple_of(r * cr, 16), cr)

    @pl.when(f == 0)
    def _():
        srow = slot_t_s[pl.ds(e, 1), :]

        def pack(r, _):
            want = (r * cr + lax.broadcasted_iota(jnp.int32, (cr, tm), 0)).astype(F32)
            onehot = jnp.where(srow == want, 1.0, 0.0).astype(BF16)
            xg_s[pass_rows(r), :] = _dot(onehot, hb_s[...]).astype(BF16)
            return 0
        lax.fori_loop(0, n_pass, pack, 0)

    def expert_pass(r, _):
        x = xg_s[pass_rows(r), :]
        a = _dot(x, wg_ref[...])
        act = (a * jax.nn.sigmoid(a)) * _dot(x, wu_ref[...])
        y = _dot(act.astype(BF16), wd_ref[...])

        @pl.when(f == 0)
        def _():
            y_s[pass_rows(r), :] = y

        @pl.when(f > 0)
        def _():
            y_s[pass_rows(r), :] = y_s[pass_rows(r), :] + y
        return 0
    lax.fori_loop(0, n_pass, expert_pass, 0)

    @pl.when(f == nf - 1)
    def _():
        reps = (cr + LANES - 1) // LANES
        slot_w = jnp.concatenate([slot_b] * reps, axis=1)[:, :cr]
        gate_w = jnp.concatenate([gate_b[...]] * (dm // LANES), axis=1)

        def unpack(r, _):
            want = (r * cr + lax.broadcasted_iota(jnp.int32, (tm, cr), 1)).astype(F32)
            onehot = jnp.where(slot_w == want, 1.0, 0.0).astype(BF16)
            o_ref[...] += gate_w * _dot(onehot, y_s[pass_rows(r), :].astype(BF16))
            return 0
        lax.fori_loop(0, n_pass, unpack, 0)

    @pl.when(last)
    def _():
        o_ref[...] = _layer_norm(alpha * h_ref[...] + o_ref[...], g2_ref[...], b2_ref[...])


def _moe(h, wrh, wrl, br, wg, wu, wd, g2, b2, tm, tf, alpha):
    m, dm = h.shape
    n_exp, _, dff = wg.shape
    nf = dff // tf
    cr = min(MOE_CHUNK_ROWS, tm)
    cap = -(-tm // cr) * cr
    c2 = lambda a: pl.BlockSpec(a.shape, lambda i, e, f: (0, 0))
    return pl.pallas_call(
        functools.partial(_moe_kernel, alpha=alpha, n_exp=n_exp, cr=cr),
        grid=(m // tm, n_exp, nf),
        in_specs=[pl.BlockSpec((tm, dm), lambda i, e, f: (i, 0), pipeline_mode=pl.Buffered(1)),
                  c2(wrh), c2(wrl), c2(br),
                  pl.BlockSpec((None, dm, tf), lambda i, e, f: (e, 0, f)),
                  pl.BlockSpec((None, dm, tf), lambda i, e, f: (e, 0, f)),
                  pl.BlockSpec((None, tf, dm), lambda i, e, f: (e, f, 0)),
                  c2(g2), c2(b2)],
        out_specs=pl.BlockSpec((tm, dm), lambda i, e, f: (i, 0)),
        out_shape=jax.ShapeDtypeStruct((m, dm), F32),
        scratch_shapes=[pltpu.VMEM((tm, dm), BF16),
                        pltpu.VMEM((cap, dm), BF16),
                        pltpu.VMEM((cap, dm), F32),
                        pltpu.VMEM((tm, LANES), F32),
                        pltpu.VMEM((tm, LANES), F32),
                        pltpu.VMEM((tm, LANES), F32),
                        pltpu.VMEM((tm, LANES), F32),
                        pltpu.VMEM((max(8, n_exp), tm), F32)],
        compiler_params=_params("parallel", "arbitrary", "arbitrary"),
        name="moe",
    )(h, wrh, wrl, br, wg, wu, wd, g2, b2)


def _row_tile(m, target):
    t = math.gcd(m, target)
    return t if t % 8 == 0 else m


def _chunk_len(t):
    return math.gcd(t, 128)


def _rope_tables(pos, reps):
    half = HEAD_DIM // 2
    inv = ROPE_THETA ** (-(jnp.arange(half, dtype=F32) / half))
    ang = pos.astype(F32)[:, None] * inv[None, :]
    cos, sin = jnp.cos(ang), jnp.sin(ang)
    cos_h = jnp.concatenate([cos, cos], axis=-1)
    sin_h = jnp.concatenate([-sin, sin], axis=-1)
    return jnp.tile(cos_h, (1, reps)), jnp.tile(sin_h, (1, reps))


def _s5_params(a_re, a_im, b_re, b_im, c_re, c_im, log_dt):
    ng, ns = a_re.shape
    lr, li = a_re.astype(F32), a_im.astype(F32)
    dt = jnp.exp(log_dt.astype(F32))[:, None]
    mag = jnp.exp(lr * dt)
    ar, ai = mag * jnp.cos(li * dt), mag * jnp.sin(li * dt)
    den = lr * lr + li * li
    qr = ((ar - 1.0) * lr + ai * li) / den
    qi = (ai * lr - (ar - 1.0) * li) / den
    br, bi = b_re.astype(F32), b_im.astype(F32)
    bbr = qr[..., None] * br - qi[..., None] * bi
    bbi = qr[..., None] * bi + qi[..., None] * br
    eye = jnp.eye(ng, dtype=F32)
    gpl = LANES // ns
    nslab = ng // gpl

    def b_mat(x):
        full = jnp.einsum('gpc,gh->gchp', x, eye).reshape(ng * S5_CH, ng * ns)
        return jnp.transpose(full.reshape(ng * S5_CH, nslab, LANES), (1, 0, 2)).astype(BF16)

    def c_mat(x):
        return jnp.einsum('gcp,gh->gphc', x, eye).reshape(nslab, LANES, ng * S5_CH)

    ccat = jnp.concatenate([c_mat(c_re.astype(F32)), -c_mat(c_im.astype(F32))], axis=1).astype(BF16)
    return b_mat(bbr), b_mat(bbi), ar.reshape(nslab, 1, LANES), ai.reshape(nslab, 1, LANES), ccat


def _to_chunks(x, nl):
    nb, t, w = x.shape
    return jnp.swapaxes(x.reshape(nb, t // nl, nl, w), 1, 2)


def _from_chunks(x):
    nb, nl, nr, w = x.shape
    return jnp.swapaxes(x, 1, 2).reshape(nb, nr * nl, w)


def _mixers(lp, zmix, nb, t, s5_re0, s5_im0, lru_h0, lru_buf0, cc_buf0, chained):
    w = lp['gw']
    zmix3 = zmix.reshape(nb, t, zmix.shape[-1])
    tt = math.gcd(t, 512)
    a, b, ycc, lru_buf, cc_buf = _convmix(zmix3, lru_buf0, cc_buf0, lp['lru_conv_w'], lp['lru_conv_b'],
                                          lp['lru_wg'], lp['lru_bg'], lp['lru_sp'], lp['cc_dw_w'], lp['cc_dw_b'],
                                          lp['cc_ln_g'], lp['cc_ln_b'], tt)
    u = zmix3[:, :, :w].astype(BF16)
    ns = s5_re0.shape[-2] * s5_re0.shape[-1]
    if chained:
        nl = _chunk_len(t)
        a4, b4, u4 = _to_chunks(a, nl), _to_chunks(b, nl), _to_chunks(u, nl)
        h0 = lru_h0.reshape(nb, 1, w)
        s0r, s0i = s5_re0.reshape(nb, 1, ns), s5_im0.reshape(nb, 1, ns)
    else:
        to_rows = lambda x: jnp.swapaxes(x, 0, 1)[None]
        a4, b4, u4 = to_rows(a), to_rows(b), to_rows(u)
        h0 = lru_h0.reshape(1, nb, w)
        s0r, s0i = s5_re0.reshape(1, nb, ns), s5_im0.reshape(1, nb, ns)
    h4, lru_h = _lru_scan(a4, b4, h0, chained)
    ys4, s5_re, s5_im = _s5_scan(u4, lp['s5_bre'], lp['s5_bim'], lp['s5_are'], lp['s5_aim'], lp['s5_ccat'],
                                 s0r, s0i, chained)
    if chained:
        h, ys = _from_chunks(h4), _from_chunks(ys4)
    else:
        h, ys = jnp.swapaxes(h4[0], 0, 1), jnp.swapaxes(ys4[0], 0, 1)
    states = (s5_re.reshape(s5_re0.shape), s5_im.reshape(s5_im0.shape), lru_h.reshape(lru_h0.shape),
              lru_buf, cc_buf)
    return ys.reshape(nb * t, w), h.reshape(nb * t, w), ycc.reshape(nb * t, w), states


def _idx3(x, order):
    hi, lo = _split3(x)
    parts = [hi if o == 'h' else lo for o in order]
    pad = jnp.zeros(x.shape[:-1] + (2 * LANES - len(order) * x.shape[-1],), BF16)
    return jnp.concatenate(parts + [pad], axis=-1)


def _attend_prompt(q, qi, k, v, kw, nb, t):
    w = q.shape[-1]
    topk = min(TOPK_MAX, t // 4)
    to_cols = lambda x: jnp.swapaxes(x.reshape(nb, t, -1), 1, 2)
    qt = to_cols(q)
    qih, qil = _split3(qi)
    wi = kw[:, IDX_DIM:IDX_DIM + IDX_HEADS].reshape(nb, t, IDX_HEADS)
    wit = jnp.pad(jnp.swapaxes(wi, 1, 2), ((0, 0), (0, 8 - IDX_HEADS), (0, 0)))
    kidx3 = _idx3(kw[:, :IDX_DIM].reshape(nb, t, IDX_DIM), 'hlh')
    kb = k.reshape(nb, t, w).astype(BF16)
    vt = jnp.swapaxes(v.reshape(nb, t, w), 1, 2).astype(BF16)
    kc = math.gcd(t // 2, 512)
    yt = _dsa_prompt(qt, to_cols(qih), to_cols(qil), wit, kidx3, kb, vt, topk, kc)
    return jnp.swapaxes(yt, 1, 2).reshape(nb * t, w)


def _attend_sample(layer, q, qi, k, v, kw, nb, t, pool_k, pool_v, pool_kidx, page_table):
    w = q.shape[-1]
    n_heads = w // HEAD_DIM
    psz = pool_kidx.shape[2]
    past = page_table.shape[1] * psz
    topk = min(TOPK_MAX, (past + t) // 4)
    pad_q = lambda x: jnp.pad(x, ((0, 0), (0, 0), (0, QROWS - t), (0, 0)))

    qi_h = jnp.swapaxes(qi.reshape(nb, t, IDX_HEADS, IDX_DIM), 1, 2)
    qhi, qlo = _split3(pad_q(qi_h).reshape(nb, IDX_HEADS * QROWS, IDX_DIM))
    wi = kw[:, IDX_DIM:IDX_DIM + IDX_HEADS].reshape(nb, t, IDX_HEADS)
    wi_b = jnp.broadcast_to(pad_q(jnp.swapaxes(wi, 1, 2)[..., None]).reshape(nb, IDX_HEADS * QROWS, 1),
                            (nb, IDX_HEADS * QROWS, psz))
    new_page_t = lambda x: jnp.swapaxes(jnp.pad(x, ((0, 0), (0, psz - t), (0, 0))), 1, 2)
    kidx_new = new_page_t(kw[:, :IDX_DIM].reshape(nb, t, IDX_DIM))
    bias = _dsa_sample_select(layer, page_table, qhi, qlo, wi_b, kidx_new, jnp.swapaxes(pool_kidx, 2, 3),
                              topk, t)

    head_of_lane = jnp.arange(w) // HEAD_DIM
    q_rows = jnp.broadcast_to(pad_q(q.reshape(nb, 1, t, w)), (nb, n_heads, QROWS, w))
    q32 = jnp.where(head_of_lane[None, None, None, :] == jnp.arange(n_heads)[None, :, None, None],
                    q_rows, jnp.zeros_like(q_rows)).reshape(nb, n_heads * QROWS, w)
    k_new = new_page_t(k.reshape(nb, t, w))
    v_new = new_page_t(v.reshape(nb, t, w))
    pool_t = lambda x: jnp.transpose(x, (0, 1, 3, 4, 2)).reshape(x.shape[:2] + (w, psz))
    y = _dsa_sample_attend(layer, page_table, q32, k_new, v_new, bias, pool_t(pool_k), pool_t(pool_v))
    return y[:, :t, :].reshape(nb * t, w)


def _prep_layer(l, w_in, s5_a_re, s5_a_im, s5_b_re, s5_b_im, s5_c_re, s5_c_im, s5_d, s5_log_dt, s5_w_glu,
                s5_b_glu, lru_conv_w, lru_conv_b, lru_w_a, lru_b_a, lru_w_x, lru_b_x, lru_lambda, cc_dw_w,
                cc_dw_b, cc_ln_g, cc_ln_b, w_out, ln1_g, ln1_b, ln2_g, ln2_b):
    gw = s5_d.shape[1]
    d_in = w_in.shape[2]
    mix_w = 5 * gw
    pad_to = mix_w + 4 * gw + LANES
    assert d_in == mix_w + 4 * gw + IDX_DIM + IDX_HEADS and gw == IDX_HEADS * IDX_DIM
    row = lambda x: x[l].reshape(1, -1).astype(F32)
    nh, hd, _ = lru_w_a.shape[1:]
    eye = jnp.eye(nh, dtype=F32)
    bd = lambda wt: jnp.einsum('hij,hk->hikj', wt[l].astype(F32), eye).reshape(nh * hd, nh * hd)
    s5_bre, s5_bim, s5_are, s5_aim, s5_ccat = _s5_params(s5_a_re[l], s5_a_im[l], s5_b_re[l], s5_b_im[l],
                                                         s5_c_re[l], s5_c_im[l], s5_log_dt[l])
    return dict(
        gw=gw, mix_w=mix_w,
        w_in=jnp.pad(w_in[l], ((0, 0), (0, pad_to - d_in))).astype(BF16),
        s5_bre=s5_bre, s5_bim=s5_bim, s5_are=s5_are, s5_aim=s5_aim, s5_ccat=s5_ccat,
        s5_d=row(s5_d), s5_w_glu=s5_w_glu[l].astype(BF16), s5_b_glu=row(s5_b_glu),
        lru_conv_w=lru_conv_w[l].astype(F32), lru_conv_b=row(lru_conv_b),
        lru_wg=jnp.concatenate([bd(lru_w_a), bd(lru_w_x)], axis=1).astype(BF16),
        lru_bg=jnp.concatenate([row(lru_b_a), row(lru_b_x)], axis=1),
        lru_sp=jax.nn.softplus(-row(lru_lambda)),
        cc_dw_w=cc_dw_w[l].astype(F32), cc_dw_b=row(cc_dw_b), cc_ln_g=row(cc_ln_g), cc_ln_b=row(cc_ln_b),
        w_out=w_out[l].astype(BF16), ln1_g=row(ln1_g), ln1_b=row(ln1_b), ln2_g=row(ln2_g), ln2_b=row(ln2_b),
    )


def kernel(x_prompt, x_sample, cache_k, cache_v, cache_kidx, state_s5_re, state_s5_im, state_lru_h, state_lru_conv, state_cc_conv, page_table, w_in, s5_a_re, s5_a_im, s5_b_re, s5_b_im, s5_c_re, s5_c_im, s5_d, s5_log_dt, s5_w_glu, s5_b_glu, lru_conv_w, lru_conv_b, lru_w_a, lru_b_a, lru_w_x, lru_b_x, lru_lambda, cc_dw_w, cc_dw_b, cc_ln_g, cc_ln_b, w_out, ln1_g, ln1_b, ln2_g, ln2_b, ffn_w_gate, ffn_w_up, ffn_w_down, moe_w_router, moe_b_router, moe_w_gate, moe_w_up, moe_w_down):
    bp, tp, dm = x_prompt.shape
    bs, ts, _ = x_sample.shape
    depth = w_in.shape[0]
    past = page_table.shape[1] * cache_k.shape[2]
    alpha = (2.0 * depth) ** 0.25
    n_heads = cache_k.shape[3]
    n_exp = moe_w_router.shape[-1]
    gw = s5_d.shape[1]

    cos_p, sin_p = _rope_tables(jnp.arange(tp, dtype=jnp.int32), gw // HEAD_DIM)
    cos_s, sin_s = _rope_tables(past + jnp.arange(ts, dtype=jnp.int32), gw // HEAD_DIM)
    cos_s, sin_s = jnp.tile(cos_s, (bs, 1)), jnp.tile(sin_s, (bs, 1))

    tm_p = _row_tile(tp, 512)
    xp = x_prompt.reshape(bp * tp, dm)
    xs = x_sample.reshape(bs * ts, dm)
    zeros_p = lambda *shape: jnp.zeros((bp,) + shape, x_prompt.dtype)
    states_p, states_s = [], []

    for l in range(depth):
        lp = _prep_layer(l, w_in, s5_a_re, s5_a_im, s5_b_re, s5_b_im, s5_c_re, s5_c_im, s5_d, s5_log_dt,
                         s5_w_glu, s5_b_glu, lru_conv_w, lru_conv_b, lru_w_a, lru_b_a, lru_w_x, lru_b_x,
                         lru_lambda, cc_dw_w, cc_dw_b, cc_ln_g, cc_ln_b, w_out, ln1_g, ln1_b, ln2_g, ln2_b)
        mix_w = lp['mix_w']
        j = l // 2
        if l % 2 == 0:
            tf = math.gcd(ffn_w_gate.shape[2], FFN_COL_TILE)
            cw = (ffn_w_gate[j].astype(BF16), ffn_w_up[j].astype(BF16), ffn_w_down[j].astype(BF16))
        else:
            wr = jnp.pad(moe_w_router[j].astype(F32), ((0, 0), (0, LANES - n_exp)))
            wrh, wrl = _split3(wr)
            br = jnp.pad(moe_b_router[j].astype(F32), (0, LANES - n_exp), constant_values=NEG_BIG).reshape(1, LANES)
            tf = math.gcd(moe_w_gate.shape[3], MOE_COL_TILE)
            cw = (wrh, wrl, br, moe_w_gate[j].astype(BF16), moe_w_up[j].astype(BF16), moe_w_down[j].astype(BF16))

        def channel(h):
            if l % 2 == 0:
                return _ffn(h, *cw, lp['ln2_g'], lp['ln2_b'], _row_tile(h.shape[0], FFN_ROW_TILE), tf, alpha)
            return _moe(h, *cw, lp['ln2_g'], lp['ln2_b'], _row_tile(h.shape[0], MOE_ROW_TILE), tf, alpha)

        zmix, q, qi, k, v, kw = _in_proj(xp, lp['w_in'], cos_p, sin_p, tm_p, mix_w, gw)
        ys, h, ycc, st = _mixers(lp, zmix, bp, tp,
                                 zeros_p(*state_s5_re.shape[2:]), zeros_p(*state_s5_im.shape[2:]),
                                 zeros_p(*state_lru_h.shape[2:]), zeros_p(*state_lru_conv.shape[2:]),
                                 zeros_p(*state_cc_conv.shape[2:]), chained=True)
        yatt = _attend_prompt(q, qi, k, v, kw, bp, tp)
        h1 = _mixout(xp, zmix, ys, h, ycc, yatt, lp['s5_d'], lp['s5_w_glu'], lp['s5_b_glu'], lp['w_out'],
                     lp['ln1_g'], lp['ln1_b'], tm_p, alpha)
        xp = channel(h1)
        states_p.append((k.reshape(bp, tp, n_heads, HEAD_DIM), v.reshape(bp, tp, n_heads, HEAD_DIM),
                         kw[:, :IDX_DIM].reshape(bp, tp, IDX_DIM)) + st)

        ms = bs * ts
        zmix, q, qi, k, v, kw = _in_proj(xs, lp['w_in'], cos_s, sin_s, ms, mix_w, gw)
        ys, h, ycc, st = _mixers(lp, zmix, bs, ts, state_s5_re[l], state_s5_im[l], state_lru_h[l],
                                 state_lru_conv[l], state_cc_conv[l], chained=False)
        yatt = _attend_sample(l, q, qi, k, v, kw, bs, ts, cache_k, cache_v, cache_kidx, page_table)
        h1 = _mixout(xs, zmix, ys, h, ycc, yatt, lp['s5_d'], lp['s5_w_glu'], lp['s5_b_glu'], lp['w_out'],
                     lp['ln1_g'], lp['ln1_b'], ms, alpha)
        xs = channel(h1)
        states_s.append((k.reshape(bs, ts, n_heads, HEAD_DIM), v.reshape(bs, ts, n_heads, HEAD_DIM),
                         kw[:, :IDX_DIM].reshape(bs, ts, IDX_DIM)) + st)

    new_p = [jnp.stack(col) for col in zip(*states_p)]
    new_s = [jnp.stack(col) for col in zip(*states_s)]
    out = [xp.reshape(bp, tp, dm), xs.reshape(bs, ts, dm)]
    for a, b in zip(new_p, new_s):
        out += [a, b]
    return tuple(out)
```

```python
import functools
import math

import jax
import jax.numpy as jnp
from jax import lax
from jax.experimental import pallas as pl
from jax.experimental.pallas import tpu as pltpu

F32 = jnp.float32
BF16 = jnp.bfloat16

S5_CH = 16
LRU_C = 8.0
HEAD_DIM = 64
IDX_DIM = 64
IDX_HEADS = 4
TOPK_MAX = 256
Q_BLOCK = 128
ROPE_THETA = 10000.0
MOE_TOP_K = 2
LN_EPS = 1e-5
ATT_SCALE = HEAD_DIM ** -0.5
LOG2_E = math.log2(math.e)
IDX_SCALE = (IDX_DIM * IDX_HEADS) ** -0.5

LANES = 128
NEG_BIG = -1e30
VMEM_LIMIT = 56 * 1024 * 1024


def _params(*sem):
    return pltpu.CompilerParams(dimension_semantics=sem, vmem_limit_bytes=VMEM_LIMIT)


def _dot(a, b):
    return jnp.dot(a, b, preferred_element_type=F32)


def _dot_t(a, b):
    return lax.dot_general(a, b, (((1,), (1,)), ((), ())), preferred_element_type=F32)


def _split3(x):
    hi = x.astype(BF16)
    lo = (x - hi.astype(F32)).astype(BF16)
    return hi, lo


def _layer_norm(x, g, b):
    mu = jnp.mean(x, axis=-1, keepdims=True)
    xc = x - mu
    var = jnp.mean(xc * xc, axis=-1, keepdims=True)
    return xc * lax.rsqrt(var + LN_EPS) * g + b


def _rope(z, cos, sin):
    w = z.shape[-1]
    half = HEAD_DIM // 2
    lane = lax.broadcasted_iota(jnp.int32, z.shape, 1)
    first = (lane & (HEAD_DIM - 1)) < half
    partner = jnp.where(first, pltpu.roll(z, w - half, axis=1), pltpu.roll(z, half, axis=1))
    return z * cos + partner * sin


def _in_proj_kernel(x_ref, w_ref, cos_ref, sin_ref, zmix_ref, q_ref, qi_ref, k_ref, v_ref, kw_ref,
                    *, mix_w, gw):
    xb = x_ref[...].astype(BF16)
    cos = cos_ref[...]
    sin = sin_ref[...]

    def proj(lo, width):
        return _dot(xb, w_ref[:, lo:lo + width])

    zmix_ref[...] = proj(0, mix_w)
    o = mix_w
    q_ref[...] = (_rope(proj(o, gw), cos, sin) * (ATT_SCALE * LOG2_E)).astype(BF16)
    k_ref[...] = _rope(proj(o + gw, gw), cos, sin)
    v_ref[...] = proj(o + 2 * gw, gw)
    qi_ref[...] = _rope(proj(o + 3 * gw, gw), cos, sin)
    kw = proj(o + 4 * gw, LANES)
    is_key = lax.broadcasted_iota(jnp.int32, kw.shape, 1) < IDX_DIM
    kw_ref[...] = _rope(kw, jnp.where(is_key, cos[:, :LANES], 1.0), jnp.where(is_key, sin[:, :LANES], 0.0))


def _in_proj(x2, w_pad, cos_t, sin_t, tm, mix_w, gw):
    m, d = x2.shape
    tab_blocks = cos_t.shape[0] // tm
    row = lambda i: (i, 0)
    tab = lambda i: (i % tab_blocks, 0)
    outs = [
        jax.ShapeDtypeStruct((m, mix_w), F32),
        jax.ShapeDtypeStruct((m, gw), BF16),
        jax.ShapeDtypeStruct((m, gw), F32),
        jax.ShapeDtypeStruct((m, gw), F32),
        jax.ShapeDtypeStruct((m, gw), F32),
        jax.ShapeDtypeStruct((m, LANES), F32),
    ]
    return pl.pallas_call(
        functools.partial(_in_proj_kernel, mix_w=mix_w, gw=gw),
        grid=(m // tm,),
        in_specs=[pl.BlockSpec((tm, d), row),
                  pl.BlockSpec(w_pad.shape, lambda i: (0, 0)),
                  pl.BlockSpec((tm, gw), tab),
                  pl.BlockSpec((tm, gw), tab)],
        out_specs=[pl.BlockSpec((tm, s.shape[1]), row) for s in outs],
        out_shape=outs,
        compiler_params=_params("parallel"),
        name="in_proj",
    )(x2, w_pad, cos_t, sin_t)


LRU_HALO = 8
CC_HALO = 32


SUBLANES = 8


def _tap_sum(x_s, first, taps, tt, w_ref):
    n = x_s.shape[0]
    if n % SUBLANES or tt % SUBLANES:
        acc = x_s[first:first + tt, :] * w_ref[0:1, :]
        for j in range(1, taps):
            acc = acc + x_s[first + j:first + j + tt, :] * w_ref[j:j + 1, :]
        return acc
    full = x_s[...]
    acc = None
    for phase in range(SUBLANES):
        js = [j for j in range(taps) if (first + j) % SUBLANES == phase]
        if not js:
            continue
        z = full if phase == 0 else pltpu.roll(full, n - phase, axis=0)
        for j in js:
            a0 = first + j - phase
            term = z[a0:a0 + tt, :] * w_ref[j:j + 1, :]
            acc = term if acc is None else acc + term
    return acc


def _convmix_kernel(xl_ref, ca_ref, cg_ref, lbuf_ref, cbuf_ref, lw_ref, lb_ref, wg_ref, bg_ref, sp_ref,
                    cw_ref, cb_ref, lng_ref, lnb_ref,
                    a_ref, b_ref, ycc_ref, lbuf_o, cbuf_o, xl_s, xc_s, *, tt):
    i = pl.program_id(1)
    kl = lw_ref.shape[0]
    kc = cw_ref.shape[0]
    w = xl_ref.shape[-1]
    l0 = LRU_HALO - (kl - 1)
    c0 = CC_HALO - (kc - 1)

    @pl.when(i == 0)
    def _():
        xl_s[l0:LRU_HALO, :] = lbuf_ref[...]
        xc_s[c0:CC_HALO, :] = cbuf_ref[...]

    xl_s[LRU_HALO:LRU_HALO + tt, :] = xl_ref[...]
    xc_s[CC_HALO:CC_HALO + tt, :] = ca_ref[...] * jax.nn.sigmoid(cg_ref[...])

    conv = _tap_sum(xl_s, l0, kl, tt, lw_ref) + lb_ref[...]
    gates = _dot(conv.astype(BF16), wg_ref[...]) + bg_ref[...]
    r = jax.nn.sigmoid(gates[:, :w])
    ig = jax.nn.sigmoid(gates[:, w:])
    log_a = (-LRU_C) * r * sp_ref[...]
    a = jnp.exp(log_a)
    a_ref[...] = a
    b_ref[...] = jnp.sqrt(-jnp.tanh(log_a) * (a * a + 1.0)) * (ig * conv)

    c = _tap_sum(xc_s, c0, kc, tt, cw_ref)
    c = _layer_norm(c + cb_ref[...], lng_ref[...], lnb_ref[...])
    ycc_ref[...] = c * jax.nn.sigmoid(c)

    new_l = xl_s[l0 + tt:LRU_HALO + tt, :]
    new_c = xc_s[c0 + tt:CC_HALO + tt, :]
    xl_s[l0:LRU_HALO, :] = new_l
    xc_s[c0:CC_HALO, :] = new_c
    lbuf_o[...] = new_l
    cbuf_o[...] = new_c


def _convmix(zmix3, lbuf, cbuf, lw, lb, wg, bg, sp, cw, cb, lng, lnb, tt):
    nb, t, _ = zmix3.shape
    w = lw.shape[1]
    col = lambda c: pl.BlockSpec((None, tt, w), lambda b, i, c=c: (b, i, c))
    full2 = lambda a: pl.BlockSpec(a.shape, lambda b, i: (0, 0))
    per_b = lambda a: pl.BlockSpec((None,) + a.shape[1:], lambda b, i: (b, 0, 0))
    seq_out = jax.ShapeDtypeStruct((nb, t, w), F32)
    return pl.pallas_call(
        functools.partial(_convmix_kernel, tt=tt),
        grid=(nb, t // tt),
        in_specs=[col(1), col(3), col(4), per_b(lbuf), per_b(cbuf), full2(lw), full2(lb), full2(wg),
                  full2(bg), full2(sp), full2(cw), full2(cb), full2(lng), full2(lnb)],
        out_specs=[pl.BlockSpec((None, tt, w), lambda b, i: (b, i, 0))] * 3 + [per_b(lbuf), per_b(cbuf)],
        out_shape=[seq_out, seq_out, seq_out,
                   jax.ShapeDtypeStruct(lbuf.shape, F32), jax.ShapeDtypeStruct(cbuf.shape, F32)],
        scratch_shapes=[pltpu.VMEM((LRU_HALO + tt, w), F32), pltpu.VMEM((CC_HALO + tt, w), F32)],
        compiler_params=_params("parallel", "arbitrary"),
        name="convmix",
    )(zmix3, zmix3, zmix3, lbuf, cbuf, lw, lb, wg, bg, sp, cw, cb, lng, lnb)


def _lru_scan_kernel(a_ref, b_ref, h0_ref, h_ref, hl_ref, e_s, p_s, hi_s, *, chained):
    nl, nr, _ = a_ref.shape

    def run(init):
        def step(t, h):
            h = a_ref[t] * h + b_ref[t]
            h_ref[t] = h
            return h
        return lax.fori_loop(0, nl, step, init)

    if not chained:
        hl_ref[...] = run(h0_ref[...])
        return

    def local(t, c):
        h, p = c
        at = a_ref[t]
        return at * h + b_ref[t], at * p
    zeros = jnp.zeros(e_s.shape, F32)
    e, p = lax.fori_loop(0, nl, local, (zeros, zeros + 1.0))
    e_s[...] = e
    p_s[...] = p
    hi_s[0:1, :] = h0_ref[...]

    def carry(c, _):
        prev = pl.ds(c - 1, 1)
        hi_s[pl.ds(c, 1), :] = p_s[prev, :] * hi_s[prev, :] + e_s[prev, :]
        return 0
    lax.fori_loop(1, nr, carry, 0)
    last = run(hi_s[...])
    hl_ref[...] = last[nr - 1:nr, :]


def _lru_scan(a4, b4, h0, chained):
    nb, nl, nr, w = a4.shape
    r0 = h0.shape[1]
    nw = w // LANES
    blk = pl.BlockSpec((None, nl, nr, LANES), lambda n, j: (n, 0, 0, j))
    st = pl.BlockSpec((None, r0, LANES), lambda n, j: (n, 0, j))
    return pl.pallas_call(
        functools.partial(_lru_scan_kernel, chained=chained),
        grid=(nb, nw),
        in_specs=[blk, blk, st],
        out_specs=[blk, st],
        out_shape=[jax.ShapeDtypeStruct(a4.shape, F32), jax.ShapeDtypeStruct(h0.shape, F32)],
        scratch_shapes=[pltpu.VMEM((nr, LANES), F32)] * 3,
        compiler_params=_params("parallel", "parallel"),
        name="lru_scan",
    )(a4, b4, h0)


def _s5_scan_kernel(u_ref, bre_ref, bim_ref, are_ref, aim_ref, cc_ref, h0r_ref, h0i_ref,
                    ys_ref, hlr_ref, hli_ref, br_s, bi_s, er_s, ei_s, hr_s, hi_s, *, chained, rows_per_dot):
    g = pl.program_id(1)
    nl, nr, _ = u_ref.shape
    n = nl * nr
    ar = are_ref[...]
    ai = aim_ref[...]

    b_cat = jnp.concatenate([bre_ref[...], bim_ref[...]], axis=1)

    def drive(c, _):
        r0 = pl.multiple_of(c * rows_per_dot, rows_per_dot)
        ub = u_ref[pl.ds(c * (rows_per_dot // nr), rows_per_dot // nr)].reshape(rows_per_dot, u_ref.shape[-1])
        d = _dot(ub, b_cat)
        br_s[pl.ds(r0, rows_per_dot), :] = d[:, :LANES]
        bi_s[pl.ds(r0, rows_per_dot), :] = d[:, LANES:]
        return 0
    lax.fori_loop(0, n // rows_per_dot, drive, 0)

    def step_fn(store):
        def step(t, c):
            hr, hi = c
            rows = pl.ds(pl.multiple_of(t * nr, nr), nr)
            nhr = ar * hr - ai * hi + br_s[rows, :]
            nhi = ar * hi + ai * hr + bi_s[rows, :]
            if store:
                br_s[rows, :] = nhr
                bi_s[rows, :] = nhi
            return nhr, nhi
        return step

    if chained:
        zeros = jnp.zeros((nr, LANES), F32)
        er, ei = lax.fori_loop(0, nl, step_fn(False), (zeros, zeros))
        er_s[...] = er
        ei_s[...] = ei

        def power(_, c):
            pr, pi = c
            return ar * pr - ai * pi, ar * pi + ai * pr
        alr, ali = lax.fori_loop(0, nl - 1, power, (ar, ai))
        hr_s[0:1, :] = h0r_ref[...]
        hi_s[0:1, :] = h0i_ref[...]

        def carry(c, _):
            prev = pl.ds(c - 1, 1)
            pr = hr_s[prev, :]
            pi = hi_s[prev, :]
            hr_s[pl.ds(c, 1), :] = alr * pr - ali * pi + er_s[prev, :]
            hi_s[pl.ds(c, 1), :] = alr * pi + ali * pr + ei_s[prev, :]
            return 0
        lax.fori_loop(1, nr, carry, 0)
        init = (hr_s[...], hi_s[...])
    else:
        init = (h0r_ref[...], h0i_ref[...])

    lr, li = lax.fori_loop(0, nl, step_fn(True), init)
    if chained:
        hlr_ref[...] = lr[nr - 1:nr, :]
        hli_ref[...] = li[nr - 1:nr, :]
    else:
        hlr_ref[...] = lr
        hli_ref[...] = li

    def project(c, _):
        r0 = pl.multiple_of(c * rows_per_dot, rows_per_dot)
        rows = pl.ds(r0, rows_per_dot)
        hcat = jnp.concatenate([br_s[rows, :], bi_s[rows, :]], axis=1).astype(BF16)
        y = _dot(hcat, cc_ref[...]).reshape(rows_per_dot // nr, nr, ys_ref.shape[-1])
        tsl = pl.ds(c * (rows_per_dot // nr), rows_per_dot // nr)

        @pl.when(g == 0)
        def _():
            ys_ref[tsl] = y

        @pl.when(g > 0)
        def _():
            ys_ref[tsl] = ys_ref[tsl] + y
        return 0
    lax.fori_loop(0, n // rows_per_dot, project, 0)


def _s5_scan(u4, bre, bim, are, aim, ccat, h0r, h0i, chained):
    nb, nl, nr, w = u4.shape
    ng = bre.shape[0]
    r0 = h0r.shape[1]
    n = nl * nr
    rows_per_dot = math.gcd(n, 1024)
    rows_per_dot = max(rows_per_dot, nr)
    seq = pl.BlockSpec((None, nl, nr, w), lambda b, g: (b, 0, 0, 0))
    per_g = lambda a: pl.BlockSpec((None,) + a.shape[1:], lambda b, g: (g, 0, 0))
    st = pl.BlockSpec((None, r0, LANES), lambda b, g: (b, 0, g))
    small = pltpu.VMEM((nr, LANES), F32)
    return pl.pallas_call(
        functools.partial(_s5_scan_kernel, chained=chained, rows_per_dot=rows_per_dot),
        grid=(nb, ng),
        in_specs=[seq, per_g(bre), per_g(bim), per_g(are), per_g(aim), per_g(ccat), st, st],
        out_specs=[seq, st, st],
        out_shape=[jax.ShapeDtypeStruct(u4.shape, F32), jax.ShapeDtypeStruct(h0r.shape, F32),
                   jax.ShapeDtypeStruct(h0i.shape, F32)],
        scratch_shapes=[pltpu.VMEM((n, LANES), F32), pltpu.VMEM((n, LANES), F32), small, small, small, small],
        compiler_params=_params("parallel", "arbitrary"),
        name="s5_scan",
    )(u4, bre, bim, are, aim, ccat, h0r, h0i)


BISECT_ITERS = 24
PROBE_QUERIES = 12
PROBE_GAP = 3


def _select_threshold(count_ge, max_below, mn, mx, kq, all_sel, probe_queries):
    inf = jnp.full(mn.shape, jnp.inf, F32)
    zero = jnp.zeros(mn.shape, F32)
    state = (mn, inf, mn, jnp.where(all_sel, 1.0, zero), zero)

    def unfinished(st):
        return jnp.min(st[3]) == 0.0

    def bisect(c):
        it, (lo, hi, theta, done, tie) = c
        mid = jnp.where(hi == jnp.inf, mx, 0.5 * lo + 0.5 * hi)
        cnt = count_ge(mid)
        live = done == 0.0
        collapsed = jnp.logical_and(live, jnp.logical_or(mid == lo, mid == hi))
        hit = jnp.logical_and(live, cnt == kq)
        move = jnp.logical_and(live, jnp.logical_not(collapsed))
        theta = jnp.where(hit, mid, jnp.where(collapsed, hi, theta))
        tie = jnp.where(collapsed, 1.0, tie)
        done = jnp.where(jnp.logical_or(hit, collapsed), 1.0, done)
        lo = jnp.where(jnp.logical_and(move, cnt > kq), mid, lo)
        hi = jnp.where(jnp.logical_and(move, cnt < kq), mid, hi)
        return it + 1, (lo, hi, theta, done, tie)

    _, state = lax.while_loop(
        lambda c: jnp.logical_and(c[0] < BISECT_ITERS, jnp.sum(1.0 - c[1][3]) > float(probe_queries)),
        lambda c: bisect(bisect(c)), (jnp.int32(0), state))

    def exact_step(st):
        lo, hi, theta, done, tie = st
        live = done == 0.0
        tau = max_below(hi)
        cnt = count_ge(tau)
        found = jnp.logical_and(live, cnt >= kq)
        clean = jnp.logical_and(found, cnt == kq)
        theta = jnp.where(clean, tau, jnp.where(found, hi, theta))
        tie = jnp.where(jnp.logical_and(found, jnp.logical_not(clean)), 1.0, tie)
        lo = jnp.where(found, tau, lo)
        hi = jnp.where(jnp.logical_and(live, jnp.logical_not(found)), tau, hi)
        done = jnp.where(found, 1.0, done)
        return lo, hi, theta, done, tie

    def settle(st):
        st = exact_step(st)
        _, st = lax.while_loop(lambda c: jnp.logical_and(c[0] < PROBE_GAP, unfinished(c[1])),
                               bisect, (jnp.int32(0), st))
        return st

    lo, hi, theta, done, tie = lax.while_loop(unfinished, settle, state)
    return theta, lo, tie


COUNT_ROWS = 64


def _dsa_prompt_kernel(qt_ref, qih_ref, qil_ref, wit_ref, kidx_ref, k_ref, vt_ref, o_ref,
                       s_scr, lg_a, lg_b, qp_s, qip_s, *, topk, kc):
    i = pl.program_id(1)
    qb = Q_BLOCK
    n_pairs = qp_s.shape[0]
    pw = 2 * HEAD_DIM

    zq = jnp.zeros((HEAD_DIM, qb), BF16)
    for p in range(n_pairs):
        top = jnp.concatenate([qt_ref[p * pw:p * pw + HEAD_DIM, :], zq], axis=1)
        bot = jnp.concatenate([zq, qt_ref[p * pw + HEAD_DIM:(p + 1) * pw, :]], axis=1)
        qp_s[p] = jnp.concatenate([top, bot], axis=0)
        cols = []
        for h in (2 * p, 2 * p + 1):
            hi = qih_ref[h * IDX_DIM:(h + 1) * IDX_DIM, :]
            cols.append(jnp.concatenate([hi, hi, qil_ref[h * IDX_DIM:(h + 1) * IDX_DIM, :], zq], axis=0))
        qip_s[p] = jnp.concatenate(cols, axis=1)
    nch = ((i + 1) * qb + kc - 1) // kc
    qpos = i * qb + lax.broadcasted_iota(jnp.int32, (1, qb), 1)
    kiota = lax.broadcasted_iota(jnp.int32, (kc, qb), 0)

    def chunk_rows(c):
        return pl.ds(pl.multiple_of(c * kc, kc), kc)

    def score_chunk(c, carry, masked):
        mn, mx = carry
        kk = kidx_ref[chunk_rows(c), :]
        s = jnp.zeros((kc, qb), F32)
        for p in range(n_pairs):
            sp = _dot(kk, qip_s[p])
            for hh in range(2):
                h = 2 * p + hh
                s = s + jnp.maximum(sp[:, hh * qb:(hh + 1) * qb], 0.0) * wit_ref[h:h + 1, :]
        s = s * IDX_SCALE
        if not masked:
            s_scr[chunk_rows(c), :] = s
            return (jnp.minimum(mn, jnp.min(s, axis=0, keepdims=True)),
                    jnp.maximum(mx, jnp.max(s, axis=0, keepdims=True)))
        vis = (c * kc + kiota) <= qpos
        s_scr[chunk_rows(c), :] = jnp.where(vis, s, -jnp.inf)
        mn = jnp.minimum(mn, jnp.min(jnp.where(vis, s, jnp.inf), axis=0, keepdims=True))
        mx = jnp.maximum(mx, jnp.max(jnp.where(vis, s, -jnp.inf), axis=0, keepdims=True))
        return mn, mx
    nch2 = (nch + 1) // 2
    n_full = (i * qb + 1) // kc
    carry = lax.fori_loop(0, n_full, functools.partial(score_chunk, masked=False),
                          (jnp.full((1, qb), jnp.inf, F32), jnp.full((1, qb), -jnp.inf, F32)))
    mn, mx = lax.fori_loop(n_full, 2 * nch2, functools.partial(score_chunk, masked=True), carry)

    def fold(x, op):
        return op(x.reshape(kc // COUNT_ROWS, COUNT_ROWS, qb), axis=0)

    def count(pred_fn):
        def body(c, acc):
            hit = jnp.where(pred_fn(s_scr[chunk_rows(c), :], c * kc + kiota), 1.0, 0.0)
            return acc + fold(hit, jnp.sum)
        acc = lax.fori_loop(0, nch, body, jnp.zeros((COUNT_ROWS, qb), F32))
        return jnp.sum(acc, axis=0, keepdims=True)

    def max_below(x):
        def body(c, acc):
            s = s_scr[chunk_rows(c), :]
            return jnp.maximum(acc, fold(jnp.where(s < x, s, -jnp.inf), jnp.max))
        acc = lax.fori_loop(0, nch, body, jnp.full((COUNT_ROWS, qb), -jnp.inf, F32))
        return jnp.max(acc, axis=0, keepdims=True)

    n_vis = (qpos + 1).astype(F32)
    kq = jnp.minimum(n_vis, float(topk))
    theta, lo, tie = _select_threshold(lambda x: count(lambda s, kpos: s >= x), max_below,
                                       mn, mx, kq, n_vis <= float(topk), PROBE_QUERIES)

    @pl.when(jnp.max(tie) > 0.0)
    def _():
        need = kq - count(lambda s, kpos: s >= theta)
        tied = lambda s: jnp.logical_and(tie > 0.0, s == lo)
        tri = jnp.where(lax.broadcasted_iota(jnp.int32, (kc, kc), 0) >= lax.broadcasted_iota(jnp.int32, (kc, kc), 1),
                        1.0, 0.0).astype(BF16)

        def promote(c, seen):
            s = s_scr[chunk_rows(c), :]
            is_tied = tied(s)
            rank = seen + _dot(tri, jnp.where(is_tied, 1.0, 0.0).astype(BF16))
            keep = jnp.logical_and(is_tied, rank <= need)
            s_scr[chunk_rows(c), :] = jnp.where(keep, theta, s)
            return rank[kc - 1:kc, :]
        lax.fori_loop(0, nch, promote, jnp.zeros((1, qb), F32))

    n_heads = 2 * n_pairs
    kc2 = 2 * kc

    def chunk2_rows(c):
        return pl.ds(pl.multiple_of(jnp.minimum(c, nch2 - 1) * kc2, kc2), kc2)

    def put_logits(dst, c):
        for p in range(n_pairs):
            dst[p] = _dot(k_ref[chunk2_rows(c), p * pw:(p + 1) * pw], qp_s[p])

    def absorb(src, c, carry):
        ms, ls, accs = carry
        rows2 = chunk2_rows(c)
        sel = s_scr[rows2, :] >= jnp.where(c < nch2, theta, jnp.nan)
        new_m, new_l, new_acc = [], [], []
        for p in range(n_pairs):
            for hh in range(2):
                h = 2 * p + hh
                logit = jnp.where(sel, src[p, :, hh * qb:(hh + 1) * qb], NEG_BIG)
                m_new = jnp.maximum(ms[h], jnp.max(logit, axis=0, keepdims=True))
                alpha = jnp.exp2(ms[h] - m_new)
                pr = jnp.exp2(logit - m_new)
                new_m.append(m_new)
                new_l.append(alpha * ls[h] + jnp.sum(pr, axis=0, keepdims=True))
                pv = _dot(vt_ref[h * HEAD_DIM:(h + 1) * HEAD_DIM, rows2], pr.astype(BF16))
                new_acc.append(alpha * accs[h] + pv)
        return tuple(new_m), tuple(new_l), tuple(new_acc)

    def att_pair(j, carry):
        c = 2 * j
        put_logits(lg_b, c + 1)
        carry = absorb(lg_a, c, carry)
        put_logits(lg_a, c + 2)
        return absorb(lg_b, c + 1, carry)

    init = (tuple(jnp.full((1, qb), NEG_BIG, F32) for _ in range(n_heads)),
            tuple(jnp.zeros((1, qb), F32) for _ in range(n_heads)),
            tuple(jnp.zeros((HEAD_DIM, qb), F32) for _ in range(n_heads)))
    put_logits(lg_a, 0)
    _, ls, accs = lax.fori_loop(0, (nch2 + 1) // 2, att_pair, init)
    for h in range(n_heads):
        o_ref[h * HEAD_DIM:(h + 1) * HEAD_DIM, :] = accs[h] / ls[h]


def _dsa_prompt(qt, qiht, qilt, wit, kidx3, kb, vt, topk, kc):
    nb, w, t = vt.shape
    n_pairs = w // (2 * HEAD_DIM)
    per_b = lambda a: pl.BlockSpec((None,) + a.shape[1:], lambda b, i: (b, 0, 0))
    qcol = lambda a: pl.BlockSpec((None, a.shape[1], Q_BLOCK), lambda b, i: (b, 0, i))
    return pl.pallas_call(
        functools.partial(_dsa_prompt_kernel, topk=topk, kc=kc),
        grid=(nb, t // Q_BLOCK),
        in_specs=[qcol(qt), qcol(qiht), qcol(qilt), qcol(wit), per_b(kidx3), per_b(kb), per_b(vt)],
        out_specs=pl.BlockSpec((None, w, Q_BLOCK), lambda b, i: (b, 0, i)),
        out_shape=jax.ShapeDtypeStruct((nb, w, t), F32),
        scratch_shapes=[pltpu.VMEM((t, Q_BLOCK), F32),
                        pltpu.VMEM((n_pairs, 2 * kc, 2 * Q_BLOCK), F32),
                        pltpu.VMEM((n_pairs, 2 * kc, 2 * Q_BLOCK), F32),
                        pltpu.VMEM((n_pairs, 2 * HEAD_DIM, 2 * Q_BLOCK), BF16),
                        pltpu.VMEM((n_pairs, kidx3.shape[2], 2 * Q_BLOCK), BF16)],
        compiler_params=_params("parallel", "arbitrary"),
        name="dsa_prompt",
    )(qt, qiht, qilt, wit, kidx3, kb, vt)


SAMPLE_PAGES_PER_STEP = 16
QROWS = 8


SELECT_BATCHES_PER_STEP = 8


def _dsa_sample_select_kernel(pt_ref, qhi_ref, qlo_ref, wi_ref, knew_ref, *rest, topk, n_new, gp, gb):
    pages = rest[:gb * gp]
    bias_ref, s_scr = rest[gb * gp], rest[gb * gp + 1]
    j = pl.program_id(1)
    nj = pl.num_programs(1)
    psz = pages[0].shape[1]
    past = nj * gp * psz
    rows = gb * QROWS

    def scores(g, kpage_t):
        khi, klo = _split3(kpage_t)
        qhi = qhi_ref[g]
        s = _dot(qhi, khi) + _dot(qhi, klo) + _dot(qlo_ref[g], khi)
        s = jnp.maximum(s, 0.0) * wi_ref[g]
        tot = s[0:QROWS, :]
        for h in range(1, IDX_HEADS):
            tot = tot + s[h * QROWS:(h + 1) * QROWS, :]
        return tot * IDX_SCALE

    for g in range(gb):
        for p in range(gp):
            off = pl.multiple_of((j * gp + p) * psz, psz)
            s_scr[g * QROWS:(g + 1) * QROWS, pl.ds(off, psz)] = scores(g, pages[g * gp + p][...])

    @pl.when(j == nj - 1)
    def _():
        qrow = lax.broadcasted_iota(jnp.int32, (QROWS, psz), 0)
        kcol = lax.broadcasted_iota(jnp.int32, (QROWS, psz), 1)
        vis_new = jnp.logical_and(kcol <= qrow, kcol < n_new)
        for g in range(gb):
            s_scr[g * QROWS:(g + 1) * QROWS, past:past + psz] = jnp.where(vis_new, scores(g, knew_ref[g]), -jnp.inf)
        s = s_scr[...]
        nk = s.shape[1]
        kpos = lax.broadcasted_iota(jnp.int32, s.shape, 1)
        qrow1 = lax.broadcasted_iota(jnp.int32, (rows, 1), 0) & (QROWS - 1)
        n_vis = (past + jnp.minimum(qrow1 + 1, n_new)).astype(F32)
        kq = jnp.minimum(n_vis, float(topk))
        fin = s > -jnp.inf
        mn = jnp.min(jnp.where(fin, s, jnp.inf), axis=1, keepdims=True)
        mx = jnp.max(s, axis=1, keepdims=True)
        count = lambda pred: jnp.sum(jnp.where(pred, 1.0, 0.0), axis=1, keepdims=True)
        theta, lo, tie = _select_threshold(
            lambda x: count(s >= x),
            lambda x: jnp.max(jnp.where(s < x, s, -jnp.inf), axis=1, keepdims=True),
            mn, mx, kq, jnp.logical_or(n_vis <= float(topk), qrow1 >= n_new), 0)

        sel = s >= theta
        bias_ref[...] = jnp.where(sel, 0.0, NEG_BIG).reshape(bias_ref.shape)

        @pl.when(jnp.max(tie) > 0.0)
        def _():
            need = kq - count(sel)
            tied = jnp.logical_and(tie > 0.0, s == lo)
            nbits = max(1, (nk - 1).bit_length())

            def jstep(b, jl):
                cand = jl + (jnp.int32(1) << (nbits - 1 - b))
                c_lt = count(jnp.logical_and(tied, kpos < cand))
                return jnp.where(c_lt < need, cand, jl)
            jlast = lax.fori_loop(0, nbits, jstep, jnp.zeros((rows, 1), jnp.int32))
            keep = jnp.logical_or(sel, jnp.logical_and(tied, kpos <= jlast))
            bias_ref[...] = jnp.where(keep, 0.0, NEG_BIG).reshape(bias_ref.shape)


def _dsa_sample_select(layer, page_table, qhi, qlo, wi_b, knew, pool_kidx, topk, n_new):
    nb, n_pages = page_table.shape
    gp = math.gcd(n_pages, SAMPLE_PAGES_PER_STEP)
    gb = math.gcd(nb, SELECT_BATCHES_PER_STEP)
    di, psz = pool_kidx.shape[2], pool_kidx.shape[3]
    nk = (n_pages + 1) * psz
    per_b = lambda a: pl.BlockSpec((gb,) + a.shape[1:], lambda b, j, pt: (b, 0, 0))
    page = lambda g, p: pl.BlockSpec((None, None, di, psz),
                                     lambda b, j, pt: (layer, pt[b * gb + g, j * gp + p], 0, 0))
    grid_spec = pltpu.PrefetchScalarGridSpec(
        num_scalar_prefetch=1,
        grid=(nb // gb, n_pages // gp),
        in_specs=[per_b(qhi), per_b(qlo), per_b(wi_b), per_b(knew)]
        + [page(g, p) for g in range(gb) for p in range(gp)],
        out_specs=pl.BlockSpec((gb, QROWS, nk), lambda b, j, pt: (b, 0, 0)),
        scratch_shapes=[pltpu.VMEM((gb * QROWS, nk), F32)],
    )
    return pl.pallas_call(
        functools.partial(_dsa_sample_select_kernel, topk=topk, n_new=n_new, gp=gp, gb=gb),
        grid_spec=grid_spec,
        out_shape=jax.ShapeDtypeStruct((nb, QROWS, nk), F32),
        compiler_params=_params("parallel", "arbitrary"),
        name="dsa_sample_select",
    )(page_table, qhi, qlo, wi_b, knew, *([pool_kidx] * (gb * gp)))


def _dsa_sample_attend_kernel(pt_ref, q_ref, knew_ref, vnew_ref, *rest, gp):
    bias_refs = rest[:gp]
    kpages = rest[gp:2 * gp]
    vpages = rest[2 * gp:3 * gp]
    bnew_ref = rest[3 * gp]
    o_ref, m_s, l_s, acc_s = rest[3 * gp + 1:]
    j = pl.program_id(1)
    nj = pl.num_programs(1)
    q = q_ref[...]
    n_heads = q.shape[0] // QROWS

    @pl.when(j == 0)
    def _():
        m_s[...] = jnp.full(m_s.shape, NEG_BIG, F32)
        l_s[...] = jnp.zeros(l_s.shape, F32)
        acc_s[...] = jnp.zeros(acc_s.shape, F32)

    def absorb(kts, vts, biases):
        psz = kts[0].shape[1]
        logit = jnp.concatenate(
            [_dot(q, kt[...].astype(BF16)) + jnp.concatenate([b[...]] * n_heads, axis=0)
             for kt, b in zip(kts, biases)], axis=1)
        m = m_s[...]
        m_new = jnp.maximum(m, jnp.max(logit, axis=1, keepdims=True))
        alpha = jnp.exp2(m - m_new)
        p = jnp.exp2(logit - m_new)
        l_s[...] = alpha * l_s[...] + jnp.sum(p, axis=1, keepdims=True)
        pv = _dot_t(p[:, 0:psz].astype(BF16), vts[0][...].astype(BF16))
        for g in range(1, len(vts)):
            pv = pv + _dot_t(p[:, g * psz:(g + 1) * psz].astype(BF16), vts[g][...].astype(BF16))
        acc_s[...] = alpha * acc_s[...] + pv
        m_s[...] = m_new

    absorb(kpages, vpages, bias_refs)

    @pl.when(j == nj - 1)
    def _():
        absorb([knew_ref], [vnew_ref], [bnew_ref])
        out = acc_s[...] / l_s[...]
        lane = lax.broadcasted_iota(jnp.int32, (QROWS, out.shape[1]), 1)
        y = jnp.zeros((QROWS, out.shape[1]), F32)
        for h in range(n_heads):
            in_head = jnp.logical_and(lane >= h * HEAD_DIM, lane < (h + 1) * HEAD_DIM)
            y = y + jnp.where(in_head, out[h * QROWS:(h + 1) * QROWS, :], 0.0)
        o_ref[...] = y


def _dsa_sample_attend(layer, page_table, q32, knew, vnew, bias, pool_k, pool_v):
    nb, n_pages = page_table.shape
    gp = math.gcd(n_pages, 2 * SAMPLE_PAGES_PER_STEP)
    w, psz = pool_k.shape[2], pool_k.shape[3]
    per_b = lambda a: pl.BlockSpec((None,) + a.shape[1:], lambda b, j, pt: (b, 0, 0))
    page = lambda g: pl.BlockSpec((None, None, w, psz), lambda b, j, pt, g=g: (layer, pt[b, j * gp + g], 0, 0))
    bias_pg = lambda g: pl.BlockSpec((None, QROWS, psz), lambda b, j, pt, g=g: (b, 0, j * gp + g))
    bias_new = pl.BlockSpec((None, QROWS, psz), lambda b, j, pt: (b, 0, n_pages))
    grid_spec = pltpu.PrefetchScalarGridSpec(
        num_scalar_prefetch=1,
        grid=(nb, n_pages // gp),
        in_specs=([per_b(q32), per_b(knew), per_b(vnew)] + [bias_pg(g) for g in range(gp)]
                  + [page(g) for g in range(gp)] + [page(g) for g in range(gp)] + [bias_new]),
        out_specs=pl.BlockSpec((None, QROWS, w), lambda b, j, pt: (b, 0, 0)),
        scratch_shapes=[pltpu.VMEM((q32.shape[1], 1), F32), pltpu.VMEM((q32.shape[1], 1), F32),
                        pltpu.VMEM((q32.shape[1], w), F32)],
    )
    return pl.pallas_call(
        functools.partial(_dsa_sample_attend_kernel, gp=gp),
        grid_spec=grid_spec,
        out_shape=jax.ShapeDtypeStruct((nb, QROWS, w), F32),
        compiler_params=_params("parallel", "arbitrary"),
        name="dsa_sample_attend",
    )(page_table, q32, knew, vnew, *([bias] * gp), *([pool_k] * gp), *([pool_v] * gp), bias)


def _mixout_kernel(x_ref, u_ref, g_ref, ys_ref, h_ref, ycc_ref, yatt_ref, d_ref, wglu_ref, bglu_ref,
                   wout_ref, g1_ref, b1_ref, o_ref, *, alpha):
    w = u_ref.shape[-1]
    y = ys_ref[...] + d_ref[...] * u_ref[...]
    z = _dot(jax.nn.gelu(y).astype(BF16), wglu_ref[...]) + bglu_ref[...]
    y_s5 = z[:, :w] * jax.nn.sigmoid(z[:, w:])
    y_lru = h_ref[...] * jax.nn.gelu(g_ref[...])
    acc = _dot(y_s5.astype(BF16), wout_ref[0:w, :])
    acc = acc + _dot(y_lru.astype(BF16), wout_ref[w:2 * w, :])
    acc = acc + _dot(ycc_ref[...].astype(BF16), wout_ref[2 * w:3 * w, :])
    acc = acc + _dot(yatt_ref[...].astype(BF16), wout_ref[3 * w:4 * w, :])
    o_ref[...] = _layer_norm(alpha * x_ref[...] + acc, g1_ref[...], b1_ref[...])


def _mixout(x2, zmix, ys, h, ycc, yatt, d, wglu, bglu, wout, g1, b1, tm, alpha):
    m, dm = x2.shape
    w = ys.shape[1]
    row = lambda a: pl.BlockSpec((tm, a.shape[1]), lambda i: (i, 0))
    full = lambda a: pl.BlockSpec(a.shape, lambda i: (0, 0))
    zcol = lambda c: pl.BlockSpec((tm, w), lambda i, c=c: (i, c))
    return pl.pallas_call(
        functools.partial(_mixout_kernel, alpha=alpha),
        grid=(m // tm,),
        in_specs=[row(x2), zcol(0), zcol(2), row(ys), row(h), row(ycc), row(yatt), full(d), full(wglu),
                  full(bglu), full(wout), full(g1), full(b1)],
        out_specs=pl.BlockSpec((tm, dm), lambda i: (i, 0)),
        out_shape=jax.ShapeDtypeStruct((m, dm), F32),
        compiler_params=_params("parallel"),
        name="mixout",
    )(x2, zmix, zmix, ys, h, ycc, yatt, d, wglu, bglu, wout, g1, b1)


def _ffn_kernel(h_ref, wg_ref, wu_ref, wd_ref, g2_ref, b2_ref, o_ref, acc_s, hb_s, *, alpha):
    f = pl.program_id(1)

    @pl.when(f == 0)
    def _():
        acc_s[...] = jnp.zeros(acc_s.shape, F32)
        hb_s[...] = h_ref[...].astype(BF16)

    hb = hb_s[...]
    a = _dot(hb, wg_ref[...])
    act = (a * jax.nn.sigmoid(a)) * _dot(hb, wu_ref[...])
    acc_s[...] += _dot(act.astype(BF16), wd_ref[...])

    @pl.when(f == pl.num_programs(1) - 1)
    def _():
        o_ref[...] = _layer_norm(alpha * h_ref[...] + acc_s[...], g2_ref[...], b2_ref[...])


def _ffn(h, wg, wu, wd, g2, b2, tm, tf, alpha):
    m, dm = h.shape
    nf = wg.shape[1] // tf
    return pl.pallas_call(
        functools.partial(_ffn_kernel, alpha=alpha),
        grid=(m // tm, nf),
        in_specs=[pl.BlockSpec((tm, dm), lambda i, f: (i, 0)),
                  pl.BlockSpec((dm, tf), lambda i, f: (0, f)),
                  pl.BlockSpec((dm, tf), lambda i, f: (0, f)),
                  pl.BlockSpec((tf, dm), lambda i, f: (f, 0)),
                  pl.BlockSpec(g2.shape, lambda i, f: (0, 0)),
                  pl.BlockSpec(b2.shape, lambda i, f: (0, 0))],
        out_specs=pl.BlockSpec((tm, dm), lambda i, f: (i, 0)),
        out_shape=jax.ShapeDtypeStruct((m, dm), F32),
        scratch_shapes=[pltpu.VMEM((tm, dm), F32), pltpu.VMEM((tm, dm), BF16)],
        compiler_params=_params("parallel", "arbitrary"),
        name="ffn",
    )(h, wg, wu, wd, g2, b2)


MOE_CHUNK_ROWS = 288
MOE_COL_TILE = 1792
MOE_ROW_TILE = 1024
FFN_COL_TILE = 256
FFN_ROW_TILE = 1024


def _moe_kernel(h_ref, wrh_ref, wrl_ref, br_ref, wg_ref, wu_ref, wd_ref, g2_ref, b2_ref, o_ref,
                hb_s, xg_s, y_s, gate_c, slot_c, gate_b, slot_b_s, slot_t_s, *, alpha, n_exp, cr):
    e = pl.program_id(1)
    f = pl.program_id(2)
    nf = pl.num_programs(2)
    tm, dm = h_ref.shape
    first = jnp.logical_and(e == 0, f == 0)
    last = jnp.logical_and(e == pl.num_programs(1) - 1, f == nf - 1)

    @pl.when(first)
    def _():
        hhi, hlo = _split3(h_ref[...])
        hb_s[...] = hhi
        logits = _dot(hhi, wrh_ref[...]) + _dot(hhi, wrl_ref[...]) + _dot(hlo, wrh_ref[...]) + br_ref[...]
        lane = lax.broadcasted_iota(jnp.int32, logits.shape, 1)
        m1 = jnp.max(logits, axis=1, keepdims=True)
        i1 = jnp.min(jnp.where(logits == m1, lane, LANES), axis=1, keepdims=True)
        rest = jnp.where(lane == i1, -jnp.inf, logits)
        m2 = jnp.max(rest, axis=1, keepdims=True)
        i2 = jnp.min(jnp.where(rest == m2, lane, LANES), axis=1, keepdims=True)
        e2 = jnp.exp(m2 - m1)
        den = 1.0 + e2
        gate = jnp.where(lane == i1, 1.0 / den, 0.0) + jnp.where(lane == i2, e2 / den, 0.0)
        routed = jnp.logical_or(lane == i1, lane == i2)
        ind = jnp.where(routed, 1.0, 0.0).astype(BF16)
        tri = jnp.where(lax.broadcasted_iota(jnp.int32, (tm, tm), 0) >= lax.broadcasted_iota(jnp.int32, (tm, tm), 1),
                        1.0, 0.0).astype(BF16)
        slot = jnp.where(routed, _dot(tri, ind) - 1.0, -1.0)
        eye = jnp.where(lax.broadcasted_iota(jnp.int32, (LANES, LANES), 0)
                        == lax.broadcasted_iota(jnp.int32, (LANES, LANES), 1), 1.0, 0.0).astype(BF16)
        ind_t = _dot_t(eye, ind)
        slot_t = jnp.where(ind_t > 0.0, _dot_t(ind_t.astype(BF16), tri) - 1.0, -1.0)
        slot_t_s[...] = slot_t[0:slot_t_s.shape[0], :]
        gate_c[...] = gate
        slot_c[...] = slot
        o_ref[...] = jnp.zeros(o_ref.shape, F32)

    @pl.when(f == 0)
    def _():
        lane = lax.broadcasted_iota(jnp.int32, gate_c.shape, 1)
        pick = lambda x: jnp.sum(jnp.where(lane == e, x, 0.0), axis=1, keepdims=True)
        gate_b[...] = jnp.broadcast_to(pick(gate_c[...]), gate_b.shape)
        slot_b_s[...] = jnp.broadcast_to(pick(slot_c[...]), slot_b_s.shape)

    slot_b = slot_b_s[...]
    n_pass = ((jnp.max(slot_b) + 1.0).astype(jnp.int32) + cr - 1) // cr

    def pass_rows(r):
        return pl.ds(pl.multiple_of(r * cr, 16), cr)

    @pl.when(f == 0)
    def _():
        srow = slot_t_s[pl.ds(e, 1), :]

        def pack(r, _):
            want = (r * cr + lax.broadcasted_iota(jnp.int32, (cr, tm), 0)).astype(F32)
            onehot = jnp.where(srow == want, 1.0, 0.0).astype(BF16)
            xg_s[pass_rows(r), :] = _dot(onehot, hb_s[...]).astype(BF16)
            return 0
        lax.fori_loop(0, n_pass, pack, 0)

    def expert_pass(r, _):
        x = xg_s[pass_rows(r), :]
        a = _dot(x, wg_ref[...])
        act = (a * jax.nn.sigmoid(a)) * _dot(x, wu_ref[...])
        y = _dot(act.astype(BF16), wd_ref[...])

        @pl.when(f == 0)
        def _():
            y_s[pass_rows(r), :] = y

        @pl.when(f > 0)
        def _():
            y_s[pass_rows(r), :] = y_s[pass_rows(r), :] + y
        return 0
    lax.fori_loop(0, n_pass, expert_pass, 0)

    @pl.when(f == nf - 1)
    def _():
        reps = (cr + LANES - 1) // LANES
        slot_w = jnp.concatenate([slot_b] * reps, axis=1)[:, :cr]
        gate_w = jnp.concatenate([gate_b[...]] * (dm // LANES), axis=1)

        def unpack(r, _):
            want = (r * cr + lax.broadcasted_iota(jnp.int32, (tm, cr), 1)).astype(F32)
            onehot = jnp.where(slot_w == want, 1.0, 0.0).astype(BF16)
            o_ref[...] += gate_w * _dot(onehot, y_s[pass_rows(r), :].astype(BF16))
            return 0
        lax.fori_loop(0, n_pass, unpack, 0)

    @pl.when(last)
    def _():
        o_ref[...] = _layer_norm(alpha * h_ref[...] + o_ref[...], g2_ref[...], b2_ref[...])


def _moe(h, wrh, wrl, br, wg, wu, wd, g2, b2, tm, tf, alpha):
    m, dm = h.shape
    n_exp, _, dff = wg.shape
    nf = dff // tf
    cr = min(MOE_CHUNK_ROWS, tm)
    cap = -(-tm // cr) * cr
    c2 = lambda a: pl.BlockSpec(a.shape, lambda i, e, f: (0, 0))
    return pl.pallas_call(
        functools.partial(_moe_kernel, alpha=alpha, n_exp=n_exp, cr=cr),
        grid=(m // tm, n_exp, nf),
        in_specs=[pl.BlockSpec((tm, dm), lambda i, e, f: (i, 0), pipeline_mode=pl.Buffered(1)),
                  c2(wrh), c2(wrl), c2(br),
                  pl.BlockSpec((None, dm, tf), lambda i, e, f: (e, 0, f)),
                  pl.BlockSpec((None, dm, tf), lambda i, e, f: (e, 0, f)),
                  pl.BlockSpec((None, tf, dm), lambda i, e, f: (e, f, 0)),
                  c2(g2), c2(b2)],
        out_specs=pl.BlockSpec((tm, dm), lambda i, e, f: (i, 0)),
        out_shape=jax.ShapeDtypeStruct((m, dm), F32),
        scratch_shapes=[pltpu.VMEM((tm, dm), BF16),
                        pltpu.VMEM((cap, dm), BF16),
                        pltpu.VMEM((cap, dm), F32),
                        pltpu.VMEM((tm, LANES), F32),
                        pltpu.VMEM((tm, LANES), F32),
                        pltpu.VMEM((tm, LANES), F32),
                        pltpu.VMEM((tm, LANES), F32),
                        pltpu.VMEM((max(8, n_exp), tm), F32)],
        compiler_params=_params("parallel", "arbitrary", "arbitrary"),
        name="moe",
    )(h, wrh, wrl, br, wg, wu, wd, g2, b2)


def _row_tile(m, target):
    t = math.gcd(m, target)
    return t if t % 8 == 0 else m


def _chunk_len(t):
    return math.gcd(t, 128)


def _rope_tables(pos, reps):
    half = HEAD_DIM // 2
    inv = ROPE_THETA ** (-(jnp.arange(half, dtype=F32) / half))
    ang = pos.astype(F32)[:, None] * inv[None, :]
    cos, sin = jnp.cos(ang), jnp.sin(ang)
    cos_h = jnp.concatenate([cos, cos], axis=-1)
    sin_h = jnp.concatenate([-sin, sin], axis=-1)
    return jnp.tile(cos_h, (1, reps)), jnp.tile(sin_h, (1, reps))


def _s5_params(a_re, a_im, b_re, b_im, c_re, c_im, log_dt):
    ng, ns = a_re.shape
    lr, li = a_re.astype(F32), a_im.astype(F32)
    dt = jnp.exp(log_dt.astype(F32))[:, None]
    mag = jnp.exp(lr * dt)
    ar, ai = mag * jnp.cos(li * dt), mag * jnp.sin(li * dt)
    den = lr * lr + li * li
    qr = ((ar - 1.0) * lr + ai * li) / den
    qi = (ai * lr - (ar - 1.0) * li) / den
    br, bi = b_re.astype(F32), b_im.astype(F32)
    bbr = qr[..., None] * br - qi[..., None] * bi
    bbi = qr[..., None] * bi + qi[..., None] * br
    eye = jnp.eye(ng, dtype=F32)
    gpl = LANES // ns
    nslab = ng // gpl

    def b_mat(x):
        full = jnp.einsum('gpc,gh->gchp', x, eye).reshape(ng * S5_CH, ng * ns)
        return jnp.transpose(full.reshape(ng * S5_CH, nslab, LANES), (1, 0, 2)).astype(BF16)

    def c_mat(x):
        return jnp.einsum('gcp,gh->gphc', x, eye).reshape(nslab, LANES, ng * S5_CH)

    ccat = jnp.concatenate([c_mat(c_re.astype(F32)), -c_mat(c_im.astype(F32))], axis=1).astype(BF16)
    return b_mat(bbr), b_mat(bbi), ar.reshape(nslab, 1, LANES), ai.reshape(nslab, 1, LANES), ccat


def _to_chunks(x, nl):
    nb, t, w = x.shape
    return jnp.swapaxes(x.reshape(nb, t // nl, nl, w), 1, 2)


def _from_chunks(x):
    nb, nl, nr, w = x.shape
    return jnp.swapaxes(x, 1, 2).reshape(nb, nr * nl, w)


def _mixers(lp, zmix, nb, t, s5_re0, s5_im0, lru_h0, lru_buf0, cc_buf0, chained):
    w = lp['gw']
    zmix3 = zmix.reshape(nb, t, zmix.shape[-1])
    tt = math.gcd(t, 512)
    a, b, ycc, lru_buf, cc_buf = _convmix(zmix3, lru_buf0, cc_buf0, lp['lru_conv_w'], lp['lru_conv_b'],
                                          lp['lru_wg'], lp['lru_bg'], lp['lru_sp'], lp['cc_dw_w'], lp['cc_dw_b'],
                                          lp['cc_ln_g'], lp['cc_ln_b'], tt)
    u = zmix3[:, :, :w].astype(BF16)
    ns = s5_re0.shape[-2] * s5_re0.shape[-1]
    if chained:
        nl = _chunk_len(t)
        a4, b4, u4 = _to_chunks(a, nl), _to_chunks(b, nl), _to_chunks(u, nl)
        h0 = lru_h0.reshape(nb, 1, w)
        s0r, s0i = s5_re0.reshape(nb, 1, ns), s5_im0.reshape(nb, 1, ns)
    else:
        to_rows = lambda x: jnp.swapaxes(x, 0, 1)[None]
        a4, b4, u4 = to_rows(a), to_rows(b), to_rows(u)
        h0 = lru_h0.reshape(1, nb, w)
        s0r, s0i = s5_re0.reshape(1, nb, ns), s5_im0.reshape(1, nb, ns)
    h4, lru_h = _lru_scan(a4, b4, h0, chained)
    ys4, s5_re, s5_im = _s5_scan(u4, lp['s5_bre'], lp['s5_bim'], lp['s5_are'], lp['s5_aim'], lp['s5_ccat'],
                                 s0r, s0i, chained)
    if chained:
        h, ys = _from_chunks(h4), _from_chunks(ys4)
    else:
        h, ys = jnp.swapaxes(h4[0], 0, 1), jnp.swapaxes(ys4[0], 0, 1)
    states = (s5_re.reshape(s5_re0.shape), s5_im.reshape(s5_im0.shape), lru_h.reshape(lru_h0.shape),
              lru_buf, cc_buf)
    return ys.reshape(nb * t, w), h.reshape(nb * t, w), ycc.reshape(nb * t, w), states


def _idx3(x, order):
    hi, lo = _split3(x)
    parts = [hi if o == 'h' else lo for o in order]
    pad = jnp.zeros(x.shape[:-1] + (2 * LANES - len(order) * x.shape[-1],), BF16)
    return jnp.concatenate(parts + [pad], axis=-1)


def _attend_prompt(q, qi, k, v, kw, nb, t):
    w = q.shape[-1]
    topk = min(TOPK_MAX, t // 4)
    to_cols = lambda x: jnp.swapaxes(x.reshape(nb, t, -1), 1, 2)
    qt = to_cols(q)
    qih, qil = _split3(qi)
    wi = kw[:, IDX_DIM:IDX_DIM + IDX_HEADS].reshape(nb, t, IDX_HEADS)
    wit = jnp.pad(jnp.swapaxes(wi, 1, 2), ((0, 0), (0, 8 - IDX_HEADS), (0, 0)))
    kidx3 = _idx3(kw[:, :IDX_DIM].reshape(nb, t, IDX_DIM), 'hlh')
    kb = k.reshape(nb, t, w).astype(BF16)
    vt = jnp.swapaxes(v.reshape(nb, t, w), 1, 2).astype(BF16)
    kc = math.gcd(t // 2, 512)
    yt = _dsa_prompt(qt, to_cols(qih), to_cols(qil), wit, kidx3, kb, vt, topk, kc)
    return jnp.swapaxes(yt, 1, 2).reshape(nb * t, w)


def _attend_sample(layer, q, qi, k, v, kw, nb, t, pool_k, pool_v, pool_kidx, page_table):
    w = q.shape[-1]
    n_heads = w // HEAD_DIM
    psz = pool_kidx.shape[2]
    past = page_table.shape[1] * psz
    topk = min(TOPK_MAX, (past + t) // 4)
    pad_q = lambda x: jnp.pad(x, ((0, 0), (0, 0), (0, QROWS - t), (0, 0)))

    qi_h = jnp.swapaxes(qi.reshape(nb, t, IDX_HEADS, IDX_DIM), 1, 2)
    qhi, qlo = _split3(pad_q(qi_h).reshape(nb, IDX_HEADS * QROWS, IDX_DIM))
    wi = kw[:, IDX_DIM:IDX_DIM + IDX_HEADS].reshape(nb, t, IDX_HEADS)
    wi_b = jnp.broadcast_to(pad_q(jnp.swapaxes(wi, 1, 2)[..., None]).reshape(nb, IDX_HEADS * QROWS, 1),
                            (nb, IDX_HEADS * QROWS, psz))
    new_page_t = lambda x: jnp.swapaxes(jnp.pad(x, ((0, 0), (0, psz - t), (0, 0))), 1, 2)
    kidx_new = new_page_t(kw[:, :IDX_DIM].reshape(nb, t, IDX_DIM))
    bias = _dsa_sample_select(layer, page_table, qhi, qlo, wi_b, kidx_new, jnp.swapaxes(pool_kidx, 2, 3),
                              topk, t)

    head_of_lane = jnp.arange(w) // HEAD_DIM
    q_rows = jnp.broadcast_to(pad_q(q.reshape(nb, 1, t, w)), (nb, n_heads, QROWS, w))
    q32 = jnp.where(head_of_lane[None, None, None, :] == jnp.arange(n_heads)[None, :, None, None],
                    q_rows, jnp.zeros_like(q_rows)).reshape(nb, n_heads * QROWS, w)
    k_new = new_page_t(k.reshape(nb, t, w))
    v_new = new_page_t(v.reshape(nb, t, w))
    pool_t = lambda x: jnp.transpose(x, (0, 1, 3, 4, 2)).reshape(x.shape[:2] + (w, psz))
    y = _dsa_sample_attend(layer, page_table, q32, k_new, v_new, bias, pool_t(pool_k), pool_t(pool_v))
    return y[:, :t, :].reshape(nb * t, w)


def _prep_layer(l, w_in, s5_a_re, s5_a_im, s5_b_re, s5_b_im, s5_c_re, s5_c_im, s5_d, s5_log_dt, s5_w_glu,
                s5_b_glu, lru_conv_w, lru_conv_b, lru_w_a, lru_b_a, lru_w_x, lru_b_x, lru_lambda, cc_dw_w,
                cc_dw_b, cc_ln_g, cc_ln_b, w_out, ln1_g, ln1_b, ln2_g, ln2_b):
    gw = s5_d.shape[1]
    d_in = w_in.shape[2]
    mix_w = 5 * gw
    pad_to = mix_w + 4 * gw + LANES
    assert d_in == mix_w + 4 * gw + IDX_DIM + IDX_HEADS and gw == IDX_HEADS * IDX_DIM
    row = lambda x: x[l].reshape(1, -1).astype(F32)
    nh, hd, _ = lru_w_a.shape[1:]
    eye = jnp.eye(nh, dtype=F32)
    bd = lambda wt: jnp.einsum('hij,hk->hikj', wt[l].astype(F32), eye).reshape(nh * hd, nh * hd)
    s5_bre, s5_bim, s5_are, s5_aim, s5_ccat = _s5_params(s5_a_re[l], s5_a_im[l], s5_b_re[l], s5_b_im[l],
                                                         s5_c_re[l], s5_c_im[l], s5_log_dt[l])
    return dict(
        gw=gw, mix_w=mix_w,
        w_in=jnp.pad(w_in[l], ((0, 0), (0, pad_to - d_in))).astype(BF16),
        s5_bre=s5_bre, s5_bim=s5_bim, s5_are=s5_are, s5_aim=s5_aim, s5_ccat=s5_ccat,
        s5_d=row(s5_d), s5_w_glu=s5_w_glu[l].astype(BF16), s5_b_glu=row(s5_b_glu),
        lru_conv_w=lru_conv_w[l].astype(F32), lru_conv_b=row(lru_conv_b),
        lru_wg=jnp.concatenate([bd(lru_w_a), bd(lru_w_x)], axis=1).astype(BF16),
        lru_bg=jnp.concatenate([row(lru_b_a), row(lru_b_x)], axis=1),
        lru_sp=jax.nn.softplus(-row(lru_lambda)),
        cc_dw_w=cc_dw_w[l].astype(F32), cc_dw_b=row(cc_dw_b), cc_ln_g=row(cc_ln_g), cc_ln_b=row(cc_ln_b),
        w_out=w_out[l].astype(BF16), ln1_g=row(ln1_g), ln1_b=row(ln1_b), ln2_g=row(ln2_g), ln2_b=row(ln2_b),
    )


def kernel(x_prompt, x_sample, cache_k, cache_v, cache_kidx, state_s5_re, state_s5_im, state_lru_h, state_lru_conv, state_cc_conv, page_table, w_in, s5_a_re, s5_a_im, s5_b_re, s5_b_im, s5_c_re, s5_c_im, s5_d, s5_log_dt, s5_w_glu, s5_b_glu, lru_conv_w, lru_conv_b, lru_w_a, lru_b_a, lru_w_x, lru_b_x, lru_lambda, cc_dw_w, cc_dw_b, cc_ln_g, cc_ln_b, w_out, ln1_g, ln1_b, ln2_g, ln2_b, ffn_w_gate, ffn_w_up, ffn_w_down, moe_w_router, moe_b_router, moe_w_gate, moe_w_up, moe_w_down):
    bp, tp, dm = x_prompt.shape
    bs, ts, _ = x_sample.shape
    depth = w_in.shape[0]
    past = page_table.shape[1] * cache_k.shape[2]
    alpha = (2.0 * depth) ** 0.25
    n_heads = cache_k.shape[3]
    n_exp = moe_w_router.shape[-1]
    gw = s5_d.shape[1]

    cos_p, sin_p = _rope_tables(jnp.arange(tp, dtype=jnp.int32), gw // HEAD_DIM)
    cos_s, sin_s = _rope_tables(past + jnp.arange(ts, dtype=jnp.int32), gw // HEAD_DIM)
    cos_s, sin_s = jnp.tile(cos_s, (bs, 1)), jnp.tile(sin_s, (bs, 1))

    tm_p = _row_tile(tp, 512)
    xp = x_prompt.reshape(bp * tp, dm)
    xs = x_sample.reshape(bs * ts, dm)
    zeros_p = lambda *shape: jnp.zeros((bp,) + shape, x_prompt.dtype)
    states_p, states_s = [], []

    for l in range(depth):
        lp = _prep_layer(l, w_in, s5_a_re, s5_a_im, s5_b_re, s5_b_im, s5_c_re, s5_c_im, s5_d, s5_log_dt,
                         s5_w_glu, s5_b_glu, lru_conv_w, lru_conv_b, lru_w_a, lru_b_a, lru_w_x, lru_b_x,
                         lru_lambda, cc_dw_w, cc_dw_b, cc_ln_g, cc_ln_b, w_out, ln1_g, ln1_b, ln2_g, ln2_b)
        mix_w = lp['mix_w']
        j = l // 2
        if l % 2 == 0:
            tf = math.gcd(ffn_w_gate.shape[2], FFN_COL_TILE)
            cw = (ffn_w_gate[j].astype(BF16), ffn_w_up[j].astype(BF16), ffn_w_down[j].astype(BF16))
        else:
            wr = jnp.pad(moe_w_router[j].astype(F32), ((0, 0), (0, LANES - n_exp)))
            wrh, wrl = _split3(wr)
            br = jnp.pad(moe_b_router[j].astype(F32), (0, LANES - n_exp), constant_values=NEG_BIG).reshape(1, LANES)
            tf = math.gcd(moe_w_gate.shape[3], MOE_COL_TILE)
            cw = (wrh, wrl, br, moe_w_gate[j].astype(BF16), moe_w_up[j].astype(BF16), moe_w_down[j].astype(BF16))

        def channel(h):
            if l % 2 == 0:
                return _ffn(h, *cw, lp['ln2_g'], lp['ln2_b'], _row_tile(h.shape[0], FFN_ROW_TILE), tf, alpha)
            return _moe(h, *cw, lp['ln2_g'], lp['ln2_b'], _row_tile(h.shape[0], MOE_ROW_TILE), tf, alpha)

        zmix, q, qi, k, v, kw = _in_proj(xp, lp['w_in'], cos_p, sin_p, tm_p, mix_w, gw)
        ys, h, ycc, st = _mixers(lp, zmix, bp, tp,
                                 zeros_p(*state_s5_re.shape[2:]), zeros_p(*state_s5_im.shape[2:]),
                                 zeros_p(*state_lru_h.shape[2:]), zeros_p(*state_lru_conv.shape[2:]),
                                 zeros_p(*state_cc_conv.shape[2:]), chained=True)
        yatt = _attend_prompt(q, qi, k, v, kw, bp, tp)
        h1 = _mixout(xp, zmix, ys, h, ycc, yatt, lp['s5_d'], lp['s5_w_glu'], lp['s5_b_glu'], lp['w_out'],
                     lp['ln1_g'], lp['ln1_b'], tm_p, alpha)
        xp = channel(h1)
        states_p.append((k.reshape(bp, tp, n_heads, HEAD_DIM), v.reshape(bp, tp, n_heads, HEAD_DIM),
                         kw[:, :IDX_DIM].reshape(bp, tp, IDX_DIM)) + st)

        ms = bs * ts
        zmix, q, qi, k, v, kw = _in_proj(xs, lp['w_in'], cos_s, sin_s, ms, mix_w, gw)
        ys, h, ycc, st = _mixers(lp, zmix, bs, ts, state_s5_re[l], state_s5_im[l], state_lru_h[l],
                                 state_lru_conv[l], state_cc_conv[l], chained=False)
        yatt = _attend_sample(l, q, qi, k, v, kw, bs, ts, cache_k, cache_v, cache_kidx, page_table)
        h1 = _mixout(xs, zmix, ys, h, ycc, yatt, lp['s5_d'], lp['s5_w_glu'], lp['s5_b_glu'], lp['w_out'],
                     lp['ln1_g'], lp['ln1_b'], ms, alpha)
        xs = channel(h1)
        states_s.append((k.reshape(bs, ts, n_heads, HEAD_DIM), v.reshape(bs, ts, n_heads, HEAD_DIM),
                         kw[:, :IDX_DIM].reshape(bs, ts, IDX_DIM)) + st)

    new_p = [jnp.stack(col) for col in zip(*states_p)]
    new_s = [jnp.stack(col) for col in zip(*states_s)]
    out = [xp.reshape(bp, tp, dm), xs.reshape(bs, ts, dm)]
    for a, b in zip(new_p, new_s):
        out += [a, b]
    return tuple(out)
```

```python
import functools
import math

import jax
import jax.numpy as jnp
from jax import lax
from jax.experimental import pallas as pl
from jax.experimental.pallas import tpu as pltpu

F32 = jnp.float32
BF16 = jnp.bfloat16

S5_CH = 16
LRU_C = 8.0
HEAD_DIM = 64
IDX_DIM = 64
IDX_HEADS = 4
TOPK_MAX = 256
Q_BLOCK = 128
ROPE_THETA = 10000.0
MOE_TOP_K = 2
LN_EPS = 1e-5
ATT_SCALE = HEAD_DIM ** -0.5
LOG2_E = math.log2(math.e)
IDX_SCALE = (IDX_DIM * IDX_HEADS) ** -0.5

LANES = 128
NEG_BIG = -1e30
VMEM_LIMIT = 56 * 1024 * 1024


def _params(*sem):
    return pltpu.CompilerParams(dimension_semantics=sem, vmem_limit_bytes=VMEM_LIMIT)


def _dot(a, b):
    return jnp.dot(a, b, preferred_element_type=F32)


def _dot_t(a, b):
    return lax.dot_general(a, b, (((1,), (1,)), ((), ())), preferred_element_type=F32)


def _split3(x):
    hi = x.astype(BF16)
    lo = (x - hi.astype(F32)).astype(BF16)
    return hi, lo


def _layer_norm(x, g, b):
    mu = jnp.mean(x, axis=-1, keepdims=True)
    xc = x - mu
    var = jnp.mean(xc * xc, axis=-1, keepdims=True)
    return xc * lax.rsqrt(var + LN_EPS) * g + b


def _rope(z, cos, sin):
    w = z.shape[-1]
    half = HEAD_DIM // 2
    lane = lax.broadcasted_iota(jnp.int32, z.shape, 1)
    first = (lane & (HEAD_DIM - 1)) < half
    partner = jnp.where(first, pltpu.roll(z, w - half, axis=1), pltpu.roll(z, half, axis=1))
    return z * cos + partner * sin


def _in_proj_kernel(x_ref, w_ref, cos_ref, sin_ref, zmix_ref, q_ref, qi_ref, k_ref, v_ref, kw_ref,
                    *, mix_w, gw):
    xb = x_ref[...].astype(BF16)
    cos = cos_ref[...]
    sin = sin_ref[...]

    def proj(lo, width):
        return _dot(xb, w_ref[:, lo:lo + width])

    zmix_ref[...] = proj(0, mix_w)
    o = mix_w
    q_ref[...] = (_rope(proj(o, gw), cos, sin) * (ATT_SCALE * LOG2_E)).astype(BF16)
    k_ref[...] = _rope(proj(o + gw, gw), cos, sin)
    v_ref[...] = proj(o + 2 * gw, gw)
    qi_ref[...] = _rope(proj(o + 3 * gw, gw), cos, sin)
    kw = proj(o + 4 * gw, LANES)
    is_key = lax.broadcasted_iota(jnp.int32, kw.shape, 1) < IDX_DIM
    kw_ref[...] = _rope(kw, jnp.where(is_key, cos[:, :LANES], 1.0), jnp.where(is_key, sin[:, :LANES], 0.0))


def _in_proj(x2, w_pad, cos_t, sin_t, tm, mix_w, gw):
    m, d = x2.shape
    tab_blocks = cos_t.shape[0] // tm
    row = lambda i: (i, 0)
    tab = lambda i: (i % tab_blocks, 0)
    outs = [
        jax.ShapeDtypeStruct((m, mix_w), F32),
        jax.ShapeDtypeStruct((m, gw), BF16),
        jax.ShapeDtypeStruct((m, gw), F32),
        jax.ShapeDtypeStruct((m, gw), F32),
        jax.ShapeDtypeStruct((m, gw), F32),
        jax.ShapeDtypeStruct((m, LANES), F32),
    ]
    return pl.pallas_call(
        functools.partial(_in_proj_kernel, mix_w=mix_w, gw=gw),
        grid=(m // tm,),
        in_specs=[pl.BlockSpec((tm, d), row),
                  pl.BlockSpec(w_pad.shape, lambda i: (0, 0)),
                  pl.BlockSpec((tm, gw), tab),
                  pl.BlockSpec((tm, gw), tab)],
        out_specs=[pl.BlockSpec((tm, s.shape[1]), row) for s in outs],
        out_shape=outs,
        compiler_params=_params("parallel"),
        name="in_proj",
    )(x2, w_pad, cos_t, sin_t)


LRU_HALO = 8
CC_HALO = 32


SUBLANES = 8


def _tap_sum(x_s, first, taps, tt, w_ref):
    n = x_s.shape[0]
    if n % SUBLANES or tt % SUBLANES:
        acc = x_s[first:first + tt, :] * w_ref[0:1, :]
        for j in range(1, taps):
            acc = acc + x_s[first + j:first + j + tt, :] * w_ref[j:j + 1, :]
        return acc
    full = x_s[...]
    acc = None
    for phase in range(SUBLANES):
        js = [j for j in range(taps) if (first + j) % SUBLANES == phase]
        if not js:
            continue
        z = full if phase == 0 else pltpu.roll(full, n - phase, axis=0)
        for j in js:
            a0 = first + j - phase
            term = z[a0:a0 + tt, :] * w_ref[j:j + 1, :]
            acc = term if acc is None else acc + term
    return acc


def _convmix_kernel(xl_ref, ca_ref, cg_ref, lbuf_ref, cbuf_ref, lw_ref, lb_ref, wg_ref, bg_ref, sp_ref,
                    cw_ref, cb_ref, lng_ref, lnb_ref,
                    a_ref, b_ref, ycc_ref, lbuf_o, cbuf_o, xl_s, xc_s, *, tt):
    i = pl.program_id(1)
    kl = lw_ref.shape[0]
    kc = cw_ref.shape[0]
    w = xl_ref.shape[-1]
    l0 = LRU_HALO - (kl - 1)
    c0 = CC_HALO - (kc - 1)

    @pl.when(i == 0)
    def _():
        xl_s[l0:LRU_HALO, :] = lbuf_ref[...]
        xc_s[c0:CC_HALO, :] = cbuf_ref[...]

    xl_s[LRU_HALO:LRU_HALO + tt, :] = xl_ref[...]
    xc_s[CC_HALO:CC_HALO + tt, :] = ca_ref[...] * jax.nn.sigmoid(cg_ref[...])

    conv = _tap_sum(xl_s, l0, kl, tt, lw_ref) + lb_ref[...]
    gates = _dot(conv.astype(BF16), wg_ref[...]) + bg_ref[...]
    r = jax.nn.sigmoid(gates[:, :w])
    ig = jax.nn.sigmoid(gates[:, w:])
    log_a = (-LRU_C) * r * sp_ref[...]
    a = jnp.exp(log_a)
    a_ref[...] = a
    b_ref[...] = jnp.sqrt(-jnp.tanh(log_a) * (a * a + 1.0)) * (ig * conv)

    c = _tap_sum(xc_s, c0, kc, tt, cw_ref)
    c = _layer_norm(c + cb_ref[...], lng_ref[...], lnb_ref[...])
    ycc_ref[...] = c * jax.nn.sigmoid(c)

    new_l = xl_s[l0 + tt:LRU_HALO + tt, :]
    new_c = xc_s[c0 + tt:CC_HALO + tt, :]
    xl_s[l0:LRU_HALO, :] = new_l
    xc_s[c0:CC_HALO, :] = new_c
    lbuf_o[...] = new_l
    cbuf_o[...] = new_c


def _convmix(zmix3, lbuf, cbuf, lw, lb, wg, bg, sp, cw, cb, lng, lnb, tt):
    nb, t, _ = zmix3.shape
    w = lw.shape[1]
    col = lambda c: pl.BlockSpec((None, tt, w), lambda b, i, c=c: (b, i, c))
    full2 = lambda a: pl.BlockSpec(a.shape, lambda b, i: (0, 0))
    per_b = lambda a: pl.BlockSpec((None,) + a.shape[1:], lambda b, i: (b, 0, 0))
    seq_out = jax.ShapeDtypeStruct((nb, t, w), F32)
    return pl.pallas_call(
        functools.partial(_convmix_kernel, tt=tt),
        grid=(nb, t // tt),
        in_specs=[col(1), col(3), col(4), per_b(lbuf), per_b(cbuf), full2(lw), full2(lb), full2(wg),
                  full2(bg), full2(sp), full2(cw), full2(cb), full2(lng), full2(lnb)],
        out_specs=[pl.BlockSpec((None, tt, w), lambda b, i: (b, i, 0))] * 3 + [per_b(lbuf), per_b(cbuf)],
        out_shape=[seq_out, seq_out, seq_out,
                   jax.ShapeDtypeStruct(lbuf.shape, F32), jax.ShapeDtypeStruct(cbuf.shape, F32)],
        scratch_shapes=[pltpu.VMEM((LRU_HALO + tt, w), F32), pltpu.VMEM((CC_HALO + tt, w), F32)],
        compiler_params=_params("parallel", "arbitrary"),
        name="convmix",
    )(zmix3, zmix3, zmix3, lbuf, cbuf, lw, lb, wg, bg, sp, cw, cb, lng, lnb)


def _lru_scan_kernel(a_ref, b_ref, h0_ref, h_ref, hl_ref, e_s, p_s, hi_s, *, chained):
    nl, nr, _ = a_ref.shape

    def run(init):
        def step(t, h):
            h = a_ref[t] * h + b_ref[t]
            h_ref[t] = h
            return h
        return lax.fori_loop(0, nl, step, init)

    if not chained:
        hl_ref[...] = run(h0_ref[...])
        return

    def local(t, c):
        h, p = c
        at = a_ref[t]
        return at * h + b_ref[t], at * p
    zeros = jnp.zeros(e_s.shape, F32)
    e, p = lax.fori_loop(0, nl, local, (zeros, zeros + 1.0))
    e_s[...] = e
    p_s[...] = p
    hi_s[0:1, :] = h0_ref[...]

    def carry(c, _):
        prev = pl.ds(c - 1, 1)
        hi_s[pl.ds(c, 1), :] = p_s[prev, :] * hi_s[prev, :] + e_s[prev, :]
        return 0
    lax.fori_loop(1, nr, carry, 0)
    last = run(hi_s[...])
    hl_ref[...] = last[nr - 1:nr, :]


def _lru_scan(a4, b4, h0, chained):
    nb, nl, nr, w = a4.shape
    r0 = h0.shape[1]
    nw = w // LANES
    blk = pl.BlockSpec((None, nl, nr, LANES), lambda n, j: (n, 0, 0, j))
    st = pl.BlockSpec((None, r0, LANES), lambda n, j: (n, 0, j))
    return pl.pallas_call(
        functools.partial(_lru_scan_kernel, chained=chained),
        grid=(nb, nw),
        in_specs=[blk, blk, st],
        out_specs=[blk, st],
        out_shape=[jax.ShapeDtypeStruct(a4.shape, F32), jax.ShapeDtypeStruct(h0.shape, F32)],
        scratch_shapes=[pltpu.VMEM((nr, LANES), F32)] * 3,
        compiler_params=_params("parallel", "parallel"),
        name="lru_scan",
    )(a4, b4, h0)


def _s5_scan_kernel(u_ref, bre_ref, bim_ref, are_ref, aim_ref, cc_ref, h0r_ref, h0i_ref,
                    ys_ref, hlr_ref, hli_ref, br_s, bi_s, er_s, ei_s, hr_s, hi_s, *, chained, rows_per_dot):
    g = pl.program_id(1)
    nl, nr, _ = u_ref.shape
    n = nl * nr
    ar = are_ref[...]
    ai = aim_ref[...]

    b_cat = jnp.concatenate([bre_ref[...], bim_ref[...]], axis=1)

    def drive(c, _):
        r0 = pl.multiple_of(c * rows_per_dot, rows_per_dot)
        ub = u_ref[pl.ds(c * (rows_per_dot // nr), rows_per_dot // nr)].reshape(rows_per_dot, u_ref.shape[-1])
        d = _dot(ub, b_cat)
        br_s[pl.ds(r0, rows_per_dot), :] = d[:, :LANES]
        bi_s[pl.ds(r0, rows_per_dot), :] = d[:, LANES:]
        return 0
    lax.fori_loop(0, n // rows_per_dot, drive, 0)

    def step_fn(store):
        def step(t, c):
            hr, hi = c
            rows = pl.ds(pl.multiple_of(t * nr, nr), nr)
            nhr = ar * hr - ai * hi + br_s[rows, :]
            nhi = ar * hi + ai * hr + bi_s[rows, :]
            if store:
                br_s[rows, :] = nhr
                bi_s[rows, :] = nhi
            return nhr, nhi
        return step

    if chained:
        zeros = jnp.zeros((nr, LANES), F32)
        er, ei = lax.fori_loop(0, nl, step_fn(False), (zeros, zeros))
        er_s[...] = er
        ei_s[...] = ei

        def power(_, c):
            pr, pi = c
            return ar * pr - ai * pi, ar * pi + ai * pr
        alr, ali = lax.fori_loop(0, nl - 1, power, (ar, ai))
        hr_s[0:1, :] = h0r_ref[...]
        hi_s[0:1, :] = h0i_ref[...]

        def carry(c, _):
            prev = pl.ds(c - 1, 1)
            pr = hr_s[prev, :]
            pi = hi_s[prev, :]
            hr_s[pl.ds(c, 1), :] = alr * pr - ali * pi + er_s[prev, :]
            hi_s[pl.ds(c, 1), :] = alr * pi + ali * pr + ei_s[prev, :]
            return 0
        lax.fori_loop(1, nr, carry, 0)
        init = (hr_s[...], hi_s[...])
    else:
        init = (h0r_ref[...], h0i_ref[...])

    lr, li = lax.fori_loop(0, nl, step_fn(True), init)
    if chained:
        hlr_ref[...] = lr[nr - 1:nr, :]
        hli_ref[...] = li[nr - 1:nr, :]
    else:
        hlr_ref[...] = lr
        hli_ref[...] = li

    def project(c, _):
        r0 = pl.multiple_of(c * rows_per_dot, rows_per_dot)
        rows = pl.ds(r0, rows_per_dot)
        hcat = jnp.concatenate([br_s[rows, :], bi_s[rows, :]], axis=1).astype(BF16)
        y = _dot(hcat, cc_ref[...]).reshape(rows_per_dot // nr, nr, ys_ref.shape[-1])
        tsl = pl.ds(c * (rows_per_dot // nr), rows_per_dot // nr)

        @pl.when(g == 0)
        def _():
            ys_ref[tsl] = y

        @pl.when(g > 0)
        def _():
            ys_ref[tsl] = ys_ref[tsl] + y
        return 0
    lax.fori_loop(0, n // rows_per_dot, project, 0)


def _s5_scan(u4, bre, bim, are, aim, ccat, h0r, h0i, chained):
    nb, nl, nr, w = u4.shape
    ng = bre.shape[0]
    r0 = h0r.shape[1]
    n = nl * nr
    rows_per_dot = math.gcd(n, 1024)
    rows_per_dot = max(rows_per_dot, nr)
    seq = pl.BlockSpec((None, nl, nr, w), lambda b, g: (b, 0, 0, 0))
    per_g = lambda a: pl.BlockSpec((None,) + a.shape[1:], lambda b, g: (g, 0, 0))
    st = pl.BlockSpec((None, r0, LANES), lambda b, g: (b, 0, g))
    small = pltpu.VMEM((nr, LANES), F32)
    return pl.pallas_call(
        functools.partial(_s5_scan_kernel, chained=chained, rows_per_dot=rows_per_dot),
        grid=(nb, ng),
        in_specs=[seq, per_g(bre), per_g(bim), per_g(are), per_g(aim), per_g(ccat), st, st],
        out_specs=[seq, st, st],
        out_shape=[jax.ShapeDtypeStruct(u4.shape, F32), jax.ShapeDtypeStruct(h0r.shape, F32),
                   jax.ShapeDtypeStruct(h0i.shape, F32)],
        scratch_shapes=[pltpu.VMEM((n, LANES), F32), pltpu.VMEM((n, LANES), F32), small, small, small, small],
        compiler_params=_params("parallel", "arbitrary"),
        name="s5_scan",
    )(u4, bre, bim, are, aim, ccat, h0r, h0i)


BISECT_ITERS = 24
PROBE_QUERIES = 12
PROBE_GAP = 2


def _select_threshold(count_ge, max_below, mn, mx, kq, all_sel, probe_queries):
    zero = jnp.zeros(mn.shape, F32)
    hi0 = mx + (jnp.abs(mx) + 1.0)
    state = (mn, hi0, mn, jnp.where(all_sel, 1.0, zero), zero)

    def unfinished(st):
        return jnp.min(st[3]) == 0.0

    def bisect(c):
        it, (lo, hi, theta, done, tie) = c
        mid = jnp.where(hi == jnp.inf, mx, 0.5 * lo + 0.5 * hi)
        cnt = count_ge(mid)
        live = done == 0.0
        collapsed = jnp.logical_and(live, jnp.logical_or(mid == lo, mid == hi))
        hit = jnp.logical_and(live, cnt == kq)
        move = jnp.logical_and(live, jnp.logical_not(collapsed))
        theta = jnp.where(hit, mid, jnp.where(collapsed, hi, theta))
        tie = jnp.where(collapsed, 1.0, tie)
        done = jnp.where(jnp.logical_or(hit, collapsed), 1.0, done)
        lo = jnp.where(jnp.logical_and(move, cnt > kq), mid, lo)
        hi = jnp.where(jnp.logical_and(move, cnt < kq), mid, hi)
        return it + 1, (lo, hi, theta, done, tie)

    _, state = lax.while_loop(
        lambda c: jnp.logical_and(c[0] < BISECT_ITERS, jnp.sum(1.0 - c[1][3]) > float(probe_queries)),
        lambda c: bisect(bisect(c)), (jnp.int32(0), state))

    def exact_step(st):
        lo, hi, theta, done, tie = st
        live = done == 0.0
        tau = max_below(hi)
        cnt = count_ge(tau)
        found = jnp.logical_and(live, cnt >= kq)
        clean = jnp.logical_and(found, cnt == kq)
        theta = jnp.where(clean, tau, jnp.where(found, hi, theta))
        tie = jnp.where(jnp.logical_and(found, jnp.logical_not(clean)), 1.0, tie)
        lo = jnp.where(found, tau, lo)
        hi = jnp.where(jnp.logical_and(live, jnp.logical_not(found)), tau, hi)
        done = jnp.where(found, 1.0, done)
        return lo, hi, theta, done, tie

    def settle(st):
        st = exact_step(st)
        _, st = lax.while_loop(lambda c: jnp.logical_and(c[0] < PROBE_GAP, unfinished(c[1])),
                               bisect, (jnp.int32(0), st))
        return st

    lo, hi, theta, done, tie = lax.while_loop(unfinished, settle, state)
    return theta, lo, tie


COUNT_ROWS = 64


def _dsa_prompt_kernel(qt_ref, qih_ref, qil_ref, wit_ref, kidx_ref, k_ref, vt_ref, o_ref,
                       s_scr, lg_a, lg_b, qp_s, qip_s, *, topk, kc):
    i = pl.program_id(1)
    qb = Q_BLOCK
    n_pairs = qp_s.shape[0]
    pw = 2 * HEAD_DIM

    zq = jnp.zeros((HEAD_DIM, qb), BF16)
    for p in range(n_pairs):
        top = jnp.concatenate([qt_ref[p * pw:p * pw + HEAD_DIM, :], zq], axis=1)
        bot = jnp.concatenate([zq, qt_ref[p * pw + HEAD_DIM:(p + 1) * pw, :]], axis=1)
        qp_s[p] = jnp.concatenate([top, bot], axis=0)
        cols = []
        for h in (2 * p, 2 * p + 1):
            hi = qih_ref[h * IDX_DIM:(h + 1) * IDX_DIM, :]
            cols.append(jnp.concatenate([hi, hi, qil_ref[h * IDX_DIM:(h + 1) * IDX_DIM, :], zq], axis=0))
        qip_s[p] = jnp.concatenate(cols, axis=1)
    nch = ((i + 1) * qb + kc - 1) // kc
    qpos = i * qb + lax.broadcasted_iota(jnp.int32, (1, qb), 1)
    kiota = lax.broadcasted_iota(jnp.int32, (kc, qb), 0)

    def chunk_rows(c):
        return pl.ds(pl.multiple_of(c * kc, kc), kc)

    def score_chunk(c, carry, masked):
        mn, mx = carry
        kk = kidx_ref[chunk_rows(c), :]
        s = jnp.zeros((kc, qb), F32)
        for p in range(n_pairs):
            sp = _dot(kk, qip_s[p])
            for hh in range(2):
                h = 2 * p + hh
                s = s + jnp.maximum(sp[:, hh * qb:(hh + 1) * qb], 0.0) * wit_ref[h:h + 1, :]
        s = s * IDX_SCALE
        if not masked:
            s_scr[chunk_rows(c), :] = s
            return (jnp.minimum(mn, jnp.min(s, axis=0, keepdims=True)),
                    jnp.maximum(mx, jnp.max(s, axis=0, keepdims=True)))
        vis = (c * kc + kiota) <= qpos
        s_scr[chunk_rows(c), :] = jnp.where(vis, s, -jnp.inf)
        mn = jnp.minimum(mn, jnp.min(jnp.where(vis, s, jnp.inf), axis=0, keepdims=True))
        mx = jnp.maximum(mx, jnp.max(jnp.where(vis, s, -jnp.inf), axis=0, keepdims=True))
        return mn, mx
    nch2 = (nch + 1) // 2
    n_full = (i * qb + 1) // kc
    carry = lax.fori_loop(0, n_full, functools.partial(score_chunk, masked=False),
                          (jnp.full((1, qb), jnp.inf, F32), jnp.full((1, qb), -jnp.inf, F32)))
    mn, mx = lax.fori_loop(n_full, 2 * nch2, functools.partial(score_chunk, masked=True), carry)

    def fold(x, op):
        return op(x.reshape(kc // COUNT_ROWS, COUNT_ROWS, qb), axis=0)

    def count(pred_fn):
        def body(c, acc):
            hit = jnp.where(pred_fn(s_scr[chunk_rows(c), :], c * kc + kiota), 1.0, 0.0)
            return acc + fold(hit, jnp.sum)
        acc = lax.fori_loop(0, nch, body, jnp.zeros((COUNT_ROWS, qb), F32))
        return jnp.sum(acc, axis=0, keepdims=True)

    def max_below(x):
        def body(c, acc):
            s = s_scr[chunk_rows(c), :]
            return jnp.maximum(acc, fold(jnp.where(s < x, s, -jnp.inf), jnp.max))
        acc = lax.fori_loop(0, nch, body, jnp.full((COUNT_ROWS, qb), -jnp.inf, F32))
        return jnp.max(acc, axis=0, keepdims=True)

    n_vis = (qpos + 1).astype(F32)
    kq = jnp.minimum(n_vis, float(topk))
    theta, lo, tie = _select_threshold(lambda x: count(lambda s, kpos: s >= x), max_below,
                                       mn, mx, kq, n_vis <= float(topk), PROBE_QUERIES)

    @pl.when(jnp.max(tie) > 0.0)
    def _():
        need = kq - count(lambda s, kpos: s >= theta)
        tied = lambda s: jnp.logical_and(tie > 0.0, s == lo)
        tri = jnp.where(lax.broadcasted_iota(jnp.int32, (kc, kc), 0) >= lax.broadcasted_iota(jnp.int32, (kc, kc), 1),
                        1.0, 0.0).astype(BF16)

        def promote(c, seen):
            s = s_scr[chunk_rows(c), :]
            is_tied = tied(s)
            rank = seen + _dot(tri, jnp.where(is_tied, 1.0, 0.0).astype(BF16))
            keep = jnp.logical_and(is_tied, rank <= need)
            s_scr[chunk_rows(c), :] = jnp.where(keep, theta, s)
            return rank[kc - 1:kc, :]
        lax.fori_loop(0, nch, promote, jnp.zeros((1, qb), F32))

    n_heads = 2 * n_pairs
    kc2 = 2 * kc

    def chunk2_rows(c):
        return pl.ds(pl.multiple_of(jnp.minimum(c, nch2 - 1) * kc2, kc2), kc2)

    def put_logits(dst, c):
        for p in range(n_pairs):
            dst[p] = _dot(k_ref[chunk2_rows(c), p * pw:(p + 1) * pw], qp_s[p])

    def absorb(src, c, carry):
        ms, ls, accs = carry
        rows2 = chunk2_rows(c)
        sel = s_scr[rows2, :] >= jnp.where(c < nch2, theta, jnp.nan)
        new_m, new_l, new_acc = [], [], []
        for p in range(n_pairs):
            for hh in range(2):
                h = 2 * p + hh
                logit = jnp.where(sel, src[p, :, hh * qb:(hh + 1) * qb], NEG_BIG)
                m_new = jnp.maximum(ms[h], jnp.max(logit, axis=0, keepdims=True))
                alpha = jnp.exp2(ms[h] - m_new)
                pr = jnp.exp2(logit - m_new)
                new_m.append(m_new)
                new_l.append(alpha * ls[h] + jnp.sum(pr, axis=0, keepdims=True))
                pv = _dot(vt_ref[h * HEAD_DIM:(h + 1) * HEAD_DIM, rows2], pr.astype(BF16))
                new_acc.append(alpha * accs[h] + pv)
        return tuple(new_m), tuple(new_l), tuple(new_acc)

    def att_pair(j, carry):
        c = 2 * j
        put_logits(lg_b, c + 1)
        carry = absorb(lg_a, c, carry)
        put_logits(lg_a, c + 2)
        return absorb(lg_b, c + 1, carry)

    init = (tuple(jnp.full((1, qb), NEG_BIG, F32) for _ in range(n_heads)),
            tuple(jnp.zeros((1, qb), F32) for _ in range(n_heads)),
            tuple(jnp.zeros((HEAD_DIM, qb), F32) for _ in range(n_heads)))
    put_logits(lg_a, 0)
    _, ls, accs = lax.fori_loop(0, (nch2 + 1) // 2, att_pair, init)
    for h in range(n_heads):
        o_ref[h * HEAD_DIM:(h + 1) * HEAD_DIM, :] = accs[h] / ls[h]


def _dsa_prompt(qt, qiht, qilt, wit, kidx3, kb, vt, topk, kc):
    nb, w, t = vt.shape
    n_pairs = w // (2 * HEAD_DIM)
    per_b = lambda a: pl.BlockSpec((None,) + a.shape[1:], lambda b, i: (b, 0, 0))
    qcol = lambda a: pl.BlockSpec((None, a.shape[1], Q_BLOCK), lambda b, i: (b, 0, i))
    return pl.pallas_call(
        functools.partial(_dsa_prompt_kernel, topk=topk, kc=kc),
        grid=(nb, t // Q_BLOCK),
        in_specs=[qcol(qt), qcol(qiht), qcol(qilt), qcol(wit), per_b(kidx3), per_b(kb), per_b(vt)],
        out_specs=pl.BlockSpec((None, w, Q_BLOCK), lambda b, i: (b, 0, i)),
        out_shape=jax.ShapeDtypeStruct((nb, w, t), F32),
        scratch_shapes=[pltpu.VMEM((t, Q_BLOCK), F32),
                        pltpu.VMEM((n_pairs, 2 * kc, 2 * Q_BLOCK), F32),
                        pltpu.VMEM((n_pairs, 2 * kc, 2 * Q_BLOCK), F32),
                        pltpu.VMEM((n_pairs, 2 * HEAD_DIM, 2 * Q_BLOCK), BF16),
                        pltpu.VMEM((n_pairs, kidx3.shape[2], 2 * Q_BLOCK), BF16)],
        compiler_params=_params("parallel", "arbitrary"),
        name="dsa_prompt",
    )(qt, qiht, qilt, wit, kidx3, kb, vt)


SAMPLE_PAGES_PER_STEP = 16
QROWS = 8


SELECT_BATCHES_PER_STEP = 8


def _dsa_sample_select_kernel(pt_ref, qhi_ref, qlo_ref, wi_ref, knew_ref, *rest, topk, n_new, gp, gb):
    pages = rest[:gb * gp]
    bias_ref, s_scr = rest[gb * gp], rest[gb * gp + 1]
    j = pl.program_id(1)
    nj = pl.num_programs(1)
    psz = pages[0].shape[1]
    past = nj * gp * psz
    rows = gb * QROWS

    def scores(g, kpage_t):
        khi, klo = _split3(kpage_t)
        qhi = qhi_ref[g]
        s = _dot(qhi, khi) + _dot(qhi, klo) + _dot(qlo_ref[g], khi)
        s = jnp.maximum(s, 0.0) * wi_ref[g]
        tot = s[0:QROWS, :]
        for h in range(1, IDX_HEADS):
            tot = tot + s[h * QROWS:(h + 1) * QROWS, :]
        return tot * IDX_SCALE

    for g in range(gb):
        for p in range(gp):
            off = pl.multiple_of((j * gp + p) * psz, psz)
            s_scr[g * QROWS:(g + 1) * QROWS, pl.ds(off, psz)] = scores(g, pages[g * gp + p][...])

    @pl.when(j == nj - 1)
    def _():
        qrow = lax.broadcasted_iota(jnp.int32, (QROWS, psz), 0)
        kcol = lax.broadcasted_iota(jnp.int32, (QROWS, psz), 1)
        vis_new = jnp.logical_and(kcol <= qrow, kcol < n_new)
        for g in range(gb):
            s_scr[g * QROWS:(g + 1) * QROWS, past:past + psz] = jnp.where(vis_new, scores(g, knew_ref[g]), -jnp.inf)
        s = s_scr[...]
        nk = s.shape[1]
        kpos = lax.broadcasted_iota(jnp.int32, s.shape, 1)
        qrow1 = lax.broadcasted_iota(jnp.int32, (rows, 1), 0) & (QROWS - 1)
        n_vis = (past + jnp.minimum(qrow1 + 1, n_new)).astype(F32)
        kq = jnp.minimum(n_vis, float(topk))
        fin = s > -jnp.inf
        mn = jnp.min(jnp.where(fin, s, jnp.inf), axis=1, keepdims=True)
        mx = jnp.max(s, axis=1, keepdims=True)
        count = lambda pred: jnp.sum(jnp.where(pred, 1.0, 0.0), axis=1, keepdims=True)
        theta, lo, tie = _select_threshold(
            lambda x: count(s >= x),
            lambda x: jnp.max(jnp.where(s < x, s, -jnp.inf), axis=1, keepdims=True),
            mn, mx, kq, jnp.logical_or(n_vis <= float(topk), qrow1 >= n_new), 0)

        sel = s >= theta
        bias_ref[...] = jnp.where(sel, 0.0, NEG_BIG).reshape(bias_ref.shape)

        @pl.when(jnp.max(tie) > 0.0)
        def _():
            need = kq - count(sel)
            tied = jnp.logical_and(tie > 0.0, s == lo)
            nbits = max(1, (nk - 1).bit_length())

            def jstep(b, jl):
                cand = jl + (jnp.int32(1) << (nbits - 1 - b))
                c_lt = count(jnp.logical_and(tied, kpos < cand))
                return jnp.where(c_lt < need, cand, jl)
            jlast = lax.fori_loop(0, nbits, jstep, jnp.zeros((rows, 1), jnp.int32))
            keep = jnp.logical_or(sel, jnp.logical_and(tied, kpos <= jlast))
            bias_ref[...] = jnp.where(keep, 0.0, NEG_BIG).reshape(bias_ref.shape)


def _dsa_sample_select(layer, page_table, qhi, qlo, wi_b, knew, pool_kidx, topk, n_new):
    nb, n_pages = page_table.shape
    gp = math.gcd(n_pages, SAMPLE_PAGES_PER_STEP)
    gb = math.gcd(nb, SELECT_BATCHES_PER_STEP)
    di, psz = pool_kidx.shape[2], pool_kidx.shape[3]
    nk = (n_pages + 1) * psz
    per_b = lambda a: pl.BlockSpec((gb,) + a.shape[1:], lambda b, j, pt: (b, 0, 0))
    page = lambda g, p: pl.BlockSpec((None, None, di, psz),
                                     lambda b, j, pt: (layer, pt[b * gb + g, j * gp + p], 0, 0))
    grid_spec = pltpu.PrefetchScalarGridSpec(
        num_scalar_prefetch=1,
        grid=(nb // gb, n_pages // gp),
        in_specs=[per_b(qhi), per_b(qlo), per_b(wi_b), per_b(knew)]
        + [page(g, p) for g in range(gb) for p in range(gp)],
        out_specs=pl.BlockSpec((gb, QROWS, nk), lambda b, j, pt: (b, 0, 0)),
        scratch_shapes=[pltpu.VMEM((gb * QROWS, nk), F32)],
    )
    return pl.pallas_call(
        functools.partial(_dsa_sample_select_kernel, topk=topk, n_new=n_new, gp=gp, gb=gb),
        grid_spec=grid_spec,
        out_shape=jax.ShapeDtypeStruct((nb, QROWS, nk), F32),
        compiler_params=_params("parallel", "arbitrary"),
        name="dsa_sample_select",
    )(page_table, qhi, qlo, wi_b, knew, *([pool_kidx] * (gb * gp)))


def _dsa_sample_attend_kernel(pt_ref, q_ref, knew_ref, vnew_ref, *rest, gp):
    bias_refs = rest[:gp]
    kpages = rest[gp:2 * gp]
    vpages = rest[2 * gp:3 * gp]
    bnew_ref = rest[3 * gp]
    o_ref, m_s, l_s, acc_s = rest[3 * gp + 1:]
    j = pl.program_id(1)
    nj = pl.num_programs(1)
    q = q_ref[...]
    n_heads = q.shape[0] // QROWS

    @pl.when(j == 0)
    def _():
        m_s[...] = jnp.full(m_s.shape, NEG_BIG, F32)
        l_s[...] = jnp.zeros(l_s.shape, F32)
        acc_s[...] = jnp.zeros(acc_s.shape, F32)

    def absorb(kts, vts, biases):
        psz = kts[0].shape[1]
        logit = jnp.concatenate(
            [_dot(q, kt[...].astype(BF16)) + jnp.concatenate([b[...]] * n_heads, axis=0)
             for kt, b in zip(kts, biases)], axis=1)
        m = m_s[...]
        m_new = jnp.maximum(m, jnp.max(logit, axis=1, keepdims=True))
        alpha = jnp.exp2(m - m_new)
        p = jnp.exp2(logit - m_new)
        l_s[...] = alpha * l_s[...] + jnp.sum(p, axis=1, keepdims=True)
        pv = _dot_t(p[:, 0:psz].astype(BF16), vts[0][...].astype(BF16))
        for g in range(1, len(vts)):
            pv = pv + _dot_t(p[:, g * psz:(g + 1) * psz].astype(BF16), vts[g][...].astype(BF16))
        acc_s[...] = alpha * acc_s[...] + pv
        m_s[...] = m_new

    absorb(kpages, vpages, bias_refs)

    @pl.when(j == nj - 1)
    def _():
        absorb([knew_ref], [vnew_ref], [bnew_ref])
        out = acc_s[...] / l_s[...]
        lane = lax.broadcasted_iota(jnp.int32, (QROWS, out.shape[1]), 1)
        y = jnp.zeros((QROWS, out.shape[1]), F32)
        for h in range(n_heads):
            in_head = jnp.logical_and(lane >= h * HEAD_DIM, lane < (h + 1) * HEAD_DIM)
            y = y + jnp.where(in_head, out[h * QROWS:(h + 1) * QROWS, :], 0.0)
        o_ref[...] = y


def _dsa_sample_attend(layer, page_table, q32, knew, vnew, bias, pool_k, pool_v):
    nb, n_pages = page_table.shape
    gp = math.gcd(n_pages, 2 * SAMPLE_PAGES_PER_STEP)
    w, psz = pool_k.shape[2], pool_k.shape[3]
    per_b = lambda a: pl.BlockSpec((None,) + a.shape[1:], lambda b, j, pt: (b, 0, 0))
    page = lambda g: pl.BlockSpec((None, None, w, psz), lambda b, j, pt, g=g: (layer, pt[b, j * gp + g], 0, 0))
    bias_pg = lambda g: pl.BlockSpec((None, QROWS, psz), lambda b, j, pt, g=g: (b, 0, j * gp + g))
    bias_new = pl.BlockSpec((None, QROWS, psz), lambda b, j, pt: (b, 0, n_pages))
    grid_spec = pltpu.PrefetchScalarGridSpec(
        num_scalar_prefetch=1,
        grid=(nb, n_pages // gp),
        in_specs=([per_b(q32), per_b(knew), per_b(vnew)] + [bias_pg(g) for g in range(gp)]
                  + [page(g) for g in range(gp)] + [page(g) for g in range(gp)] + [bias_new]),
        out_specs=pl.BlockSpec((None, QROWS, w), lambda b, j, pt: (b, 0, 0)),
        scratch_shapes=[pltpu.VMEM((q32.shape[1], 1), F32), pltpu.VMEM((q32.shape[1], 1), F32),
                        pltpu.VMEM((q32.shape[1], w), F32)],
    )
    return pl.pallas_call(
        functools.partial(_dsa_sample_attend_kernel, gp=gp),
        grid_spec=grid_spec,
        out_shape=jax.ShapeDtypeStruct((nb, QROWS, w), F32),
        compiler_params=_params("parallel", "arbitrary"),
        name="dsa_sample_attend",
    )(page_table, q32, knew, vnew, *([bias] * gp), *([pool_k] * gp), *([pool_v] * gp), bias)


def _mixout_kernel(x_ref, u_ref, g_ref, ys_ref, h_ref, ycc_ref, yatt_ref, d_ref, wglu_ref, bglu_ref,
                   wout_ref, g1_ref, b1_ref, o_ref, *, alpha):
    w = u_ref.shape[-1]
    y = ys_ref[...] + d_ref[...] * u_ref[...]
    z = _dot(jax.nn.gelu(y).astype(BF16), wglu_ref[...]) + bglu_ref[...]
    y_s5 = z[:, :w] * jax.nn.sigmoid(z[:, w:])
    y_lru = h_ref[...] * jax.nn.gelu(g_ref[...])
    acc = _dot(y_s5.astype(BF16), wout_ref[0:w, :])
    acc = acc + _dot(y_lru.astype(BF16), wout_ref[w:2 * w, :])
    acc = acc + _dot(ycc_ref[...].astype(BF16), wout_ref[2 * w:3 * w, :])
    acc = acc + _dot(yatt_ref[...].astype(BF16), wout_ref[3 * w:4 * w, :])
    o_ref[...] = _layer_norm(alpha * x_ref[...] + acc, g1_ref[...], b1_ref[...])


def _mixout(x2, zmix, ys, h, ycc, yatt, d, wglu, bglu, wout, g1, b1, tm, alpha):
    m, dm = x2.shape
    w = ys.shape[1]
    row = lambda a: pl.BlockSpec((tm, a.shape[1]), lambda i: (i, 0))
    full = lambda a: pl.BlockSpec(a.shape, lambda i: (0, 0))
    zcol = lambda c: pl.BlockSpec((tm, w), lambda i, c=c: (i, c))
    return pl.pallas_call(
        functools.partial(_mixout_kernel, alpha=alpha),
        grid=(m // tm,),
        in_specs=[row(x2), zcol(0), zcol(2), row(ys), row(h), row(ycc), row(yatt), full(d), full(wglu),
                  full(bglu), full(wout), full(g1), full(b1)],
        out_specs=pl.BlockSpec((tm, dm), lambda i: (i, 0)),
        out_shape=jax.ShapeDtypeStruct((m, dm), F32),
        compiler_params=_params("parallel"),
        name="mixout",
    )(x2, zmix, zmix, ys, h, ycc, yatt, d, wglu, bglu, wout, g1, b1)


def _ffn_kernel(h_ref, wg_ref, wu_ref, wd_ref, g2_ref, b2_ref, o_ref, acc_s, hb_s, *, alpha):
    f = pl.program_id(1)

    @pl.when(f == 0)
    def _():
        acc_s[...] = jnp.zeros(acc_s.shape, F32)
        hb_s[...] = h_ref[...].astype(BF16)

    hb = hb_s[...]
    a = _dot(hb, wg_ref[...])
    act = (a * jax.nn.sigmoid(a)) * _dot(hb, wu_ref[...])
    acc_s[...] += _dot(act.astype(BF16), wd_ref[...])

    @pl.when(f == pl.num_programs(1) - 1)
    def _():
        o_ref[...] = _layer_norm(alpha * h_ref[...] + acc_s[...], g2_ref[...], b2_ref[...])


def _ffn(h, wg, wu, wd, g2, b2, tm, tf, alpha):
    m, dm = h.shape
    nf = wg.shape[1] // tf
    return pl.pallas_call(
        functools.partial(_ffn_kernel, alpha=alpha),
        grid=(m // tm, nf),
        in_specs=[pl.BlockSpec((tm, dm), lambda i, f: (i, 0)),
                  pl.BlockSpec((dm, tf), lambda i, f: (0, f)),
                  pl.BlockSpec((dm, tf), lambda i, f: (0, f)),
                  pl.BlockSpec((tf, dm), lambda i, f: (f, 0)),
                  pl.BlockSpec(g2.shape, lambda i, f: (0, 0)),
                  pl.BlockSpec(b2.shape, lambda i, f: (0, 0))],
        out_specs=pl.BlockSpec((tm, dm), lambda i, f: (i, 0)),
        out_shape=jax.ShapeDtypeStruct((m, dm), F32),
        scratch_shapes=[pltpu.VMEM((tm, dm), F32), pltpu.VMEM((tm, dm), BF16)],
        compiler_params=_params("parallel", "arbitrary"),
        name="ffn",
    )(h, wg, wu, wd, g2, b2)


MOE_CHUNK_ROWS = 288
MOE_COL_TILE = 1792
MOE_ROW_TILE = 1024
FFN_COL_TILE = 256
FFN_ROW_TILE = 1024


def _moe_kernel(h_ref, wrh_ref, wrl_ref, br_ref, wg_ref, wu_ref, wd_ref, g2_ref, b2_ref, o_ref,
                hb_s, xg_s, y_s, gate_c, slot_c, gate_b, slot_b_s, slot_t_s, *, alpha, n_exp, cr):
    e = pl.program_id(1)
    f = pl.program_id(2)
    nf = pl.num_programs(2)
    tm, dm = h_ref.shape
    first = jnp.logical_and(e == 0, f == 0)
    last = jnp.logical_and(e == pl.num_programs(1) - 1, f == nf - 1)

    @pl.when(first)
    def _():
        hhi, hlo = _split3(h_ref[...])
        hb_s[...] = hhi
        logits = _dot(hhi, wrh_ref[...]) + _dot(hhi, wrl_ref[...]) + _dot(hlo, wrh_ref[...]) + br_ref[...]
        lane = lax.broadcasted_iota(jnp.int32, logits.shape, 1)
        m1 = jnp.max(logits, axis=1, keepdims=True)
        i1 = jnp.min(jnp.where(logits == m1, lane, LANES), axis=1, keepdims=True)
        rest = jnp.where(lane == i1, -jnp.inf, logits)
        m2 = jnp.max(rest, axis=1, keepdims=True)
        i2 = jnp.min(jnp.where(rest == m2, lane, LANES), axis=1, keepdims=True)
        e2 = jnp.exp(m2 - m1)
        den = 1.0 + e2
        gate = jnp.where(lane == i1, 1.0 / den, 0.0) + jnp.where(lane == i2, e2 / den, 0.0)
        routed = jnp.logical_or(lane == i1, lane == i2)
        ind = jnp.where(routed, 1.0, 0.0).astype(BF16)
        tri = jnp.where(lax.broadcasted_iota(jnp.int32, (tm, tm), 0) >= lax.broadcasted_iota(jnp.int32, (tm, tm), 1),
                        1.0, 0.0).astype(BF16)
        slot = jnp.where(routed, _dot(tri, ind) - 1.0, -1.0)
        eye = jnp.where(lax.broadcasted_iota(jnp.int32, (LANES, LANES), 0)
                        == lax.broadcasted_iota(jnp.int32, (LANES, LANES), 1), 1.0, 0.0).astype(BF16)
        ind_t = _dot_t(eye, ind)
        slot_t = jnp.where(ind_t > 0.0, _dot_t(ind_t.astype(BF16), tri) - 1.0, -1.0)
        slot_t_s[...] = slot_t[0:slot_t_s.shape[0], :]
        gate_c[...] = gate
        slot_c[...] = slot
        o_ref[...] = jnp.zeros(o_ref.shape, F32)

    @pl.when(f == 0)
    def _():
        lane = lax.broadcasted_iota(jnp.int32, gate_c.shape, 1)
        pick = lambda x: jnp.sum(jnp.where(lane == e, x, 0.0), axis=1, keepdims=True)
        gate_b[...] = jnp.broadcast_to(pick(gate_c[...]), gate_b.shape)
        slot_b_s[...] = jnp.broadcast_to(pick(slot_c[...]), slot_b_s.shape)

    slot_b = slot_b_s[...]
    n_pass = ((jnp.max(slot_b) + 1.0).astype(jnp.int32) + cr - 1) // cr

    def pass_rows(r):
        return pl.ds(pl.multiple_of(r * cr, 16), cr)

    @pl.when(f == 0)
    def _():
        srow = slot_t_s[pl.ds(e, 1), :]

        def pack(r, _):
            want = (r * cr + lax.broadcasted_iota(jnp.int32, (cr, tm), 0)).astype(F32)
            onehot = jnp.where(srow == want, 1.0, 0.0).astype(BF16)
            xg_s[pass_rows(r), :] = _dot(onehot, hb_s[...]).astype(BF16)
            return 0
        lax.fori_loop(0, n_pass, pack, 0)

    def expert_pass(r, _):
        x = xg_s[pass_rows(r), :]
        a = _dot(x, wg_ref[...])
        act = (a * jax.nn.sigmoid(a)) * _dot(x, wu_ref[...])
        y = _dot(act.astype(BF16), wd_ref[...])

        @pl.when(f == 0)
        def _():
            y_s[pass_rows(r), :] = y

        @pl.when(f > 0)
        def _():
            y_s[pass_rows(r), :] = y_s[pass_rows(r), :] + y
        return 0
    lax.fori_loop(0, n_pass, expert_pass, 0)

    @pl.when(f == nf - 1)
    def _():
        reps = (cr + LANES - 1) // LANES
        slot_w = jnp.concatenate([slot_b] * reps, axis=1)[:, :cr]
        gate_w = jnp.concatenate([gate_b[...]] * (dm // LANES), axis=1)

        def unpack(r, _):
            want = (r * cr + lax.broadcasted_iota(jnp.int32, (tm, cr), 1)).astype(F32)
            onehot = jnp.where(slot_w == want, 1.0, 0.0).astype(BF16)
            o_ref[...] += gate_w * _dot(onehot, y_s[pass_rows(r), :].astype(BF16))
            return 0
        lax.fori_loop(0, n_pass, unpack, 0)

    @pl.when(last)
    def _():
        o_ref[...] = _layer_norm(alpha * h_ref[...] + o_ref[...], g2_ref[...], b2_ref[...])


def _moe(h, wrh, wrl, br, wg, wu, wd, g2, b2, tm, tf, alpha):
    m, dm = h.shape
    n_exp, _, dff = wg.shape
    nf = dff // tf
    cr = min(MOE_CHUNK_ROWS, tm)
    cap = -(-tm // cr) * cr
    c2 = lambda a: pl.BlockSpec(a.shape, lambda i, e, f: (0, 0))
    return pl.pallas_call(
        functools.partial(_moe_kernel, alpha=alpha, n_exp=n_exp, cr=cr),
        grid=(m // tm, n_exp, nf),
        in_specs=[pl.BlockSpec((tm, dm), lambda i, e, f: (i, 0), pipeline_mode=pl.Buffered(1)),
                  c2(wrh), c2(wrl), c2(br),
                  pl.BlockSpec((None, dm, tf), lambda i, e, f: (e, 0, f)),
                  pl.BlockSpec((None, dm, tf), lambda i, e, f: (e, 0, f)),
                  pl.BlockSpec((None, tf, dm), lambda i, e, f: (e, f, 0)),
                  c2(g2), c2(b2)],
        out_specs=pl.BlockSpec((tm, dm), lambda i, e, f: (i, 0)),
        out_shape=jax.ShapeDtypeStruct((m, dm), F32),
        scratch_shapes=[pltpu.VMEM((tm, dm), BF16),
                        pltpu.VMEM((cap, dm), BF16),
                        pltpu.VMEM((cap, dm), F32),
                        pltpu.VMEM((tm, LANES), F32),
                        pltpu.VMEM((tm, LANES), F32),
                        pltpu.VMEM((tm, LANES), F32),
                        pltpu.VMEM((tm, LANES), F32),
                        pltpu.VMEM((max(8, n_exp), tm), F32)],
        compiler_params=_params("parallel", "arbitrary", "arbitrary"),
        name="moe",
    )(h, wrh, wrl, br, wg, wu, wd, g2, b2)


def _row_tile(m, target):
    t = math.gcd(m, target)
    return t if t % 8 == 0 else m


def _chunk_len(t):
    return math.gcd(t, 128)


def _rope_tables(pos, reps):
    half = HEAD_DIM // 2
    inv = ROPE_THETA ** (-(jnp.arange(half, dtype=F32) / half))
    ang = pos.astype(F32)[:, None] * inv[None, :]
    cos, sin = jnp.cos(ang), jnp.sin(ang)
    cos_h = jnp.concatenate([cos, cos], axis=-1)
    sin_h = jnp.concatenate([-sin, sin], axis=-1)
    return jnp.tile(cos_h, (1, reps)), jnp.tile(sin_h, (1, reps))


def _s5_params(a_re, a_im, b_re, b_im, c_re, c_im, log_dt):
    ng, ns = a_re.shape
    lr, li = a_re.astype(F32), a_im.astype(F32)
    dt = jnp.exp(log_dt.astype(F32))[:, None]
    mag = jnp.exp(lr * dt)
    ar, ai = mag * jnp.cos(li * dt), mag * jnp.sin(li * dt)
    den = lr * lr + li * li
    qr = ((ar - 1.0) * lr + ai * li) / den
    qi = (ai * lr - (ar - 1.0) * li) / den
    br, bi = b_re.astype(F32), b_im.astype(F32)
    bbr = qr[..., None] * br - qi[..., None] * bi
    bbi = qr[..., None] * bi + qi[..., None] * br
    eye = jnp.eye(ng, dtype=F32)
    gpl = LANES // ns
    nslab = ng // gpl

    def b_mat(x):
        full = jnp.einsum('gpc,gh->gchp', x, eye).reshape(ng * S5_CH, ng * ns)
        return jnp.transpose(full.reshape(ng * S5_CH, nslab, LANES), (1, 0, 2)).astype(BF16)

    def c_mat(x):
        return jnp.einsum('gcp,gh->gphc', x, eye).reshape(nslab, LANES, ng * S5_CH)

    ccat = jnp.concatenate([c_mat(c_re.astype(F32)), -c_mat(c_im.astype(F32))], axis=1).astype(BF16)
    return b_mat(bbr), b_mat(bbi), ar.reshape(nslab, 1, LANES), ai.reshape(nslab, 1, LANES), ccat


def _to_chunks(x, nl):
    nb, t, w = x.shape
    return jnp.swapaxes(x.reshape(nb, t // nl, nl, w), 1, 2)


def _from_chunks(x):
    nb, nl, nr, w = x.shape
    return jnp.swapaxes(x, 1, 2).reshape(nb, nr * nl, w)


def _mixers(lp, zmix, nb, t, s5_re0, s5_im0, lru_h0, lru_buf0, cc_buf0, chained):
    w = lp['gw']
    zmix3 = zmix.reshape(nb, t, zmix.shape[-1])
    tt = math.gcd(t, 512)
    a, b, ycc, lru_buf, cc_buf = _convmix(zmix3, lru_buf0, cc_buf0, lp['lru_conv_w'], lp['lru_conv_b'],
                                          lp['lru_wg'], lp['lru_bg'], lp['lru_sp'], lp['cc_dw_w'], lp['cc_dw_b'],
                                          lp['cc_ln_g'], lp['cc_ln_b'], tt)
    u = zmix3[:, :, :w].astype(BF16)
    ns = s5_re0.shape[-2] * s5_re0.shape[-1]
    if chained:
        nl = _chunk_len(t)
        a4, b4, u4 = _to_chunks(a, nl), _to_chunks(b, nl), _to_chunks(u, nl)
        h0 = lru_h0.reshape(nb, 1, w)
        s0r, s0i = s5_re0.reshape(nb, 1, ns), s5_im0.reshape(nb, 1, ns)
    else:
        to_rows = lambda x: jnp.swapaxes(x, 0, 1)[None]
        a4, b4, u4 = to_rows(a), to_rows(b), to_rows(u)
        h0 = lru_h0.reshape(1, nb, w)
        s0r, s0i = s5_re0.reshape(1, nb, ns), s5_im0.reshape(1, nb, ns)
    h4, lru_h = _lru_scan(a4, b4, h0, chained)
    ys4, s5_re, s5_im = _s5_scan(u4, lp['s5_bre'], lp['s5_bim'], lp['s5_are'], lp['s5_aim'], lp['s5_ccat'],
                                 s0r, s0i, chained)
    if chained:
        h, ys = _from_chunks(h4), _from_chunks(ys4)
    else:
        h, ys = jnp.swapaxes(h4[0], 0, 1), jnp.swapaxes(ys4[0], 0, 1)
    states = (s5_re.reshape(s5_re0.shape), s5_im.reshape(s5_im0.shape), lru_h.reshape(lru_h0.shape),
              lru_buf, cc_buf)
    return ys.reshape(nb * t, w), h.reshape(nb * t, w), ycc.reshape(nb * t, w), states


def _idx3(x, order):
    hi, lo = _split3(x)
    parts = [hi if o == 'h' else lo for o in order]
    pad = jnp.zeros(x.shape[:-1] + (2 * LANES - len(order) * x.shape[-1],), BF16)
    return jnp.concatenate(parts + [pad], axis=-1)


def _attend_prompt(q, qi, k, v, kw, nb, t):
    w = q.shape[-1]
    topk = min(TOPK_MAX, t // 4)
    to_cols = lambda x: jnp.swapaxes(x.reshape(nb, t, -1), 1, 2)
    qt = to_cols(q)
    qih, qil = _split3(qi)
    wi = kw[:, IDX_DIM:IDX_DIM + IDX_HEADS].reshape(nb, t, IDX_HEADS)
    wit = jnp.pad(jnp.swapaxes(wi, 1, 2), ((0, 0), (0, 8 - IDX_HEADS), (0, 0)))
    kidx3 = _idx3(kw[:, :IDX_DIM].reshape(nb, t, IDX_DIM), 'hlh')
    kb = k.reshape(nb, t, w).astype(BF16)
    vt = jnp.swapaxes(v.reshape(nb, t, w), 1, 2).astype(BF16)
    kc = math.gcd(t // 2, 512)
    yt = _dsa_prompt(qt, to_cols(qih), to_cols(qil), wit, kidx3, kb, vt, topk, kc)
    return jnp.swapaxes(yt, 1, 2).reshape(nb * t, w)


def _attend_sample(layer, q, qi, k, v, kw, nb, t, pool_k, pool_v, pool_kidx, page_table):
    w = q.shape[-1]
    n_heads = w // HEAD_DIM
    psz = pool_kidx.shape[2]
    past = page_table.shape[1] * psz
    topk = min(TOPK_MAX, (past + t) // 4)
    pad_q = lambda x: jnp.pad(x, ((0, 0), (0, 0), (0, QROWS - t), (0, 0)))

    qi_h = jnp.swapaxes(qi.reshape(nb, t, IDX_HEADS, IDX_DIM), 1, 2)
    qhi, qlo = _split3(pad_q(qi_h).reshape(nb, IDX_HEADS * QROWS, IDX_DIM))
    wi = kw[:, IDX_DIM:IDX_DIM + IDX_HEADS].reshape(nb, t, IDX_HEADS)
    wi_b = jnp.broadcast_to(pad_q(jnp.swapaxes(wi, 1, 2)[..., None]).reshape(nb, IDX_HEADS * QROWS, 1),
                            (nb, IDX_HEADS * QROWS, psz))
    new_page_t = lambda x: jnp.swapaxes(jnp.pad(x, ((0, 0), (0, psz - t), (0, 0))), 1, 2)
    kidx_new = new_page_t(kw[:, :IDX_DIM].reshape(nb, t, IDX_DIM))
    bias = _dsa_sample_select(layer, page_table, qhi, qlo, wi_b, kidx_new, jnp.swapaxes(pool_kidx, 2, 3),
                              topk, t)

    head_of_lane = jnp.arange(w) // HEAD_DIM
    q_rows = jnp.broadcast_to(pad_q(q.reshape(nb, 1, t, w)), (nb, n_heads, QROWS, w))
    q32 = jnp.where(head_of_lane[None, None, None, :] == jnp.arange(n_heads)[None, :, None, None],
                    q_rows, jnp.zeros_like(q_rows)).reshape(nb, n_heads * QROWS, w)
    k_new = new_page_t(k.reshape(nb, t, w))
    v_new = new_page_t(v.reshape(nb, t, w))
    pool_t = lambda x: jnp.transpose(x, (0, 1, 3, 4, 2)).reshape(x.shape[:2] + (w, psz))
    y = _dsa_sample_attend(layer, page_table, q32, k_new, v_new, bias, pool_t(pool_k), pool_t(pool_v))
    return y[:, :t, :].reshape(nb * t, w)


def _prep_layer(l, w_in, s5_a_re, s5_a_im, s5_b_re, s5_b_im, s5_c_re, s5_c_im, s5_d, s5_log_dt, s5_w_glu,
                s5_b_glu, lru_conv_w, lru_conv_b, lru_w_a, lru_b_a, lru_w_x, lru_b_x, lru_lambda, cc_dw_w,
                cc_dw_b, cc_ln_g, cc_ln_b, w_out, ln1_g, ln1_b, ln2_g, ln2_b):
    gw = s5_d.shape[1]
    d_in = w_in.shape[2]
    mix_w = 5 * gw
    pad_to = mix_w + 4 * gw + LANES
    assert d_in == mix_w + 4 * gw + IDX_DIM + IDX_HEADS and gw == IDX_HEADS * IDX_DIM
    row = lambda x: x[l].reshape(1, -1).astype(F32)
    nh, hd, _ = lru_w_a.shape[1:]
    eye = jnp.eye(nh, dtype=F32)
    bd = lambda wt: jnp.einsum('hij,hk->hikj', wt[l].astype(F32), eye).reshape(nh * hd, nh * hd)
    s5_bre, s5_bim, s5_are, s5_aim, s5_ccat = _s5_params(s5_a_re[l], s5_a_im[l], s5_b_re[l], s5_b_im[l],
                                                         s5_c_re[l], s5_c_im[l], s5_log_dt[l])
    return dict(
        gw=gw, mix_w=mix_w,
        w_in=jnp.pad(w_in[l], ((0, 0), (0, pad_to - d_in))).astype(BF16),
        s5_bre=s5_bre, s5_bim=s5_bim, s5_are=s5_are, s5_aim=s5_aim, s5_ccat=s5_ccat,
        s5_d=row(s5_d), s5_w_glu=s5_w_glu[l].astype(BF16), s5_b_glu=row(s5_b_glu),
        lru_conv_w=lru_conv_w[l].astype(F32), lru_conv_b=row(lru_conv_b),
        lru_wg=jnp.concatenate([bd(lru_w_a), bd(lru_w_x)], axis=1).astype(BF16),
        lru_bg=jnp.concatenate([row(lru_b_a), row(lru_b_x)], axis=1),
        lru_sp=jax.nn.softplus(-row(lru_lambda)),
        cc_dw_w=cc_dw_w[l].astype(F32), cc_dw_b=row(cc_dw_b), cc_ln_g=row(cc_ln_g), cc_ln_b=row(cc_ln_b),
        w_out=w_out[l].astype(BF16), ln1_g=row(ln1_g), ln1_b=row(ln1_b), ln2_g=row(ln2_g), ln2_b=row(ln2_b),
    )


def kernel(x_prompt, x_sample, cache_k, cache_v, cache_kidx, state_s5_re, state_s5_im, state_lru_h, state_lru_conv, state_cc_conv, page_table, w_in, s5_a_re, s5_a_im, s5_b_re, s5_b_im, s5_c_re, s5_c_im, s5_d, s5_log_dt, s5_w_glu, s5_b_glu, lru_conv_w, lru_conv_b, lru_w_a, lru_b_a, lru_w_x, lru_b_x, lru_lambda, cc_dw_w, cc_dw_b, cc_ln_g, cc_ln_b, w_out, ln1_g, ln1_b, ln2_g, ln2_b, ffn_w_gate, ffn_w_up, ffn_w_down, moe_w_router, moe_b_router, moe_w_gate, moe_w_up, moe_w_down):
    bp, tp, dm = x_prompt.shape
    bs, ts, _ = x_sample.shape
    depth = w_in.shape[0]
    past = page_table.shape[1] * cache_k.shape[2]
    alpha = (2.0 * depth) ** 0.25
    n_heads = cache_k.shape[3]
    n_exp = moe_w_router.shape[-1]
    gw = s5_d.shape[1]

    cos_p, sin_p = _rope_tables(jnp.arange(tp, dtype=jnp.int32), gw // HEAD_DIM)
    cos_s, sin_s = _rope_tables(past + jnp.arange(ts, dtype=jnp.int32), gw // HEAD_DIM)
    cos_s, sin_s = jnp.tile(cos_s, (bs, 1)), jnp.tile(sin_s, (bs, 1))

    tm_p = _row_tile(tp, 512)
    xp = x_prompt.reshape(bp * tp, dm)
    xs = x_sample.reshape(bs * ts, dm)
    zeros_p = lambda *shape: jnp.zeros((bp,) + shape, x_prompt.dtype)
    states_p, states_s = [], []

    for l in range(depth):
        lp = _prep_layer(l, w_in, s5_a_re, s5_a_im, s5_b_re, s5_b_im, s5_c_re, s5_c_im, s5_d, s5_log_dt,
                         s5_w_glu, s5_b_glu, lru_conv_w, lru_conv_b, lru_w_a, lru_b_a, lru_w_x, lru_b_x,
                         lru_lambda, cc_dw_w, cc_dw_b, cc_ln_g, cc_ln_b, w_out, ln1_g, ln1_b, ln2_g, ln2_b)
        mix_w = lp['mix_w']
        j = l // 2
        if l % 2 == 0:
            tf = math.gcd(ffn_w_gate.shape[2], FFN_COL_TILE)
            cw = (ffn_w_gate[j].astype(BF16), ffn_w_up[j].astype(BF16), ffn_w_down[j].astype(BF16))
        else:
            wr = jnp.pad(moe_w_router[j].astype(F32), ((0, 0), (0, LANES - n_exp)))
            wrh, wrl = _split3(wr)
            br = jnp.pad(moe_b_router[j].astype(F32), (0, LANES - n_exp), constant_values=NEG_BIG).reshape(1, LANES)
            tf = math.gcd(moe_w_gate.shape[3], MOE_COL_TILE)
            cw = (wrh, wrl, br, moe_w_gate[j].astype(BF16), moe_w_up[j].astype(BF16), moe_w_down[j].astype(BF16))

        def channel(h):
            if l % 2 == 0:
                return _ffn(h, *cw, lp['ln2_g'], lp['ln2_b'], _row_tile(h.shape[0], FFN_ROW_TILE), tf, alpha)
            return _moe(h, *cw, lp['ln2_g'], lp['ln2_b'], _row_tile(h.shape[0], MOE_ROW_TILE), tf, alpha)

        zmix, q, qi, k, v, kw = _in_proj(xp, lp['w_in'], cos_p, sin_p, tm_p, mix_w, gw)
        ys, h, ycc, st = _mixers(lp, zmix, bp, tp,
                                 zeros_p(*state_s5_re.shape[2:]), zeros_p(*state_s5_im.shape[2:]),
                                 zeros_p(*state_lru_h.shape[2:]), zeros_p(*state_lru_conv.shape[2:]),
                                 zeros_p(*state_cc_conv.shape[2:]), chained=True)
        yatt = _attend_prompt(q, qi, k, v, kw, bp, tp)
        h1 = _mixout(xp, zmix, ys, h, ycc, yatt, lp['s5_d'], lp['s5_w_glu'], lp['s5_b_glu'], lp['w_out'],
                     lp['ln1_g'], lp['ln1_b'], tm_p, alpha)
        xp = channel(h1)
        states_p.append((k.reshape(bp, tp, n_heads, HEAD_DIM), v.reshape(bp, tp, n_heads, HEAD_DIM),
                         kw[:, :IDX_DIM].reshape(bp, tp, IDX_DIM)) + st)

        ms = bs * ts
        zmix, q, qi, k, v, kw = _in_proj(xs, lp['w_in'], cos_s, sin_s, ms, mix_w, gw)
        ys, h, ycc, st = _mixers(lp, zmix, bs, ts, state_s5_re[l], state_s5_im[l], state_lru_h[l],
                                 state_lru_conv[l], state_cc_conv[l], chained=False)
        yatt = _attend_sample(l, q, qi, k, v, kw, bs, ts, cache_k, cache_v, cache_kidx, page_table)
        h1 = _mixout(xs, zmix, ys, h, ycc, yatt, lp['s5_d'], lp['s5_w_glu'], lp['s5_b_glu'], lp['w_out'],
                     lp['ln1_g'], lp['ln1_b'], ms, alpha)
        xs = channel(h1)
        states_s.append((k.reshape(bs, ts, n_heads, HEAD_DIM), v.reshape(bs, ts, n_heads, HEAD_DIM),
                         kw[:, :IDX_DIM].reshape(bs, ts, IDX_DIM)) + st)

    new_p = [jnp.stack(col) for col in zip(*states_p)]
    new_s = [jnp.stack(col) for col in zip(*states_s)]
    out = [xp.reshape(bp, tp, dm), xs.reshape(bs, ts, dm)]
    for a, b in zip(new_p, new_s):
        out += [a, b]
    return tuple(out)
```
